```python
import math
import jax, jax.numpy as jnp
from jax import lax
import numpy as np

D_MODEL = 1024
BATCH = 4
SEQ = 4096
DEPTH = 4
DEC_BATCH = 32
DEC_SEQ = 1
PAST_LEN = 8192
PAGE_SIZE = 128

SSD_INNER = D_MODEL
SSD_HEAD_DIM = 64
SSD_HEADS = SSD_INNER // SSD_HEAD_DIM
SSD_GROUPS = 2
SSD_HPG = SSD_HEADS // SSD_GROUPS
SSD_STATE = 128
SSD_CONV = 4
SSD_CHUNK = 128
SSD_CONV_DIM = SSD_INNER + 2 * SSD_GROUPS * SSD_STATE
ATT_PATTERNS = ((128, 1), (512, 4), (2048, 16))
ATT_N_GROUPS = len(ATT_PATTERNS)
ATT_HEADS_PER_GROUP = 8
ATT_HEAD_DIM = 64
ATT_BLOCK = 128
ATT_OUT = ATT_HEADS_PER_GROUP * ATT_HEAD_DIM
MOE_GROUPS = 4
MOE_PER_GROUP = 4
MOE_EXPERTS = MOE_GROUPS * MOE_PER_GROUP
MOE_TOPK = 2
MOE_FF = 512
IN_Z = SSD_INNER
IN_XBC = SSD_CONV_DIM
IN_DT = SSD_HEADS
IN_QKV = 3 * ATT_N_GROUPS * ATT_OUT
IN_GATE = 2 * D_MODEL
IN_WIDTH = IN_Z + IN_XBC + IN_DT + IN_QKV + IN_GATE
IN_SPLITS = [IN_Z, IN_Z + IN_XBC, IN_Z + IN_XBC + IN_DT, IN_Z + IN_XBC + IN_DT + IN_QKV]
EPS = 1e-6

kernel_name = 'hybrid_ssd_dilated_attn_hmoe_decode_step'


def _rmsnorm(x, g):
    xf = x.astype(jnp.float32)
    y = xf * lax.rsqrt(jnp.mean(xf * xf, axis=-1, keepdims=True) + EPS)
    return (y * g.astype(jnp.float32)).astype(x.dtype)


def _modulate(h, shift, scale):
    return h * (1 + scale[:, None, :]) + shift[:, None, :]


def _causal_dwconv(xbc, buf, w, b):
    xp = jnp.concatenate([buf.astype(xbc.dtype), xbc], axis=1)
    y = lax.conv_general_dilated(xp, w[:, None, :].astype(xbc.dtype), window_strides=(1,), padding='VALID',
                                 dimension_numbers=('NWC', 'WIO', 'NWC'), feature_group_count=xbc.shape[-1])
    return y + b.astype(xbc.dtype), xp[:, xp.shape[1] - (SSD_CONV - 1):]


def _ssd_chunked(x, dt, a, bm, cm, h0):
    bsz, T = x.shape[:2]
    q = min(SSD_CHUNK, T)
    nc = -(-T // q)
    pad = nc * q - T

    def blocks(t):
        t = jnp.pad(t.astype(jnp.float32), [(0, 0), (0, pad)] + [(0, 0)] * (t.ndim - 2))
        return t.reshape((bsz, nc, q) + t.shape[2:])

    x, dt, bm, cm = blocks(x), blocks(dt), blocks(bm), blocks(cm)
    cum = jnp.cumsum(dt * a, axis=2)
    cum_t = jnp.moveaxis(cum, 2, -1)
    causal = jnp.tril(jnp.ones((q, q), dtype=bool))
    seg = cum_t[..., :, None] - cum_t[..., None, :]
    decay = jnp.where(causal, jnp.exp(jnp.where(causal, seg, 0.0)), 0.0)
    xdt = x * dt[..., None]
    cb = jnp.einsum('bclgn,bcsgn->bcgls', cm, bm)
    y_diag = jnp.einsum('bcgels,bcsgep->bclgep', cb[:, :, :, None] * decay, xdt)
    decay_end = jnp.exp(cum[:, :, -1:] - cum)
    chunk_states = jnp.einsum('bclgn,bclge,bclgep->bcgepn', bm, decay_end, xdt)
    chunk_decay = jnp.exp(cum[:, :, -1])

    def step(h, inp):
        st, dec = inp
        return h * dec[..., None, None] + st, h

    h_last, h_in = lax.scan(step, h0.astype(jnp.float32),
                            (jnp.moveaxis(chunk_states, 1, 0), jnp.moveaxis(chunk_decay, 1, 0)))
    h_in = jnp.moveaxis(h_in, 0, 1)
    y_off = jnp.einsum('bclgn,bcgepn,bclge->bclgep', cm, h_in, jnp.exp(cum))
    y = (y_diag + y_off).reshape((bsz, nc * q) + x.shape[3:])[:, :T]
    return y, h_last


def _dilated_attn_prompt(q, k, v, window, dil):
    bsz, S, H, hd = q.shape
    band = window // dil
    L = S // dil
    nb = -(-L // ATT_BLOCK)
    Lp = nb * ATT_BLOCK

    def by_residue(t):
        t = t.astype(jnp.float32).reshape(bsz, L, dil, H, hd).transpose(0, 2, 1, 3, 4)
        t = jnp.pad(t, ((0, 0), (0, 0), (0, Lp - L), (0, 0), (0, 0)))
        return t.reshape(bsz, dil, nb, ATT_BLOCK, H, hd)

    def with_prev(t):
        prev = jnp.pad(t[:, :, :-1], ((0, 0), (0, 0), (1, 0), (0, 0), (0, 0), (0, 0)))
        return jnp.concatenate([prev, t], axis=3)

    qb, kb, vb = by_residue(q), by_residue(k), by_residue(v)
    kk, vv = with_prev(kb), with_prev(vb)
    s = jnp.einsum('brnqhd,brnkhd->brnhqk', qb, kk) * (hd ** -0.5)
    qi = jnp.arange(ATT_BLOCK)[:, None]
    ki = jnp.arange(2 * ATT_BLOCK)[None, :]
    dist = qi + ATT_BLOCK - ki
    valid = (dist >= 0) & (dist <= band)
    valid = valid[None] & ((jnp.arange(nb)[:, None, None] > 0) | (ki >= ATT_BLOCK)[None])
    s = jnp.where(valid[:, None], s, -jnp.inf)
    m = jnp.max(s, axis=-1, keepdims=True)
    e = jnp.exp(s - m)
    den = jnp.sum(e, axis=-1)
    o = jnp.einsum('brnhqk,brnkhd->brnqhd', e, vv) / jnp.moveaxis(den, -2, -1)[..., None]
    lse = jnp.moveaxis(m[..., 0] + jnp.log(den), -2, -1)
    o = o.reshape(bsz, dil, Lp, H, hd)[:, :, :L].transpose(0, 2, 1, 3, 4).reshape(bsz, S, H, hd)
    lse = lse.reshape(bsz, dil, Lp, H)[:, :, :L].transpose(0, 2, 1, 3).reshape(bsz, S, H)
    return o, lse


def _dilated_attn_step(q, k, v, kbuf, vbuf, window, dil):
    bsz, T, H, hd = q.shape
    Lb = kbuf.shape[1]
    band = window // dil
    kc = jnp.concatenate([kbuf.astype(k.dtype), k], axis=1)
    vc = jnp.concatenate([vbuf.astype(v.dtype), v], axis=1)
    idx = Lb + jnp.arange(T)[:, None] - dil * jnp.arange(band + 1)[None, :]
    valid = idx >= 0
    idx = jnp.maximum(idx, 0)
    kg = kc[:, idx].astype(jnp.float32)
    vg = vc[:, idx].astype(jnp.float32)
    s = jnp.einsum('bthd,btjhd->bthj', q.astype(jnp.float32), kg) * (hd ** -0.5)
    s = jnp.where(valid[None, :, None, :], s, -jnp.inf)
    m = jnp.max(s, axis=-1, keepdims=True)
    e = jnp.exp(s - m)
    den = jnp.sum(e, axis=-1)
    o = jnp.einsum('bthj,btjhd->bthd', e, vg) / den[..., None]
    lse = m[..., 0] + jnp.log(den)
    return o, lse, kc[:, T:], vc[:, T:]


def _hier_moe(h, lw):
    bsz, T, dm = h.shape
    ht = h.reshape(bsz * T, dm)
    g_prob = jax.nn.softmax((ht @ lw['w_router_group'] + lw['b_router_group']).astype(jnp.float32), axis=-1)
    g_w, g_idx = lax.top_k(g_prob, 1)
    e_logits = (ht @ lw['w_router_expert'] + lw['b_router_expert']).astype(jnp.float32)
    e_logits = e_logits.reshape(-1, MOE_GROUPS, MOE_PER_GROUP)
    e_in_group = jnp.take_along_axis(e_logits, g_idx[:, :, None], axis=1)[:, 0]
    e_w, e_idx = lax.top_k(jax.nn.softmax(e_in_group, axis=-1), MOE_TOPK)
    e_w = e_w / jnp.sum(e_w, axis=-1, keepdims=True)
    expert = g_idx * MOE_PER_GROUP + e_idx
    gate = jnp.sum(jax.nn.one_hot(expert, MOE_EXPERTS, dtype=jnp.float32) * (g_w * e_w)[..., None], axis=1)
    hg = jnp.einsum('nd,edf->nef', ht, lw['w_exp_gate'])
    hu = jnp.einsum('nd,edf->nef', ht, lw['w_exp_up'])
    act = jax.nn.silu(hg) * hu * gate.astype(ht.dtype)[..., None]
    out = jnp.einsum('nef,efd->nd', act, lw['w_exp_down'])
    return out.reshape(bsz, T, dm)


def _layer(x, c, lw, conv_buf, ssm_state, kbufs, vbufs):
    bsz, T, _ = x.shape
    f32 = jnp.float32
    mod = jax.nn.silu(c) @ lw['w_ada'] + lw['b_ada']
    sh1, sc1, g1, sh2, sc2, g2 = jnp.split(mod, 6, axis=-1)
    h = _modulate(_rmsnorm(x, lw['norm1_g']), sh1, sc1)
    u = h @ lw['w_in']
    z, xbc, dt, qkv, gates = jnp.split(u, IN_SPLITS, axis=-1)

    xbc, conv_new = _causal_dwconv(xbc, conv_buf, lw['conv_w'], lw['conv_b'])
    xbc = jax.nn.silu(xbc)
    xs, bm, cm = jnp.split(xbc, [SSD_INNER, SSD_INNER + SSD_GROUPS * SSD_STATE], axis=-1)
    xs = xs.reshape(bsz, T, SSD_GROUPS, SSD_HPG, SSD_HEAD_DIM)
    bm = bm.reshape(bsz, T, SSD_GROUPS, SSD_STATE)
    cm = cm.reshape(bsz, T, SSD_GROUPS, SSD_STATE)
    dt = jax.nn.softplus(dt.astype(f32) + lw['dt_bias'].astype(f32)).reshape(bsz, T, SSD_GROUPS, SSD_HPG)
    a = -jnp.exp(lw['a_log'].astype(f32)).reshape(SSD_GROUPS, SSD_HPG)
    y, ssm_new = _ssd_chunked(xs, dt, a, bm, cm, ssm_state)
    y = y + lw['d_skip'].astype(f32).reshape(SSD_GROUPS, SSD_HPG)[:, :, None] * xs.astype(f32)
    zg = jax.nn.silu(z.astype(f32)).reshape(bsz, T, SSD_GROUPS, SSD_HPG, SSD_HEAD_DIM)
    y = _rmsnorm((y * zg).reshape(bsz, T, SSD_GROUPS, SSD_HPG * SSD_HEAD_DIM),
                 lw['ssd_norm_g'].reshape(SSD_GROUPS, SSD_HPG * SSD_HEAD_DIM))
    ssd_branch = y.reshape(bsz, T, SSD_INNER).astype(x.dtype) @ lw['w_ssd_proj']

    qkv = qkv.reshape(bsz, T, 3, ATT_N_GROUPS, ATT_HEADS_PER_GROUP, ATT_HEAD_DIM)
    outs, lses, new_k, new_v = [], [], [], []
    for gi, (win, dil) in enumerate(ATT_PATTERNS):
        q, k, v = qkv[:, :, 0, gi], qkv[:, :, 1, gi], qkv[:, :, 2, gi]
        if kbufs is None:
            o, lse = _dilated_attn_prompt(q, k, v, win, dil)
            keep = min(win, T)
            nk, nv = k[:, T - keep:], v[:, T - keep:]
        else:
            o, lse, nk, nv = _dilated_attn_step(q, k, v, kbufs[gi], vbufs[gi], win, dil)
        outs.append(o)
        lses.append(lse)
        new_k.append(nk)
        new_v.append(nv)
    wts = jax.nn.softmax(jnp.stack(lses), axis=0)
    o = jnp.sum(wts[..., None] * jnp.stack(outs), axis=0).reshape(bsz, T, ATT_OUT).astype(x.dtype)
    attn_branch = o @ lw['w_attn_proj']

    ga, gb = jnp.split(gates, 2, axis=-1)
    mixed = jax.nn.sigmoid(ga) * ssd_branch + jax.nn.sigmoid(gb) * attn_branch
    x = x + g1[:, None, :] * (mixed @ lw['w_out'])

    h2 = _modulate(_rmsnorm(x, lw['norm2_g']), sh2, sc2)
    x = x + g2[:, None, :] * _hier_moe(h2, lw)
    return x, conv_new, ssm_new, new_k, new_v


def setup_inputs(seed: int = 0) -> dict:
    key = jax.random.key(seed)
    ks = iter(jax.random.split(key, 48))
    D = D_MODEL
    H, hd = ATT_HEADS_PER_GROUP, ATT_HEAD_DIM
    lb = [min(w, PAST_LEN) for w, _ in ATT_PATTERNS]

    def nrm(shape, scale):
        return jax.random.normal(next(ks), shape, jnp.float32) * scale

    dt0 = jnp.exp(jax.random.uniform(next(ks), (DEPTH, SSD_HEADS), jnp.float32, math.log(1e-3), math.log(1e-1)))
    inputs = {
        'x_prompt': nrm((BATCH, SEQ, D), 1.0),
        'x_sample': nrm((DEC_BATCH, DEC_SEQ, D), 1.0),
        'c_prompt': nrm((BATCH, D), 1.0),
        'c_sample': nrm((DEC_BATCH, D), 1.0),
        'state_conv': nrm((DEPTH, DEC_BATCH, SSD_CONV - 1, SSD_CONV_DIM), 1.0),
        'state_ssm': nrm((DEPTH, DEC_BATCH, SSD_GROUPS, SSD_HPG, SSD_HEAD_DIM, SSD_STATE), 0.1),
        'cache_k_win128': nrm((DEPTH, DEC_BATCH, lb[0], H, hd), 1.0),
        'cache_v_win128': nrm((DEPTH, DEC_BATCH, lb[0], H, hd), 1.0),
        'cache_k_win512': nrm((DEPTH, DEC_BATCH, lb[1], H, hd), 1.0),
        'cache_v_win512': nrm((DEPTH, DEC_BATCH, lb[1], H, hd), 1.0),
        'cache_k_win2048': nrm((DEPTH, DEC_BATCH, lb[2], H, hd), 1.0),
        'cache_v_win2048': nrm((DEPTH, DEC_BATCH, lb[2], H, hd), 1.0),
        'norm1_g': 1.0 + nrm((DEPTH, D), 0.02),
        'w_ada': nrm((DEPTH, D, 6 * D), 0.5 * D ** -0.5),
        'b_ada': nrm((DEPTH, 6 * D), 0.02),
        'w_in': nrm((DEPTH, D, IN_WIDTH), D ** -0.5),
        'conv_w': nrm((DEPTH, SSD_CONV, SSD_CONV_DIM), SSD_CONV ** -0.5),
        'conv_b': nrm((DEPTH, SSD_CONV_DIM), 0.02),
        'dt_bias': dt0 + jnp.log(-jnp.expm1(-dt0)),
        'a_log': jnp.log(jax.random.uniform(next(ks), (DEPTH, SSD_HEADS), jnp.float32, 1.0, 16.0)),
        'd_skip': 1.0 + nrm((DEPTH, SSD_HEADS), 0.02),
        'ssd_norm_g': 1.0 + nrm((DEPTH, SSD_INNER), 0.02),
        'w_ssd_proj': nrm((DEPTH, SSD_INNER, D), SSD_INNER ** -0.5),
        'w_attn_proj': nrm((DEPTH, ATT_OUT, D), ATT_OUT ** -0.5),
        'w_out': nrm((DEPTH, D, D), D ** -0.5),
        'norm2_g': 1.0 + nrm((DEPTH, D), 0.02),
        'w_router_group': nrm((DEPTH, D, MOE_GROUPS), D ** -0.5),
        'b_router_group': nrm((DEPTH, MOE_GROUPS), 0.01),
        'w_router_expert': nrm((DEPTH, D, MOE_EXPERTS), D ** -0.5),
        'b_router_expert': nrm((DEPTH, MOE_EXPERTS), 0.01),
        'w_exp_gate': nrm((DEPTH, MOE_EXPERTS, D, MOE_FF), D ** -0.5),
        'w_exp_up': nrm((DEPTH, MOE_EXPERTS, D, MOE_FF), D ** -0.5),
        'w_exp_down': nrm((DEPTH, MOE_EXPERTS, MOE_FF, D), MOE_FF ** -0.5),
        'final_norm_g': 1.0 + nrm((D,), 0.02),
    }
    return inputs


def reference(x_prompt, x_sample, c_prompt, c_sample, state_conv, state_ssm,
              cache_k_win128, cache_v_win128, cache_k_win512, cache_v_win512, cache_k_win2048, cache_v_win2048,
              norm1_g, w_ada, b_ada, w_in, conv_w, conv_b, dt_bias, a_log, d_skip, ssd_norm_g,
              w_ssd_proj, w_attn_proj, w_out, norm2_g, w_router_group, b_router_group,
              w_router_expert, b_router_expert, w_exp_gate, w_exp_up, w_exp_down, final_norm_g):
    xp, xs = x_prompt, x_sample
    bp = x_prompt.shape[0]
    conv_p, conv_s, ssm_p, ssm_s = [], [], [], []
    kp = [[] for _ in ATT_PATTERNS]
    vp = [[] for _ in ATT_PATTERNS]
    ksm = [[] for _ in ATT_PATTERNS]
    vsm = [[] for _ in ATT_PATTERNS]
    for l in range(DEPTH):
        lw = dict(norm1_g=norm1_g[l], w_ada=w_ada[l], b_ada=b_ada[l], w_in=w_in[l], conv_w=conv_w[l],
                  conv_b=conv_b[l], dt_bias=dt_bias[l], a_log=a_log[l], d_skip=d_skip[l],
                  ssd_norm_g=ssd_norm_g[l], w_ssd_proj=w_ssd_proj[l], w_attn_proj=w_attn_proj[l],
                  w_out=w_out[l], norm2_g=norm2_g[l], w_router_group=w_router_group[l],
                  b_router_group=b_router_group[l], w_router_expert=w_router_expert[l],
                  b_router_expert=b_router_expert[l], w_exp_gate=w_exp_gate[l], w_exp_up=w_exp_up[l],
                  w_exp_down=w_exp_down[l])
        zero_conv = jnp.zeros((bp, SSD_CONV - 1, SSD_CONV_DIM), xp.dtype)
        zero_ssm = jnp.zeros((bp, SSD_GROUPS, SSD_HPG, SSD_HEAD_DIM, SSD_STATE), jnp.float32)
        xp, cnp, snp, nkp, nvp = _layer(xp, c_prompt, lw, zero_conv, zero_ssm, None, None)
        xs, cns, sns, nks, nvs = _layer(xs, c_sample, lw, state_conv[l], state_ssm[l],
                                        (cache_k_win128[l], cache_k_win512[l], cache_k_win2048[l]),
                                        (cache_v_win128[l], cache_v_win512[l], cache_v_win2048[l]))
        conv_p.append(cnp)
        conv_s.append(cns)
        ssm_p.append(snp)
        ssm_s.append(sns)
        for gi in range(ATT_N_GROUPS):
            kp[gi].append(nkp[gi])
            vp[gi].append(nvp[gi])
            ksm[gi].append(nks[gi])
            vsm[gi].append(nvs[gi])
    y_prompt = _rmsnorm(xp, final_norm_g)
    y_sample = _rmsnorm(xs, final_norm_g)
    st = jnp.stack
    return (y_prompt, y_sample, st(conv_p), st(conv_s), st(ssm_p), st(ssm_s),
            st(kp[0]), st(ksm[0]), st(vp[0]), st(vsm[0]),
            st(kp[1]), st(ksm[1]), st(vp[1]), st(vsm[1]),
            st(kp[2]), st(ksm[2]), st(vp[2]), st(vsm[2]))
```

```python
import functools
import math

import jax
import jax.numpy as jnp
from jax import lax
from jax.experimental import pallas as pl
from jax.experimental.pallas import tpu as pltpu

F32 = jnp.float32
BF16 = jnp.bfloat16

D_MODEL = 1024
SSD_INNER = 1024
SSD_HEAD_DIM = 64
SSD_HEADS = 16
SSD_GROUPS = 2
SSD_HPG = 8
SSD_STATE = 128
SSD_CONV = 4
SSD_CHUNK = 128
SSD_CONV_DIM = SSD_INNER + 2 * SSD_GROUPS * SSD_STATE
ATT_PATTERNS = ((128, 1), (512, 4), (2048, 16))
ATT_GROUPS = 3
ATT_HEADS = 8
ATT_HEAD_DIM = 64
ATT_BLOCK = 128
ATT_OUT = ATT_HEADS * ATT_HEAD_DIM
MOE_GROUPS = 4
MOE_PER_GROUP = 4
MOE_EXPERTS = 16
MOE_FF = 512
IN_QKV = 3 * ATT_GROUPS * ATT_OUT
IN_GATE = 2 * D_MODEL
OFF_XBC = SSD_INNER
OFF_DT = OFF_XBC + SSD_CONV_DIM
OFF_QKV = OFF_DT + SSD_HEADS
OFF_GATE = OFF_QKV + IN_QKV
IN_WIDTH = OFF_GATE + IN_GATE
EPS = 1e-6

LANES = 128
VMEM_LIMIT = 56 * 1024 * 1024


def _cparams(sem, vmem=VMEM_LIMIT):
    return pltpu.CompilerParams(dimension_semantics=sem, vmem_limit_bytes=vmem)


def _split2(a):
    hi = a.astype(BF16)
    lo = (a - hi.astype(F32)).astype(BF16)
    return hi, lo


def _split3(a):
    hi = a.astype(BF16)
    r = a - hi.astype(F32)
    mid = r.astype(BF16)
    lo = (r - mid.astype(F32)).astype(BF16)
    return hi, mid, lo


def _dot(a, b):
    return jnp.dot(a, b, preferred_element_type=F32)


def _dot_nt(a, b):
    return lax.dot_general(a, b, (((1,), (1,)), ((), ())), preferred_element_type=F32)


def _mm(a, w, passes):
    if passes == 1:
        return _dot(a.astype(BF16), w.astype(BF16))
    a = a.astype(F32)
    w = w.astype(F32)
    a_hi, a_lo = _split2(a)
    w_hi, w_lo = _split2(w)
    return _dot(a_hi, w_hi) + (_dot(a_lo, w_hi) + _dot(a_hi, w_lo))


def _mm01(a, e01, terms):
    parts = _split3(a) if terms == 3 else _split2(a)
    out = _dot(parts[0], e01)
    for p in parts[1:]:
        out = out + _dot(p, e01)
    return out


def _sigmoid(x):
    return 1.0 / (1.0 + jnp.exp(-x))


def _silu(x):
    return x * _sigmoid(x)


def _softplus(x):
    return jnp.maximum(x, 0.0) + jnp.log(1.0 + jnp.exp(-jnp.abs(x)))


def _head_expand_matrix(rows, n_heads, width):
    r = lax.broadcasted_iota(jnp.int32, (rows, n_heads * width), 0)
    c = lax.broadcasted_iota(jnp.int32, (rows, n_heads * width), 1)
    return jnp.where((c // width) == r, 1.0, 0.0).astype(BF16)


def _rms_modulate(x, g, sh, sc):
    ms = jnp.mean(x * x, axis=-1, keepdims=True)
    y = x * lax.rsqrt(ms + EPS) * g
    return y * (1.0 + sc) + sh


def _mod_kernel(c_ref, w_ref, b_ref, o_ref):
    a = _silu(c_ref[...])
    o_ref[...] = _mm(a, w_ref[...], 3) + b_ref[...]


def _modulation(c_all, w_ada, b_ada):
    depth, d, n6 = w_ada.shape
    rows = c_all.shape[0]
    tn = 1024
    return pl.pallas_call(
        _mod_kernel,
        out_shape=jax.ShapeDtypeStruct((depth, rows, n6), F32),
        grid=(depth, n6 // tn),
        in_specs=[
            pl.BlockSpec((rows, d), lambda l, j: (0, 0)),
            pl.BlockSpec((None, d, tn), lambda l, j: (l, 0, j)),
            pl.BlockSpec((None, 1, tn), lambda l, j: (l, 0, j)),
        ],
        out_specs=pl.BlockSpec((None, rows, tn), lambda l, j: (l, 0, j)),
        compiler_params=_cparams(("arbitrary", "arbitrary")),
        name="adaln_mod",
    )(c_all, w_ada, b_ada.reshape(depth, 1, n6))


_IN_CHUNK = 512


def _in_kernel(x_ref, mod_ref, g_ref, wz_ref, wx_ref, wd_ref, wq_ref, wg_ref,
               z_ref, xbc_ref, dt_ref, qkv_ref, gates_ref):
    sh = mod_ref[:, 0:D_MODEL]
    sc = mod_ref[:, D_MODEL:2 * D_MODEL]
    h = _rms_modulate(x_ref[...], g_ref[...], sh, sc).astype(BF16)
    for w_ref, o_ref in ((wz_ref, z_ref), (wx_ref, xbc_ref), (wq_ref, qkv_ref), (wg_ref, gates_ref)):
        width = o_ref.shape[-1]
        for c0 in range(0, width, _IN_CHUNK):
            o_ref[:, c0:c0 + _IN_CHUNK] = _dot(h, w_ref[:, c0:c0 + _IN_CHUNK]).astype(o_ref.dtype)
    dt_ref[...] = _dot(h, wd_ref[...])


def _in_proj_prompt(x, mod, norm_g, wz, wx, wd, wq, wg, tokens_per_seq, tm=512):
    n, d = x.shape
    tiles_per_seq = tokens_per_seq // tm

    def const(shape):
        return pl.BlockSpec(shape, lambda i: (0, 0), pipeline_mode=pl.Buffered(1))

    return pl.pallas_call(
        _in_kernel,
        out_shape=(
            jax.ShapeDtypeStruct((n, SSD_INNER), BF16),
            jax.ShapeDtypeStruct((n, SSD_CONV_DIM), F32),
            jax.ShapeDtypeStruct((n, LANES), F32),
            jax.ShapeDtypeStruct((n, IN_QKV), BF16),
            jax.ShapeDtypeStruct((n, IN_GATE), BF16),
        ),
        grid=(n // tm,),
        in_specs=[
            pl.BlockSpec((tm, d), lambda i: (i, 0)),
            pl.BlockSpec((None, 1, 2 * d), lambda i: (i // tiles_per_seq, 0, 0)),
            const((1, d)),
            const(wz.shape), const(wx.shape), const(wd.shape), const(wq.shape), const(wg.shape),
        ],
        out_specs=(
            pl.BlockSpec((tm, SSD_INNER), lambda i: (i, 0)),
            pl.BlockSpec((tm, SSD_CONV_DIM), lambda i: (i, 0)),
            pl.BlockSpec((tm, LANES), lambda i: (i, 0)),
            pl.BlockSpec((tm, IN_QKV), lambda i: (i, 0)),
            pl.BlockSpec((tm, IN_GATE), lambda i: (i, 0)),
        ),
        compiler_params=_cparams(("arbitrary",)),
        name="in_proj_prompt",
    )(x, mod, norm_g, wz, wx, wd, wq, wg)


def _in_small_kernel(x_ref, mod_ref, g_ref, w_ref, o_ref):
    sh = mod_ref[:, 0:D_MODEL]
    sc = mod_ref[:, D_MODEL:2 * D_MODEL]
    h = _rms_modulate(x_ref[...], g_ref[...], sh, sc)
    o_ref[...] = _mm(h, w_ref[...], 3)


def _in_proj_sample(x, mod, norm_g, w_in_l, tn=1024):
    m, d = x.shape
    width = w_in_l.shape[1]
    return pl.pallas_call(
        _in_small_kernel,
        out_shape=jax.ShapeDtypeStruct((m, width), F32),
        grid=(pl.cdiv(width, tn),),
        in_specs=[
            pl.BlockSpec((m, d), lambda j: (0, 0)),
            pl.BlockSpec((m, 2 * d), lambda j: (0, 0)),
            pl.BlockSpec((1, d), lambda j: (0, 0)),
            pl.BlockSpec((d, tn), lambda j: (0, j)),
        ],
        out_specs=pl.BlockSpec((m, tn), lambda j: (0, j)),
        compiler_params=_cparams(("arbitrary",)),
        name="in_proj_sample",
    )(x, mod, norm_g, w_in_l)


def _ssd_kernel(xbc_ref, z_ref, dt_ref, cw_ref, cb_ref, dtb_ref, alog_ref, dskip_ref, ng_ref,
                yn_ref, hout_ref, h_scr, xp_scr):
    q = SSD_CHUNK
    c = pl.program_id(1)

    @pl.when(c == 0)
    def _():
        h_scr[...] = jnp.zeros_like(h_scr)
        xp_scr[0:8, :] = jnp.zeros((8, SSD_CONV_DIM), F32)

    xp_scr[8:8 + q, :] = xbc_ref[...]
    acc = cb_ref[...] + cw_ref[3:4, :] * xp_scr[8:8 + q, :]
    for k in range(SSD_CONV - 1):
        acc = acc + cw_ref[k:k + 1, :] * xp_scr[5 + k:5 + k + q, :]
    xp_scr[0:8, :] = xp_scr[q:q + 8, :]
    xc = _silu(acc)
    xs = xc[:, 0:SSD_INNER]
    xs_bf = xs.astype(BF16)

    lane_q = lax.broadcasted_iota(jnp.int32, (q, LANES), 1)
    row_q = lax.broadcasted_iota(jnp.int32, (q, LANES), 0)
    causal = row_q >= lane_q
    tri = jnp.where(causal, 1.0, 0.0).astype(BF16)
    tri_t = jnp.where(lane_q >= row_q, 1.0, 0.0).astype(BF16)
    e_heads = _head_expand_matrix(LANES, SSD_HEADS, SSD_HEAD_DIM)

    dt = _softplus(dt_ref[...] + dtb_ref[...])
    dt = jnp.where(lane_q < SSD_HEADS, dt, 0.0)
    a = -jnp.exp(alog_ref[...])
    d_a = dt * a
    cum = _mm01_left(tri, d_a)
    d_a_t = d_a.T
    dt_t = dt.T
    cum_t = _mm01(d_a_t, tri_t, 3)
    cum_last = cum[q - 1:q, :]
    exp_cum = jnp.exp(cum)
    dec_end = jnp.exp(cum_last - cum)
    stack = jnp.concatenate([exp_cum, dec_end * dt], axis=0)
    full = _mm01(stack, e_heads, 2)
    exp_cum_full = full[0:q]
    w_full = full[q:2 * q]
    chunk_dec_t = jnp.exp(cum_t[:, q - 1:q])

    lane_half = lax.broadcasted_iota(jnp.int32, (q, LANES), 1) < SSD_HEAD_DIM
    y_parts = []
    for g in range(SSD_GROUPS):
        b_off = SSD_INNER + g * SSD_STATE
        c_off = SSD_INNER + SSD_GROUPS * SSD_STATE + g * SSD_STATE
        bm = xc[:, b_off:b_off + SSD_STATE].astype(BF16)
        cm = xc[:, c_off:c_off + SSD_STATE].astype(BF16)
        cbm = _dot_nt(cm, bm)
        gw = SSD_HPG * SSD_HEAD_DIM
        g0 = g * gw
        yd = []
        for pair in range(SSD_HPG // 2):
            x_pair = xs_bf[:, g0 + pair * LANES:g0 + (pair + 1) * LANES]
            halves = []
            for hh in range(2):
                h = g * SSD_HPG + pair * 2 + hh
                seg = cum[:, h:h + 1] - cum_t[h:h + 1, :]
                dec = jnp.where(causal, jnp.exp(jnp.where(causal, seg, 0.0)), 0.0)
                m_h = (cbm * dec * dt_t[h:h + 1, :]).astype(BF16)
                halves.append(_dot(m_h, x_pair))
            yd.append(jnp.where(lane_half, halves[0], halves[1]))
        y_diag = jnp.concatenate(yd, axis=1)
        h_g = h_scr[g0:g0 + gw, :]
        y_off = _dot_nt(cm, h_g.astype(BF16)) * exp_cum_full[:, g0:g0 + gw]
        y_parts.append(y_diag + y_off)
        xw = (xs[:, g0:g0 + gw] * w_full[:, g0:g0 + gw])
        st = _dot(xw.T.astype(BF16), bm)
        for e in range(SSD_HPG):
            h = g * SSD_HPG + e
            r0 = g0 + e * SSD_HEAD_DIM
            h_scr[r0:r0 + SSD_HEAD_DIM, :] = (h_scr[r0:r0 + SSD_HEAD_DIM, :] * chunk_dec_t[h:h + 1, :]
                                              + st[e * SSD_HEAD_DIM:(e + 1) * SSD_HEAD_DIM, :])

    y = jnp.concatenate(y_parts, axis=1) + dskip_ref[...] * xs
    y = y * _silu(z_ref[...].astype(F32))
    outs = []
    for g in range(SSD_GROUPS):
        gw = SSD_HPG * SSD_HEAD_DIM
        yg = y[:, g * gw:(g + 1) * gw]
        ms = jnp.mean(yg * yg, axis=-1, keepdims=True)
        outs.append(yg * lax.rsqrt(ms + EPS) * ng_ref[:, g * gw:(g + 1) * gw])
    yn_ref[...] = jnp.concatenate(outs, axis=1).astype(yn_ref.dtype)

    @pl.when(c == pl.num_programs(1) - 1)
    def _():
        hout_ref[...] = h_scr[...]


def _mm01_left(tri01, a):
    hi, mid, lo = _split3(a)
    return _dot(tri01, hi) + (_dot(tri01, mid) + _dot(tri01, lo))


def _ssd_prompt(xbc, z, dt_raw, conv_w, conv_b, dt_bias, a_log, d_skip_full, ssd_norm_g, bsz, seq):
    q = SSD_CHUNK
    nc = seq // q
    row = lambda b, c: (b * nc + c, 0)
    const = lambda b, c: (0, 0)
    return pl.pallas_call(
        _ssd_kernel,
        out_shape=(
            jax.ShapeDtypeStruct((bsz * seq, SSD_INNER), BF16),
            jax.ShapeDtypeStruct((bsz, SSD_INNER, SSD_STATE), F32),
        ),
        grid=(bsz, nc),
        in_specs=[
            pl.BlockSpec((q, SSD_CONV_DIM), row),
            pl.BlockSpec((q, SSD_INNER), row),
            pl.BlockSpec((q, LANES), row),
            pl.BlockSpec((SSD_CONV, SSD_CONV_DIM), const),
            pl.BlockSpec((1, SSD_CONV_DIM), const),
            pl.BlockSpec((1, LANES), const),
            pl.BlockSpec((1, LANES), const),
            pl.BlockSpec((1, SSD_INNER), const),
            pl.BlockSpec((1, SSD_INNER), const),
        ],
        out_specs=(
            pl.BlockSpec((q, SSD_INNER), row),
            pl.BlockSpec((None, SSD_INNER, SSD_STATE), lambda b, c: (b, 0, 0)),
        ),
        scratch_shapes=[
            pltpu.VMEM((SSD_INNER, SSD_STATE), F32),
            pltpu.VMEM((q + 8, SSD_CONV_DIM), F32),
        ],
        compiler_params=_cparams(("arbitrary", "arbitrary")),
        name="ssd_prompt",
    )(xbc, z, dt_raw, conv_w, conv_b, dt_bias, a_log, d_skip_full, ssd_norm_g)


def _attn_kernel(q_ref, kc_ref, kp_ref, vc_ref, vp_ref, o_ref, lse_ref, kwin, vwin, *, tq, band):
    blk = ATT_BLOCK
    j = pl.program_id(2)
    kwin[0:blk, :] = kp_ref[...]
    kwin[blk:blk + tq, :] = kc_ref[...]
    vwin[0:blk, :] = vp_ref[...]
    vwin[blk:blk + tq, :] = vc_ref[...]

    qi = lax.broadcasted_iota(jnp.int32, (blk, 2 * blk), 0)
    ki = lax.broadcasted_iota(jnp.int32, (blk, 2 * blk), 1)
    dist = qi + blk - ki
    in_band = (dist >= 0) & (dist <= band)
    lane = lax.broadcasted_iota(jnp.int32, (blk, LANES), 1)
    lane_half = lane < ATT_HEAD_DIM
    scale = ATT_HEAD_DIM ** -0.5
    zero_bf = jnp.zeros((blk, LANES), BF16)

    for i in range(tq // blk):
        if i == 0:
            valid = in_band & ((ki >= blk) | (j > 0))
        else:
            valid = in_band
        lse_tile = jnp.zeros((blk, LANES), F32)
        for hp in range(ATT_HEADS // 2):
            c0 = hp * LANES
            q_pair = q_ref[i * blk:(i + 1) * blk, c0:c0 + LANES]
            k_pair = kwin[i * blk:(i + 2) * blk, c0:c0 + LANES]
            v_pair = vwin[i * blk:(i + 2) * blk, c0:c0 + LANES]
            halves = []
            for hh in range(2):
                q_m = jnp.where(lane_half if hh == 0 else jnp.logical_not(lane_half), q_pair, zero_bf)
                s = _dot_nt(q_m, k_pair) * scale
                s = jnp.where(valid, s, -jnp.inf)
                m = jnp.max(s, axis=-1, keepdims=True)
                e = jnp.exp(s - m)
                den = jnp.sum(e, axis=-1, keepdims=True)
                pv = _dot(e.astype(BF16), v_pair)
                halves.append(pv / den)
                lse = m + jnp.log(den)
                head = hp * 2 + hh
                lse_tile = jnp.where((lane == head) | (lane == ATT_HEADS + head), lse, lse_tile)
            o_ref[i * blk:(i + 1) * blk, c0:c0 + LANES] = jnp.where(lane_half, halves[0], halves[1]).astype(o_ref.dtype)
        lse_ref[i * blk:(i + 1) * blk, :] = lse_tile


def _attn_prompt(qkv, gi, bsz, seq):
    win, dil = ATT_PATTERNS[gi]
    band = win // dil
    blk = ATT_BLOCK
    length = seq // dil
    tq = min(512, length)
    nj = length // tq
    sub = tq // blk
    n_qkv = 3 * ATT_GROUPS
    qkv3 = qkv.reshape(bsz, length, dil * IN_QKV)
    col = lambda which: (lambda b, r, j: (b, j, r * n_qkv + which * ATT_GROUPS + gi))
    col_prev = lambda which: (lambda b, r, j: (b, jnp.maximum(j * sub - 1, 0), r * n_qkv + which * ATT_GROUPS + gi))
    o, lse = pl.pallas_call(
        functools.partial(_attn_kernel, tq=tq, band=band),
        out_shape=(
            jax.ShapeDtypeStruct((bsz, length, dil * ATT_OUT), BF16),
            jax.ShapeDtypeStruct((bsz, length, dil * LANES), F32),
        ),
        grid=(bsz, dil, nj),
        in_specs=[
            pl.BlockSpec((None, tq, ATT_OUT), col(0)),
            pl.BlockSpec((None, tq, ATT_OUT), col(1)),
            pl.BlockSpec((None, blk, ATT_OUT), col_prev(1)),
            pl.BlockSpec((None, tq, ATT_OUT), col(2)),
            pl.BlockSpec((None, blk, ATT_OUT), col_prev(2)),
        ],
        out_specs=(
            pl.BlockSpec((None, tq, ATT_OUT), lambda b, r, j: (b, j, r)),
            pl.BlockSpec((None, tq, LANES), lambda b, r, j: (b, j, r)),
        ),
        scratch_shapes=[
            pltpu.VMEM((blk + tq, ATT_OUT), BF16),
            pltpu.VMEM((blk + tq, ATT_OUT), BF16),
        ],
        compiler_params=_cparams(("arbitrary", "arbitrary", "arbitrary")),
        name=f"attn_prompt_w{win}",
    )(qkv3, qkv3, qkv3, qkv3, qkv3)
    return o.reshape(bsz * seq, ATT_OUT), lse.reshape(bsz * seq, LANES)


def _router_gates(logits):
    shape = logits.shape
    lane = lax.broadcasted_iota(jnp.int32, shape, 1)
    big = jnp.int32(1 << 20)
    neg = -jnp.inf
    is_grp = (lane >= MOE_EXPERTS) & (lane < MOE_EXPERTS + MOE_GROUPS)
    lg = jnp.where(is_grp, logits, neg)
    gm = jnp.max(lg, axis=-1, keepdims=True)
    g_lane = jnp.min(jnp.where(lg == gm, lane, big), axis=-1, keepdims=True)
    g_sum = jnp.sum(jnp.exp(lg - gm), axis=-1, keepdims=True)
    g_w = 1.0 / g_sum
    lo = (g_lane - MOE_EXPERTS) * MOE_PER_GROUP
    in_grp = (lane >= lo) & (lane < lo + MOE_PER_GROUP)
    le = jnp.where(in_grp, logits, neg)
    m1 = jnp.max(le, axis=-1, keepdims=True)
    i1 = jnp.min(jnp.where(le == m1, lane, big), axis=-1, keepdims=True)
    le2 = jnp.where(lane == i1, neg, le)
    m2 = jnp.max(le2, axis=-1, keepdims=True)
    i2 = jnp.min(jnp.where(le2 == m2, lane, big), axis=-1, keepdims=True)
    t = jnp.exp(m2 - m1)
    w1 = 1.0 / (1.0 + t)
    w2 = t / (1.0 + t)
    return jnp.where(lane == i1, g_w * w1, jnp.where(lane == i2, g_w * w2, 0.0))


def _post_kernel(*refs, passes, merge_attn):
    if merge_attn:
        (x_ref, yn_ref, o0_ref, o1_ref, o2_ref, l0_ref, l1_ref, l2_ref, gates_ref, mod_ref, n2_ref,
         wssd_ref, wattn_ref, wout_ref, wr_ref, br_ref, x1_ref, h2_ref, gate_ref) = refs
        l0, l1, l2 = l0_ref[...], l1_ref[...], l2_ref[...]
        mx = jnp.maximum(jnp.maximum(l0, l1), l2)
        e0, e1, e2 = jnp.exp(l0 - mx), jnp.exp(l1 - mx), jnp.exp(l2 - mx)
        inv = 1.0 / (e0 + e1 + e2)
        lane = lax.broadcasted_iota(jnp.int32, l0.shape, 1)
        e8 = _head_expand_matrix(LANES, ATT_HEADS, ATT_HEAD_DIM)
        r = lax.broadcasted_iota(jnp.int32, e8.shape, 0)
        c = lax.broadcasted_iota(jnp.int32, e8.shape, 1)
        e8 = jnp.where((c // ATT_HEAD_DIM) == (r - ATT_HEADS), 1.0, e8.astype(F32)).astype(BF16)
        o = None
        for e_g, o_ref in ((e0, o0_ref), (e1, o1_ref), (e2, o2_ref)):
            w = e_g * inv
            hi, lo = _split2(w)
            w_exp = _dot(jnp.where(lane < ATT_HEADS, hi, lo), e8)
            term = w_exp * o_ref[...].astype(F32)
            o = term if o is None else o + term
    else:
        (x_ref, yn_ref, o_ref, gates_ref, mod_ref, n2_ref,
         wssd_ref, wattn_ref, wout_ref, wr_ref, br_ref, x1_ref, h2_ref, gate_ref) = refs
        o = o_ref[...]
    d = D_MODEL
    g1 = mod_ref[:, 0:d]
    sh2 = mod_ref[:, d:2 * d]
    sc2 = mod_ref[:, 2 * d:3 * d]
    ssd_branch = _mm(yn_ref[...], wssd_ref[...], passes)
    attn_branch = _mm(o, wattn_ref[...], passes)
    ga = gates_ref[:, 0:d].astype(F32)
    gb = gates_ref[:, d:2 * d].astype(F32)
    mixed = _sigmoid(ga) * ssd_branch + _sigmoid(gb) * attn_branch
    x1 = x_ref[...] + g1 * _mm(mixed, wout_ref[...], passes)
    x1_ref[...] = x1
    h2 = _rms_modulate(x1, n2_ref[...], sh2, sc2)
    h2_ref[...] = h2.astype(h2_ref.dtype)
    logits = _mm(h2, wr_ref[...], 3) + br_ref[...]
    gate_ref[...] = _router_gates(logits)


def _post(x, yn, attn_in, gates, mod, norm2_g, wssd, wattn, wout, wr, br, *, passes, merge_attn,
          tm, rows_per_mod, h2_dtype):
    n, d = x.shape
    per_row_mod = rows_per_mod == 1
    if per_row_mod:
        mod_spec = pl.BlockSpec((tm, 3 * d), lambda i: (i, 0))
    else:
        tiles = rows_per_mod // tm
        mod_spec = pl.BlockSpec((None, 1, 3 * d), lambda i: (i // tiles, 0, 0))
    row = lambda w: pl.BlockSpec((tm, w), lambda i: (i, 0))
    const = lambda a: pl.BlockSpec(a.shape, lambda i: (0, 0), pipeline_mode=pl.Buffered(1))
    if merge_attn:
        o_list, lse_list = attn_in
        attn_args = list(o_list) + list(lse_list)
        attn_specs = [row(ATT_OUT)] * 3 + [row(LANES)] * 3
    else:
        attn_args = [attn_in]
        attn_specs = [row(ATT_OUT)]
    return pl.pallas_call(
        functools.partial(_post_kernel, passes=passes, merge_attn=merge_attn),
        out_shape=(
            jax.ShapeDtypeStruct((n, d), F32),
            jax.ShapeDtypeStruct((n, d), h2_dtype),
            jax.ShapeDtypeStruct((n, LANES), F32),
        ),
        grid=(n // tm,),
        in_specs=[row(d), row(SSD_INNER)] + attn_specs + [row(IN_GATE), mod_spec, const(norm2_g),
                                                          const(wssd), const(wattn), const(wout),
                                                          const(wr), const(br)],
        out_specs=(row(d), row(d), row(LANES)),
        compiler_params=_cparams(("arbitrary",)),
        name="post_merge" if merge_attn else "post_sample",
    )(x, yn, *attn_args, gates, mod, norm2_g, wssd, wattn, wout, wr, br)


def _moe_kernel(h2_ref, gate_ref, x1_ref, g2_ref, wg_ref, wu_ref, wd_ref, x2_ref, acc_ref, *, passes):
    e = pl.program_id(1)

    @pl.when(e == 0)
    def _():
        acc_ref[...] = jnp.zeros_like(acc_ref)

    h2 = h2_ref[...]
    hg = _mm(h2, wg_ref[...], passes)
    hu = _mm(h2, wu_ref[...], passes)
    gate = gate_ref[...]
    lane = lax.broadcasted_iota(jnp.int32, gate.shape, 1)
    gcol = jnp.sum(jnp.where(lane == e, gate, 0.0), axis=-1, keepdims=True)
    act = _silu(hg) * hu * gcol
    acc_ref[...] += _mm(act, wd_ref[...], passes)

    @pl.when(e == pl.num_programs(1) - 1)
    def _():
        x2_ref[...] = x1_ref[...] + g2_ref[...] * acc_ref[...]


def _moe(h2, gate, x1, g2, wg, wu, wd, *, passes, tm, rows_per_mod):
    n, d = x1.shape
    if rows_per_mod == 1:
        g2_spec = pl.BlockSpec((tm, d), lambda i, e: (i, 0))
    else:
        tiles = rows_per_mod // tm
        g2_spec = pl.BlockSpec((None, 1, d), lambda i, e: (i // tiles, 0, 0))
    row = lambda w: pl.BlockSpec((tm, w), lambda i, e: (i, 0))
    return pl.pallas_call(
        functools.partial(_moe_kernel, passes=passes),
        out_shape=jax.ShapeDtypeStruct((n, d), F32),
        grid=(n // tm, MOE_EXPERTS),
        in_specs=[
            row(d), row(LANES), row(d), g2_spec,
            pl.BlockSpec((None, d, MOE_FF), lambda i, e: (e, 0, 0)),
            pl.BlockSpec((None, d, MOE_FF), lambda i, e: (e, 0, 0)),
            pl.BlockSpec((None, MOE_FF, d), lambda i, e: (e, 0, 0)),
        ],
        out_specs=row(d),
        scratch_shapes=[pltpu.VMEM((tm, d), F32)],
        compiler_params=_cparams(("arbitrary", "arbitrary")),
        name="moe_dense",
    )(h2, gate, x1, g2, wg, wu, wd)


def _final_kernel(x_ref, g_ref, o_ref):
    x = x_ref[...]
    ms = jnp.mean(x * x, axis=-1, keepdims=True)
    o_ref[...] = x * lax.rsqrt(ms + EPS) * g_ref[...]


def _final_norm(x, g, tm):
    n, d = x.shape
    return pl.pallas_call(
        _final_kernel,
        out_shape=jax.ShapeDtypeStruct((n, d), F32),
        grid=(n // tm,),
        in_specs=[pl.BlockSpec((tm, d), lambda i: (i, 0)), pl.BlockSpec((1, d), lambda i: (0, 0))],
        out_specs=pl.BlockSpec((tm, d), lambda i: (i, 0)),
        compiler_params=_cparams(("arbitrary",)),
        name="final_norm",
    )(x, g)


def _step_kernel(z_ref, xbc_ref, dt_ref, qkv_ref, cst_ref, h_ref, k0_ref, v0_ref, k1_ref, v1_ref, k2_ref, v2_ref,
                 cw_ref, cb_ref, dtb_ref, alog_ref, dskip_ref, ng_ref,
                 yn_ref, o_ref, cnew_ref, hnew_ref, col_scr):
    x_new = xbc_ref[...]
    acc = cb_ref[...] + cw_ref[3:4, :] * x_new
    for k in range(SSD_CONV - 1):
        acc = acc + cw_ref[k:k + 1, :] * cst_ref[k:k + 1, :]
    cnew_ref[0:1, :] = cst_ref[1:2, :]
    cnew_ref[1:2, :] = cst_ref[2:3, :]
    cnew_ref[2:3, :] = x_new
    xc = _silu(acc)
    xs = xc[:, 0:SSD_INNER]

    e_heads = _head_expand_matrix(LANES, SSD_HEADS, SSD_HEAD_DIM)
    dt_raw8 = jnp.broadcast_to(dt_ref[...], (8, LANES))
    dt_full = _softplus(_mm01(dt_raw8, e_heads, 3)[0:1, :] + dtb_ref[...])
    a_full = -jnp.exp(alog_ref[...])
    dec_full = jnp.exp(dt_full * a_full)
    xdt = xs * dt_full
    col_scr[...] = jnp.zeros_like(col_scr)
    col_scr[0:1, :] = xdt
    col_scr[1:2, :] = dec_full
    cols = col_scr[...].T
    y_parts = []
    gw = SSD_HPG * SSD_HEAD_DIM
    for g in range(SSD_GROUPS):
        b_off = SSD_INNER + g * SSD_STATE
        c_off = SSD_INNER + SSD_GROUPS * SSD_STATE + g * SSD_STATE
        bm = xc[:, b_off:b_off + SSD_STATE]
        cm = xc[:, c_off:c_off + SSD_STATE]
        g0 = g * gw
        hn = h_ref[g0:g0 + gw, :] * cols[g0:g0 + gw, 1:2] + cols[g0:g0 + gw, 0:1] * bm
        hnew_ref[g0:g0 + gw, :] = hn
        t = (hn * cm).T
        y_parts.append(jnp.sum(t, axis=0, keepdims=True))
    y = jnp.concatenate(y_parts, axis=1) + dskip_ref[...] * xs
    y = y * _silu(z_ref[...])
    outs = []
    for g in range(SSD_GROUPS):
        yg = y[:, g * gw:(g + 1) * gw]
        ms = jnp.mean(yg * yg, axis=-1, keepdims=True)
        outs.append(yg * lax.rsqrt(ms + EPS) * ng_ref[:, g * gw:(g + 1) * gw])
    yn_ref[...] = jnp.concatenate(outs, axis=1)

    scale = ATT_HEAD_DIM ** -0.5
    o_g, lse_g = [], []
    for gi, (k_ref, v_ref) in enumerate(((k0_ref, v0_ref), (k1_ref, v1_ref), (k2_ref, v2_ref))):
        q = qkv_ref[gi]
        k_new = qkv_ref[ATT_GROUPS + gi]
        v_new = qkv_ref[2 * ATT_GROUPS + gi]
        kk = k_ref[...]
        vv = v_ref[...]
        s = jnp.sum(kk * q[None], axis=-1, keepdims=True) * scale
        s_new = jnp.sum(k_new * q, axis=-1, keepdims=True) * scale
        m = jnp.maximum(jnp.max(s, axis=0), s_new)
        e = jnp.exp(s - m[None])
        e_new = jnp.exp(s_new - m)
        den = jnp.sum(e, axis=0) + e_new
        acc_o = jnp.sum(e * vv, axis=0) + e_new * v_new
        o_g.append(acc_o / den)
        lse_g.append(m + jnp.log(den))
    mx = jnp.maximum(jnp.maximum(lse_g[0], lse_g[1]), lse_g[2])
    w = [jnp.exp(l - mx) for l in lse_g]
    tot = w[0] + w[1] + w[2]
    o_ref[...] = (w[0] / tot) * o_g[0] + (w[1] / tot) * o_g[1] + (w[2] / tot) * o_g[2]


def _step_sample(layer, z, xbc, dt_raw, qkv, state_conv, state_ssm, caches,
                 conv_w, conv_b, dtb_full, alog_full, dskip_full, ssd_norm_g):
    bsz = z.shape[0]
    blk = ATT_BLOCK
    cache_args, cache_specs = [], []
    for gi, (win, dil) in enumerate(ATT_PATTERNS):
        for c in caches[gi]:
            depth, _, lb, hh, hd = c.shape
            assert lb == win and lb // dil == blk
            cache_args.append(c.reshape(depth, bsz, lb // dil, dil, hh, hd))
            cache_specs.append(pl.BlockSpec((None, None, blk, None, hh, hd),
                                            lambda b: (layer, b, 0, 0, 0, 0)))
    row3 = lambda w: pl.BlockSpec((None, 1, w), lambda b: (b, 0, 0))
    const = lambda a: pl.BlockSpec(a.shape, lambda b: (0,) * a.ndim)
    return pl.pallas_call(
        _step_kernel,
        out_shape=(
            jax.ShapeDtypeStruct((bsz, 1, SSD_INNER), F32),
            jax.ShapeDtypeStruct((bsz, ATT_HEADS, ATT_HEAD_DIM), F32),
            jax.ShapeDtypeStruct((bsz, SSD_CONV - 1, SSD_CONV_DIM), F32),
            jax.ShapeDtypeStruct((bsz, SSD_INNER, SSD_STATE), F32),
        ),
        grid=(bsz,),
        in_specs=[
            row3(SSD_INNER), row3(SSD_CONV_DIM), row3(LANES),
            pl.BlockSpec((None, 3 * ATT_GROUPS, ATT_HEADS, ATT_HEAD_DIM), lambda b: (b, 0, 0, 0)),
            pl.BlockSpec((None, None, SSD_CONV - 1, SSD_CONV_DIM), lambda b: (layer, b, 0, 0)),
            pl.BlockSpec((None, None, SSD_INNER, SSD_STATE), lambda b: (layer, b, 0, 0)),
        ] + cache_specs + [const(conv_w), const(conv_b), const(dtb_full), const(alog_full),
                           const(dskip_full), const(ssd_norm_g)],
        out_specs=(
            row3(SSD_INNER),
            pl.BlockSpec((None, ATT_HEADS, ATT_HEAD_DIM), lambda b: (b, 0, 0)),
            pl.BlockSpec((None, SSD_CONV - 1, SSD_CONV_DIM), lambda b: (b, 0, 0)),
            pl.BlockSpec((None, SSD_INNER, SSD_STATE), lambda b: (b, 0, 0)),
        ),
        scratch_shapes=[pltpu.VMEM((LANES, SSD_INNER), F32)],
        compiler_params=_cparams(("arbitrary",)),
        name="step_sample",
    )(z.reshape(bsz, 1, -1), xbc.reshape(bsz, 1, -1), dt_raw.reshape(bsz, 1, -1),
      qkv.reshape(bsz, 3 * ATT_GROUPS, ATT_HEADS, ATT_HEAD_DIM), state_conv, state_ssm, *cache_args,
      conv_w, conv_b, dtb_full, alog_full, dskip_full, ssd_norm_g)


def _shift_kernel(*refs, n_cache, depth):
    caches = refs[0:n_cache]
    news = refs[n_cache:2 * n_cache]
    outs = refs[2 * n_cache:3 * n_cache]
    sem = refs[3 * n_cache]
    copies = []
    for i in range(n_cache):
        lb = caches[i].shape[2]
        for l in range(depth):
            copies.append(pltpu.make_async_copy(caches[i].at[l, :, pl.ds(1, lb - 1)],
                                                outs[i].at[l, :, pl.ds(0, lb - 1)], sem.at[0, i, l]))
            copies.append(pltpu.make_async_copy(news[i].at[l], outs[i].at[l, :, pl.ds(lb - 1, 1)],
                                                sem.at[1, i, l]))
    for cp in copies:
        cp.start()
    for cp in copies:
        cp.wait()


def _shift_caches(caches, news):
    n_cache = len(caches)
    depth = caches[0].shape[0]
    any_spec = pl.BlockSpec(memory_space=pl.ANY)
    return pl.pallas_call(
        functools.partial(_shift_kernel, n_cache=n_cache, depth=depth),
        out_shape=tuple(jax.ShapeDtypeStruct(c.shape, c.dtype) for c in caches),
        in_specs=[any_spec] * (2 * n_cache),
        out_specs=tuple([any_spec] * n_cache),
        scratch_shapes=[pltpu.SemaphoreType.DMA((2, n_cache, depth))],
        name="shift_caches",
    )(*caches, *news)


def _pad_lanes(v, width=LANES):
    return jnp.pad(v, [(0, 0)] * (v.ndim - 1) + [(0, width - v.shape[-1])])


def kernel(x_prompt, x_sample, c_prompt, c_sample, state_conv, state_ssm, cache_k_win128, cache_v_win128, cache_k_win512, cache_v_win512, cache_k_win2048, cache_v_win2048, norm1_g, w_ada, b_ada, w_in, conv_w, conv_b, dt_bias, a_log, d_skip, ssd_norm_g, w_ssd_proj, w_attn_proj, w_out, norm2_g, w_router_group, b_router_group, w_router_expert, b_router_expert, w_exp_gate, w_exp_up, w_exp_down, final_norm_g):
    depth = w_in.shape[0]
    bp, seq, d = x_prompt.shape
    bs = x_sample.shape[0]
    assert x_sample.shape[1] == 1 and d == D_MODEL and w_in.shape[2] == IN_WIDTH
    assert seq % (ATT_PATTERNS[-1][1] * ATT_BLOCK) == 0
    n_p = bp * seq
    caches = ((cache_k_win128, cache_v_win128), (cache_k_win512, cache_v_win512),
              (cache_k_win2048, cache_v_win2048))

    rows = bp + bs
    rows_pad = -(-rows // 8) * 8
    c_all = jnp.pad(jnp.concatenate([c_prompt, c_sample], axis=0), ((0, rows_pad - rows), (0, 0)))
    mods = _modulation(c_all, w_ada, b_ada)

    xp = x_prompt.reshape(n_p, d)
    xs = x_sample.reshape(bs, d)
    outs = {k: [] for k in ("conv_p", "conv_s", "ssm_p", "ssm_s")}
    kv_p = [[[], []] for _ in ATT_PATTERNS]
    kv_new = [[[], []] for _ in ATT_PATTERNS]

    for l in range(depth):
        mod_p = mods[l, :bp]
        mod_s = mods[l, bp:bp + bs]
        w_in_l = w_in[l]
        wz = w_in_l[:, 0:OFF_XBC].astype(BF16)
        wx = w_in_l[:, OFF_XBC:OFF_DT].astype(BF16)
        wd = _pad_lanes(w_in_l[:, OFF_DT:OFF_QKV]).astype(BF16)
        wq = w_in_l[:, OFF_QKV:OFF_GATE].astype(BF16)
        wgt = w_in_l[:, OFF_GATE:IN_WIDTH].astype(BF16)
        g1n = norm1_g[l].reshape(1, d)
        g2n = norm2_g[l].reshape(1, d)
        cw = conv_w[l]
        cb = conv_b[l].reshape(1, -1)
        dskip_full = jnp.repeat(d_skip[l], SSD_HEAD_DIM).reshape(1, -1)
        ssd_g = ssd_norm_g[l].reshape(1, -1)
        w_router = _pad_lanes(jnp.concatenate([w_router_expert[l], w_router_group[l]], axis=1))
        b_router = _pad_lanes(jnp.concatenate([b_router_expert[l], b_router_group[l]], axis=0).reshape(1, -1))

        z, xbc, dt_raw, qkv, gates = _in_proj_prompt(
            xp, mod_p[:, 0:2 * d].reshape(bp, 1, 2 * d), g1n, wz, wx, wd, wq, wgt, seq)
        yn, ssm_new = _ssd_prompt(xbc, z, dt_raw, cw, cb, _pad_lanes(dt_bias[l].reshape(1, -1)),
                                  _pad_lanes(a_log[l].reshape(1, -1)), dskip_full, ssd_g, bp, seq)
        outs["ssm_p"].append(ssm_new.reshape(bp, SSD_GROUPS, SSD_HPG, SSD_HEAD_DIM, SSD_STATE))
        outs["conv_p"].append(xbc.reshape(bp, seq, SSD_CONV_DIM)[:, seq - (SSD_CONV - 1):])
        o_list, lse_list = [], []
        qkv4 = qkv.reshape(bp, seq, 3, ATT_GROUPS, ATT_HEADS, ATT_HEAD_DIM)
        for gi, (win, dil) in enumerate(ATT_PATTERNS):
            o_g, lse_g = _attn_prompt(qkv, gi, bp, seq)
            o_list.append(o_g)
            lse_list.append(lse_g)
            keep = min(win, seq)
            kv_p[gi][0].append(qkv4[:, seq - keep:, 1, gi].astype(F32))
            kv_p[gi][1].append(qkv4[:, seq - keep:, 2, gi].astype(F32))
        mod_post = jnp.concatenate([mod_p[:, 2 * d:3 * d], mod_p[:, 3 * d:5 * d]], axis=1).reshape(bp, 1, 3 * d)
        x1, h2, gate = _post(xp, yn, (o_list, lse_list), gates, mod_post, g2n,
                             w_ssd_proj[l].astype(BF16), w_attn_proj[l].astype(BF16), w_out[l].astype(BF16),
                             w_router, b_router, passes=1, merge_attn=True, tm=512, rows_per_mod=seq,
                             h2_dtype=BF16)
        xp = _moe(h2, gate, x1, mod_p[:, 5 * d:6 * d].reshape(bp, 1, d),
                  w_exp_gate[l].astype(BF16), w_exp_up[l].astype(BF16), w_exp_down[l].astype(BF16),
                  passes=1, tm=1024, rows_per_mod=seq)

        u = _in_proj_sample(xs, mod_s[:, 0:2 * d], g1n, w_in_l)
        z_s = u[:, 0:OFF_XBC]
        xbc_s = u[:, OFF_XBC:OFF_DT]
        dt_s = _pad_lanes(u[:, OFF_DT:OFF_QKV])
        qkv_s = u[:, OFF_QKV:OFF_GATE]
        gates_s = u[:, OFF_GATE:IN_WIDTH]
        yn_s, o_s, conv_new, ssm_new_s = _step_sample(
            l, z_s, xbc_s, dt_s, qkv_s, state_conv, state_ssm.reshape(depth, bs, SSD_INNER, SSD_STATE), caches,
            cw, cb, jnp.repeat(dt_bias[l], SSD_HEAD_DIM).reshape(1, -1),
            jnp.repeat(a_log[l], SSD_HEAD_DIM).reshape(1, -1), dskip_full, ssd_g)
        outs["conv_s"].append(conv_new)
        outs["ssm_s"].append(ssm_new_s.reshape(bs, SSD_GROUPS, SSD_HPG, SSD_HEAD_DIM, SSD_STATE))
        qkv_s5 = qkv_s.reshape(bs, 3, ATT_GROUPS, ATT_HEADS, ATT_HEAD_DIM)
        for gi in range(ATT_GROUPS):
            kv_new[gi][0].append(qkv_s5[:, 1, gi][:, None])
            kv_new[gi][1].append(qkv_s5[:, 2, gi][:, None])
        mod_post_s = jnp.concatenate([mod_s[:, 2 * d:3 * d], mod_s[:, 3 * d:5 * d]], axis=1)
        x1_s, h2_s, gate_s = _post(xs, yn_s.reshape(bs, SSD_INNER), o_s.reshape(bs, ATT_OUT), gates_s, mod_post_s,
                                   g2n, w_ssd_proj[l], w_attn_proj[l], w_out[l], w_router, b_router,
                                   passes=3, merge_attn=False, tm=bs, rows_per_mod=1, h2_dtype=F32)
        xs = _moe(h2_s, gate_s, x1_s, mod_s[:, 5 * d:6 * d], w_exp_gate[l], w_exp_up[l], w_exp_down[l],
                  passes=3, tm=bs, rows_per_mod=1)

    fg = final_norm_g.reshape(1, d)
    y_prompt = _final_norm(xp, fg, 1024).reshape(bp, seq, d)
    y_sample = _final_norm(xs, fg, bs).reshape(bs, 1, d)

    flat_caches = [c for pair in caches for c in pair]
    flat_news = [jnp.stack(kv_new[gi][w]) for gi in range(ATT_GROUPS) for w in range(2)]
    shifted = _shift_caches(flat_caches, flat_news)

    st = jnp.stack
    res = [y_prompt, y_sample, st(outs["conv_p"]), st(outs["conv_s"]), st(outs["ssm_p"]), st(outs["ssm_s"])]
    for gi in range(ATT_GROUPS):
        res += [st(kv_p[gi][0]), shifted[2 * gi], st(kv_p[gi][1]), shifted[2 * gi + 1]]
    return tuple(res)
```

```python
import functools
import math

import jax
import jax.numpy as jnp
from jax import lax
from jax.experimental import pallas as pl
from jax.experimental.pallas import tpu as pltpu

F32 = jnp.float32
BF16 = jnp.bfloat16

D_MODEL = 1024
SSD_INNER = 1024
SSD_HEAD_DIM = 64
SSD_HEADS = 16
SSD_GROUPS = 2
SSD_HPG = 8
SSD_STATE = 128
SSD_CONV = 4
SSD_CHUNK = 128
SSD_CONV_DIM = SSD_INNER + 2 * SSD_GROUPS * SSD_STATE
ATT_PATTERNS = ((128, 1), (512, 4), (2048, 16))
ATT_GROUPS = 3
ATT_HEADS = 8
ATT_HEAD_DIM = 64
ATT_BLOCK = 128
ATT_OUT = ATT_HEADS * ATT_HEAD_DIM
MOE_GROUPS = 4
MOE_PER_GROUP = 4
MOE_EXPERTS = 16
MOE_FF = 512
IN_QKV = 3 * ATT_GROUPS * ATT_OUT
IN_GATE = 2 * D_MODEL
OFF_XBC = SSD_INNER
OFF_DT = OFF_XBC + SSD_CONV_DIM
OFF_QKV = OFF_DT + SSD_HEADS
OFF_GATE = OFF_QKV + IN_QKV
IN_WIDTH = OFF_GATE + IN_GATE
EPS = 1e-6

LANES = 128
VMEM_LIMIT = 56 * 1024 * 1024


def _cparams(sem, vmem=VMEM_LIMIT):
    return pltpu.CompilerParams(dimension_semantics=sem, vmem_limit_bytes=vmem)


def _split2(a):
    hi = a.astype(BF16)
    lo = (a - hi.astype(F32)).astype(BF16)
    return hi, lo


def _split3(a):
    hi = a.astype(BF16)
    r = a - hi.astype(F32)
    mid = r.astype(BF16)
    lo = (r - mid.astype(F32)).astype(BF16)
    return hi, mid, lo


def _dot(a, b):
    return jnp.dot(a, b, preferred_element_type=F32)


def _dot_nt(a, b):
    return lax.dot_general(a, b, (((1,), (1,)), ((), ())), preferred_element_type=F32)


def _mm(a, w, passes):
    if passes == 1:
        return _dot(a.astype(BF16), w.astype(BF16))
    a = a.astype(F32)
    w = w.astype(F32)
    a_hi, a_lo = _split2(a)
    w_hi, w_lo = _split2(w)
    return _dot(a_hi, w_hi) + (_dot(a_lo, w_hi) + _dot(a_hi, w_lo))


def _mm01(a, e01, terms):
    parts = _split3(a) if terms == 3 else _split2(a)
    out = _dot(parts[0], e01)
    for p in parts[1:]:
        out = out + _dot(p, e01)
    return out


def _sigmoid(x):
    return 1.0 / (1.0 + jnp.exp(-x))


def _silu(x):
    return x * _sigmoid(x)


def _softplus(x):
    return jnp.maximum(x, 0.0) + jnp.log(1.0 + jnp.exp(-jnp.abs(x)))


def _head_expand_matrix(rows, n_heads, width):
    r = lax.broadcasted_iota(jnp.int32, (rows, n_heads * width), 0)
    c = lax.broadcasted_iota(jnp.int32, (rows, n_heads * width), 1)
    return jnp.where((c // width) == r, 1.0, 0.0).astype(BF16)


def _rms_modulate(x, g, sh, sc):
    ms = jnp.mean(x * x, axis=-1, keepdims=True)
    y = x * lax.rsqrt(ms + EPS) * g
    return y * (1.0 + sc) + sh


def _mod_kernel(c_ref, w_ref, b_ref, o_ref):
    a = _silu(c_ref[...])
    o_ref[...] = _mm(a, w_ref[...], 3) + b_ref[...]


def _modulation(c_all, w_ada, b_ada):
    depth, d, n6 = w_ada.shape
    rows = c_all.shape[0]
    tn = 1024
    return pl.pallas_call(
        _mod_kernel,
        out_shape=jax.ShapeDtypeStruct((depth, rows, n6), F32),
        grid=(depth, n6 // tn),
        in_specs=[
            pl.BlockSpec((rows, d), lambda l, j: (0, 0)),
            pl.BlockSpec((None, d, tn), lambda l, j: (l, 0, j)),
            pl.BlockSpec((None, 1, tn), lambda l, j: (l, 0, j)),
        ],
        out_specs=pl.BlockSpec((None, rows, tn), lambda l, j: (l, 0, j)),
        compiler_params=_cparams(("arbitrary", "arbitrary")),
        name="adaln_mod",
    )(c_all, w_ada, b_ada.reshape(depth, 1, n6))


_IN_CHUNK = 512
TOK_TILE = 512


def _in_kernel(x_ref, mod_ref, g_ref, wz_ref, wx_ref, wd_ref, wq0_ref, wq1_ref, wq2_ref, wg_ref,
               z_ref, xbc_ref, dt_ref, a0_ref, a1_ref, a2_ref, gates_ref, h_scr):
    sh = mod_ref[:, 0:D_MODEL]
    sc = mod_ref[:, D_MODEL:2 * D_MODEL]
    hf = _rms_modulate(x_ref[...], g_ref[...], sh, sc)
    n_cb = D_MODEL // LANES
    for cbk in range(n_cb):
        h_scr[cbk] = hf[:, cbk * LANES:(cbk + 1) * LANES]
    h = hf.astype(BF16)
    for w_ref, o_ref in ((wz_ref, z_ref), (wx_ref, xbc_ref), (wq0_ref, a0_ref), (wg_ref, gates_ref)):
        width = o_ref.shape[-1]
        for c0 in range(0, width, _IN_CHUNK):
            o_ref[:, c0:c0 + _IN_CHUNK] = _dot(h, w_ref[:, c0:c0 + _IN_CHUNK]).astype(o_ref.dtype)
    dt_ref[...] = _dot(h, wd_ref[...])
    for w_ref, a_ref in ((wq1_ref, a1_ref), (wq2_ref, a2_ref)):
        d, r_len, width = a_ref.shape
        hp = jnp.concatenate(
            [jnp.concatenate([h_scr[cbk, pl.ds(r, r_len, stride=d), :] for cbk in range(n_cb)], axis=1)
             for r in range(d)], axis=0).astype(BF16)
        for c0 in range(0, width, _IN_CHUNK):
            res = _dot(hp, w_ref[:, c0:c0 + _IN_CHUNK]).astype(BF16)
            a_ref[:, :, c0:c0 + _IN_CHUNK] = res.reshape(d, r_len, _IN_CHUNK)


def _in_proj_prompt(x, mod, norm_g, wz, wx, wd, wqs, wg, bsz, seq):
    n, d = x.shape
    tm = TOK_TILE
    tiles_per_seq = seq // tm
    qw = 3 * ATT_OUT

    def const(shape):
        return pl.BlockSpec(shape, lambda i: (0, 0), pipeline_mode=pl.Buffered(1))

    a_shapes, a_specs = [], []
    for _, dil in ATT_PATTERNS[1:]:
        a_shapes.append(jax.ShapeDtypeStruct((bsz, tiles_per_seq, dil, tm // dil, qw), BF16))
        a_specs.append(pl.BlockSpec((None, None, dil, tm // dil, qw),
                                    lambda i: (i // tiles_per_seq, i % tiles_per_seq, 0, 0, 0)))
    return pl.pallas_call(
        _in_kernel,
        out_shape=(
            jax.ShapeDtypeStruct((n, SSD_INNER), BF16),
            jax.ShapeDtypeStruct((n, SSD_CONV_DIM), F32),
            jax.ShapeDtypeStruct((n, LANES), F32),
            jax.ShapeDtypeStruct((n, qw), BF16),
            a_shapes[0], a_shapes[1],
            jax.ShapeDtypeStruct((n, IN_GATE), BF16),
        ),
        grid=(n // tm,),
        in_specs=[
            pl.BlockSpec((tm, d), lambda i: (i, 0)),
            pl.BlockSpec((None, 1, 2 * d), lambda i: (i // tiles_per_seq, 0, 0)),
            const((1, d)),
            const(wz.shape), const(wx.shape), const(wd.shape),
            const(wqs[0].shape), const(wqs[1].shape), const(wqs[2].shape), const(wg.shape),
        ],
        out_specs=(
            pl.BlockSpec((tm, SSD_INNER), lambda i: (i, 0)),
            pl.BlockSpec((tm, SSD_CONV_DIM), lambda i: (i, 0)),
            pl.BlockSpec((tm, LANES), lambda i: (i, 0)),
            pl.BlockSpec((tm, qw), lambda i: (i, 0)),
            a_specs[0], a_specs[1],
            pl.BlockSpec((tm, IN_GATE), lambda i: (i, 0)),
        ),
        scratch_shapes=[pltpu.VMEM((d // LANES, tm, LANES), F32)],
        compiler_params=_cparams(("arbitrary",)),
        name="in_proj_prompt",
    )(x, mod, norm_g, wz, wx, wd, wqs[0], wqs[1], wqs[2], wg)


def _in_small_kernel(x_ref, mod_ref, g_ref, w_ref, o_ref):
    sh = mod_ref[:, 0:D_MODEL]
    sc = mod_ref[:, D_MODEL:2 * D_MODEL]
    h = _rms_modulate(x_ref[...], g_ref[...], sh, sc)
    o_ref[...] = _mm(h, w_ref[...], 3)


def _in_proj_sample(x, mod, norm_g, w_in_l, tn=1024):
    m, d = x.shape
    width = w_in_l.shape[1]
    return pl.pallas_call(
        _in_small_kernel,
        out_shape=jax.ShapeDtypeStruct((m, width), F32),
        grid=(pl.cdiv(width, tn),),
        in_specs=[
            pl.BlockSpec((m, d), lambda j: (0, 0)),
            pl.BlockSpec((m, 2 * d), lambda j: (0, 0)),
            pl.BlockSpec((1, d), lambda j: (0, 0)),
            pl.BlockSpec((d, tn), lambda j: (0, j)),
        ],
        out_specs=pl.BlockSpec((m, tn), lambda j: (0, j)),
        compiler_params=_cparams(("arbitrary",)),
        name="in_proj_sample",
    )(x, mod, norm_g, w_in_l)


def _ssd_kernel(xbc_ref, z_ref, dt_ref, cw_ref, cb_ref, dtb_ref, alog_ref, dskip_ref, ng_ref,
                yn_ref, hout_ref, h_scr, xp_scr):
    q = SSD_CHUNK
    c = pl.program_id(1)

    @pl.when(c == 0)
    def _():
        h_scr[...] = jnp.zeros_like(h_scr)
        xp_scr[0:8, :] = jnp.zeros((8, SSD_CONV_DIM), F32)

    xp_scr[8:8 + q, :] = xbc_ref[...]
    acc = cb_ref[...] + cw_ref[3:4, :] * xp_scr[8:8 + q, :]
    for k in range(SSD_CONV - 1):
        acc = acc + cw_ref[k:k + 1, :] * xp_scr[5 + k:5 + k + q, :]
    xp_scr[0:8, :] = xp_scr[q:q + 8, :]
    xc = _silu(acc)
    xs = xc[:, 0:SSD_INNER]
    xs_bf = xs.astype(BF16)

    lane_q = lax.broadcasted_iota(jnp.int32, (q, LANES), 1)
    row_q = lax.broadcasted_iota(jnp.int32, (q, LANES), 0)
    causal = row_q >= lane_q
    tri = jnp.where(causal, 1.0, 0.0).astype(BF16)
    tri_t = jnp.where(lane_q >= row_q, 1.0, 0.0).astype(BF16)
    e_heads = _head_expand_matrix(LANES, SSD_HEADS, SSD_HEAD_DIM)

    dt = _softplus(dt_ref[...] + dtb_ref[...])
    dt = jnp.where(lane_q < SSD_HEADS, dt, 0.0)
    a = -jnp.exp(alog_ref[...])
    d_a = dt * a
    cum = _mm01_left(tri, d_a)
    d_a_t = d_a.T
    dt_t = dt.T
    cum_t = _mm01(d_a_t, tri_t, 3)
    cum_last = cum[q - 1:q, :]
    exp_cum = jnp.exp(cum)
    dec_end = jnp.exp(cum_last - cum)
    stack = jnp.concatenate([exp_cum, dec_end * dt], axis=0)
    full = _mm01(stack, e_heads, 2)
    exp_cum_full = full[0:q]
    w_full = full[q:2 * q]
    chunk_dec_t = jnp.exp(cum_t[:, q - 1:q])

    lane_half = lax.broadcasted_iota(jnp.int32, (q, LANES), 1) < SSD_HEAD_DIM
    y_parts = []
    for g in range(SSD_GROUPS):
        b_off = SSD_INNER + g * SSD_STATE
        c_off = SSD_INNER + SSD_GROUPS * SSD_STATE + g * SSD_STATE
        bm = xc[:, b_off:b_off + SSD_STATE].astype(BF16)
        cm = xc[:, c_off:c_off + SSD_STATE].astype(BF16)
        cbm = _dot_nt(cm, bm)
        gw = SSD_HPG * SSD_HEAD_DIM
        g0 = g * gw
        yd = []
        for pair in range(SSD_HPG // 2):
            x_pair = xs_bf[:, g0 + pair * LANES:g0 + (pair + 1) * LANES]
            halves = []
            for hh in range(2):
                h = g * SSD_HPG + pair * 2 + hh
                seg = cum[:, h:h + 1] - cum_t[h:h + 1, :]
                dec = jnp.where(causal, jnp.exp(jnp.where(causal, seg, 0.0)), 0.0)
                m_h = (cbm * dec * dt_t[h:h + 1, :]).astype(BF16)
                halves.append(_dot(m_h, x_pair))
            yd.append(jnp.where(lane_half, halves[0], halves[1]))
        y_diag = jnp.concatenate(yd, axis=1)
        h_g = h_scr[g0:g0 + gw, :]
        y_off = _dot_nt(cm, h_g.astype(BF16)) * exp_cum_full[:, g0:g0 + gw]
        y_parts.append(y_diag + y_off)
        xw = (xs[:, g0:g0 + gw] * w_full[:, g0:g0 + gw])
        st = _dot(xw.T.astype(BF16), bm)
        for e in range(SSD_HPG):
            h = g * SSD_HPG + e
            r0 = g0 + e * SSD_HEAD_DIM
            h_scr[r0:r0 + SSD_HEAD_DIM, :] = (h_scr[r0:r0 + SSD_HEAD_DIM, :] * chunk_dec_t[h:h + 1, :]
                                              + st[e * SSD_HEAD_DIM:(e + 1) * SSD_HEAD_DIM, :])

    y = jnp.concatenate(y_parts, axis=1) + dskip_ref[...] * xs
    y = y * _silu(z_ref[...].astype(F32))
    outs = []
    for g in range(SSD_GROUPS):
        gw = SSD_HPG * SSD_HEAD_DIM
        yg = y[:, g * gw:(g + 1) * gw]
        ms = jnp.mean(yg * yg, axis=-1, keepdims=True)
        outs.append(yg * lax.rsqrt(ms + EPS) * ng_ref[:, g * gw:(g + 1) * gw])
    yn_ref[...] = jnp.concatenate(outs, axis=1).astype(yn_ref.dtype)

    @pl.when(c == pl.num_programs(1) - 1)
    def _():
        hout_ref[...] = h_scr[...]


def _mm01_left(tri01, a):
    hi, mid, lo = _split3(a)
    return _dot(tri01, hi) + (_dot(tri01, mid) + _dot(tri01, lo))


def _ssd_prompt(xbc, z, dt_raw, conv_w, conv_b, dt_bias, a_log, d_skip_full, ssd_norm_g, bsz, seq):
    q = SSD_CHUNK
    nc = seq // q
    row = lambda b, c: (b * nc + c, 0)
    const = lambda b, c: (0, 0)
    return pl.pallas_call(
        _ssd_kernel,
        out_shape=(
            jax.ShapeDtypeStruct((bsz * seq, SSD_INNER), BF16),
            jax.ShapeDtypeStruct((bsz, SSD_INNER, SSD_STATE), F32),
        ),
        grid=(bsz, nc),
        in_specs=[
            pl.BlockSpec((q, SSD_CONV_DIM), row),
            pl.BlockSpec((q, SSD_INNER), row),
            pl.BlockSpec((q, LANES), row),
            pl.BlockSpec((SSD_CONV, SSD_CONV_DIM), const),
            pl.BlockSpec((1, SSD_CONV_DIM), const),
            pl.BlockSpec((1, LANES), const),
            pl.BlockSpec((1, LANES), const),
            pl.BlockSpec((1, SSD_INNER), const),
            pl.BlockSpec((1, SSD_INNER), const),
        ],
        out_specs=(
            pl.BlockSpec((q, SSD_INNER), row),
            pl.BlockSpec((None, SSD_INNER, SSD_STATE), lambda b, c: (b, 0, 0)),
        ),
        scratch_shapes=[
            pltpu.VMEM((SSD_INNER, SSD_STATE), F32),
            pltpu.VMEM((q + 8, SSD_CONV_DIM), F32),
        ],
        compiler_params=_cparams(("arbitrary", "arbitrary")),
        name="ssd_prompt",
    )(xbc, z, dt_raw, conv_w, conv_b, dt_bias, a_log, d_skip_full, ssd_norm_g)


def _attn_kernel(q_ref, kc_ref, kp_ref, vc_ref, vp_ref, o_out_ref, lse_out_ref,
                 q_ref_s, kwin, vwin, o_ref, lse_ref, *, tq, band):
    blk = ATT_BLOCK
    j = pl.program_id(2)
    q_ref_s[...] = q_ref[...].reshape(tq, ATT_OUT)
    kwin[0:blk, :] = kp_ref[...].reshape(blk, ATT_OUT)
    kwin[blk:blk + tq, :] = kc_ref[...].reshape(tq, ATT_OUT)
    vwin[0:blk, :] = vp_ref[...].reshape(blk, ATT_OUT)
    vwin[blk:blk + tq, :] = vc_ref[...].reshape(tq, ATT_OUT)
    q_ref = q_ref_s

    qi = lax.broadcasted_iota(jnp.int32, (blk, 2 * blk), 0)
    ki = lax.broadcasted_iota(jnp.int32, (blk, 2 * blk), 1)
    dist = qi + blk - ki
    in_band = (dist >= 0) & (dist <= band)
    lane = lax.broadcasted_iota(jnp.int32, (blk, LANES), 1)
    lane_half = lane < ATT_HEAD_DIM
    scale = ATT_HEAD_DIM ** -0.5
    zero_bf = jnp.zeros((blk, LANES), BF16)

    for i in range(tq // blk):
        if i == 0:
            valid = in_band & ((ki >= blk) | (j > 0))
        else:
            valid = in_band
        lse_tile = jnp.zeros((blk, LANES), F32)
        for hp in range(ATT_HEADS // 2):
            c0 = hp * LANES
            q_pair = q_ref[i * blk:(i + 1) * blk, c0:c0 + LANES]
            k_pair = kwin[i * blk:(i + 2) * blk, c0:c0 + LANES]
            v_pair = vwin[i * blk:(i + 2) * blk, c0:c0 + LANES]
            halves = []
            for hh in range(2):
                q_m = jnp.where(lane_half if hh == 0 else jnp.logical_not(lane_half), q_pair, zero_bf)
                s = _dot_nt(q_m, k_pair) * scale
                s = jnp.where(valid, s, -jnp.inf)
                m = jnp.max(s, axis=-1, keepdims=True)
                e = jnp.exp(s - m)
                den = jnp.sum(e, axis=-1, keepdims=True)
                pv = _dot(e.astype(BF16), v_pair)
                halves.append(pv / den)
                lse = m + jnp.log(den)
                head = hp * 2 + hh
                lse_tile = jnp.where((lane == head) | (lane == ATT_HEADS + head), lse, lse_tile)
            o_ref[i * blk:(i + 1) * blk, c0:c0 + LANES] = jnp.where(lane_half, halves[0], halves[1]).astype(o_ref.dtype)
        lse_ref[i * blk:(i + 1) * blk, :] = lse_tile
    o_out_ref[...] = o_ref[...].reshape(o_out_ref.shape)
    lse_out_ref[...] = lse_ref[...].reshape(lse_out_ref.shape)


def _attn_prompt(a_g, gi, bsz, seq):
    win, dil = ATT_PATTERNS[gi]
    band = win // dil
    blk = ATT_BLOCK
    rows = TOK_TILE // dil
    tiles = seq // TOK_TILE
    length = seq // dil
    tq = min(512, length)
    nj = length // tq
    tq_tiles = tq // rows
    cur = lambda which: pl.BlockSpec((None, tq_tiles, None, rows, ATT_OUT),
                                     lambda b, r, j: (b, j, r, 0, which))
    if rows >= blk:
        prev = lambda which: pl.BlockSpec(
            (None, None, None, blk, ATT_OUT),
            lambda b, r, j: (b, jnp.maximum(j * tq_tiles - 1, 0), r, rows // blk - 1, which))
    else:
        prev = lambda which: pl.BlockSpec(
            (None, blk // rows, None, rows, ATT_OUT),
            lambda b, r, j: (b, jnp.maximum(j * (tq // blk) - 1, 0), r, 0, which))
    return pl.pallas_call(
        functools.partial(_attn_kernel, tq=tq, band=band),
        out_shape=(
            jax.ShapeDtypeStruct((bsz, tiles, dil, rows, ATT_OUT), BF16),
            jax.ShapeDtypeStruct((bsz, tiles, dil, rows, LANES), F32),
        ),
        grid=(bsz, dil, nj),
        in_specs=[cur(0), cur(1), prev(1), cur(2), prev(2)],
        out_specs=(
            pl.BlockSpec((None, tq_tiles, None, rows, ATT_OUT), lambda b, r, j: (b, j, r, 0, 0)),
            pl.BlockSpec((None, tq_tiles, None, rows, LANES), lambda b, r, j: (b, j, r, 0, 0)),
        ),
        scratch_shapes=[
            pltpu.VMEM((tq, ATT_OUT), BF16),
            pltpu.VMEM((blk + tq, ATT_OUT), BF16),
            pltpu.VMEM((blk + tq, ATT_OUT), BF16),
            pltpu.VMEM((tq, ATT_OUT), BF16),
            pltpu.VMEM((tq, LANES), F32),
        ],
        compiler_params=_cparams(("arbitrary", "arbitrary", "arbitrary")),
        name=f"attn_prompt_w{win}",
    )(a_g, a_g, a_g, a_g, a_g)


def _router_gates(logits):
    shape = logits.shape
    lane = lax.broadcasted_iota(jnp.int32, shape, 1)
    big = jnp.int32(1 << 20)
    neg = -jnp.inf
    is_grp = (lane >= MOE_EXPERTS) & (lane < MOE_EXPERTS + MOE_GROUPS)
    lg = jnp.where(is_grp, logits, neg)
    gm = jnp.max(lg, axis=-1, keepdims=True)
    g_lane = jnp.min(jnp.where(lg == gm, lane, big), axis=-1, keepdims=True)
    g_sum = jnp.sum(jnp.exp(lg - gm), axis=-1, keepdims=True)
    g_w = 1.0 / g_sum
    lo = (g_lane - MOE_EXPERTS) * MOE_PER_GROUP
    in_grp = (lane >= lo) & (lane < lo + MOE_PER_GROUP)
    le = jnp.where(in_grp, logits, neg)
    m1 = jnp.max(le, axis=-1, keepdims=True)
    i1 = jnp.min(jnp.where(le == m1, lane, big), axis=-1, keepdims=True)
    le2 = jnp.where(lane == i1, neg, le)
    m2 = jnp.max(le2, axis=-1, keepdims=True)
    i2 = jnp.min(jnp.where(le2 == m2, lane, big), axis=-1, keepdims=True)
    t = jnp.exp(m2 - m1)
    w1 = 1.0 / (1.0 + t)
    w2 = t / (1.0 + t)
    return jnp.where(lane == i1, g_w * w1, jnp.where(lane == i2, g_w * w2, 0.0))


def _post_kernel(*refs, passes, merge_attn):
    if merge_attn:
        (x_ref, yn_ref, o0_ref, o1_ref, o2_ref, l0_ref, l1_ref, l2_ref, gates_ref, mod_ref, n2_ref,
         wssd_ref, wattn_ref, wout_ref, wr_ref, br_ref, x1_ref, h2_ref, gate_ref,
         o1_scr, o2_scr, l1_scr, l2_scr) = refs
        for src_ref, dst_ref in ((o1_ref, o1_scr), (o2_ref, o2_scr), (l1_ref, l1_scr), (l2_ref, l2_scr)):
            dil, r_len, width = src_ref.shape
            for r in range(dil):
                blk_r = src_ref[r].astype(F32)
                for cbk in range(width // LANES):
                    dst_ref[cbk, pl.ds(r, r_len, stride=dil), :] = blk_r[:, cbk * LANES:(cbk + 1) * LANES]
        o_nat = [o0_ref[...].astype(F32)] + [
            jnp.concatenate([scr[cbk] for cbk in range(ATT_OUT // LANES)], axis=1) for scr in (o1_scr, o2_scr)]
        l0, l1, l2 = l0_ref[...], l1_scr[0], l2_scr[0]
        mx = jnp.maximum(jnp.maximum(l0, l1), l2)
        e0, e1, e2 = jnp.exp(l0 - mx), jnp.exp(l1 - mx), jnp.exp(l2 - mx)
        inv = 1.0 / (e0 + e1 + e2)
        lane = lax.broadcasted_iota(jnp.int32, l0.shape, 1)
        e8 = _head_expand_matrix(LANES, ATT_HEADS, ATT_HEAD_DIM)
        r = lax.broadcasted_iota(jnp.int32, e8.shape, 0)
        c = lax.broadcasted_iota(jnp.int32, e8.shape, 1)
        e8 = jnp.where((c // ATT_HEAD_DIM) == (r - ATT_HEADS), 1.0, e8.astype(F32)).astype(BF16)
        o = None
        for e_g, o_g in zip((e0, e1, e2), o_nat):
            w = e_g * inv
            hi, lo = _split2(w)
            w_exp = _dot(jnp.where(lane < ATT_HEADS, hi, lo), e8)
            term = w_exp * o_g
            o = term if o is None else o + term
    else:
        (x_ref, yn_ref, o_ref, gates_ref, mod_ref, n2_ref,
         wssd_ref, wattn_ref, wout_ref, wr_ref, br_ref, x1_ref, h2_ref, gate_ref) = refs
        o = o_ref[...]
    d = D_MODEL
    g1 = mod_ref[:, 0:d]
    sh2 = mod_ref[:, d:2 * d]
    sc2 = mod_ref[:, 2 * d:3 * d]
    ssd_branch = _mm(yn_ref[...], wssd_ref[...], passes)
    attn_branch = _mm(o, wattn_ref[...], passes)
    ga = gates_ref[:, 0:d].astype(F32)
    gb = gates_ref[:, d:2 * d].astype(F32)
    mixed = _sigmoid(ga) * ssd_branch + _sigmoid(gb) * attn_branch
    x1 = x_ref[...] + g1 * _mm(mixed, wout_ref[...], passes)
    x1_ref[...] = x1
    h2 = _rms_modulate(x1, n2_ref[...], sh2, sc2)
    h2_ref[...] = h2.astype(h2_ref.dtype)
    logits = _mm(h2, wr_ref[...], 3) + br_ref[...]
    gate_ref[...] = _router_gates(logits)


def _post(x, yn, attn_in, gates, mod, norm2_g, wssd, wattn, wout, wr, br, *, passes, merge_attn,
          tm, rows_per_mod, h2_dtype):
    n, d = x.shape
    per_row_mod = rows_per_mod == 1
    if per_row_mod:
        mod_spec = pl.BlockSpec((tm, 3 * d), lambda i: (i, 0))
    else:
        tiles = rows_per_mod // tm
        mod_spec = pl.BlockSpec((None, 1, 3 * d), lambda i: (i // tiles, 0, 0))
    row = lambda w: pl.BlockSpec((tm, w), lambda i: (i, 0))
    const = lambda a: pl.BlockSpec(a.shape, lambda i: (0, 0), pipeline_mode=pl.Buffered(1))
    scratch = []
    if merge_attn:
        assert tm == TOK_TILE
        o_list, lse_list = attn_in
        attn_args = list(o_list) + list(lse_list)
        tiles = rows_per_mod // tm

        def tile_spec(a):
            _, _, dil, r_len, w = a.shape
            if dil == 1:
                return pl.BlockSpec((None, None, None, r_len, w), lambda i: (i // tiles, i % tiles, 0, 0, 0))
            return pl.BlockSpec((None, None, dil, r_len, w), lambda i: (i // tiles, i % tiles, 0, 0, 0))

        attn_specs = [tile_spec(a) for a in attn_args]
        scratch = [pltpu.VMEM((ATT_OUT // LANES, tm, LANES), F32), pltpu.VMEM((ATT_OUT // LANES, tm, LANES), F32),
                   pltpu.VMEM((1, tm, LANES), F32), pltpu.VMEM((1, tm, LANES), F32)]
    else:
        attn_args = [attn_in]
        attn_specs = [row(ATT_OUT)]
    return pl.pallas_call(
        functools.partial(_post_kernel, passes=passes, merge_attn=merge_attn),
        out_shape=(
            jax.ShapeDtypeStruct((n, d), F32),
            jax.ShapeDtypeStruct((n, d), h2_dtype),
            jax.ShapeDtypeStruct((n, LANES), F32),
        ),
        grid=(n // tm,),
        in_specs=[row(d), row(SSD_INNER)] + attn_specs + [row(IN_GATE), mod_spec, const(norm2_g),
                                                          const(wssd), const(wattn), const(wout),
                                                          const(wr), const(br)],
        out_specs=(row(d), row(d), row(LANES)),
        scratch_shapes=scratch,
        compiler_params=_cparams(("arbitrary",)),
        name="post_merge" if merge_attn else "post_sample",
    )(x, yn, *attn_args, gates, mod, norm2_g, wssd, wattn, wout, wr, br)


def _moe_kernel(h2_ref, gate_ref, x1_ref, g2_ref, wg_ref, wu_ref, wd_ref, x2_ref, acc_ref, *, passes):
    e = pl.program_id(1)

    @pl.when(e == 0)
    def _():
        acc_ref[...] = jnp.zeros_like(acc_ref)

    h2 = h2_ref[...]
    hg = _mm(h2, wg_ref[...], passes)
    hu = _mm(h2, wu_ref[...], passes)
    gate = gate_ref[...]
    lane = lax.broadcasted_iota(jnp.int32, gate.shape, 1)
    gcol = jnp.sum(jnp.where(lane == e, gate, 0.0), axis=-1, keepdims=True)
    act = _silu(hg) * hu * gcol
    acc_ref[...] += _mm(act, wd_ref[...], passes)

    @pl.when(e == pl.num_programs(1) - 1)
    def _():
        x2_ref[...] = x1_ref[...] + g2_ref[...] * acc_ref[...]


def _moe(layer, h2, gate, x1, g2, wg, wu, wd, *, passes, tm, rows_per_mod):
    n, d = x1.shape
    if rows_per_mod == 1:
        g2_spec = pl.BlockSpec((tm, d), lambda i, e: (i, 0))
    else:
        tiles = rows_per_mod // tm
        g2_spec = pl.BlockSpec((None, 1, d), lambda i, e: (i // tiles, 0, 0))
    row = lambda w: pl.BlockSpec((tm, w), lambda i, e: (i, 0))
    return pl.pallas_call(
        functools.partial(_moe_kernel, passes=passes),
        out_shape=jax.ShapeDtypeStruct((n, d), F32),
        grid=(n // tm, MOE_EXPERTS),
        in_specs=[
            row(d), row(LANES), row(d), g2_spec,
            pl.BlockSpec((None, None, d, MOE_FF), lambda i, e: (layer, e, 0, 0)),
            pl.BlockSpec((None, None, d, MOE_FF), lambda i, e: (layer, e, 0, 0)),
            pl.BlockSpec((None, None, MOE_FF, d), lambda i, e: (layer, e, 0, 0)),
        ],
        out_specs=row(d),
        scratch_shapes=[pltpu.VMEM((tm, d), F32)],
        compiler_params=_cparams(("arbitrary", "arbitrary")),
        name="moe_dense",
    )(h2, gate, x1, g2, wg, wu, wd)


def _final_kernel(x_ref, g_ref, o_ref):
    x = x_ref[...]
    ms = jnp.mean(x * x, axis=-1, keepdims=True)
    o_ref[...] = x * lax.rsqrt(ms + EPS) * g_ref[...]


def _final_norm(x, g, tm):
    n, d = x.shape
    return pl.pallas_call(
        _final_kernel,
        out_shape=jax.ShapeDtypeStruct((n, d), F32),
        grid=(n // tm,),
        in_specs=[pl.BlockSpec((tm, d), lambda i: (i, 0)), pl.BlockSpec((1, d), lambda i: (0, 0))],
        out_specs=pl.BlockSpec((tm, d), lambda i: (i, 0)),
        compiler_params=_cparams(("arbitrary",)),
        name="final_norm",
    )(x, g)


def _step_kernel(z_ref, xbc_ref, dt_ref, cst_ref, h_ref,
                 cw_ref, cb_ref, dtb_ref, alog_ref, dskip_ref, ng_ref,
                 yn_ref, cnew_ref, hnew_ref, col_scr):
    x_new = xbc_ref[...]
    acc = cb_ref[...] + cw_ref[3:4, :] * x_new
    for k in range(SSD_CONV - 1):
        acc = acc + cw_ref[k:k + 1, :] * cst_ref[k:k + 1, :]
    cnew_ref[0:1, :] = cst_ref[1:2, :]
    cnew_ref[1:2, :] = cst_ref[2:3, :]
    cnew_ref[2:3, :] = x_new
    xc = _silu(acc)
    xs = xc[:, 0:SSD_INNER]

    e_heads = _head_expand_matrix(LANES, SSD_HEADS, SSD_HEAD_DIM)
    dt_raw8 = jnp.broadcast_to(dt_ref[...], (8, LANES))
    dt_full = _softplus(_mm01(dt_raw8, e_heads, 3)[0:1, :] + dtb_ref[...])
    a_full = -jnp.exp(alog_ref[...])
    dec_full = jnp.exp(dt_full * a_full)
    xdt = xs * dt_full
    col_scr[...] = jnp.zeros_like(col_scr)
    col_scr[0:1, :] = xdt
    col_scr[1:2, :] = dec_full
    cols = col_scr[...].T
    y_parts = []
    gw = SSD_HPG * SSD_HEAD_DIM
    for g in range(SSD_GROUPS):
        b_off = SSD_INNER + g * SSD_STATE
        c_off = SSD_INNER + SSD_GROUPS * SSD_STATE + g * SSD_STATE
        bm = xc[:, b_off:b_off + SSD_STATE]
        cm = xc[:, c_off:c_off + SSD_STATE]
        g0 = g * gw
        hn = h_ref[g0:g0 + gw, :] * cols[g0:g0 + gw, 1:2] + cols[g0:g0 + gw, 0:1] * bm
        hnew_ref[g0:g0 + gw, :] = hn
        t = (hn * cm).T
        y_parts.append(jnp.sum(t, axis=0, keepdims=True))
    y = jnp.concatenate(y_parts, axis=1) + dskip_ref[...] * xs
    y = y * _silu(z_ref[...])
    outs = []
    for g in range(SSD_GROUPS):
        yg = y[:, g * gw:(g + 1) * gw]
        ms = jnp.mean(yg * yg, axis=-1, keepdims=True)
        outs.append(yg * lax.rsqrt(ms + EPS) * ng_ref[:, g * gw:(g + 1) * gw])
    yn_ref[...] = jnp.concatenate(outs, axis=1)


def _step_sample(layer, z, xbc, dt_raw, state_conv, state_ssm,
                 conv_w, conv_b, dtb_full, alog_full, dskip_full, ssd_norm_g):
    bsz = z.shape[0]
    row3 = lambda w: pl.BlockSpec((None, 1, w), lambda b: (b, 0, 0))
    const = lambda a: pl.BlockSpec(a.shape, lambda b: (0,) * a.ndim)
    return pl.pallas_call(
        _step_kernel,
        out_shape=(
            jax.ShapeDtypeStruct((bsz, 1, SSD_INNER), F32),
            jax.ShapeDtypeStruct((bsz, SSD_CONV - 1, SSD_CONV_DIM), F32),
            jax.ShapeDtypeStruct((bsz, SSD_INNER, SSD_STATE), F32),
        ),
        grid=(bsz,),
        in_specs=[
            row3(SSD_INNER), row3(SSD_CONV_DIM), row3(LANES),
            pl.BlockSpec((None, None, SSD_CONV - 1, SSD_CONV_DIM), lambda b: (layer, b, 0, 0)),
            pl.BlockSpec((None, None, SSD_INNER, SSD_STATE), lambda b: (layer, b, 0, 0)),
            const(conv_w), const(conv_b), const(dtb_full), const(alog_full), const(dskip_full), const(ssd_norm_g),
        ],
        out_specs=(
            row3(SSD_INNER),
            pl.BlockSpec((None, SSD_CONV - 1, SSD_CONV_DIM), lambda b: (b, 0, 0)),
            pl.BlockSpec((None, SSD_INNER, SSD_STATE), lambda b: (b, 0, 0)),
        ),
        scratch_shapes=[pltpu.VMEM((LANES, SSD_INNER), F32)],
        compiler_params=_cparams(("arbitrary",)),
        name="step_sample",
    )(z.reshape(bsz, 1, -1), xbc.reshape(bsz, 1, -1), dt_raw.reshape(bsz, 1, -1), state_conv, state_ssm,
      conv_w, conv_b, dtb_full, alog_full, dskip_full, ssd_norm_g)


_HEAD_SPLIT = 2
_HROWS = ATT_OUT // _HEAD_SPLIT


def _cache_attn_kernel(*refs, n_alias):
    qkv_ref = refs[0]
    cache_refs = refs[1:7]
    out_refs = refs[7 + n_alias:13 + n_alias]
    o_ref = refs[13 + n_alias]
    b = pl.program_id(0)
    nh = _HROWS // ATT_HEAD_DIM
    scale = ATT_HEAD_DIM ** -0.5
    qkv = qkv_ref[...]
    lane_b = lax.broadcasted_iota(jnp.int32, qkv.shape, 2)
    cols = jnp.sum(jnp.where(lane_b == b, qkv, 0.0), axis=-1, keepdims=True)

    def per_head_rows(v):
        return jnp.concatenate([jnp.broadcast_to(v[h:h + 1, :], (ATT_HEAD_DIM, 1)) for h in range(nh)], axis=0)

    o_g, lse_g = [], []
    for gi, (_, dil) in enumerate(ATT_PATTERNS):
        k_ref, v_ref = cache_refs[2 * gi], cache_refs[2 * gi + 1]
        ko_ref, vo_ref = out_refs[2 * gi], out_refs[2 * gi + 1]
        q = cols[gi]
        k_new = cols[ATT_GROUPS + gi]
        v_new = cols[2 * ATT_GROUPS + gi]
        kk = k_ref[...]
        vv = v_ref[...]
        length = kk.shape[1]
        lane = lax.broadcasted_iota(jnp.int32, (nh, length), 1)
        s = jnp.sum((kk * q).reshape(nh, ATT_HEAD_DIM, length), axis=1) * scale
        s = jnp.where((lane & (dil - 1)) == 0, s, -jnp.inf)
        s_new = jnp.sum((k_new * q).reshape(nh, ATT_HEAD_DIM, 1), axis=1) * scale
        m = jnp.maximum(jnp.max(s, axis=-1, keepdims=True), s_new)
        e = jnp.exp(s - m)
        e_new = jnp.exp(s_new - m)
        den = jnp.sum(e, axis=-1, keepdims=True) + e_new
        acc = jnp.sum(vv.reshape(nh, ATT_HEAD_DIM, length) * e[:, None, :], axis=-1, keepdims=True)
        acc = acc.reshape(_HROWS, 1) + per_head_rows(e_new) * v_new
        o_g.append(acc / per_head_rows(den))
        lse_g.append(per_head_rows(m + jnp.log(den)))
        lane_full = lax.broadcasted_iota(jnp.int32, kk.shape, 1)
        last = lane_full == length - 1
        ko_ref[...] = jnp.where(last, k_new, pltpu.roll(kk, length - 1, axis=1))
        vo_ref[...] = jnp.where(last, v_new, pltpu.roll(vv, length - 1, axis=1))
    mx = jnp.maximum(jnp.maximum(lse_g[0], lse_g[1]), lse_g[2])
    w = [jnp.exp(l - mx) for l in lse_g]
    tot = w[0] + w[1] + w[2]
    o_ref[...] = (w[0] / tot) * o_g[0] + (w[1] / tot) * o_g[1] + (w[2] / tot) * o_g[2]


def _cache_attn(layer, qkv_t, cache_views, prev_outs):
    depth, bsz, _, _ = cache_views[0].shape
    n_alias = 0 if prev_outs is None else len(prev_outs)
    q4 = qkv_t.reshape(3 * ATT_GROUPS, _HEAD_SPLIT, _HROWS, bsz)
    blk = lambda c: pl.BlockSpec((None, None, _HROWS, c.shape[3]), lambda b, hh: (layer, b, hh, 0))
    any_spec = pl.BlockSpec(memory_space=pl.ANY)
    args = [q4] + list(cache_views) + ([] if prev_outs is None else list(prev_outs))
    res = pl.pallas_call(
        functools.partial(_cache_attn_kernel, n_alias=n_alias),
        out_shape=tuple(jax.ShapeDtypeStruct(c.shape, c.dtype) for c in cache_views)
        + (jax.ShapeDtypeStruct((bsz, _HEAD_SPLIT, _HROWS, 1), F32),),
        grid=(bsz, _HEAD_SPLIT),
        in_specs=[pl.BlockSpec((3 * ATT_GROUPS, None, _HROWS, bsz), lambda b, hh: (0, hh, 0, 0))]
        + [blk(c) for c in cache_views] + [any_spec] * n_alias,
        out_specs=tuple(blk(c) for c in cache_views)
        + (pl.BlockSpec((None, None, _HROWS, 1), lambda b, hh: (b, hh, 0, 0)),),
        input_output_aliases={7 + i: i for i in range(n_alias)},
        compiler_params=_cparams(("arbitrary", "arbitrary")),
        name="cache_attn",
    )(*args)
    return list(res[:6]), res[6].reshape(bsz, ATT_OUT)


def _pad_lanes(v, width=LANES):
    return jnp.pad(v, [(0, 0)] * (v.ndim - 1) + [(0, width - v.shape[-1])])


def kernel(x_prompt, x_sample, c_prompt, c_sample, state_conv, state_ssm, cache_k_win128, cache_v_win128, cache_k_win512, cache_v_win512, cache_k_win2048, cache_v_win2048, norm1_g, w_ada, b_ada, w_in, conv_w, conv_b, dt_bias, a_log, d_skip, ssd_norm_g, w_ssd_proj, w_attn_proj, w_out, norm2_g, w_router_group, b_router_group, w_router_expert, b_router_expert, w_exp_gate, w_exp_up, w_exp_down, final_norm_g):
    depth = w_in.shape[0]
    bp, seq, d = x_prompt.shape
    bs = x_sample.shape[0]
    assert x_sample.shape[1] == 1 and d == D_MODEL and w_in.shape[2] == IN_WIDTH
    assert seq % (ATT_PATTERNS[-1][1] * ATT_BLOCK) == 0
    n_p = bp * seq
    caches = ((cache_k_win128, cache_v_win128), (cache_k_win512, cache_v_win512),
              (cache_k_win2048, cache_v_win2048))

    rows = bp + bs
    rows_pad = -(-rows // 8) * 8
    c_all = jnp.pad(jnp.concatenate([c_prompt, c_sample], axis=0), ((0, rows_pad - rows), (0, 0)))
    mods = _modulation(c_all, w_ada, b_ada)

    xp = x_prompt.reshape(n_p, d)
    xs = x_sample.reshape(bs, d)
    outs = {k: [] for k in ("conv_p", "conv_s", "ssm_p", "ssm_s")}
    kv_p = [[[], []] for _ in ATT_PATTERNS]
    tiles = seq // TOK_TILE
    assert bs <= LANES and all(c.shape[2] == win for (win, _), pair in zip(ATT_PATTERNS, caches) for c in pair)
    cache_views = [jnp.transpose(c, (0, 1, 3, 4, 2)).reshape(depth, bs, ATT_OUT, c.shape[2])
                   for pair in caches for c in pair]
    shifted = None
    wg_bf, wu_bf, wd_bf = w_exp_gate.astype(BF16), w_exp_up.astype(BF16), w_exp_down.astype(BF16)

    for l in range(depth):
        mod_p = mods[l, :bp]
        mod_s = mods[l, bp:bp + bs]
        w_in_l = w_in[l]
        wz = w_in_l[:, 0:OFF_XBC].astype(BF16)
        wx = w_in_l[:, OFF_XBC:OFF_DT].astype(BF16)
        wd = _pad_lanes(w_in_l[:, OFF_DT:OFF_QKV]).astype(BF16)
        wqs = [jnp.concatenate([w_in_l[:, OFF_QKV + (which * ATT_GROUPS + gi) * ATT_OUT:
                                          OFF_QKV + (which * ATT_GROUPS + gi + 1) * ATT_OUT]
                                for which in range(3)], axis=1).astype(BF16) for gi in range(ATT_GROUPS)]
        wgt = w_in_l[:, OFF_GATE:IN_WIDTH].astype(BF16)
        g1n = norm1_g[l].reshape(1, d)
        g2n = norm2_g[l].reshape(1, d)
        cw = conv_w[l]
        cb = conv_b[l].reshape(1, -1)
        dskip_full = jnp.repeat(d_skip[l], SSD_HEAD_DIM).reshape(1, -1)
        ssd_g = ssd_norm_g[l].reshape(1, -1)
        w_router = _pad_lanes(jnp.concatenate([w_router_expert[l], w_router_group[l]], axis=1))
        b_router = _pad_lanes(jnp.concatenate([b_router_expert[l], b_router_group[l]], axis=0).reshape(1, -1))

        z, xbc, dt_raw, a0, a1, a2, gates = _in_proj_prompt(
            xp, mod_p[:, 0:2 * d].reshape(bp, 1, 2 * d), g1n, wz, wx, wd, wqs, wgt, bp, seq)
        yn, ssm_new = _ssd_prompt(xbc, z, dt_raw, cw, cb, _pad_lanes(dt_bias[l].reshape(1, -1)),
                                  _pad_lanes(a_log[l].reshape(1, -1)), dskip_full, ssd_g, bp, seq)
        outs["ssm_p"].append(ssm_new.reshape(bp, SSD_GROUPS, SSD_HPG, SSD_HEAD_DIM, SSD_STATE))
        outs["conv_p"].append(xbc.reshape(bp, seq, SSD_CONV_DIM)[:, seq - (SSD_CONV - 1):])
        a_groups = [a0.reshape(bp, tiles, 1, TOK_TILE, 3 * ATT_OUT), a1, a2]
        o_list, lse_list = [], []
        for gi, (win, dil) in enumerate(ATT_PATTERNS):
            o_g, lse_g = _attn_prompt(a_groups[gi], gi, bp, seq)
            o_list.append(o_g)
            lse_list.append(lse_g)
            keep = min(win, seq)
            rows = TOK_TILE // dil
            if keep >= TOK_TILE:
                nt = keep // TOK_TILE
                tail = a_groups[gi][:, tiles - nt:, :, :, ATT_OUT:3 * ATT_OUT]
                tail = jnp.transpose(tail, (0, 1, 3, 2, 4)).reshape(bp, keep, 2 * ATT_OUT)
            else:
                assert dil == 1
                tail = a_groups[gi][:, tiles - 1, 0, TOK_TILE - keep:, ATT_OUT:3 * ATT_OUT]
            tail = tail.astype(F32)
            kv_p[gi][0].append(tail[:, :, 0:ATT_OUT].reshape(bp, keep, ATT_HEADS, ATT_HEAD_DIM))
            kv_p[gi][1].append(tail[:, :, ATT_OUT:2 * ATT_OUT].reshape(bp, keep, ATT_HEADS, ATT_HEAD_DIM))
        mod_post = jnp.concatenate([mod_p[:, 2 * d:3 * d], mod_p[:, 3 * d:5 * d]], axis=1).reshape(bp, 1, 3 * d)
        x1, h2, gate = _post(xp, yn, (o_list, lse_list), gates, mod_post, g2n,
                             w_ssd_proj[l].astype(BF16), w_attn_proj[l].astype(BF16), w_out[l].astype(BF16),
                             w_router, b_router, passes=1, merge_attn=True, tm=TOK_TILE, rows_per_mod=seq,
                             h2_dtype=BF16)
        xp = _moe(l, h2, gate, x1, mod_p[:, 5 * d:6 * d].reshape(bp, 1, d), wg_bf, wu_bf, wd_bf,
                  passes=1, tm=1024, rows_per_mod=seq)

        u = _in_proj_sample(xs, mod_s[:, 0:2 * d], g1n, w_in_l)
        z_s = u[:, 0:OFF_XBC]
        xbc_s = u[:, OFF_XBC:OFF_DT]
        dt_s = _pad_lanes(u[:, OFF_DT:OFF_QKV])
        qkv_s = u[:, OFF_QKV:OFF_GATE]
        gates_s = u[:, OFF_GATE:IN_WIDTH]
        yn_s, conv_new, ssm_new_s = _step_sample(
            l, z_s, xbc_s, dt_s, state_conv, state_ssm.reshape(depth, bs, SSD_INNER, SSD_STATE),
            cw, cb, jnp.repeat(dt_bias[l], SSD_HEAD_DIM).reshape(1, -1),
            jnp.repeat(a_log[l], SSD_HEAD_DIM).reshape(1, -1), dskip_full, ssd_g)
        outs["conv_s"].append(conv_new)
        outs["ssm_s"].append(ssm_new_s.reshape(bs, SSD_GROUPS, SSD_HPG, SSD_HEAD_DIM, SSD_STATE))
        shifted, o_s = _cache_attn(l, qkv_s.T, cache_views, shifted)
        mod_post_s = jnp.concatenate([mod_s[:, 2 * d:3 * d], mod_s[:, 3 * d:5 * d]], axis=1)
        x1_s, h2_s, gate_s = _post(xs, yn_s.reshape(bs, SSD_INNER), o_s, gates_s, mod_post_s,
                                   g2n, w_ssd_proj[l], w_attn_proj[l], w_out[l], w_router, b_router,
                                   passes=3, merge_attn=False, tm=bs, rows_per_mod=1, h2_dtype=F32)
        xs = _moe(l, h2_s, gate_s, x1_s, mod_s[:, 5 * d:6 * d], w_exp_gate, w_exp_up, w_exp_down,
                  passes=3, tm=bs, rows_per_mod=1)

    fg = final_norm_g.reshape(1, d)
    y_prompt = _final_norm(xp, fg, 1024).reshape(bp, seq, d)
    y_sample = _final_norm(xs, fg, bs).reshape(bs, 1, d)

    shifted = [jnp.transpose(s.reshape(depth, bs, ATT_HEADS, ATT_HEAD_DIM, s.shape[3]), (0, 1, 4, 2, 3))
               for s in shifted]

    st = jnp.stack
    res = [y_prompt, y_sample, st(outs["conv_p"]), st(outs["conv_s"]), st(outs["ssm_p"]), st(outs["ssm_s"])]
    for gi in range(ATT_GROUPS):
        res += [st(kv_p[gi][0]), shifted[2 * gi], st(kv_p[gi][1]), shifted[2 * gi + 1]]
    return tuple(res)
```

```python
import functools
import math

import jax
import jax.numpy as jnp
from jax import lax
from jax.experimental import pallas as pl
from jax.experimental.pallas import tpu as pltpu

F32 = jnp.float32
BF16 = jnp.bfloat16

D_MODEL = 1024
SSD_INNER = 1024
SSD_HEAD_DIM = 64
SSD_HEADS = 16
SSD_GROUPS = 2
SSD_HPG = 8
SSD_STATE = 128
SSD_CONV = 4
SSD_CHUNK = 128
SSD_CONV_DIM = SSD_INNER + 2 * SSD_GROUPS * SSD_STATE
ATT_PATTERNS = ((128, 1), (512, 4), (2048, 16))
ATT_GROUPS = 3
ATT_HEADS = 8
ATT_HEAD_DIM = 64
ATT_BLOCK = 128
ATT_OUT = ATT_HEADS * ATT_HEAD_DIM
MOE_GROUPS = 4
MOE_PER_GROUP = 4
MOE_EXPERTS = 16
MOE_FF = 512
MOE_PAIRS = 6
MOE_CLASSES = MOE_GROUPS * MOE_PAIRS
MOE_TILE = 256
ROW_SUB = 8
IN_QKV = 3 * ATT_GROUPS * ATT_OUT
IN_GATE = 2 * D_MODEL
OFF_XBC = SSD_INNER
OFF_DT = OFF_XBC + SSD_CONV_DIM
OFF_QKV = OFF_DT + SSD_HEADS
OFF_GATE = OFF_QKV + IN_QKV
IN_WIDTH = OFF_GATE + IN_GATE
EPS = 1e-6

LANES = 128
VMEM_LIMIT = 56 * 1024 * 1024


def _cparams(sem, vmem=VMEM_LIMIT):
    return pltpu.CompilerParams(dimension_semantics=sem, vmem_limit_bytes=vmem)


def _split2(a):
    hi = a.astype(BF16)
    lo = (a - hi.astype(F32)).astype(BF16)
    return hi, lo


def _split3(a):
    hi = a.astype(BF16)
    r = a - hi.astype(F32)
    mid = r.astype(BF16)
    lo = (r - mid.astype(F32)).astype(BF16)
    return hi, mid, lo


def _dot(a, b):
    return jnp.dot(a, b, preferred_element_type=F32)


def _dot_nt(a, b):
    return lax.dot_general(a, b, (((1,), (1,)), ((), ())), preferred_element_type=F32)


def _mm(a, w, passes):
    if passes == 1:
        return _dot(a.astype(BF16), w.astype(BF16))
    a = a.astype(F32)
    w = w.astype(F32)
    a_hi, a_lo = _split2(a)
    w_hi, w_lo = _split2(w)
    return _dot(a_hi, w_hi) + (_dot(a_lo, w_hi) + _dot(a_hi, w_lo))


def _mm01(a, e01, terms):
    parts = _split3(a) if terms == 3 else _split2(a)
    out = _dot(parts[0], e01)
    for p in parts[1:]:
        out = out + _dot(p, e01)
    return out


def _sigmoid(x):
    return 1.0 / (1.0 + jnp.exp(-x))


def _silu(x):
    return x * _sigmoid(x)


def _softplus(x):
    return jnp.maximum(x, 0.0) + jnp.log(1.0 + jnp.exp(-jnp.abs(x)))


def _head_expand_matrix(rows, n_heads, width):
    r = lax.broadcasted_iota(jnp.int32, (rows, n_heads * width), 0)
    c = lax.broadcasted_iota(jnp.int32, (rows, n_heads * width), 1)
    return jnp.where((c // width) == r, 1.0, 0.0).astype(BF16)


def _rms_modulate(x, g, sh, sc):
    ms = jnp.mean(x * x, axis=-1, keepdims=True)
    y = x * lax.rsqrt(ms + EPS) * g
    return y * (1.0 + sc) + sh


def _mod_kernel(c_ref, w_ref, b_ref, o_ref):
    a = _silu(c_ref[...])
    o_ref[...] = _mm(a, w_ref[...], 3) + b_ref[...]


def _modulation(c_all, w_ada, b_ada):
    depth, d, n6 = w_ada.shape
    rows = c_all.shape[0]
    tn = 1024
    return pl.pallas_call(
        _mod_kernel,
        out_shape=jax.ShapeDtypeStruct((depth, rows, n6), F32),
        grid=(depth, n6 // tn),
        in_specs=[
            pl.BlockSpec((rows, d), lambda l, j: (0, 0)),
            pl.BlockSpec((None, d, tn), lambda l, j: (l, 0, j)),
            pl.BlockSpec((None, 1, tn), lambda l, j: (l, 0, j)),
        ],
        out_specs=pl.BlockSpec((None, rows, tn), lambda l, j: (l, 0, j)),
        compiler_params=_cparams(("arbitrary", "arbitrary")),
        name="adaln_mod",
    )(c_all, w_ada, b_ada.reshape(depth, 1, n6))


_IN_CHUNK = 512
TOK_TILE = 512


def _in_kernel(x_ref, mod_ref, g_ref, wz_ref, wx_ref, wd_ref, wq0_ref, wq1_ref, wq2_ref, wg_ref,
               z_ref, xbc_ref, dt_ref, a0_ref, a1_ref, a2_ref, gates_ref, h_scr):
    sh = mod_ref[:, 0:D_MODEL]
    sc = mod_ref[:, D_MODEL:2 * D_MODEL]
    hf = _rms_modulate(x_ref[...], g_ref[...], sh, sc)
    n_cb = D_MODEL // LANES
    for cbk in range(n_cb):
        h_scr[cbk] = hf[:, cbk * LANES:(cbk + 1) * LANES]
    h = hf.astype(BF16)
    for w_ref, o_ref in ((wz_ref, z_ref), (wx_ref, xbc_ref), (wq0_ref, a0_ref), (wg_ref, gates_ref)):
        width = o_ref.shape[-1]
        for c0 in range(0, width, _IN_CHUNK):
            o_ref[:, c0:c0 + _IN_CHUNK] = _dot(h, w_ref[:, c0:c0 + _IN_CHUNK]).astype(o_ref.dtype)
    dt_ref[...] = _dot(h, wd_ref[...])
    for w_ref, a_ref in ((wq1_ref, a1_ref), (wq2_ref, a2_ref)):
        d, r_len, width = a_ref.shape
        hp = jnp.concatenate(
            [jnp.concatenate([h_scr[cbk, pl.ds(r, r_len, stride=d), :] for cbk in range(n_cb)], axis=1)
             for r in range(d)], axis=0).astype(BF16)
        for c0 in range(0, width, _IN_CHUNK):
            res = _dot(hp, w_ref[:, c0:c0 + _IN_CHUNK]).astype(BF16)
            a_ref[:, :, c0:c0 + _IN_CHUNK] = res.reshape(d, r_len, _IN_CHUNK)


def _in_proj_prompt(x, mod, norm_g, wz, wx, wd, wqs, wg, bsz, seq):
    n, d = x.shape
    tm = TOK_TILE
    tiles_per_seq = seq // tm
    qw = 3 * ATT_OUT

    def const(shape):
        return pl.BlockSpec(shape, lambda i: (0, 0), pipeline_mode=pl.Buffered(1))

    a_shapes, a_specs = [], []
    for _, dil in ATT_PATTERNS[1:]:
        a_shapes.append(jax.ShapeDtypeStruct((bsz, tiles_per_seq, dil, tm // dil, qw), BF16))
        a_specs.append(pl.BlockSpec((None, None, dil, tm // dil, qw),
                                    lambda i: (i // tiles_per_seq, i % tiles_per_seq, 0, 0, 0)))
    return pl.pallas_call(
        _in_kernel,
        out_shape=(
            jax.ShapeDtypeStruct((n, SSD_INNER), BF16),
            jax.ShapeDtypeStruct((n, SSD_CONV_DIM), F32),
            jax.ShapeDtypeStruct((n, LANES), F32),
            jax.ShapeDtypeStruct((n, qw), BF16),
            a_shapes[0], a_shapes[1],
            jax.ShapeDtypeStruct((n, IN_GATE), BF16),
        ),
        grid=(n // tm,),
        in_specs=[
            pl.BlockSpec((tm, d), lambda i: (i, 0)),
            pl.BlockSpec((None, 1, 2 * d), lambda i: (i // tiles_per_seq, 0, 0)),
            const((1, d)),
            const(wz.shape), const(wx.shape), const(wd.shape),
            const(wqs[0].shape), const(wqs[1].shape), const(wqs[2].shape), const(wg.shape),
        ],
        out_specs=(
            pl.BlockSpec((tm, SSD_INNER), lambda i: (i, 0)),
            pl.BlockSpec((tm, SSD_CONV_DIM), lambda i: (i, 0)),
            pl.BlockSpec((tm, LANES), lambda i: (i, 0)),
            pl.BlockSpec((tm, qw), lambda i: (i, 0)),
            a_specs[0], a_specs[1],
            pl.BlockSpec((tm, IN_GATE), lambda i: (i, 0)),
        ),
        scratch_shapes=[pltpu.VMEM((d // LANES, tm, LANES), F32)],
        compiler_params=_cparams(("arbitrary",)),
        name="in_proj_prompt",
    )(x, mod, norm_g, wz, wx, wd, wqs[0], wqs[1], wqs[2], wg)


def _in_small_kernel(x_ref, mod_ref, g_ref, w_ref, o_ref):
    sh = mod_ref[:, 0:D_MODEL]
    sc = mod_ref[:, D_MODEL:2 * D_MODEL]
    h = _rms_modulate(x_ref[...], g_ref[...], sh, sc)
    o_ref[...] = _mm(h, w_ref[...], 3)


def _in_proj_sample(x, mod, norm_g, w_in_l, tn=1024):
    m, d = x.shape
    width = w_in_l.shape[1]
    return pl.pallas_call(
        _in_small_kernel,
        out_shape=jax.ShapeDtypeStruct((m, width), F32),
        grid=(pl.cdiv(width, tn),),
        in_specs=[
            pl.BlockSpec((m, d), lambda j: (0, 0)),
            pl.BlockSpec((m, 2 * d), lambda j: (0, 0)),
            pl.BlockSpec((1, d), lambda j: (0, 0)),
            pl.BlockSpec((d, tn), lambda j: (0, j)),
        ],
        out_specs=pl.BlockSpec((m, tn), lambda j: (0, j)),
        compiler_params=_cparams(("arbitrary",)),
        name="in_proj_sample",
    )(x, mod, norm_g, w_in_l)


def _ssd_kernel(xbc_ref, z_ref, dt_ref, cw_ref, cb_ref, dtb_ref, alog_ref, dskip_ref, ng_ref,
                yn_ref, hout_ref, h_scr, xp_scr):
    q = SSD_CHUNK
    c = pl.program_id(1)

    @pl.when(c == 0)
    def _():
        h_scr[...] = jnp.zeros_like(h_scr)
        xp_scr[0:8, :] = jnp.zeros((8, SSD_CONV_DIM), F32)

    xp_scr[8:8 + q, :] = xbc_ref[...]
    acc = cb_ref[...] + cw_ref[3:4, :] * xp_scr[8:8 + q, :]
    for k in range(SSD_CONV - 1):
        acc = acc + cw_ref[k:k + 1, :] * xp_scr[5 + k:5 + k + q, :]
    xp_scr[0:8, :] = xp_scr[q:q + 8, :]
    xc = _silu(acc)
    xs = xc[:, 0:SSD_INNER]
    xs_bf = xs.astype(BF16)

    lane_q = lax.broadcasted_iota(jnp.int32, (q, LANES), 1)
    row_q = lax.broadcasted_iota(jnp.int32, (q, LANES), 0)
    causal = row_q >= lane_q
    tri = jnp.where(causal, 1.0, 0.0).astype(BF16)
    tri_t = jnp.where(lane_q >= row_q, 1.0, 0.0).astype(BF16)
    e_heads = _head_expand_matrix(LANES, SSD_HEADS, SSD_HEAD_DIM)

    dt = _softplus(dt_ref[...] + dtb_ref[...])
    dt = jnp.where(lane_q < SSD_HEADS, dt, 0.0)
    a = -jnp.exp(alog_ref[...])
    d_a = dt * a
    cum = _mm01_left(tri, d_a)
    d_a_t = d_a.T
    dt_t = dt.T
    cum_t = _mm01(d_a_t, tri_t, 3)
    cum_last = cum[q - 1:q, :]
    exp_cum = jnp.exp(cum)
    dec_end = jnp.exp(cum_last - cum)
    stack = jnp.concatenate([exp_cum, dec_end * dt], axis=0)
    full = _mm01(stack, e_heads, 2)
    exp_cum_full = full[0:q]
    w_full = full[q:2 * q]
    chunk_dec_t = jnp.exp(cum_t[:, q - 1:q])

    lane_half = lax.broadcasted_iota(jnp.int32, (q, LANES), 1) < SSD_HEAD_DIM
    y_parts = []
    for g in range(SSD_GROUPS):
        b_off = SSD_INNER + g * SSD_STATE
        c_off = SSD_INNER + SSD_GROUPS * SSD_STATE + g * SSD_STATE
        bm = xc[:, b_off:b_off + SSD_STATE].astype(BF16)
        cm = xc[:, c_off:c_off + SSD_STATE].astype(BF16)
        cbm = _dot_nt(cm, bm)
        gw = SSD_HPG * SSD_HEAD_DIM
        g0 = g * gw
        yd = []
        for pair in range(SSD_HPG // 2):
            x_pair = xs_bf[:, g0 + pair * LANES:g0 + (pair + 1) * LANES]
            halves = []
            for hh in range(2):
                h = g * SSD_HPG + pair * 2 + hh
                seg = cum[:, h:h + 1] - cum_t[h:h + 1, :]
                dec = jnp.where(causal, jnp.exp(jnp.where(causal, seg, 0.0)), 0.0)
                m_h = (cbm * dec * dt_t[h:h + 1, :]).astype(BF16)
                halves.append(_dot(m_h, x_pair))
            yd.append(jnp.where(lane_half, halves[0], halves[1]))
        y_diag = jnp.concatenate(yd, axis=1)
        h_g = h_scr[g0:g0 + gw, :]
        y_off = _dot_nt(cm, h_g.astype(BF16)) * exp_cum_full[:, g0:g0 + gw]
        y_parts.append(y_diag + y_off)
        xw = (xs[:, g0:g0 + gw] * w_full[:, g0:g0 + gw])
        st = _dot(xw.T.astype(BF16), bm)
        for e in range(SSD_HPG):
            h = g * SSD_HPG + e
            r0 = g0 + e * SSD_HEAD_DIM
            h_scr[r0:r0 + SSD_HEAD_DIM, :] = (h_scr[r0:r0 + SSD_HEAD_DIM, :] * chunk_dec_t[h:h + 1, :]
                                              + st[e * SSD_HEAD_DIM:(e + 1) * SSD_HEAD_DIM, :])

    y = jnp.concatenate(y_parts, axis=1) + dskip_ref[...] * xs
    y = y * _silu(z_ref[...].astype(F32))
    outs = []
    for g in range(SSD_GROUPS):
        gw = SSD_HPG * SSD_HEAD_DIM
        yg = y[:, g * gw:(g + 1) * gw]
        ms = jnp.mean(yg * yg, axis=-1, keepdims=True)
        outs.append(yg * lax.rsqrt(ms + EPS) * ng_ref[:, g * gw:(g + 1) * gw])
    yn_ref[...] = jnp.concatenate(outs, axis=1).astype(yn_ref.dtype)

    @pl.when(c == pl.num_programs(1) - 1)
    def _():
        hout_ref[...] = h_scr[...]


def _mm01_left(tri01, a):
    hi, mid, lo = _split3(a)
    return _dot(tri01, hi) + (_dot(tri01, mid) + _dot(tri01, lo))


def _ssd_prompt(xbc, z, dt_raw, conv_w, conv_b, dt_bias, a_log, d_skip_full, ssd_norm_g, bsz, seq):
    q = SSD_CHUNK
    nc = seq // q
    row = lambda b, c: (b * nc + c, 0)
    const = lambda b, c: (0, 0)
    return pl.pallas_call(
        _ssd_kernel,
        out_shape=(
            jax.ShapeDtypeStruct((bsz * seq, SSD_INNER), BF16),
            jax.ShapeDtypeStruct((bsz, SSD_INNER, SSD_STATE), F32),
        ),
        grid=(bsz, nc),
        in_specs=[
            pl.BlockSpec((q, SSD_CONV_DIM), row),
            pl.BlockSpec((q, SSD_INNER), row),
            pl.BlockSpec((q, LANES), row),
            pl.BlockSpec((SSD_CONV, SSD_CONV_DIM), const),
            pl.BlockSpec((1, SSD_CONV_DIM), const),
            pl.BlockSpec((1, LANES), const),
            pl.BlockSpec((1, LANES), const),
            pl.BlockSpec((1, SSD_INNER), const),
            pl.BlockSpec((1, SSD_INNER), const),
        ],
        out_specs=(
            pl.BlockSpec((q, SSD_INNER), row),
            pl.BlockSpec((None, SSD_INNER, SSD_STATE), lambda b, c: (b, 0, 0)),
        ),
        scratch_shapes=[
            pltpu.VMEM((SSD_INNER, SSD_STATE), F32),
            pltpu.VMEM((q + 8, SSD_CONV_DIM), F32),
        ],
        compiler_params=_cparams(("arbitrary", "arbitrary")),
        name="ssd_prompt",
    )(xbc, z, dt_raw, conv_w, conv_b, dt_bias, a_log, d_skip_full, ssd_norm_g)


def _attn_kernel(q_ref, kc_ref, kp_ref, vc_ref, vp_ref, o_out_ref, lse_out_ref,
                 q_ref_s, kwin, vwin, o_ref, lse_ref, *, tq, band):
    blk = ATT_BLOCK
    j = pl.program_id(2)
    q_ref_s[...] = q_ref[...].reshape(tq, ATT_OUT)
    kwin[0:blk, :] = kp_ref[...].reshape(blk, ATT_OUT)
    kwin[blk:blk + tq, :] = kc_ref[...].reshape(tq, ATT_OUT)
    vwin[0:blk, :] = vp_ref[...].reshape(blk, ATT_OUT)
    vwin[blk:blk + tq, :] = vc_ref[...].reshape(tq, ATT_OUT)
    q_ref = q_ref_s

    qi = lax.broadcasted_iota(jnp.int32, (blk, 2 * blk), 0)
    ki = lax.broadcasted_iota(jnp.int32, (blk, 2 * blk), 1)
    dist = qi + blk - ki
    in_band = (dist >= 0) & (dist <= band)
    lane = lax.broadcasted_iota(jnp.int32, (blk, LANES), 1)
    lane_half = lane < ATT_HEAD_DIM
    scale = ATT_HEAD_DIM ** -0.5
    zero_bf = jnp.zeros((blk, LANES), BF16)

    for i in range(tq // blk):
        if i == 0:
            valid = in_band & ((ki >= blk) | (j > 0))
        else:
            valid = in_band
        lse_tile = jnp.zeros((blk, LANES), F32)
        for hp in range(ATT_HEADS // 2):
            c0 = hp * LANES
            q_pair = q_ref[i * blk:(i + 1) * blk, c0:c0 + LANES]
            k_pair = kwin[i * blk:(i + 2) * blk, c0:c0 + LANES]
            v_pair = vwin[i * blk:(i + 2) * blk, c0:c0 + LANES]
            halves = []
            for hh in range(2):
                q_m = jnp.where(lane_half if hh == 0 else jnp.logical_not(lane_half), q_pair, zero_bf)
                s = _dot_nt(q_m, k_pair) * scale
                s = jnp.where(valid, s, -jnp.inf)
                m = jnp.max(s, axis=-1, keepdims=True)
                e = jnp.exp(s - m)
                den = jnp.sum(e, axis=-1, keepdims=True)
                pv = _dot(e.astype(BF16), v_pair)
                halves.append(pv / den)
                lse = m + jnp.log(den)
                head = hp * 2 + hh
                lse_tile = jnp.where((lane == head) | (lane == ATT_HEADS + head), lse, lse_tile)
            o_ref[i * blk:(i + 1) * blk, c0:c0 + LANES] = jnp.where(lane_half, halves[0], halves[1]).astype(o_ref.dtype)
        lse_ref[i * blk:(i + 1) * blk, :] = lse_tile
    o_out_ref[...] = o_ref[...].reshape(o_out_ref.shape)
    lse_out_ref[...] = lse_ref[...].reshape(lse_out_ref.shape)


def _attn_prompt(a_g, gi, bsz, seq):
    win, dil = ATT_PATTERNS[gi]
    band = win // dil
    blk = ATT_BLOCK
    rows = TOK_TILE // dil
    tiles = seq // TOK_TILE
    length = seq // dil
    tq = min(512, length)
    nj = length // tq
    tq_tiles = tq // rows
    cur = lambda which: pl.BlockSpec((None, tq_tiles, None, rows, ATT_OUT),
                                     lambda b, r, j: (b, j, r, 0, which))
    if rows >= blk:
        prev = lambda which: pl.BlockSpec(
            (None, None, None, blk, ATT_OUT),
            lambda b, r, j: (b, jnp.maximum(j * tq_tiles - 1, 0), r, rows // blk - 1, which))
    else:
        prev = lambda which: pl.BlockSpec(
            (None, blk // rows, None, rows, ATT_OUT),
            lambda b, r, j: (b, jnp.maximum(j * (tq // blk) - 1, 0), r, 0, which))
    return pl.pallas_call(
        functools.partial(_attn_kernel, tq=tq, band=band),
        out_shape=(
            jax.ShapeDtypeStruct((bsz, tiles, dil, rows, ATT_OUT), BF16),
            jax.ShapeDtypeStruct((bsz, tiles, dil, rows, LANES), F32),
        ),
        grid=(bsz, dil, nj),
        in_specs=[cur(0), cur(1), prev(1), cur(2), prev(2)],
        out_specs=(
            pl.BlockSpec((None, tq_tiles, None, rows, ATT_OUT), lambda b, r, j: (b, j, r, 0, 0)),
            pl.BlockSpec((None, tq_tiles, None, rows, LANES), lambda b, r, j: (b, j, r, 0, 0)),
        ),
        scratch_shapes=[
            pltpu.VMEM((tq, ATT_OUT), BF16),
            pltpu.VMEM((blk + tq, ATT_OUT), BF16),
            pltpu.VMEM((blk + tq, ATT_OUT), BF16),
            pltpu.VMEM((tq, ATT_OUT), BF16),
            pltpu.VMEM((tq, LANES), F32),
        ],
        compiler_params=_cparams(("arbitrary", "arbitrary", "arbitrary")),
        name=f"attn_prompt_w{win}",
    )(a_g, a_g, a_g, a_g, a_g)


def _router_gates(logits):
    shape = logits.shape
    lane = lax.broadcasted_iota(jnp.int32, shape, 1)
    big = jnp.int32(1 << 20)
    neg = -jnp.inf
    is_grp = (lane >= MOE_EXPERTS) & (lane < MOE_EXPERTS + MOE_GROUPS)
    lg = jnp.where(is_grp, logits, neg)
    gm = jnp.max(lg, axis=-1, keepdims=True)
    g_lane = jnp.min(jnp.where(lg == gm, lane, big), axis=-1, keepdims=True)
    g_sum = jnp.sum(jnp.exp(lg - gm), axis=-1, keepdims=True)
    g_w = 1.0 / g_sum
    lo = (g_lane - MOE_EXPERTS) * MOE_PER_GROUP
    in_grp = (lane >= lo) & (lane < lo + MOE_PER_GROUP)
    le = jnp.where(in_grp, logits, neg)
    m1 = jnp.max(le, axis=-1, keepdims=True)
    i1 = jnp.min(jnp.where(le == m1, lane, big), axis=-1, keepdims=True)
    le2 = jnp.where(lane == i1, neg, le)
    m2 = jnp.max(le2, axis=-1, keepdims=True)
    i2 = jnp.min(jnp.where(le2 == m2, lane, big), axis=-1, keepdims=True)
    t = jnp.exp(m2 - m1)
    w1 = 1.0 / (1.0 + t)
    w2 = t / (1.0 + t)
    gate = jnp.where(lane == i1, g_w * w1, jnp.where(lane == i2, g_w * w2, 0.0))
    e_lo = jnp.minimum(i1, i2)
    e_hi = jnp.maximum(i1, i2)
    lo_l = e_lo & (MOE_PER_GROUP - 1)
    hi_l = e_hi & (MOE_PER_GROUP - 1)
    cls = (e_lo >> 2) * MOE_PAIRS + ((lo_l * (7 - lo_l)) >> 1) + (hi_l - lo_l - 1)
    return gate, cls


def _post_kernel(*refs, passes, merge_attn):
    if merge_attn:
        (x_ref, yn_ref, o0_ref, o1_ref, o2_ref, l0_ref, l1_ref, l2_ref, gates_ref, mod_ref, n2_ref,
         wssd_ref, wattn_ref, wout_ref, wr_ref, br_ref, x1_ref, h2_ref, pos_ref, cnt_ref,
         o1_scr, o2_scr, l1_scr, l2_scr) = refs
        for src_ref, dst_ref in ((o1_ref, o1_scr), (o2_ref, o2_scr), (l1_ref, l1_scr), (l2_ref, l2_scr)):
            dil, r_len, width = src_ref.shape
            for r in range(dil):
                blk_r = src_ref[r].astype(F32)
                for cbk in range(width // LANES):
                    dst_ref[cbk, pl.ds(r, r_len, stride=dil), :] = blk_r[:, cbk * LANES:(cbk + 1) * LANES]
        o_nat = [o0_ref[...].astype(F32)] + [
            jnp.concatenate([scr[cbk] for cbk in range(ATT_OUT // LANES)], axis=1) for scr in (o1_scr, o2_scr)]
        l0, l1, l2 = l0_ref[...], l1_scr[0], l2_scr[0]
        mx = jnp.maximum(jnp.maximum(l0, l1), l2)
        e0, e1, e2 = jnp.exp(l0 - mx), jnp.exp(l1 - mx), jnp.exp(l2 - mx)
        inv = 1.0 / (e0 + e1 + e2)
        lane = lax.broadcasted_iota(jnp.int32, l0.shape, 1)
        e8 = _head_expand_matrix(LANES, ATT_HEADS, ATT_HEAD_DIM)
        r = lax.broadcasted_iota(jnp.int32, e8.shape, 0)
        c = lax.broadcasted_iota(jnp.int32, e8.shape, 1)
        e8 = jnp.where((c // ATT_HEAD_DIM) == (r - ATT_HEADS), 1.0, e8.astype(F32)).astype(BF16)
        o = None
        for e_g, o_g in zip((e0, e1, e2), o_nat):
            w = e_g * inv
            hi, lo = _split2(w)
            w_exp = _dot(jnp.where(lane < ATT_HEADS, hi, lo), e8)
            term = w_exp * o_g
            o = term if o is None else o + term
    else:
        (x_ref, yn_ref, o_ref, gates_ref, mod_ref, n2_ref,
         wssd_ref, wattn_ref, wout_ref, wr_ref, br_ref, x1_ref, h2_ref, gate_ref) = refs
        o = o_ref[...]
    d = D_MODEL
    g1 = mod_ref[:, 0:d]
    sh2 = mod_ref[:, d:2 * d]
    sc2 = mod_ref[:, 2 * d:3 * d]
    ssd_branch = _mm(yn_ref[...], wssd_ref[...], passes)
    attn_branch = _mm(o, wattn_ref[...], passes)
    ga = gates_ref[:, 0:d].astype(F32)
    gb = gates_ref[:, d:2 * d].astype(F32)
    mixed = _sigmoid(ga) * ssd_branch + _sigmoid(gb) * attn_branch
    x1 = x_ref[...] + g1 * _mm(mixed, wout_ref[...], passes)
    x1_ref[...] = x1
    h2 = _rms_modulate(x1, n2_ref[...], sh2, sc2)
    logits = _mm(h2, wr_ref[...], 3) + br_ref[...]
    gate, cls = _router_gates(logits)
    if not merge_attn:
        h2_ref[...] = h2
        gate_ref[...] = gate
        return
    tm = h2.shape[0]
    for cbk in range(ROW_SUB):
        h2_ref[pl.ds(cbk, tm, stride=ROW_SUB), :] = h2[:, cbk * LANES:(cbk + 1) * LANES]
    step = pl.program_id(0)

    @pl.when(step == 0)
    def _():
        cnt_ref[...] = jnp.zeros_like(cnt_ref)

    lane = lax.broadcasted_iota(jnp.int32, (tm, LANES), 1)
    onehot = lane == cls
    ri = lax.broadcasted_iota(jnp.int32, (tm, tm), 0)
    ci = lax.broadcasted_iota(jnp.int32, (tm, tm), 1)
    before = jnp.where(ci < ri, 1.0, 0.0).astype(BF16)
    seen = _dot(before, jnp.where(onehot, 1.0, 0.0).astype(BF16)) + cnt_ref[...]
    rank = jnp.sum(jnp.where(onehot, seen, 0.0), axis=-1, keepdims=True)
    n_tokens = tm * pl.num_programs(0)
    pos_ref[...] = jnp.broadcast_to(cls.astype(F32) * n_tokens + rank, (tm, LANES))
    cnt_ref[...] = cnt_ref[...] + jnp.sum(jnp.where(onehot, 1.0, 0.0), axis=0, keepdims=True)


def _post(x, yn, attn_in, gates, mod, norm2_g, wssd, wattn, wout, wr, br, *, passes, merge_attn,
          tm, rows_per_mod, h2_dtype):
    n, d = x.shape
    per_row_mod = rows_per_mod == 1
    if per_row_mod:
        mod_spec = pl.BlockSpec((tm, 3 * d), lambda i: (i, 0))
    else:
        tiles = rows_per_mod // tm
        mod_spec = pl.BlockSpec((None, 1, 3 * d), lambda i: (i // tiles, 0, 0))
    row = lambda w: pl.BlockSpec((tm, w), lambda i: (i, 0))
    const = lambda a: pl.BlockSpec(a.shape, lambda i: (0, 0), pipeline_mode=pl.Buffered(1))
    scratch = []
    if merge_attn:
        assert tm == TOK_TILE
        o_list, lse_list = attn_in
        attn_args = list(o_list) + list(lse_list)
        tiles = rows_per_mod // tm

        def tile_spec(a):
            _, _, dil, r_len, w = a.shape
            if dil == 1:
                return pl.BlockSpec((None, None, None, r_len, w), lambda i: (i // tiles, i % tiles, 0, 0, 0))
            return pl.BlockSpec((None, None, dil, r_len, w), lambda i: (i // tiles, i % tiles, 0, 0, 0))

        attn_specs = [tile_spec(a) for a in attn_args]
        scratch = [pltpu.VMEM((ATT_OUT // LANES, tm, LANES), F32), pltpu.VMEM((ATT_OUT // LANES, tm, LANES), F32),
                   pltpu.VMEM((1, tm, LANES), F32), pltpu.VMEM((1, tm, LANES), F32)]
    else:
        attn_args = [attn_in]
        attn_specs = [row(ATT_OUT)]
    if merge_attn:
        out_shape = (jax.ShapeDtypeStruct((n, d), F32), jax.ShapeDtypeStruct((n * ROW_SUB, LANES), F32),
                     jax.ShapeDtypeStruct((n, LANES), F32), jax.ShapeDtypeStruct((1, LANES), F32))
        out_specs = (row(d), pl.BlockSpec((tm * ROW_SUB, LANES), lambda i: (i, 0)), row(LANES),
                     pl.BlockSpec((1, LANES), lambda i: (0, 0)))
    else:
        out_shape = (jax.ShapeDtypeStruct((n, d), F32), jax.ShapeDtypeStruct((n, d), h2_dtype),
                     jax.ShapeDtypeStruct((n, LANES), F32))
        out_specs = (row(d), row(d), row(LANES))
    return pl.pallas_call(
        functools.partial(_post_kernel, passes=passes, merge_attn=merge_attn),
        out_shape=out_shape,
        grid=(n // tm,),
        in_specs=[row(d), row(SSD_INNER)] + attn_specs + [row(IN_GATE), mod_spec, const(norm2_g),
                                                          const(wssd), const(wattn), const(wout),
                                                          const(wr), const(br)],
        out_specs=out_specs,
        scratch_shapes=scratch,
        compiler_params=_cparams(("arbitrary",)),
        name="post_merge" if merge_attn else "post_sample",
    )(x, yn, *attn_args, gates, mod, norm2_g, wssd, wattn, wout, wr, br)


def _moe_kernel(h2_ref, gate_ref, x1_ref, g2_ref, wg_ref, wu_ref, wd_ref, x2_ref, acc_ref, *, passes):
    e = pl.program_id(1)

    @pl.when(e == 0)
    def _():
        acc_ref[...] = jnp.zeros_like(acc_ref)

    h2 = h2_ref[...]
    hg = _mm(h2, wg_ref[...], passes)
    hu = _mm(h2, wu_ref[...], passes)
    gate = gate_ref[...]
    lane = lax.broadcasted_iota(jnp.int32, gate.shape, 1)
    gcol = jnp.sum(jnp.where(lane == e, gate, 0.0), axis=-1, keepdims=True)
    act = _silu(hg) * hu * gcol
    acc_ref[...] += _mm(act, wd_ref[...], passes)

    @pl.when(e == pl.num_programs(1) - 1)
    def _():
        x2_ref[...] = x1_ref[...] + g2_ref[...] * acc_ref[...]


def _moe(layer, h2, gate, x1, g2, wg, wu, wd, *, passes, tm, rows_per_mod):
    n, d = x1.shape
    if rows_per_mod == 1:
        g2_spec = pl.BlockSpec((tm, d), lambda i, e: (i, 0))
    else:
        tiles = rows_per_mod // tm
        g2_spec = pl.BlockSpec((None, 1, d), lambda i, e: (i // tiles, 0, 0))
    row = lambda w: pl.BlockSpec((tm, w), lambda i, e: (i, 0))
    return pl.pallas_call(
        functools.partial(_moe_kernel, passes=passes),
        out_shape=jax.ShapeDtypeStruct((n, d), F32),
        grid=(n // tm, MOE_EXPERTS),
        in_specs=[
            row(d), row(LANES), row(d), g2_spec,
            pl.BlockSpec((None, None, d, MOE_FF), lambda i, e: (layer, e, 0, 0)),
            pl.BlockSpec((None, None, d, MOE_FF), lambda i, e: (layer, e, 0, 0)),
            pl.BlockSpec((None, None, MOE_FF, d), lambda i, e: (layer, e, 0, 0)),
        ],
        out_specs=row(d),
        scratch_shapes=[pltpu.VMEM((tm, d), F32)],
        compiler_params=_cparams(("arbitrary", "arbitrary")),
        name="moe_dense",
    )(h2, gate, x1, g2, wg, wu, wd)


_PAIR_LO = (0, 0, 0, 1, 1, 2)
_PAIR_HI = (1, 2, 3, 2, 3, 3)


def _moe_plan(counts, n_tokens):
    tm = MOE_TILE
    blocks_per_class = n_tokens // tm
    ntile = (counts + tm - 1) // tm
    cum = jnp.cumsum(ntile)
    total = cum[-1]
    t_max = blocks_per_class + MOE_CLASSES
    t_eff = jnp.minimum(jnp.arange(t_max, dtype=jnp.int32), total - 1)
    cls = jnp.searchsorted(cum, t_eff, side="right").astype(jnp.int32)
    blk = cls * blocks_per_class + (t_eff - (cum - ntile)[cls])
    grp, pair = cls // MOE_PAIRS, cls % MOE_PAIRS
    e_lo = grp * MOE_PER_GROUP + jnp.asarray(_PAIR_LO, jnp.int32)[pair]
    e_hi = grp * MOE_PER_GROUP + jnp.asarray(_PAIR_HI, jnp.int32)[pair]
    z_need = ((counts % tm) != 0).astype(jnp.int32)
    z_blk = jnp.arange(MOE_CLASSES, dtype=jnp.int32) * blocks_per_class + jnp.maximum(ntile - 1, 0)
    return (blk.astype(jnp.int32), e_lo, e_hi, total.reshape(1).astype(jnp.int32), z_need, z_blk.astype(jnp.int32))


def _row_copy(src, src_row, dst, dst_row, sem):
    return pltpu.make_async_copy(src.at[pl.ds(pl.multiple_of(src_row * ROW_SUB, ROW_SUB), ROW_SUB), :],
                                 dst.at[pl.ds(pl.multiple_of(dst_row * ROW_SUB, ROW_SUB), ROW_SUB), :], sem)


def _dispatch_kernel(pos_ref, zneed_ref, zblk_ref, rows_ref, xs_ref, zero_scr, sem, zsem, *, tm):
    step = pl.program_id(0)
    tile_rows = MOE_TILE * ROW_SUB

    def zero_copy(c):
        start = pl.multiple_of(zblk_ref[c] * tile_rows, tile_rows)
        return pltpu.make_async_copy(zero_scr, xs_ref.at[pl.ds(start, tile_rows), :], zsem)

    @pl.when(step == 0)
    def _():
        zero_scr[...] = jnp.zeros_like(zero_scr)
        for c in range(MOE_CLASSES):
            @pl.when(zneed_ref[c] != 0)
            def _():
                zero_copy(c).start()
        for c in range(MOE_CLASSES):
            @pl.when(zneed_ref[c] != 0)
            def _():
                zero_copy(c).wait()

    base = step * tm

    def body(i, carry):
        _row_copy(rows_ref, i, xs_ref, pos_ref[base + i], sem).start()
        return carry

    lax.fori_loop(0, tm, body, 0)
    pltpu.make_async_copy(rows_ref, xs_ref.at[pl.ds(0, tm * ROW_SUB), :], sem).wait()


def _dispatch(pos, z_need, z_blk, rows, n_tokens, tm=TOK_TILE):
    sorted_rows = MOE_CLASSES * n_tokens * ROW_SUB
    return pl.pallas_call(
        functools.partial(_dispatch_kernel, tm=tm),
        out_shape=jax.ShapeDtypeStruct((sorted_rows, LANES), F32),
        grid_spec=pltpu.PrefetchScalarGridSpec(
            num_scalar_prefetch=3,
            grid=(n_tokens // tm,),
            in_specs=[pl.BlockSpec((tm * ROW_SUB, LANES), lambda i, *_: (i, 0))],
            out_specs=pl.BlockSpec(memory_space=pl.ANY),
            scratch_shapes=[pltpu.VMEM((MOE_TILE * ROW_SUB, LANES), F32),
                            pltpu.SemaphoreType.DMA(()), pltpu.SemaphoreType.DMA(())],
        ),
        compiler_params=pltpu.CompilerParams(dimension_semantics=("arbitrary",), vmem_limit_bytes=VMEM_LIMIT,
                                             disable_bounds_checks=True),
        name="moe_dispatch",
    )(pos, z_need, z_blk, rows)


def _moe_sparse_kernel(blk_ref, elo_ref, ehi_ref, nact_ref, xs_ref, wr_ref, br_ref,
                       wg_lo_ref, wu_lo_ref, wd_lo_ref, wg_hi_ref, wu_hi_ref, wd_hi_ref, ys_ref):
    t = pl.program_id(0)
    tm = MOE_TILE

    @pl.when(t < nact_ref[0])
    def _():
        x = jnp.concatenate([xs_ref[pl.ds(c, tm, stride=ROW_SUB), :] for c in range(ROW_SUB)], axis=1)
        xb = x.astype(BF16)
        w_hi16, w_lo16 = _split2(wr_ref[...])
        logits = _dot(xb, w_hi16) + _dot(xb, w_lo16) + br_ref[...]
        lane = lax.broadcasted_iota(jnp.int32, logits.shape, 1)
        e_lo, e_hi = elo_ref[t], ehi_ref[t]
        g_lane = MOE_EXPERTS + (e_lo >> 2)
        pick = lambda ln: jnp.sum(jnp.where(lane == ln, logits, 0.0), axis=-1, keepdims=True)
        l_lo, l_hi, l_g = pick(e_lo), pick(e_hi), pick(g_lane)
        is_grp = (lane >= MOE_EXPERTS) & (lane < MOE_EXPERTS + MOE_GROUPS)
        g_w = 1.0 / jnp.sum(jnp.where(is_grp, jnp.exp(logits - l_g), 0.0), axis=-1, keepdims=True)
        w_a = g_w / (1.0 + jnp.exp(l_hi - l_lo))
        w_b = g_w / (1.0 + jnp.exp(l_lo - l_hi))
        act_a = (_silu(_dot(xb, wg_lo_ref[...])) * _dot(xb, wu_lo_ref[...]) * w_a).astype(BF16)
        act_b = (_silu(_dot(xb, wg_hi_ref[...])) * _dot(xb, wu_hi_ref[...]) * w_b).astype(BF16)
        y = _dot(act_a, wd_lo_ref[...]) + _dot(act_b, wd_hi_ref[...])
        for c in range(ROW_SUB):
            ys_ref[pl.ds(c, tm, stride=ROW_SUB), :] = y[:, c * LANES:(c + 1) * LANES]


def _moe_sparse(layer, plan, xs, wr, br, wg, wu, wd, n_tokens):
    blk, e_lo, e_hi, n_act = plan
    tm = MOE_TILE
    t_max = blk.shape[0]
    d = D_MODEL
    tile = pl.BlockSpec((tm * ROW_SUB, LANES), lambda t, blk, lo, hi, na: (blk[t], 0))
    w_lo = lambda shape: pl.BlockSpec((None, None) + shape, lambda t, blk, lo, hi, na: (layer, lo[t], 0, 0))
    w_hi = lambda shape: pl.BlockSpec((None, None) + shape, lambda t, blk, lo, hi, na: (layer, hi[t], 0, 0))
    const = lambda a: pl.BlockSpec(a.shape, lambda t, *_: (0, 0))
    return pl.pallas_call(
        _moe_sparse_kernel,
        out_shape=jax.ShapeDtypeStruct(xs.shape, F32),
        grid_spec=pltpu.PrefetchScalarGridSpec(
            num_scalar_prefetch=4,
            grid=(t_max,),
            in_specs=[tile, const(wr), const(br),
                      w_lo((d, MOE_FF)), w_lo((d, MOE_FF)), w_lo((MOE_FF, d)),
                      w_hi((d, MOE_FF)), w_hi((d, MOE_FF)), w_hi((MOE_FF, d))],
            out_specs=tile,
        ),
        compiler_params=_cparams(("arbitrary",)),
        name="moe_sparse",
    )(blk, e_lo, e_hi, n_act, xs, wr, br, wg, wu, wd, wg, wu, wd)


def _combine_kernel(pos_ref, ys_ref, x1_ref, g2_ref, x2_ref, buf, sem, *, tm):
    base = pl.program_id(0) * tm

    def body(i, carry):
        _row_copy(ys_ref, pos_ref[base + i], buf, i, sem).start()
        return carry

    lax.fori_loop(0, tm, body, 0)
    pltpu.make_async_copy(ys_ref.at[pl.ds(0, tm * ROW_SUB), :], buf, sem).wait()
    y = jnp.concatenate([buf[pl.ds(c, tm, stride=ROW_SUB), :] for c in range(ROW_SUB)], axis=1)
    x2_ref[...] = x1_ref[...] + g2_ref[...] * y


def _combine(pos, ys, x1, g2, rows_per_mod, tm=TOK_TILE):
    n, d = x1.shape
    tiles = rows_per_mod // tm
    return pl.pallas_call(
        functools.partial(_combine_kernel, tm=tm),
        out_shape=jax.ShapeDtypeStruct((n, d), F32),
        grid_spec=pltpu.PrefetchScalarGridSpec(
            num_scalar_prefetch=1,
            grid=(n // tm,),
            in_specs=[pl.BlockSpec(memory_space=pl.ANY),
                      pl.BlockSpec((tm, d), lambda i, *_: (i, 0)),
                      pl.BlockSpec((None, 1, d), lambda i, *_: (i // tiles, 0, 0))],
            out_specs=pl.BlockSpec((tm, d), lambda i, *_: (i, 0)),
            scratch_shapes=[pltpu.VMEM((tm * ROW_SUB, LANES), F32), pltpu.SemaphoreType.DMA(())],
        ),
        compiler_params=pltpu.CompilerParams(dimension_semantics=("arbitrary",), vmem_limit_bytes=VMEM_LIMIT,
                                             disable_bounds_checks=True),
        name="moe_combine",
    )(pos, ys, x1, g2)


def _final_kernel(x_ref, g_ref, o_ref):
    x = x_ref[...]
    ms = jnp.mean(x * x, axis=-1, keepdims=True)
    o_ref[...] = x * lax.rsqrt(ms + EPS) * g_ref[...]


def _final_norm(x, g, tm):
    n, d = x.shape
    return pl.pallas_call(
        _final_kernel,
        out_shape=jax.ShapeDtypeStruct((n, d), F32),
        grid=(n // tm,),
        in_specs=[pl.BlockSpec((tm, d), lambda i: (i, 0)), pl.BlockSpec((1, d), lambda i: (0, 0))],
        out_specs=pl.BlockSpec((tm, d), lambda i: (i, 0)),
        compiler_params=_cparams(("arbitrary",)),
        name="final_norm",
    )(x, g)


def _step_kernel(z_ref, xbc_ref, dt_ref, cst_ref, h_ref,
                 cw_ref, cb_ref, dtb_ref, alog_ref, dskip_ref, ng_ref,
                 yn_ref, cnew_ref, hnew_ref, col_scr):
    x_new = xbc_ref[...]
    acc = cb_ref[...] + cw_ref[3:4, :] * x_new
    for k in range(SSD_CONV - 1):
        acc = acc + cw_ref[k:k + 1, :] * cst_ref[k:k + 1, :]
    cnew_ref[0:1, :] = cst_ref[1:2, :]
    cnew_ref[1:2, :] = cst_ref[2:3, :]
    cnew_ref[2:3, :] = x_new
    xc = _silu(acc)
    xs = xc[:, 0:SSD_INNER]

    e_heads = _head_expand_matrix(LANES, SSD_HEADS, SSD_HEAD_DIM)
    dt_raw8 = jnp.broadcast_to(dt_ref[...], (8, LANES))
    dt_full = _softplus(_mm01(dt_raw8, e_heads, 3)[0:1, :] + dtb_ref[...])
    a_full = -jnp.exp(alog_ref[...])
    dec_full = jnp.exp(dt_full * a_full)
    xdt = xs * dt_full
    col_scr[...] = jnp.zeros_like(col_scr)
    col_scr[0:1, :] = xdt
    col_scr[1:2, :] = dec_full
    cols = col_scr[...].T
    y_parts = []
    gw = SSD_HPG * SSD_HEAD_DIM
    for g in range(SSD_GROUPS):
        b_off = SSD_INNER + g * SSD_STATE
        c_off = SSD_INNER + SSD_GROUPS * SSD_STATE + g * SSD_STATE
        bm = xc[:, b_off:b_off + SSD_STATE]
        cm = xc[:, c_off:c_off + SSD_STATE]
        g0 = g * gw
        hn = h_ref[g0:g0 + gw, :] * cols[g0:g0 + gw, 1:2] + cols[g0:g0 + gw, 0:1] * bm
        hnew_ref[g0:g0 + gw, :] = hn
        t = (hn * cm).T
        y_parts.append(jnp.sum(t, axis=0, keepdims=True))
    y = jnp.concatenate(y_parts, axis=1) + dskip_ref[...] * xs
    y = y * _silu(z_ref[...])
    outs = []
    for g in range(SSD_GROUPS):
        yg = y[:, g * gw:(g + 1) * gw]
        ms = jnp.mean(yg * yg, axis=-1, keepdims=True)
        outs.append(yg * lax.rsqrt(ms + EPS) * ng_ref[:, g * gw:(g + 1) * gw])
    yn_ref[...] = jnp.concatenate(outs, axis=1)


def _step_sample(layer, z, xbc, dt_raw, state_conv, state_ssm,
                 conv_w, conv_b, dtb_full, alog_full, dskip_full, ssd_norm_g):
    bsz = z.shape[0]
    row3 = lambda w: pl.BlockSpec((None, 1, w), lambda b: (b, 0, 0))
    const = lambda a: pl.BlockSpec(a.shape, lambda b: (0,) * a.ndim)
    return pl.pallas_call(
        _step_kernel,
        out_shape=(
            jax.ShapeDtypeStruct((bsz, 1, SSD_INNER), F32),
            jax.ShapeDtypeStruct((bsz, SSD_CONV - 1, SSD_CONV_DIM), F32),
            jax.ShapeDtypeStruct((bsz, SSD_INNER, SSD_STATE), F32),
        ),
        grid=(bsz,),
        in_specs=[
            row3(SSD_INNER), row3(SSD_CONV_DIM), row3(LANES),
            pl.BlockSpec((None, None, SSD_CONV - 1, SSD_CONV_DIM), lambda b: (layer, b, 0, 0)),
            pl.BlockSpec((None, None, SSD_INNER, SSD_STATE), lambda b: (layer, b, 0, 0)),
            const(conv_w), const(conv_b), const(dtb_full), const(alog_full), const(dskip_full), const(ssd_norm_g),
        ],
        out_specs=(
            row3(SSD_INNER),
            pl.BlockSpec((None, SSD_CONV - 1, SSD_CONV_DIM), lambda b: (b, 0, 0)),
            pl.BlockSpec((None, SSD_INNER, SSD_STATE), lambda b: (b, 0, 0)),
        ),
        scratch_shapes=[pltpu.VMEM((LANES, SSD_INNER), F32)],
        compiler_params=_cparams(("arbitrary",)),
        name="step_sample",
    )(z.reshape(bsz, 1, -1), xbc.reshape(bsz, 1, -1), dt_raw.reshape(bsz, 1, -1), state_conv, state_ssm,
      conv_w, conv_b, dtb_full, alog_full, dskip_full, ssd_norm_g)


_HEAD_SPLIT = 2
_HROWS = ATT_OUT // _HEAD_SPLIT


def _cache_attn_kernel(*refs, n_alias):
    qkv_ref = refs[0]
    cache_refs = refs[1:7]
    out_refs = refs[7 + n_alias:13 + n_alias]
    o_ref = refs[13 + n_alias]
    b = pl.program_id(0)
    nh = _HROWS // ATT_HEAD_DIM
    scale = ATT_HEAD_DIM ** -0.5
    qkv = qkv_ref[...]
    lane_b = lax.broadcasted_iota(jnp.int32, qkv.shape, 2)
    cols = jnp.sum(jnp.where(lane_b == b, qkv, 0.0), axis=-1, keepdims=True)

    def per_head_rows(v):
        return jnp.concatenate([jnp.broadcast_to(v[h:h + 1, :], (ATT_HEAD_DIM, 1)) for h in range(nh)], axis=0)

    o_g, lse_g = [], []
    for gi, (_, dil) in enumerate(ATT_PATTERNS):
        k_ref, v_ref = cache_refs[2 * gi], cache_refs[2 * gi + 1]
        ko_ref, vo_ref = out_refs[2 * gi], out_refs[2 * gi + 1]
        q = cols[gi]
        k_new = cols[ATT_GROUPS + gi]
        v_new = cols[2 * ATT_GROUPS + gi]
        kk = k_ref[...]
        vv = v_ref[...]
        length = kk.shape[1]
        lane = lax.broadcasted_iota(jnp.int32, (nh, length), 1)
        s = jnp.sum((kk * q).reshape(nh, ATT_HEAD_DIM, length), axis=1) * scale
        s = jnp.where((lane & (dil - 1)) == 0, s, -jnp.inf)
        s_new = jnp.sum((k_new * q).reshape(nh, ATT_HEAD_DIM, 1), axis=1) * scale
        m = jnp.maximum(jnp.max(s, axis=-1, keepdims=True), s_new)
        e = jnp.exp(s - m)
        e_new = jnp.exp(s_new - m)
        den = jnp.sum(e, axis=-1, keepdims=True) + e_new
        acc = jnp.sum(vv.reshape(nh, ATT_HEAD_DIM, length) * e[:, None, :], axis=-1, keepdims=True)
        acc = acc.reshape(_HROWS, 1) + per_head_rows(e_new) * v_new
        o_g.append(acc / per_head_rows(den))
        lse_g.append(per_head_rows(m + jnp.log(den)))
        lane_full = lax.broadcasted_iota(jnp.int32, kk.shape, 1)
        last = lane_full == length - 1
        ko_ref[...] = jnp.where(last, k_new, pltpu.roll(kk, length - 1, axis=1))
        vo_ref[...] = jnp.where(last, v_new, pltpu.roll(vv, length - 1, axis=1))
    mx = jnp.maximum(jnp.maximum(lse_g[0], lse_g[1]), lse_g[2])
    w = [jnp.exp(l - mx) for l in lse_g]
    tot = w[0] + w[1] + w[2]
    o_ref[...] = (w[0] / tot) * o_g[0] + (w[1] / tot) * o_g[1] + (w[2] / tot) * o_g[2]


def _cache_attn(layer, qkv_t, cache_views, prev_outs):
    depth, bsz, _, _ = cache_views[0].shape
    n_alias = 0 if prev_outs is None else len(prev_outs)
    q4 = qkv_t.reshape(3 * ATT_GROUPS, _HEAD_SPLIT, _HROWS, bsz)
    blk = lambda c: pl.BlockSpec((None, None, _HROWS, c.shape[3]), lambda b, hh: (layer, b, hh, 0))
    any_spec = pl.BlockSpec(memory_space=pl.ANY)
    args = [q4] + list(cache_views) + ([] if prev_outs is None else list(prev_outs))
    res = pl.pallas_call(
        functools.partial(_cache_attn_kernel, n_alias=n_alias),
        out_shape=tuple(jax.ShapeDtypeStruct(c.shape, c.dtype) for c in cache_views)
        + (jax.ShapeDtypeStruct((bsz, _HEAD_SPLIT, _HROWS, 1), F32),),
        grid=(bsz, _HEAD_SPLIT),
        in_specs=[pl.BlockSpec((3 * ATT_GROUPS, None, _HROWS, bsz), lambda b, hh: (0, hh, 0, 0))]
        + [blk(c) for c in cache_views] + [any_spec] * n_alias,
        out_specs=tuple(blk(c) for c in cache_views)
        + (pl.BlockSpec((None, None, _HROWS, 1), lambda b, hh: (b, hh, 0, 0)),),
        input_output_aliases={7 + i: i for i in range(n_alias)},
        compiler_params=_cparams(("arbitrary", "arbitrary")),
        name="cache_attn",
    )(*args)
    return list(res[:6]), res[6].reshape(bsz, ATT_OUT)


def _pad_lanes(v, width=LANES):
    return jnp.pad(v, [(0, 0)] * (v.ndim - 1) + [(0, width - v.shape[-1])])


def kernel(x_prompt, x_sample, c_prompt, c_sample, state_conv, state_ssm, cache_k_win128, cache_v_win128, cache_k_win512, cache_v_win512, cache_k_win2048, cache_v_win2048, norm1_g, w_ada, b_ada, w_in, conv_w, conv_b, dt_bias, a_log, d_skip, ssd_norm_g, w_ssd_proj, w_attn_proj, w_out, norm2_g, w_router_group, b_router_group, w_router_expert, b_router_expert, w_exp_gate, w_exp_up, w_exp_down, final_norm_g):
    depth = w_in.shape[0]
    bp, seq, d = x_prompt.shape
    bs = x_sample.shape[0]
    assert x_sample.shape[1] == 1 and d == D_MODEL and w_in.shape[2] == IN_WIDTH
    assert seq % (ATT_PATTERNS[-1][1] * ATT_BLOCK) == 0
    n_p = bp * seq
    caches = ((cache_k_win128, cache_v_win128), (cache_k_win512, cache_v_win512),
              (cache_k_win2048, cache_v_win2048))

    rows = bp + bs
    rows_pad = -(-rows // 8) * 8
    c_all = jnp.pad(jnp.concatenate([c_prompt, c_sample], axis=0), ((0, rows_pad - rows), (0, 0)))
    mods = _modulation(c_all, w_ada, b_ada)

    xp = x_prompt.reshape(n_p, d)
    xs = x_sample.reshape(bs, d)
    outs = {k: [] for k in ("conv_p", "conv_s", "ssm_p", "ssm_s")}
    kv_p = [[[], []] for _ in ATT_PATTERNS]
    tiles = seq // TOK_TILE
    assert bs <= LANES and all(c.shape[2] == win for (win, _), pair in zip(ATT_PATTERNS, caches) for c in pair)
    cache_views = [jnp.transpose(c, (0, 1, 3, 4, 2)).reshape(depth, bs, ATT_OUT, c.shape[2])
                   for pair in caches for c in pair]
    shifted = None
    wg_bf, wu_bf, wd_bf = w_exp_gate.astype(BF16), w_exp_up.astype(BF16), w_exp_down.astype(BF16)

    for l in range(depth):
        mod_p = mods[l, :bp]
        mod_s = mods[l, bp:bp + bs]
        w_in_l = w_in[l]
        wz = w_in_l[:, 0:OFF_XBC].astype(BF16)
        wx = w_in_l[:, OFF_XBC:OFF_DT].astype(BF16)
        wd = _pad_lanes(w_in_l[:, OFF_DT:OFF_QKV]).astype(BF16)
        wqs = [jnp.concatenate([w_in_l[:, OFF_QKV + (which * ATT_GROUPS + gi) * ATT_OUT:
                                          OFF_QKV + (which * ATT_GROUPS + gi + 1) * ATT_OUT]
                                for which in range(3)], axis=1).astype(BF16) for gi in range(ATT_GROUPS)]
        wgt = w_in_l[:, OFF_GATE:IN_WIDTH].astype(BF16)
        g1n = norm1_g[l].reshape(1, d)
        g2n = norm2_g[l].reshape(1, d)
        cw = conv_w[l]
        cb = conv_b[l].reshape(1, -1)
        dskip_full = jnp.repeat(d_skip[l], SSD_HEAD_DIM).reshape(1, -1)
        ssd_g = ssd_norm_g[l].reshape(1, -1)
        w_router = _pad_lanes(jnp.concatenate([w_router_expert[l], w_router_group[l]], axis=1))
        b_router = _pad_lanes(jnp.concatenate([b_router_expert[l], b_router_group[l]], axis=0).reshape(1, -1))

        z, xbc, dt_raw, a0, a1, a2, gates = _in_proj_prompt(
            xp, mod_p[:, 0:2 * d].reshape(bp, 1, 2 * d), g1n, wz, wx, wd, wqs, wgt, bp, seq)
        yn, ssm_new = _ssd_prompt(xbc, z, dt_raw, cw, cb, _pad_lanes(dt_bias[l].reshape(1, -1)),
                                  _pad_lanes(a_log[l].reshape(1, -1)), dskip_full, ssd_g, bp, seq)
        outs["ssm_p"].append(ssm_new.reshape(bp, SSD_GROUPS, SSD_HPG, SSD_HEAD_DIM, SSD_STATE))
        outs["conv_p"].append(xbc.reshape(bp, seq, SSD_CONV_DIM)[:, seq - (SSD_CONV - 1):])
        a_groups = [a0.reshape(bp, tiles, 1, TOK_TILE, 3 * ATT_OUT), a1, a2]
        o_list, lse_list = [], []
        for gi, (win, dil) in enumerate(ATT_PATTERNS):
            o_g, lse_g = _attn_prompt(a_groups[gi], gi, bp, seq)
            o_list.append(o_g)
            lse_list.append(lse_g)
            keep = min(win, seq)
            rows = TOK_TILE // dil
            if keep >= TOK_TILE:
                nt = keep // TOK_TILE
                tail = a_groups[gi][:, tiles - nt:, :, :, ATT_OUT:3 * ATT_OUT]
                tail = jnp.transpose(tail, (0, 1, 3, 2, 4)).reshape(bp, keep, 2 * ATT_OUT)
            else:
                assert dil == 1
                tail = a_groups[gi][:, tiles - 1, 0, TOK_TILE - keep:, ATT_OUT:3 * ATT_OUT]
            tail = tail.astype(F32)
            kv_p[gi][0].append(tail[:, :, 0:ATT_OUT].reshape(bp, keep, ATT_HEADS, ATT_HEAD_DIM))
            kv_p[gi][1].append(tail[:, :, ATT_OUT:2 * ATT_OUT].reshape(bp, keep, ATT_HEADS, ATT_HEAD_DIM))
        mod_post = jnp.concatenate([mod_p[:, 2 * d:3 * d], mod_p[:, 3 * d:5 * d]], axis=1).reshape(bp, 1, 3 * d)
        x1, rows, slot, cnt = _post(xp, yn, (o_list, lse_list), gates, mod_post, g2n,
                                    w_ssd_proj[l].astype(BF16), w_attn_proj[l].astype(BF16),
                                    w_out[l].astype(BF16), w_router, b_router, passes=1, merge_attn=True,
                                    tm=TOK_TILE, rows_per_mod=seq, h2_dtype=F32)
        pos = slot[:, 0].astype(jnp.int32)
        blk, e_lo, e_hi, n_act, z_need, z_blk = _moe_plan(cnt[0, :MOE_CLASSES].astype(jnp.int32), n_p)
        xs_sorted = _dispatch(pos, z_need, z_blk, rows, n_p)
        ys_sorted = _moe_sparse(l, (blk, e_lo, e_hi, n_act), xs_sorted, w_router, b_router,
                                wg_bf, wu_bf, wd_bf, n_p)
        xp = _combine(pos, ys_sorted, x1, mod_p[:, 5 * d:6 * d].reshape(bp, 1, d), seq)

        u = _in_proj_sample(xs, mod_s[:, 0:2 * d], g1n, w_in_l)
        z_s = u[:, 0:OFF_XBC]
        xbc_s = u[:, OFF_XBC:OFF_DT]
        dt_s = _pad_lanes(u[:, OFF_DT:OFF_QKV])
        qkv_s = u[:, OFF_QKV:OFF_GATE]
        gates_s = u[:, OFF_GATE:IN_WIDTH]
        yn_s, conv_new, ssm_new_s = _step_sample(
            l, z_s, xbc_s, dt_s, state_conv, state_ssm.reshape(depth, bs, SSD_INNER, SSD_STATE),
            cw, cb, jnp.repeat(dt_bias[l], SSD_HEAD_DIM).reshape(1, -1),
            jnp.repeat(a_log[l], SSD_HEAD_DIM).reshape(1, -1), dskip_full, ssd_g)
        outs["conv_s"].append(conv_new)
        outs["ssm_s"].append(ssm_new_s.reshape(bs, SSD_GROUPS, SSD_HPG, SSD_HEAD_DIM, SSD_STATE))
        shifted, o_s = _cache_attn(l, qkv_s.T, cache_views, shifted)
        mod_post_s = jnp.concatenate([mod_s[:, 2 * d:3 * d], mod_s[:, 3 * d:5 * d]], axis=1)
        x1_s, h2_s, gate_s = _post(xs, yn_s.reshape(bs, SSD_INNER), o_s, gates_s, mod_post_s,
                                   g2n, w_ssd_proj[l], w_attn_proj[l], w_out[l], w_router, b_router,
                                   passes=3, merge_attn=False, tm=bs, rows_per_mod=1, h2_dtype=F32)
        xs = _moe(l, h2_s, gate_s, x1_s, mod_s[:, 5 * d:6 * d], w_exp_gate, w_exp_up, w_exp_down,
                  passes=3, tm=bs, rows_per_mod=1)

    fg = final_norm_g.reshape(1, d)
    y_prompt = _final_norm(xp, fg, 1024).reshape(bp, seq, d)
    y_sample = _final_norm(xs, fg, bs).reshape(bs, 1, d)

    shifted = [jnp.transpose(s.reshape(depth, bs, ATT_HEADS, ATT_HEAD_DIM, s.shape[3]), (0, 1, 4, 2, 3))
               for s in shifted]

    st = jnp.stack
    res = [y_prompt, y_sample, st(outs["conv_p"]), st(outs["conv_s"]), st(outs["ssm_p"]), st(outs["ssm_s"])]
    for gi in range(ATT_GROUPS):
        res += [st(kv_p[gi][0]), shifted[2 * gi], st(kv_p[gi][1]), shifted[2 * gi + 1]]
    return tuple(res)
```

```python
import functools
import math

import jax
import jax.numpy as jnp
from jax import lax
from jax.experimental import pallas as pl
from jax.experimental.pallas import tpu as pltpu

F32 = jnp.float32
BF16 = jnp.bfloat16

D_MODEL = 1024
SSD_INNER = 1024
SSD_HEAD_DIM = 64
SSD_HEADS = 16
SSD_GROUPS = 2
SSD_HPG = 8
SSD_STATE = 128
SSD_CONV = 4
SSD_CHUNK = 128
SSD_CONV_DIM = SSD_INNER + 2 * SSD_GROUPS * SSD_STATE
ATT_PATTERNS = ((128, 1), (512, 4), (2048, 16))
ATT_GROUPS = 3
ATT_HEADS = 8
ATT_HEAD_DIM = 64
ATT_BLOCK = 128
ATT_OUT = ATT_HEADS * ATT_HEAD_DIM
MOE_GROUPS = 4
MOE_PER_GROUP = 4
MOE_EXPERTS = 16
MOE_FF = 512
MOE_PAIRS = 6
MOE_CLASSES = MOE_GROUPS * MOE_PAIRS
MOE_TILE = 256
ROW_SUB = 8
IN_QKV = 3 * ATT_GROUPS * ATT_OUT
IN_GATE = 2 * D_MODEL
OFF_XBC = SSD_INNER
OFF_DT = OFF_XBC + SSD_CONV_DIM
OFF_QKV = OFF_DT + SSD_HEADS
OFF_GATE = OFF_QKV + IN_QKV
IN_WIDTH = OFF_GATE + IN_GATE
EPS = 1e-6

LANES = 128
VMEM_LIMIT = 56 * 1024 * 1024


def _cparams(sem, vmem=VMEM_LIMIT):
    return pltpu.CompilerParams(dimension_semantics=sem, vmem_limit_bytes=vmem)


def _split2(a):
    hi = a.astype(BF16)
    lo = (a - hi.astype(F32)).astype(BF16)
    return hi, lo


def _split3(a):
    hi = a.astype(BF16)
    r = a - hi.astype(F32)
    mid = r.astype(BF16)
    lo = (r - mid.astype(F32)).astype(BF16)
    return hi, mid, lo


def _dot(a, b):
    return jnp.dot(a, b, preferred_element_type=F32)


def _dot_nt(a, b):
    return lax.dot_general(a, b, (((1,), (1,)), ((), ())), preferred_element_type=F32)


def _mm(a, w, passes):
    if passes == 1:
        return _dot(a.astype(BF16), w.astype(BF16))
    a = a.astype(F32)
    w = w.astype(F32)
    a_hi, a_lo = _split2(a)
    w_hi, w_lo = _split2(w)
    return _dot(a_hi, w_hi) + (_dot(a_lo, w_hi) + _dot(a_hi, w_lo))


def _mm01(a, e01, terms):
    parts = _split3(a) if terms == 3 else _split2(a)
    out = _dot(parts[0], e01)
    for p in parts[1:]:
        out = out + _dot(p, e01)
    return out


def _sigmoid(x):
    return 1.0 / (1.0 + jnp.exp(-x))


def _silu(x):
    return x * _sigmoid(x)


def _softplus(x):
    return jnp.maximum(x, 0.0) + jnp.log(1.0 + jnp.exp(-jnp.abs(x)))


def _head_expand_matrix(rows, n_heads, width):
    r = lax.broadcasted_iota(jnp.int32, (rows, n_heads * width), 0)
    c = lax.broadcasted_iota(jnp.int32, (rows, n_heads * width), 1)
    return jnp.where((c // width) == r, 1.0, 0.0).astype(BF16)


def _rms_modulate(x, g, sh, sc):
    ms = jnp.mean(x * x, axis=-1, keepdims=True)
    y = x * lax.rsqrt(ms + EPS) * g
    return y * (1.0 + sc) + sh


def _mod_kernel(c_ref, w_ref, b_ref, o_ref):
    a = _silu(c_ref[...])
    o_ref[...] = _mm(a, w_ref[...], 3) + b_ref[...]


def _modulation(c_all, w_ada, b_ada):
    depth, d, n6 = w_ada.shape
    rows = c_all.shape[0]
    tn = 1024
    return pl.pallas_call(
        _mod_kernel,
        out_shape=jax.ShapeDtypeStruct((depth, rows, n6), F32),
        grid=(depth, n6 // tn),
        in_specs=[
            pl.BlockSpec((rows, d), lambda l, j: (0, 0)),
            pl.BlockSpec((None, d, tn), lambda l, j: (l, 0, j)),
            pl.BlockSpec((None, 1, tn), lambda l, j: (l, 0, j)),
        ],
        out_specs=pl.BlockSpec((None, rows, tn), lambda l, j: (l, 0, j)),
        compiler_params=_cparams(("arbitrary", "arbitrary")),
        name="adaln_mod",
    )(c_all, w_ada, b_ada.reshape(depth, 1, n6))


_IN_CHUNK = 512
TOK_TILE = 512


def _in_kernel(x_ref, mod_ref, g_ref, wz_ref, wx_ref, wd_ref, wq0_ref, wq1_ref, wq2_ref, wg_ref,
               z_ref, xbc_ref, dt_ref, a0_ref, a1_ref, a2_ref, gates_ref, h_scr):
    sh = mod_ref[:, 0:D_MODEL]
    sc = mod_ref[:, D_MODEL:2 * D_MODEL]
    hf = _rms_modulate(x_ref[...], g_ref[...], sh, sc)
    n_cb = D_MODEL // LANES
    for cbk in range(n_cb):
        h_scr[cbk] = hf[:, cbk * LANES:(cbk + 1) * LANES]
    h = hf.astype(BF16)
    for w_ref, o_ref in ((wz_ref, z_ref), (wx_ref, xbc_ref), (wq0_ref, a0_ref), (wg_ref, gates_ref)):
        width = o_ref.shape[-1]
        for c0 in range(0, width, _IN_CHUNK):
            o_ref[:, c0:c0 + _IN_CHUNK] = _dot(h, w_ref[:, c0:c0 + _IN_CHUNK]).astype(o_ref.dtype)
    dt_ref[...] = _dot(h, wd_ref[...])
    for w_ref, a_ref in ((wq1_ref, a1_ref), (wq2_ref, a2_ref)):
        d, r_len, width = a_ref.shape
        hp = jnp.concatenate(
            [jnp.concatenate([h_scr[cbk, pl.ds(r, r_len, stride=d), :] for cbk in range(n_cb)], axis=1)
             for r in range(d)], axis=0).astype(BF16)
        for c0 in range(0, width, _IN_CHUNK):
            res = _dot(hp, w_ref[:, c0:c0 + _IN_CHUNK]).astype(BF16)
            a_ref[:, :, c0:c0 + _IN_CHUNK] = res.reshape(d, r_len, _IN_CHUNK)


def _in_proj_prompt(x, mod, norm_g, wz, wx, wd, wqs, wg, bsz, seq):
    n, d = x.shape
    tm = TOK_TILE
    tiles_per_seq = seq // tm
    qw = 3 * ATT_OUT

    def const(shape):
        return pl.BlockSpec(shape, lambda i: (0, 0), pipeline_mode=pl.Buffered(1))

    a_shapes, a_specs = [], []
    for _, dil in ATT_PATTERNS[1:]:
        a_shapes.append(jax.ShapeDtypeStruct((bsz, tiles_per_seq, dil, tm // dil, qw), BF16))
        a_specs.append(pl.BlockSpec((None, None, dil, tm // dil, qw),
                                    lambda i: (i // tiles_per_seq, i % tiles_per_seq, 0, 0, 0)))
    return pl.pallas_call(
        _in_kernel,
        out_shape=(
            jax.ShapeDtypeStruct((n, SSD_INNER), BF16),
            jax.ShapeDtypeStruct((n, SSD_CONV_DIM), F32),
            jax.ShapeDtypeStruct((n, LANES), F32),
            jax.ShapeDtypeStruct((n, qw), BF16),
            a_shapes[0], a_shapes[1],
            jax.ShapeDtypeStruct((n, IN_GATE), BF16),
        ),
        grid=(n // tm,),
        in_specs=[
            pl.BlockSpec((tm, d), lambda i: (i, 0)),
            pl.BlockSpec((None, 1, 2 * d), lambda i: (i // tiles_per_seq, 0, 0)),
            const((1, d)),
            const(wz.shape), const(wx.shape), const(wd.shape),
            const(wqs[0].shape), const(wqs[1].shape), const(wqs[2].shape), const(wg.shape),
        ],
        out_specs=(
            pl.BlockSpec((tm, SSD_INNER), lambda i: (i, 0)),
            pl.BlockSpec((tm, SSD_CONV_DIM), lambda i: (i, 0)),
            pl.BlockSpec((tm, LANES), lambda i: (i, 0)),
            pl.BlockSpec((tm, qw), lambda i: (i, 0)),
            a_specs[0], a_specs[1],
            pl.BlockSpec((tm, IN_GATE), lambda i: (i, 0)),
        ),
        scratch_shapes=[pltpu.VMEM((d // LANES, tm, LANES), F32)],
        compiler_params=_cparams(("arbitrary",)),
        name="in_proj_prompt",
    )(x, mod, norm_g, wz, wx, wd, wqs[0], wqs[1], wqs[2], wg)


def _in_small_kernel(x_ref, mod_ref, g_ref, w_ref, o_ref):
    sh = mod_ref[:, 0:D_MODEL]
    sc = mod_ref[:, D_MODEL:2 * D_MODEL]
    h = _rms_modulate(x_ref[...], g_ref[...], sh, sc)
    o_ref[...] = _mm(h, w_ref[...], 3)


def _in_proj_sample(x, mod, norm_g, w_in_l, tn=1024):
    m, d = x.shape
    width = w_in_l.shape[1]
    return pl.pallas_call(
        _in_small_kernel,
        out_shape=jax.ShapeDtypeStruct((m, width), F32),
        grid=(pl.cdiv(width, tn),),
        in_specs=[
            pl.BlockSpec((m, d), lambda j: (0, 0)),
            pl.BlockSpec((m, 2 * d), lambda j: (0, 0)),
            pl.BlockSpec((1, d), lambda j: (0, 0)),
            pl.BlockSpec((d, tn), lambda j: (0, j)),
        ],
        out_specs=pl.BlockSpec((m, tn), lambda j: (0, j)),
        compiler_params=_cparams(("arbitrary",)),
        name="in_proj_sample",
    )(x, mod, norm_g, w_in_l)


_SSD_SUB = 2


def _ssd_kernel(xbc_ref, z_ref, dt_ref, cw_ref, cb_ref, dtb_ref, alog_ref, dskip_ref, ng_ref,
                yn_ref, hout_ref, h_scr, xp_scr):
    q = SSD_CHUNK
    c = pl.program_id(1)

    @pl.when(c == 0)
    def _():
        h_scr[...] = jnp.zeros_like(h_scr)
        xp_scr[0:8, :] = jnp.zeros((8, SSD_CONV_DIM), F32)

    for sub in range(_SSD_SUB):
        rows = slice(sub * q, (sub + 1) * q)
        _ssd_chunk(xbc_ref.at[rows], z_ref.at[rows], dt_ref.at[rows], cw_ref, cb_ref, dtb_ref, alog_ref,
                   dskip_ref, ng_ref, yn_ref.at[rows], h_scr, xp_scr)

    @pl.when(c == pl.num_programs(1) - 1)
    def _():
        hout_ref[...] = h_scr[...]


def _ssd_chunk(xbc_ref, z_ref, dt_ref, cw_ref, cb_ref, dtb_ref, alog_ref, dskip_ref, ng_ref,
               yn_ref, h_scr, xp_scr):
    q = SSD_CHUNK
    xp_scr[8:8 + q, :] = xbc_ref[...]
    acc = cb_ref[...] + cw_ref[3:4, :] * xp_scr[8:8 + q, :]
    for k in range(SSD_CONV - 1):
        acc = acc + cw_ref[k:k + 1, :] * xp_scr[5 + k:5 + k + q, :]
    xp_scr[0:8, :] = xp_scr[q:q + 8, :]
    xc = _silu(acc)
    xs = xc[:, 0:SSD_INNER]
    xs_bf = xs.astype(BF16)

    lane_q = lax.broadcasted_iota(jnp.int32, (q, LANES), 1)
    row_q = lax.broadcasted_iota(jnp.int32, (q, LANES), 0)
    causal = row_q >= lane_q
    tri = jnp.where(causal, 1.0, 0.0).astype(BF16)
    tri_t = jnp.where(lane_q >= row_q, 1.0, 0.0).astype(BF16)
    e_heads = _head_expand_matrix(LANES, SSD_HEADS, SSD_HEAD_DIM)

    dt = _softplus(dt_ref[...] + dtb_ref[...])
    dt = jnp.where(lane_q < SSD_HEADS, dt, 0.0)
    a = -jnp.exp(alog_ref[...])
    d_a = dt * a
    cum = _mm01_left(tri, d_a)
    d_a_t = d_a.T
    dt_t = dt.T
    cum_t = _mm01(d_a_t, tri_t, 3)
    cum_last = cum[q - 1:q, :]
    exp_cum = jnp.exp(cum)
    dec_end = jnp.exp(cum_last - cum)
    stack = jnp.concatenate([exp_cum, dec_end * dt], axis=0)
    full = _mm01(stack, e_heads, 2)
    exp_cum_full = full[0:q]
    w_full = full[q:2 * q]
    chunk_dec_t = jnp.exp(cum_t[:, q - 1:q])

    lane_half = lax.broadcasted_iota(jnp.int32, (q, LANES), 1) < SSD_HEAD_DIM
    y_parts = []
    for g in range(SSD_GROUPS):
        b_off = SSD_INNER + g * SSD_STATE
        c_off = SSD_INNER + SSD_GROUPS * SSD_STATE + g * SSD_STATE
        bm = xc[:, b_off:b_off + SSD_STATE].astype(BF16)
        cm = xc[:, c_off:c_off + SSD_STATE].astype(BF16)
        cbm = _dot_nt(cm, bm)
        gw = SSD_HPG * SSD_HEAD_DIM
        g0 = g * gw
        yd = []
        for pair in range(SSD_HPG // 2):
            x_pair = xs_bf[:, g0 + pair * LANES:g0 + (pair + 1) * LANES]
            halves = []
            for hh in range(2):
                h = g * SSD_HPG + pair * 2 + hh
                seg = cum[:, h:h + 1] - cum_t[h:h + 1, :]
                dec = jnp.exp(jnp.where(causal, seg, -jnp.inf))
                m_h = (cbm * dec * dt_t[h:h + 1, :]).astype(BF16)
                halves.append(_dot(m_h, x_pair))
            yd.append(jnp.where(lane_half, halves[0], halves[1]))
        y_diag = jnp.concatenate(yd, axis=1)
        h_g = h_scr[g0:g0 + gw, :]
        y_off = _dot_nt(cm, h_g.astype(BF16)) * exp_cum_full[:, g0:g0 + gw]
        y_parts.append(y_diag + y_off)
        xw = (xs[:, g0:g0 + gw] * w_full[:, g0:g0 + gw])
        st = _dot(xw.T.astype(BF16), bm)
        for e in range(SSD_HPG):
            h = g * SSD_HPG + e
            r0 = g0 + e * SSD_HEAD_DIM
            h_scr[r0:r0 + SSD_HEAD_DIM, :] = (h_scr[r0:r0 + SSD_HEAD_DIM, :] * chunk_dec_t[h:h + 1, :]
                                              + st[e * SSD_HEAD_DIM:(e + 1) * SSD_HEAD_DIM, :])

    y = jnp.concatenate(y_parts, axis=1) + dskip_ref[...] * xs
    y = y * _silu(z_ref[...].astype(F32))
    outs = []
    for g in range(SSD_GROUPS):
        gw = SSD_HPG * SSD_HEAD_DIM
        yg = y[:, g * gw:(g + 1) * gw]
        ms = jnp.mean(yg * yg, axis=-1, keepdims=True)
        outs.append(yg * lax.rsqrt(ms + EPS) * ng_ref[:, g * gw:(g + 1) * gw])
    yn_ref[...] = jnp.concatenate(outs, axis=1).astype(yn_ref.dtype)


def _mm01_left(tri01, a):
    hi, mid, lo = _split3(a)
    return _dot(tri01, hi) + (_dot(tri01, mid) + _dot(tri01, lo))


def _ssd_prompt(xbc, z, dt_raw, conv_w, conv_b, dt_bias, a_log, d_skip_full, ssd_norm_g, bsz, seq):
    q = SSD_CHUNK * _SSD_SUB
    nc = seq // q
    row = lambda b, c: (b * nc + c, 0)
    const = lambda b, c: (0, 0)
    return pl.pallas_call(
        _ssd_kernel,
        out_shape=(
            jax.ShapeDtypeStruct((bsz * seq, SSD_INNER), BF16),
            jax.ShapeDtypeStruct((bsz, SSD_INNER, SSD_STATE), F32),
        ),
        grid=(bsz, nc),
        in_specs=[
            pl.BlockSpec((q, SSD_CONV_DIM), row),
            pl.BlockSpec((q, SSD_INNER), row),
            pl.BlockSpec((q, LANES), row),
            pl.BlockSpec((SSD_CONV, SSD_CONV_DIM), const),
            pl.BlockSpec((1, SSD_CONV_DIM), const),
            pl.BlockSpec((1, LANES), const),
            pl.BlockSpec((1, LANES), const),
            pl.BlockSpec((1, SSD_INNER), const),
            pl.BlockSpec((1, SSD_INNER), const),
        ],
        out_specs=(
            pl.BlockSpec((q, SSD_INNER), row),
            pl.BlockSpec((None, SSD_INNER, SSD_STATE), lambda b, c: (b, 0, 0)),
        ),
        scratch_shapes=[
            pltpu.VMEM((SSD_INNER, SSD_STATE), F32),
            pltpu.VMEM((SSD_CHUNK + 8, SSD_CONV_DIM), F32),
        ],
        compiler_params=_cparams(("arbitrary", "arbitrary")),
        name="ssd_prompt",
    )(xbc, z, dt_raw, conv_w, conv_b, dt_bias, a_log, d_skip_full, ssd_norm_g)


def _attn_kernel(q_ref, kc_ref, kp_ref, vc_ref, vp_ref, o_out_ref, lse_out_ref,
                 q_ref_s, kwin, vwin, o_ref, lse_ref, *, tq, band):
    blk = ATT_BLOCK
    j = pl.program_id(2)
    q_ref_s[...] = q_ref[...].reshape(tq, ATT_OUT)
    kwin[0:blk, :] = kp_ref[...].reshape(blk, ATT_OUT)
    kwin[blk:blk + tq, :] = kc_ref[...].reshape(tq, ATT_OUT)
    vwin[0:blk, :] = vp_ref[...].reshape(blk, ATT_OUT)
    vwin[blk:blk + tq, :] = vc_ref[...].reshape(tq, ATT_OUT)
    q_ref = q_ref_s

    qi = lax.broadcasted_iota(jnp.int32, (blk, 2 * blk), 0)
    ki = lax.broadcasted_iota(jnp.int32, (blk, 2 * blk), 1)
    dist = qi + blk - ki
    in_band = (dist >= 0) & (dist <= band)
    lane = lax.broadcasted_iota(jnp.int32, (blk, LANES), 1)
    lane_half = lane < ATT_HEAD_DIM
    scale = ATT_HEAD_DIM ** -0.5
    zero_bf = jnp.zeros((blk, LANES), BF16)

    for i in range(tq // blk):
        if i == 0:
            valid = in_band & ((ki >= blk) | (j > 0))
        else:
            valid = in_band
        lse_tile = jnp.zeros((blk, LANES), F32)
        for hp in range(ATT_HEADS // 2):
            c0 = hp * LANES
            q_pair = q_ref[i * blk:(i + 1) * blk, c0:c0 + LANES]
            k_pair = kwin[i * blk:(i + 2) * blk, c0:c0 + LANES]
            v_pair = vwin[i * blk:(i + 2) * blk, c0:c0 + LANES]
            halves = []
            for hh in range(2):
                q_m = jnp.where(lane_half if hh == 0 else jnp.logical_not(lane_half), q_pair, zero_bf)
                s = _dot_nt(q_m, k_pair) * scale
                s = jnp.where(valid, s, -jnp.inf)
                m = jnp.max(s, axis=-1, keepdims=True)
                e = jnp.exp(s - m)
                den = jnp.sum(e, axis=-1, keepdims=True)
                pv = _dot(e.astype(BF16), v_pair)
                halves.append(pv / den)
                lse = m + jnp.log(den)
                head = hp * 2 + hh
                lse_tile = jnp.where((lane == head) | (lane == ATT_HEADS + head), lse, lse_tile)
            o_ref[i * blk:(i + 1) * blk, c0:c0 + LANES] = jnp.where(lane_half, halves[0], halves[1]).astype(o_ref.dtype)
        lse_ref[i * blk:(i + 1) * blk, :] = lse_tile
    o_out_ref[...] = o_ref[...].reshape(o_out_ref.shape)
    lse_out_ref[...] = lse_ref[...].reshape(lse_out_ref.shape)


def _attn_prompt(a_g, gi, bsz, seq):
    win, dil = ATT_PATTERNS[gi]
    band = win // dil
    blk = ATT_BLOCK
    rows = TOK_TILE // dil
    tiles = seq // TOK_TILE
    length = seq // dil
    tq = min(512, length)
    nj = length // tq
    tq_tiles = tq // rows
    cur = lambda which: pl.BlockSpec((None, tq_tiles, None, rows, ATT_OUT),
                                     lambda b, r, j: (b, j, r, 0, which))
    if rows >= blk:
        prev = lambda which: pl.BlockSpec(
            (None, None, None, blk, ATT_OUT),
            lambda b, r, j: (b, jnp.maximum(j * tq_tiles - 1, 0), r, rows // blk - 1, which))
    else:
        prev = lambda which: pl.BlockSpec(
            (None, blk // rows, None, rows, ATT_OUT),
            lambda b, r, j: (b, jnp.maximum(j * (tq // blk) - 1, 0), r, 0, which))
    return pl.pallas_call(
        functools.partial(_attn_kernel, tq=tq, band=band),
        out_shape=(
            jax.ShapeDtypeStruct((bsz, tiles, dil, rows, ATT_OUT), BF16),
            jax.ShapeDtypeStruct((bsz, tiles, dil, rows, LANES), F32),
        ),
        grid=(bsz, dil, nj),
        in_specs=[cur(0), cur(1), prev(1), cur(2), prev(2)],
        out_specs=(
            pl.BlockSpec((None, tq_tiles, None, rows, ATT_OUT), lambda b, r, j: (b, j, r, 0, 0)),
            pl.BlockSpec((None, tq_tiles, None, rows, LANES), lambda b, r, j: (b, j, r, 0, 0)),
        ),
        scratch_shapes=[
            pltpu.VMEM((tq, ATT_OUT), BF16),
            pltpu.VMEM((blk + tq, ATT_OUT), BF16),
            pltpu.VMEM((blk + tq, ATT_OUT), BF16),
            pltpu.VMEM((tq, ATT_OUT), BF16),
            pltpu.VMEM((tq, LANES), F32),
        ],
        compiler_params=_cparams(("arbitrary", "arbitrary", "arbitrary")),
        name=f"attn_prompt_w{win}",
    )(a_g, a_g, a_g, a_g, a_g)


def _router_gates(logits):
    shape = logits.shape
    lane = lax.broadcasted_iota(jnp.int32, shape, 1)
    big = jnp.int32(1 << 20)
    neg = -jnp.inf
    is_grp = (lane >= MOE_EXPERTS) & (lane < MOE_EXPERTS + MOE_GROUPS)
    lg = jnp.where(is_grp, logits, neg)
    gm = jnp.max(lg, axis=-1, keepdims=True)
    g_lane = jnp.min(jnp.where(lg == gm, lane, big), axis=-1, keepdims=True)
    g_sum = jnp.sum(jnp.exp(lg - gm), axis=-1, keepdims=True)
    g_w = 1.0 / g_sum
    lo = (g_lane - MOE_EXPERTS) * MOE_PER_GROUP
    in_grp = (lane >= lo) & (lane < lo + MOE_PER_GROUP)
    le = jnp.where(in_grp, logits, neg)
    m1 = jnp.max(le, axis=-1, keepdims=True)
    i1 = jnp.min(jnp.where(le == m1, lane, big), axis=-1, keepdims=True)
    le2 = jnp.where(lane == i1, neg, le)
    m2 = jnp.max(le2, axis=-1, keepdims=True)
    i2 = jnp.min(jnp.where(le2 == m2, lane, big), axis=-1, keepdims=True)
    t = jnp.exp(m2 - m1)
    w1 = 1.0 / (1.0 + t)
    w2 = t / (1.0 + t)
    gate = jnp.where(lane == i1, g_w * w1, jnp.where(lane == i2, g_w * w2, 0.0))
    e_lo = jnp.minimum(i1, i2)
    e_hi = jnp.maximum(i1, i2)
    lo_l = e_lo & (MOE_PER_GROUP - 1)
    hi_l = e_hi & (MOE_PER_GROUP - 1)
    cls = (e_lo >> 2) * MOE_PAIRS + ((lo_l * (7 - lo_l)) >> 1) + (hi_l - lo_l - 1)
    return gate, cls


def _post_kernel(*refs, passes, merge_attn):
    if merge_attn:
        (x_ref, yn_ref, o0_ref, o1_ref, o2_ref, l0_ref, l1_ref, l2_ref, gates_ref, mod_ref, n2_ref,
         wssd_ref, wattn_ref, wout_ref, wr_ref, br_ref, x1_ref, h2_ref, pos_ref, cnt_ref,
         o1_scr, o2_scr, l1_scr, l2_scr) = refs
        for src_ref, dst_ref in ((o1_ref, o1_scr), (o2_ref, o2_scr), (l1_ref, l1_scr), (l2_ref, l2_scr)):
            dil, r_len, width = src_ref.shape
            for r in range(dil):
                blk_r = src_ref[r].astype(F32)
                for cbk in range(width // LANES):
                    dst_ref[cbk, pl.ds(r, r_len, stride=dil), :] = blk_r[:, cbk * LANES:(cbk + 1) * LANES]
        o_nat = [o0_ref[...].astype(F32)] + [
            jnp.concatenate([scr[cbk] for cbk in range(ATT_OUT // LANES)], axis=1) for scr in (o1_scr, o2_scr)]
        l0, l1, l2 = l0_ref[...], l1_scr[0], l2_scr[0]
        mx = jnp.maximum(jnp.maximum(l0, l1), l2)
        e0, e1, e2 = jnp.exp(l0 - mx), jnp.exp(l1 - mx), jnp.exp(l2 - mx)
        inv = 1.0 / (e0 + e1 + e2)
        lane = lax.broadcasted_iota(jnp.int32, l0.shape, 1)
        e8 = _head_expand_matrix(LANES, ATT_HEADS, ATT_HEAD_DIM)
        r = lax.broadcasted_iota(jnp.int32, e8.shape, 0)
        c = lax.broadcasted_iota(jnp.int32, e8.shape, 1)
        e8 = jnp.where((c // ATT_HEAD_DIM) == (r - ATT_HEADS), 1.0, e8.astype(F32)).astype(BF16)
        o = None
        for e_g, o_g in zip((e0, e1, e2), o_nat):
            w = e_g * inv
            hi, lo = _split2(w)
            w_exp = _dot(jnp.where(lane < ATT_HEADS, hi, lo), e8)
            term = w_exp * o_g
            o = term if o is None else o + term
    else:
        (x_ref, yn_ref, o_ref, gates_ref, mod_ref, n2_ref,
         wssd_ref, wattn_ref, wout_ref, wr_ref, br_ref, x1_ref, h2_ref, gate_ref) = refs
        o = o_ref[...]
    d = D_MODEL
    g1 = mod_ref[:, 0:d]
    sh2 = mod_ref[:, d:2 * d]
    sc2 = mod_ref[:, 2 * d:3 * d]
    ssd_branch = _mm(yn_ref[...], wssd_ref[...], passes)
    attn_branch = _mm(o, wattn_ref[...], passes)
    ga = gates_ref[:, 0:d].astype(F32)
    gb = gates_ref[:, d:2 * d].astype(F32)
    mixed = _sigmoid(ga) * ssd_branch + _sigmoid(gb) * attn_branch
    x1 = x_ref[...] + g1 * _mm(mixed, wout_ref[...], passes)
    x1_ref[...] = x1
    h2 = _rms_modulate(x1, n2_ref[...], sh2, sc2)
    logits = _mm(h2, wr_ref[...], 3) + br_ref[...]
    gate, cls = _router_gates(logits)
    if not merge_attn:
        h2_ref[...] = h2
        gate_ref[...] = gate
        return
    tm = h2.shape[0]
    for cbk in range(ROW_SUB):
        h2_ref[pl.ds(cbk, tm, stride=ROW_SUB), :] = h2[:, cbk * LANES:(cbk + 1) * LANES]
    step = pl.program_id(0)

    @pl.when(step == 0)
    def _():
        cnt_ref[...] = jnp.zeros_like(cnt_ref)

    lane = lax.broadcasted_iota(jnp.int32, (tm, LANES), 1)
    onehot = lane == cls
    ri = lax.broadcasted_iota(jnp.int32, (tm, tm), 0)
    ci = lax.broadcasted_iota(jnp.int32, (tm, tm), 1)
    before = jnp.where(ci < ri, 1.0, 0.0).astype(BF16)
    seen = _dot(before, jnp.where(onehot, 1.0, 0.0).astype(BF16)) + cnt_ref[...]
    rank = jnp.sum(jnp.where(onehot, seen, 0.0), axis=-1, keepdims=True)
    n_tokens = tm * pl.num_programs(0)
    pos_ref[...] = jnp.broadcast_to(cls.astype(F32) * n_tokens + rank, (tm, LANES))
    cnt_ref[...] = cnt_ref[...] + jnp.sum(jnp.where(onehot, 1.0, 0.0), axis=0, keepdims=True)


def _post(x, yn, attn_in, gates, mod, norm2_g, wssd, wattn, wout, wr, br, *, passes, merge_attn,
          tm, rows_per_mod, h2_dtype):
    n, d = x.shape
    per_row_mod = rows_per_mod == 1
    if per_row_mod:
        mod_spec = pl.BlockSpec((tm, 3 * d), lambda i: (i, 0))
    else:
        tiles = rows_per_mod // tm
        mod_spec = pl.BlockSpec((None, 1, 3 * d), lambda i: (i // tiles, 0, 0))
    row = lambda w: pl.BlockSpec((tm, w), lambda i: (i, 0))
    const = lambda a: pl.BlockSpec(a.shape, lambda i: (0, 0), pipeline_mode=pl.Buffered(1))
    scratch = []
    if merge_attn:
        assert tm == TOK_TILE
        o_list, lse_list = attn_in
        attn_args = list(o_list) + list(lse_list)
        tiles = rows_per_mod // tm

        def tile_spec(a):
            _, _, dil, r_len, w = a.shape
            if dil == 1:
                return pl.BlockSpec((None, None, None, r_len, w), lambda i: (i // tiles, i % tiles, 0, 0, 0))
            return pl.BlockSpec((None, None, dil, r_len, w), lambda i: (i // tiles, i % tiles, 0, 0, 0))

        attn_specs = [tile_spec(a) for a in attn_args]
        scratch = [pltpu.VMEM((ATT_OUT // LANES, tm, LANES), F32), pltpu.VMEM((ATT_OUT // LANES, tm, LANES), F32),
                   pltpu.VMEM((1, tm, LANES), F32), pltpu.VMEM((1, tm, LANES), F32)]
    else:
        attn_args = [attn_in]
        attn_specs = [row(ATT_OUT)]
    if merge_attn:
        out_shape = (jax.ShapeDtypeStruct((n, d), F32), jax.ShapeDtypeStruct((n * ROW_SUB, LANES), F32),
                     jax.ShapeDtypeStruct((n, LANES), F32), jax.ShapeDtypeStruct((1, LANES), F32))
        out_specs = (row(d), pl.BlockSpec((tm * ROW_SUB, LANES), lambda i: (i, 0)), row(LANES),
                     pl.BlockSpec((1, LANES), lambda i: (0, 0)))
    else:
        out_shape = (jax.ShapeDtypeStruct((n, d), F32), jax.ShapeDtypeStruct((n, d), h2_dtype),
                     jax.ShapeDtypeStruct((n, LANES), F32))
        out_specs = (row(d), row(d), row(LANES))
    return pl.pallas_call(
        functools.partial(_post_kernel, passes=passes, merge_attn=merge_attn),
        out_shape=out_shape,
        grid=(n // tm,),
        in_specs=[row(d), row(SSD_INNER)] + attn_specs + [row(IN_GATE), mod_spec, const(norm2_g),
                                                          const(wssd), const(wattn), const(wout),
                                                          const(wr), const(br)],
        out_specs=out_specs,
        scratch_shapes=scratch,
        compiler_params=_cparams(("arbitrary",)),
        name="post_merge" if merge_attn else "post_sample",
    )(x, yn, *attn_args, gates, mod, norm2_g, wssd, wattn, wout, wr, br)


def _moe_kernel(h2_ref, gate_ref, x1_ref, g2_ref, wg_ref, wu_ref, wd_ref, x2_ref, acc_ref, *, passes):
    e = pl.program_id(1)

    @pl.when(e == 0)
    def _():
        acc_ref[...] = jnp.zeros_like(acc_ref)

    h2 = h2_ref[...]
    hg = _mm(h2, wg_ref[...], passes)
    hu = _mm(h2, wu_ref[...], passes)
    gate = gate_ref[...]
    lane = lax.broadcasted_iota(jnp.int32, gate.shape, 1)
    gcol = jnp.sum(jnp.where(lane == e, gate, 0.0), axis=-1, keepdims=True)
    act = _silu(hg) * hu * gcol
    acc_ref[...] += _mm(act, wd_ref[...], passes)

    @pl.when(e == pl.num_programs(1) - 1)
    def _():
        x2_ref[...] = x1_ref[...] + g2_ref[...] * acc_ref[...]


def _moe(layer, h2, gate, x1, g2, wg, wu, wd, *, passes, tm, rows_per_mod):
    n, d = x1.shape
    if rows_per_mod == 1:
        g2_spec = pl.BlockSpec((tm, d), lambda i, e: (i, 0))
    else:
        tiles = rows_per_mod // tm
        g2_spec = pl.BlockSpec((None, 1, d), lambda i, e: (i // tiles, 0, 0))
    row = lambda w: pl.BlockSpec((tm, w), lambda i, e: (i, 0))
    return pl.pallas_call(
        functools.partial(_moe_kernel, passes=passes),
        out_shape=jax.ShapeDtypeStruct((n, d), F32),
        grid=(n // tm, MOE_EXPERTS),
        in_specs=[
            row(d), row(LANES), row(d), g2_spec,
            pl.BlockSpec((None, None, d, MOE_FF), lambda i, e: (layer, e, 0, 0)),
            pl.BlockSpec((None, None, d, MOE_FF), lambda i, e: (layer, e, 0, 0)),
            pl.BlockSpec((None, None, MOE_FF, d), lambda i, e: (layer, e, 0, 0)),
        ],
        out_specs=row(d),
        scratch_shapes=[pltpu.VMEM((tm, d), F32)],
        compiler_params=_cparams(("arbitrary", "arbitrary")),
        name="moe_dense",
    )(h2, gate, x1, g2, wg, wu, wd)


_PAIR_LO = (0, 0, 0, 1, 1, 2)
_PAIR_HI = (1, 2, 3, 2, 3, 3)
_DMA_UNROLL = 8
_ROUTE_TILE = 1024


def _moe_plan(counts, n_tokens):
    tm = MOE_TILE
    blocks_per_class = n_tokens // tm
    ntile = (counts + tm - 1) // tm
    cum = jnp.cumsum(ntile)
    total = cum[-1]
    t_max = blocks_per_class + MOE_CLASSES
    t_eff = jnp.minimum(jnp.arange(t_max, dtype=jnp.int32), total - 1)
    cls = jnp.searchsorted(cum, t_eff, side="right").astype(jnp.int32)
    blk = cls * blocks_per_class + (t_eff - (cum - ntile)[cls])
    grp, pair = cls // MOE_PAIRS, cls % MOE_PAIRS
    e_lo = grp * MOE_PER_GROUP + jnp.asarray(_PAIR_LO, jnp.int32)[pair]
    e_hi = grp * MOE_PER_GROUP + jnp.asarray(_PAIR_HI, jnp.int32)[pair]
    z_need = ((counts % tm) != 0).astype(jnp.int32)
    z_blk = jnp.arange(MOE_CLASSES, dtype=jnp.int32) * blocks_per_class + jnp.maximum(ntile - 1, 0)
    return (blk.astype(jnp.int32), e_lo, e_hi, total.reshape(1).astype(jnp.int32), z_need, z_blk.astype(jnp.int32))


def _row_copy(src, src_row, dst, dst_row, sem):
    return pltpu.make_async_copy(src.at[pl.ds(pl.multiple_of(src_row * ROW_SUB, ROW_SUB), ROW_SUB), :],
                                 dst.at[pl.ds(pl.multiple_of(dst_row * ROW_SUB, ROW_SUB), ROW_SUB), :], sem)


def _dispatch_kernel(pos_ref, zneed_ref, zblk_ref, rows_ref, xs_ref, zero_scr, sem, zsem, *, tm):
    step = pl.program_id(0)
    tile_rows = MOE_TILE * ROW_SUB

    def zero_copy(c):
        start = pl.multiple_of(zblk_ref[c] * tile_rows, tile_rows)
        return pltpu.make_async_copy(zero_scr, xs_ref.at[pl.ds(start, tile_rows), :], zsem)

    @pl.when(step == 0)
    def _():
        zero_scr[...] = jnp.zeros_like(zero_scr)
        for c in range(MOE_CLASSES):
            @pl.when(zneed_ref[c] != 0)
            def _():
                zero_copy(c).start()
        for c in range(MOE_CLASSES):
            @pl.when(zneed_ref[c] != 0)
            def _():
                zero_copy(c).wait()

    base = step * tm

    def body(i8, carry):
        for u in range(_DMA_UNROLL):
            i = i8 * _DMA_UNROLL + u
            _row_copy(rows_ref, i, xs_ref, pos_ref[base + i], sem).start(priority=u % 2)
        return carry

    lax.fori_loop(0, tm // _DMA_UNROLL, body, 0)
    pltpu.make_async_copy(rows_ref, xs_ref.at[pl.ds(0, tm * ROW_SUB), :], sem).wait()


def _dispatch(pos, z_need, z_blk, rows, n_tokens, tm=_ROUTE_TILE):
    sorted_rows = MOE_CLASSES * n_tokens * ROW_SUB
    return pl.pallas_call(
        functools.partial(_dispatch_kernel, tm=tm),
        out_shape=jax.ShapeDtypeStruct((sorted_rows, LANES), F32),
        grid_spec=pltpu.PrefetchScalarGridSpec(
            num_scalar_prefetch=3,
            grid=(n_tokens // tm,),
            in_specs=[pl.BlockSpec((tm * ROW_SUB, LANES), lambda i, *_: (i, 0))],
            out_specs=pl.BlockSpec(memory_space=pl.ANY),
            scratch_shapes=[pltpu.VMEM((MOE_TILE * ROW_SUB, LANES), F32),
                            pltpu.SemaphoreType.DMA(()), pltpu.SemaphoreType.DMA(())],
        ),
        compiler_params=pltpu.CompilerParams(dimension_semantics=("arbitrary",), vmem_limit_bytes=VMEM_LIMIT,
                                             disable_bounds_checks=True),
        name="moe_dispatch",
    )(pos, z_need, z_blk, rows)


def _moe_sparse_kernel(blk_ref, elo_ref, ehi_ref, nact_ref, xs_ref, wr_ref, br_ref,
                       wg_lo_ref, wu_lo_ref, wd_lo_ref, wg_hi_ref, wu_hi_ref, wd_hi_ref, ys_ref):
    t = pl.program_id(0)
    tm = MOE_TILE

    @pl.when(t < nact_ref[0])
    def _():
        x = jnp.concatenate([xs_ref[pl.ds(c, tm, stride=ROW_SUB), :] for c in range(ROW_SUB)], axis=1)
        xb = x.astype(BF16)
        w_hi16, w_lo16 = _split2(wr_ref[...])
        logits = _dot(xb, w_hi16) + _dot(xb, w_lo16) + br_ref[...]
        lane = lax.broadcasted_iota(jnp.int32, logits.shape, 1)
        e_lo, e_hi = elo_ref[t], ehi_ref[t]
        g_lane = MOE_EXPERTS + (e_lo >> 2)
        pick = lambda ln: jnp.sum(jnp.where(lane == ln, logits, 0.0), axis=-1, keepdims=True)
        l_lo, l_hi, l_g = pick(e_lo), pick(e_hi), pick(g_lane)
        is_grp = (lane >= MOE_EXPERTS) & (lane < MOE_EXPERTS + MOE_GROUPS)
        g_w = 1.0 / jnp.sum(jnp.where(is_grp, jnp.exp(logits - l_g), 0.0), axis=-1, keepdims=True)
        w_a = g_w / (1.0 + jnp.exp(l_hi - l_lo))
        w_b = g_w / (1.0 + jnp.exp(l_lo - l_hi))
        bf = lambda w_ref: w_ref[...].astype(BF16)
        act_a = (_silu(_dot(xb, bf(wg_lo_ref))) * _dot(xb, bf(wu_lo_ref)) * w_a).astype(BF16)
        act_b = (_silu(_dot(xb, bf(wg_hi_ref))) * _dot(xb, bf(wu_hi_ref)) * w_b).astype(BF16)
        y = _dot(act_a, bf(wd_lo_ref)) + _dot(act_b, bf(wd_hi_ref))
        for c in range(ROW_SUB):
            ys_ref[pl.ds(c, tm, stride=ROW_SUB), :] = y[:, c * LANES:(c + 1) * LANES]


def _moe_sparse(layer, plan, xs, wr, br, wg, wu, wd, n_tokens):
    blk, e_lo, e_hi, n_act = plan
    tm = MOE_TILE
    t_max = blk.shape[0]
    d = D_MODEL
    tile = pl.BlockSpec((tm * ROW_SUB, LANES), lambda t, blk, lo, hi, na: (blk[t], 0))
    w_lo = lambda shape: pl.BlockSpec((None, None) + shape, lambda t, blk, lo, hi, na: (layer, lo[t], 0, 0))
    w_hi = lambda shape: pl.BlockSpec((None, None) + shape, lambda t, blk, lo, hi, na: (layer, hi[t], 0, 0))
    const = lambda a: pl.BlockSpec(a.shape, lambda t, *_: (0, 0))
    return pl.pallas_call(
        _moe_sparse_kernel,
        out_shape=jax.ShapeDtypeStruct(xs.shape, F32),
        grid_spec=pltpu.PrefetchScalarGridSpec(
            num_scalar_prefetch=4,
            grid=(t_max,),
            in_specs=[tile, const(wr), const(br),
                      w_lo((d, MOE_FF)), w_lo((d, MOE_FF)), w_lo((MOE_FF, d)),
                      w_hi((d, MOE_FF)), w_hi((d, MOE_FF)), w_hi((MOE_FF, d))],
            out_specs=tile,
        ),
        compiler_params=_cparams(("arbitrary",)),
        name="moe_sparse",
    )(blk, e_lo, e_hi, n_act, xs, wr, br, wg, wu, wd, wg, wu, wd)


def _combine_kernel(pos_ref, ys_ref, x1_ref, g2_ref, x2_ref, buf, sem, *, tm):
    base = pl.program_id(0) * tm

    def body(i8, carry):
        for u in range(_DMA_UNROLL):
            i = i8 * _DMA_UNROLL + u
            _row_copy(ys_ref, pos_ref[base + i], buf, i, sem).start(priority=u % 2)
        return carry

    lax.fori_loop(0, tm // _DMA_UNROLL, body, 0)
    pltpu.make_async_copy(ys_ref.at[pl.ds(0, tm * ROW_SUB), :], buf, sem).wait()
    y = jnp.concatenate([buf[pl.ds(c, tm, stride=ROW_SUB), :] for c in range(ROW_SUB)], axis=1)
    x2_ref[...] = x1_ref[...] + g2_ref[...] * y


def _combine(pos, ys, x1, g2, rows_per_mod, tm=_ROUTE_TILE):
    n, d = x1.shape
    tiles = rows_per_mod // tm
    return pl.pallas_call(
        functools.partial(_combine_kernel, tm=tm),
        out_shape=jax.ShapeDtypeStruct((n, d), F32),
        grid_spec=pltpu.PrefetchScalarGridSpec(
            num_scalar_prefetch=1,
            grid=(n // tm,),
            in_specs=[pl.BlockSpec(memory_space=pl.ANY),
                      pl.BlockSpec((tm, d), lambda i, *_: (i, 0)),
                      pl.BlockSpec((None, 1, d), lambda i, *_: (i // tiles, 0, 0))],
            out_specs=pl.BlockSpec((tm, d), lambda i, *_: (i, 0)),
            scratch_shapes=[pltpu.VMEM((tm * ROW_SUB, LANES), F32), pltpu.SemaphoreType.DMA(())],
        ),
        compiler_params=pltpu.CompilerParams(dimension_semantics=("arbitrary",), vmem_limit_bytes=VMEM_LIMIT,
                                             disable_bounds_checks=True),
        name="moe_combine",
    )(pos, ys, x1, g2)


def _final_kernel(x_ref, g_ref, o_ref):
    x = x_ref[...]
    ms = jnp.mean(x * x, axis=-1, keepdims=True)
    o_ref[...] = x * lax.rsqrt(ms + EPS) * g_ref[...]


def _final_norm(x, g, tm):
    n, d = x.shape
    return pl.pallas_call(
        _final_kernel,
        out_shape=jax.ShapeDtypeStruct((n, d), F32),
        grid=(n // tm,),
        in_specs=[pl.BlockSpec((tm, d), lambda i: (i, 0)), pl.BlockSpec((1, d), lambda i: (0, 0))],
        out_specs=pl.BlockSpec((tm, d), lambda i: (i, 0)),
        compiler_params=_cparams(("arbitrary",)),
        name="final_norm",
    )(x, g)


def _step_kernel(z_ref, xbc_ref, dt_ref, cst_ref, h_ref,
                 cw_ref, cb_ref, dtb_ref, alog_ref, dskip_ref, ng_ref,
                 yn_ref, cnew_ref, hnew_ref, col_scr):
    x_new = xbc_ref[...]
    acc = cb_ref[...] + cw_ref[3:4, :] * x_new
    for k in range(SSD_CONV - 1):
        acc = acc + cw_ref[k:k + 1, :] * cst_ref[k:k + 1, :]
    cnew_ref[0:1, :] = cst_ref[1:2, :]
    cnew_ref[1:2, :] = cst_ref[2:3, :]
    cnew_ref[2:3, :] = x_new
    xc = _silu(acc)
    xs = xc[:, 0:SSD_INNER]

    e_heads = _head_expand_matrix(LANES, SSD_HEADS, SSD_HEAD_DIM)
    dt_raw8 = jnp.broadcast_to(dt_ref[...], (8, LANES))
    dt_full = _softplus(_mm01(dt_raw8, e_heads, 3)[0:1, :] + dtb_ref[...])
    a_full = -jnp.exp(alog_ref[...])
    dec_full = jnp.exp(dt_full * a_full)
    xdt = xs * dt_full
    col_scr[...] = jnp.zeros_like(col_scr)
    col_scr[0:1, :] = xdt
    col_scr[1:2, :] = dec_full
    cols = col_scr[...].T
    y_parts = []
    gw = SSD_HPG * SSD_HEAD_DIM
    for g in range(SSD_GROUPS):
        b_off = SSD_INNER + g * SSD_STATE
        c_off = SSD_INNER + SSD_GROUPS * SSD_STATE + g * SSD_STATE
        bm = xc[:, b_off:b_off + SSD_STATE]
        cm = xc[:, c_off:c_off + SSD_STATE]
        g0 = g * gw
        hn = h_ref[g0:g0 + gw, :] * cols[g0:g0 + gw, 1:2] + cols[g0:g0 + gw, 0:1] * bm
        hnew_ref[g0:g0 + gw, :] = hn
        t = (hn * cm).T
        y_parts.append(jnp.sum(t, axis=0, keepdims=True))
    y = jnp.concatenate(y_parts, axis=1) + dskip_ref[...] * xs
    y = y * _silu(z_ref[...])
    outs = []
    for g in range(SSD_GROUPS):
        yg = y[:, g * gw:(g + 1) * gw]
        ms = jnp.mean(yg * yg, axis=-1, keepdims=True)
        outs.append(yg * lax.rsqrt(ms + EPS) * ng_ref[:, g * gw:(g + 1) * gw])
    yn_ref[...] = jnp.concatenate(outs, axis=1)


def _step_sample(layer, z, xbc, dt_raw, state_conv, state_ssm,
                 conv_w, conv_b, dtb_full, alog_full, dskip_full, ssd_norm_g):
    bsz = z.shape[0]
    row3 = lambda w: pl.BlockSpec((None, 1, w), lambda b: (b, 0, 0))
    const = lambda a: pl.BlockSpec(a.shape, lambda b: (0,) * a.ndim)
    return pl.pallas_call(
        _step_kernel,
        out_shape=(
            jax.ShapeDtypeStruct((bsz, 1, SSD_INNER), F32),
            jax.ShapeDtypeStruct((bsz, SSD_CONV - 1, SSD_CONV_DIM), F32),
            jax.ShapeDtypeStruct((bsz, SSD_INNER, SSD_STATE), F32),
        ),
        grid=(bsz,),
        in_specs=[
            row3(SSD_INNER), row3(SSD_CONV_DIM), row3(LANES),
            pl.BlockSpec((None, None, SSD_CONV - 1, SSD_CONV_DIM), lambda b: (layer, b, 0, 0)),
            pl.BlockSpec((None, None, SSD_INNER, SSD_STATE), lambda b: (layer, b, 0, 0)),
            const(conv_w), const(conv_b), const(dtb_full), const(alog_full), const(dskip_full), const(ssd_norm_g),
        ],
        out_specs=(
            row3(SSD_INNER),
            pl.BlockSpec((None, SSD_CONV - 1, SSD_CONV_DIM), lambda b: (b, 0, 0)),
            pl.BlockSpec((None, SSD_INNER, SSD_STATE), lambda b: (b, 0, 0)),
        ),
        scratch_shapes=[pltpu.VMEM((LANES, SSD_INNER), F32)],
        compiler_params=_cparams(("arbitrary",)),
        name="step_sample",
    )(z.reshape(bsz, 1, -1), xbc.reshape(bsz, 1, -1), dt_raw.reshape(bsz, 1, -1), state_conv, state_ssm,
      conv_w, conv_b, dtb_full, alog_full, dskip_full, ssd_norm_g)


_HEAD_SPLIT = 2
_HROWS = ATT_OUT // _HEAD_SPLIT


def _cache_attn_kernel(*refs, n_alias):
    qkv_ref = refs[0]
    cache_refs = refs[1:7]
    out_refs = refs[7 + n_alias:13 + n_alias]
    o_ref = refs[13 + n_alias]
    b = pl.program_id(0)
    nh = _HROWS // ATT_HEAD_DIM
    scale = ATT_HEAD_DIM ** -0.5
    qkv = qkv_ref[...]
    lane_b = lax.broadcasted_iota(jnp.int32, qkv.shape, 2)
    cols = jnp.sum(jnp.where(lane_b == b, qkv, 0.0), axis=-1, keepdims=True)

    def per_head_rows(v):
        return jnp.concatenate([jnp.broadcast_to(v[h:h + 1, :], (ATT_HEAD_DIM, 1)) for h in range(nh)], axis=0)

    o_g, lse_g = [], []
    for gi, (_, dil) in enumerate(ATT_PATTERNS):
        k_ref, v_ref = cache_refs[2 * gi], cache_refs[2 * gi + 1]
        ko_ref, vo_ref = out_refs[2 * gi], out_refs[2 * gi + 1]
        q = cols[gi]
        k_new = cols[ATT_GROUPS + gi]
        v_new = cols[2 * ATT_GROUPS + gi]
        kk = k_ref[...]
        vv = v_ref[...]
        length = kk.shape[1]
        lane = lax.broadcasted_iota(jnp.int32, (nh, length), 1)
        s = jnp.sum((kk * q).reshape(nh, ATT_HEAD_DIM, length), axis=1) * scale
        s = jnp.where((lane & (dil - 1)) == 0, s, -jnp.inf)
        s_new = jnp.sum((k_new * q).reshape(nh, ATT_HEAD_DIM, 1), axis=1) * scale
        m = jnp.maximum(jnp.max(s, axis=-1, keepdims=True), s_new)
        e = jnp.exp(s - m)
        e_new = jnp.exp(s_new - m)
        den = jnp.sum(e, axis=-1, keepdims=True) + e_new
        acc = jnp.sum(vv.reshape(nh, ATT_HEAD_DIM, length) * e[:, None, :], axis=-1, keepdims=True)
        acc = acc.reshape(_HROWS, 1) + per_head_rows(e_new) * v_new
        o_g.append(acc / per_head_rows(den))
        lse_g.append(per_head_rows(m + jnp.log(den)))
        lane_full = lax.broadcasted_iota(jnp.int32, kk.shape, 1)
        last = lane_full == length - 1
        ko_ref[...] = jnp.where(last, k_new, pltpu.roll(kk, length - 1, axis=1))
        vo_ref[...] = jnp.where(last, v_new, pltpu.roll(vv, length - 1, axis=1))
    mx = jnp.maximum(jnp.maximum(lse_g[0], lse_g[1]), lse_g[2])
    w = [jnp.exp(l - mx) for l in lse_g]
    tot = w[0] + w[1] + w[2]
    o_ref[...] = (w[0] / tot) * o_g[0] + (w[1] / tot) * o_g[1] + (w[2] / tot) * o_g[2]


def _cache_attn(layer, qkv_t, cache_views, prev_outs):
    depth, bsz, _, _ = cache_views[0].shape
    n_alias = 0 if prev_outs is None else len(prev_outs)
    q4 = qkv_t.reshape(3 * ATT_GROUPS, _HEAD_SPLIT, _HROWS, bsz)
    blk = lambda c: pl.BlockSpec((None, None, _HROWS, c.shape[3]), lambda b, hh: (layer, b, hh, 0))
    any_spec = pl.BlockSpec(memory_space=pl.ANY)
    args = [q4] + list(cache_views) + ([] if prev_outs is None else list(prev_outs))
    res = pl.pallas_call(
        functools.partial(_cache_attn_kernel, n_alias=n_alias),
        out_shape=tuple(jax.ShapeDtypeStruct(c.shape, c.dtype) for c in cache_views)
        + (jax.ShapeDtypeStruct((bsz, _HEAD_SPLIT, _HROWS, 1), F32),),
        grid=(bsz, _HEAD_SPLIT),
        in_specs=[pl.BlockSpec((3 * ATT_GROUPS, None, _HROWS, bsz), lambda b, hh: (0, hh, 0, 0))]
        + [blk(c) for c in cache_views] + [any_spec] * n_alias,
        out_specs=tuple(blk(c) for c in cache_views)
        + (pl.BlockSpec((None, None, _HROWS, 1), lambda b, hh: (b, hh, 0, 0)),),
        input_output_aliases={7 + i: i for i in range(n_alias)},
        compiler_params=_cparams(("arbitrary", "arbitrary")),
        name="cache_attn",
    )(*args)
    return list(res[:6]), res[6].reshape(bsz, ATT_OUT)


def _pad_lanes(v, width=LANES):
    return jnp.pad(v, [(0, 0)] * (v.ndim - 1) + [(0, width - v.shape[-1])])


def kernel(x_prompt, x_sample, c_prompt, c_sample, state_conv, state_ssm, cache_k_win128, cache_v_win128, cache_k_win512, cache_v_win512, cache_k_win2048, cache_v_win2048, norm1_g, w_ada, b_ada, w_in, conv_w, conv_b, dt_bias, a_log, d_skip, ssd_norm_g, w_ssd_proj, w_attn_proj, w_out, norm2_g, w_router_group, b_router_group, w_router_expert, b_router_expert, w_exp_gate, w_exp_up, w_exp_down, final_norm_g):
    depth = w_in.shape[0]
    bp, seq, d = x_prompt.shape
    bs = x_sample.shape[0]
    assert x_sample.shape[1] == 1 and d == D_MODEL and w_in.shape[2] == IN_WIDTH
    assert seq % (ATT_PATTERNS[-1][1] * ATT_BLOCK) == 0
    n_p = bp * seq
    caches = ((cache_k_win128, cache_v_win128), (cache_k_win512, cache_v_win512),
              (cache_k_win2048, cache_v_win2048))

    rows = bp + bs
    rows_pad = -(-rows // 8) * 8
    c_all = jnp.pad(jnp.concatenate([c_prompt, c_sample], axis=0), ((0, rows_pad - rows), (0, 0)))
    mods = _modulation(c_all, w_ada, b_ada)

    xp = x_prompt.reshape(n_p, d)
    xs = x_sample.reshape(bs, d)
    outs = {k: [] for k in ("conv_p", "conv_s", "ssm_p", "ssm_s")}
    kv_p = [[[], []] for _ in ATT_PATTERNS]
    tiles = seq // TOK_TILE
    assert bs <= LANES and all(c.shape[2] == win for (win, _), pair in zip(ATT_PATTERNS, caches) for c in pair)
    cache_views = [jnp.transpose(c, (0, 1, 3, 4, 2)).reshape(depth, bs, ATT_OUT, c.shape[2])
                   for pair in caches for c in pair]
    shifted = None

    for l in range(depth):
        mod_p = mods[l, :bp]
        mod_s = mods[l, bp:bp + bs]
        w_in_l = w_in[l]
        wz = w_in_l[:, 0:OFF_XBC].astype(BF16)
        wx = w_in_l[:, OFF_XBC:OFF_DT].astype(BF16)
        wd = _pad_lanes(w_in_l[:, OFF_DT:OFF_QKV]).astype(BF16)
        wqs = [jnp.concatenate([w_in_l[:, OFF_QKV + (which * ATT_GROUPS + gi) * ATT_OUT:
                                          OFF_QKV + (which * ATT_GROUPS + gi + 1) * ATT_OUT]
                                for which in range(3)], axis=1).astype(BF16) for gi in range(ATT_GROUPS)]
        wgt = w_in_l[:, OFF_GATE:IN_WIDTH].astype(BF16)
        g1n = norm1_g[l].reshape(1, d)
        g2n = norm2_g[l].reshape(1, d)
        cw = conv_w[l]
        cb = conv_b[l].reshape(1, -1)
        dskip_full = jnp.repeat(d_skip[l], SSD_HEAD_DIM).reshape(1, -1)
        ssd_g = ssd_norm_g[l].reshape(1, -1)
        w_router = _pad_lanes(jnp.concatenate([w_router_expert[l], w_router_group[l]], axis=1))
        b_router = _pad_lanes(jnp.concatenate([b_router_expert[l], b_router_group[l]], axis=0).reshape(1, -1))

        z, xbc, dt_raw, a0, a1, a2, gates = _in_proj_prompt(
            xp, mod_p[:, 0:2 * d].reshape(bp, 1, 2 * d), g1n, wz, wx, wd, wqs, wgt, bp, seq)
        yn, ssm_new = _ssd_prompt(xbc, z, dt_raw, cw, cb, _pad_lanes(dt_bias[l].reshape(1, -1)),
                                  _pad_lanes(a_log[l].reshape(1, -1)), dskip_full, ssd_g, bp, seq)
        outs["ssm_p"].append(ssm_new.reshape(bp, SSD_GROUPS, SSD_HPG, SSD_HEAD_DIM, SSD_STATE))
        outs["conv_p"].append(xbc.reshape(bp, seq, SSD_CONV_DIM)[:, seq - (SSD_CONV - 1):])
        a_groups = [a0.reshape(bp, tiles, 1, TOK_TILE, 3 * ATT_OUT), a1, a2]
        o_list, lse_list = [], []
        for gi, (win, dil) in enumerate(ATT_PATTERNS):
            o_g, lse_g = _attn_prompt(a_groups[gi], gi, bp, seq)
            o_list.append(o_g)
            lse_list.append(lse_g)
            keep = min(win, seq)
            rows = TOK_TILE // dil
            if keep >= TOK_TILE:
                nt = keep // TOK_TILE
                tail = a_groups[gi][:, tiles - nt:, :, :, ATT_OUT:3 * ATT_OUT]
                tail = jnp.transpose(tail, (0, 1, 3, 2, 4)).reshape(bp, keep, 2 * ATT_OUT)
            else:
                assert dil == 1
                tail = a_groups[gi][:, tiles - 1, 0, TOK_TILE - keep:, ATT_OUT:3 * ATT_OUT]
            tail = tail.astype(F32)
            kv_p[gi][0].append(tail[:, :, 0:ATT_OUT].reshape(bp, keep, ATT_HEADS, ATT_HEAD_DIM))
            kv_p[gi][1].append(tail[:, :, ATT_OUT:2 * ATT_OUT].reshape(bp, keep, ATT_HEADS, ATT_HEAD_DIM))
        mod_post = jnp.concatenate([mod_p[:, 2 * d:3 * d], mod_p[:, 3 * d:5 * d]], axis=1).reshape(bp, 1, 3 * d)
        x1, rows, slot, cnt = _post(xp, yn, (o_list, lse_list), gates, mod_post, g2n,
                                    w_ssd_proj[l].astype(BF16), w_attn_proj[l].astype(BF16),
                                    w_out[l].astype(BF16), w_router, b_router, passes=1, merge_attn=True,
                                    tm=TOK_TILE, rows_per_mod=seq, h2_dtype=F32)
        pos = slot[:, 0].astype(jnp.int32)
        blk, e_lo, e_hi, n_act, z_need, z_blk = _moe_plan(cnt[0, :MOE_CLASSES].astype(jnp.int32), n_p)
        xs_sorted = _dispatch(pos, z_need, z_blk, rows, n_p)
        ys_sorted = _moe_sparse(l, (blk, e_lo, e_hi, n_act), xs_sorted, w_router, b_router,
                                w_exp_gate, w_exp_up, w_exp_down, n_p)
        xp = _combine(pos, ys_sorted, x1, mod_p[:, 5 * d:6 * d].reshape(bp, 1, d), seq)

        u = _in_proj_sample(xs, mod_s[:, 0:2 * d], g1n, w_in_l)
        z_s = u[:, 0:OFF_XBC]
        xbc_s = u[:, OFF_XBC:OFF_DT]
        dt_s = _pad_lanes(u[:, OFF_DT:OFF_QKV])
        qkv_s = u[:, OFF_QKV:OFF_GATE]
        gates_s = u[:, OFF_GATE:IN_WIDTH]
        yn_s, conv_new, ssm_new_s = _step_sample(
            l, z_s, xbc_s, dt_s, state_conv, state_ssm.reshape(depth, bs, SSD_INNER, SSD_STATE),
            cw, cb, jnp.repeat(dt_bias[l], SSD_HEAD_DIM).reshape(1, -1),
            jnp.repeat(a_log[l], SSD_HEAD_DIM).reshape(1, -1), dskip_full, ssd_g)
        outs["conv_s"].append(conv_new)
        outs["ssm_s"].append(ssm_new_s.reshape(bs, SSD_GROUPS, SSD_HPG, SSD_HEAD_DIM, SSD_STATE))
        shifted, o_s = _cache_attn(l, qkv_s.T, cache_views, shifted)
        mod_post_s = jnp.concatenate([mod_s[:, 2 * d:3 * d], mod_s[:, 3 * d:5 * d]], axis=1)
        x1_s, h2_s, gate_s = _post(xs, yn_s.reshape(bs, SSD_INNER), o_s, gates_s, mod_post_s,
                                   g2n, w_ssd_proj[l], w_attn_proj[l], w_out[l], w_router, b_router,
                                   passes=3, merge_attn=False, tm=bs, rows_per_mod=1, h2_dtype=F32)
        xs = _moe(l, h2_s, gate_s, x1_s, mod_s[:, 5 * d:6 * d], w_exp_gate, w_exp_up, w_exp_down,
                  passes=3, tm=bs, rows_per_mod=1)

    fg = final_norm_g.reshape(1, d)
    y_prompt = _final_norm(xp, fg, 1024).reshape(bp, seq, d)
    y_sample = _final_norm(xs, fg, bs).reshape(bs, 1, d)

    shifted = [jnp.transpose(s.reshape(depth, bs, ATT_HEADS, ATT_HEAD_DIM, s.shape[3]), (0, 1, 4, 2, 3))
               for s in shifted]

    st = jnp.stack
    res = [y_prompt, y_sample, st(outs["conv_p"]), st(outs["conv_s"]), st(outs["ssm_p"]), st(outs["ssm_s"])]
    for gi in range(ATT_GROUPS):
        res += [st(kv_p[gi][0]), shifted[2 * gi], st(kv_p[gi][1]), shifted[2 * gi + 1]]
    return tuple(res)
```

```python
import functools
import math

import jax
import jax.numpy as jnp
from jax import lax
from jax.experimental import pallas as pl
from jax.experimental.pallas import tpu as pltpu

F32 = jnp.float32
BF16 = jnp.bfloat16

D_MODEL = 1024
SSD_INNER = 1024
SSD_HEAD_DIM = 64
SSD_HEADS = 16
SSD_GROUPS = 2
SSD_HPG = 8
SSD_STATE = 128
SSD_CONV = 4
SSD_CHUNK = 128
SSD_CONV_DIM = SSD_INNER + 2 * SSD_GROUPS * SSD_STATE
ATT_PATTERNS = ((128, 1), (512, 4), (2048, 16))
ATT_GROUPS = 3
ATT_HEADS = 8
ATT_HEAD_DIM = 64
ATT_BLOCK = 128
ATT_OUT = ATT_HEADS * ATT_HEAD_DIM
MOE_GROUPS = 4
MOE_PER_GROUP = 4
MOE_EXPERTS = 16
MOE_FF = 512
MOE_PAIRS = 6
MOE_CLASSES = MOE_GROUPS * MOE_PAIRS
MOE_TILE = 256
ROW_SUB = 8
IN_QKV = 3 * ATT_GROUPS * ATT_OUT
IN_GATE = 2 * D_MODEL
OFF_XBC = SSD_INNER
OFF_DT = OFF_XBC + SSD_CONV_DIM
OFF_QKV = OFF_DT + SSD_HEADS
OFF_GATE = OFF_QKV + IN_QKV
IN_WIDTH = OFF_GATE + IN_GATE
EPS = 1e-6

LANES = 128
VMEM_LIMIT = 56 * 1024 * 1024


def _cparams(sem, vmem=VMEM_LIMIT):
    return pltpu.CompilerParams(dimension_semantics=sem, vmem_limit_bytes=vmem)


def _split2(a):
    hi = a.astype(BF16)
    lo = (a - hi.astype(F32)).astype(BF16)
    return hi, lo


def _split3(a):
    hi = a.astype(BF16)
    r = a - hi.astype(F32)
    mid = r.astype(BF16)
    lo = (r - mid.astype(F32)).astype(BF16)
    return hi, mid, lo


def _dot(a, b):
    return jnp.dot(a, b, preferred_element_type=F32)


def _dot_nt(a, b):
    return lax.dot_general(a, b, (((1,), (1,)), ((), ())), preferred_element_type=F32)


def _mm(a, w, passes):
    if passes == 1:
        return _dot(a.astype(BF16), w.astype(BF16))
    a = a.astype(F32)
    w = w.astype(F32)
    a_hi, a_lo = _split2(a)
    w_hi, w_lo = _split2(w)
    return _dot(a_hi, w_hi) + (_dot(a_lo, w_hi) + _dot(a_hi, w_lo))


def _mm01(a, e01, terms):
    parts = _split3(a) if terms == 3 else _split2(a)
    out = _dot(parts[0], e01)
    for p in parts[1:]:
        out = out + _dot(p, e01)
    return out


def _sigmoid(x):
    return 1.0 / (1.0 + jnp.exp(-x))


def _silu(x):
    return x * _sigmoid(x)


def _softplus(x):
    return jnp.maximum(x, 0.0) + jnp.log(1.0 + jnp.exp(-jnp.abs(x)))


def _head_expand_matrix(rows, n_heads, width):
    r = lax.broadcasted_iota(jnp.int32, (rows, n_heads * width), 0)
    c = lax.broadcasted_iota(jnp.int32, (rows, n_heads * width), 1)
    return jnp.where((c // width) == r, 1.0, 0.0).astype(BF16)


def _rms_modulate(x, g, sh, sc):
    ms = jnp.mean(x * x, axis=-1, keepdims=True)
    y = x * lax.rsqrt(ms + EPS) * g
    return y * (1.0 + sc) + sh


def _mod_kernel(c_ref, w_ref, b_ref, o_ref):
    a = _silu(c_ref[...])
    o_ref[...] = _mm(a, w_ref[...], 3) + b_ref[...]


def _modulation(c_all, w_ada, b_ada):
    depth, d, n6 = w_ada.shape
    rows = c_all.shape[0]
    tn = 1024
    return pl.pallas_call(
        _mod_kernel,
        out_shape=jax.ShapeDtypeStruct((depth, rows, n6), F32),
        grid=(depth, n6 // tn),
        in_specs=[
            pl.BlockSpec((rows, d), lambda l, j: (0, 0)),
            pl.BlockSpec((None, d, tn), lambda l, j: (l, 0, j)),
            pl.BlockSpec((None, 1, tn), lambda l, j: (l, 0, j)),
        ],
        out_specs=pl.BlockSpec((None, rows, tn), lambda l, j: (l, 0, j)),
        compiler_params=_cparams(("arbitrary", "arbitrary")),
        name="adaln_mod",
    )(c_all, w_ada, b_ada.reshape(depth, 1, n6))


_IN_CHUNK = 512
TOK_TILE = 512


def _in_kernel(x_ref, mod_ref, g_ref, wz_ref, wx_ref, wd_ref, wq0_ref, wq1_ref, wq2_ref, wg_ref,
               z_ref, xbc_ref, dt_ref, a0_ref, a1_ref, a2_ref, gates_ref, h_scr):
    sh = mod_ref[:, 0:D_MODEL]
    sc = mod_ref[:, D_MODEL:2 * D_MODEL]
    hf = _rms_modulate(x_ref[...], g_ref[...], sh, sc)
    n_cb = D_MODEL // LANES
    for cbk in range(n_cb):
        h_scr[cbk] = hf[:, cbk * LANES:(cbk + 1) * LANES]
    h = hf.astype(BF16)
    for w_ref, o_ref in ((wz_ref, z_ref), (wx_ref, xbc_ref), (wq0_ref, a0_ref), (wg_ref, gates_ref)):
        width = o_ref.shape[-1]
        for c0 in range(0, width, _IN_CHUNK):
            o_ref[:, c0:c0 + _IN_CHUNK] = _dot(h, w_ref[:, c0:c0 + _IN_CHUNK]).astype(o_ref.dtype)
    dt_ref[...] = _dot(h, wd_ref[...])
    for w_ref, a_ref in ((wq1_ref, a1_ref), (wq2_ref, a2_ref)):
        d, r_len, width = a_ref.shape
        hp = jnp.concatenate(
            [jnp.concatenate([h_scr[cbk, pl.ds(r, r_len, stride=d), :] for cbk in range(n_cb)], axis=1)
             for r in range(d)], axis=0).astype(BF16)
        for c0 in range(0, width, _IN_CHUNK):
            res = _dot(hp, w_ref[:, c0:c0 + _IN_CHUNK]).astype(BF16)
            a_ref[:, :, c0:c0 + _IN_CHUNK] = res.reshape(d, r_len, _IN_CHUNK)


def _in_proj_prompt(x, mod, norm_g, wz, wx, wd, wqs, wg, bsz, seq):
    n, d = x.shape
    tm = TOK_TILE
    tiles_per_seq = seq // tm
    qw = 3 * ATT_OUT

    def const(shape):
        return pl.BlockSpec(shape, lambda i: (0, 0), pipeline_mode=pl.Buffered(1))

    a_shapes, a_specs = [], []
    for _, dil in ATT_PATTERNS[1:]:
        a_shapes.append(jax.ShapeDtypeStruct((bsz, tiles_per_seq, dil, tm // dil, qw), BF16))
        a_specs.append(pl.BlockSpec((None, None, dil, tm // dil, qw),
                                    lambda i: (i // tiles_per_seq, i % tiles_per_seq, 0, 0, 0)))
    return pl.pallas_call(
        _in_kernel,
        out_shape=(
            jax.ShapeDtypeStruct((n, SSD_INNER), BF16),
            jax.ShapeDtypeStruct((n, SSD_CONV_DIM), F32),
            jax.ShapeDtypeStruct((n, LANES), F32),
            jax.ShapeDtypeStruct((n, qw), BF16),
            a_shapes[0], a_shapes[1],
            jax.ShapeDtypeStruct((n, IN_GATE), BF16),
        ),
        grid=(n // tm,),
        in_specs=[
            pl.BlockSpec((tm, d), lambda i: (i, 0)),
            pl.BlockSpec((None, 1, 2 * d), lambda i: (i // tiles_per_seq, 0, 0)),
            const((1, d)),
            const(wz.shape), const(wx.shape), const(wd.shape),
            const(wqs[0].shape), const(wqs[1].shape), const(wqs[2].shape), const(wg.shape),
        ],
        out_specs=(
            pl.BlockSpec((tm, SSD_INNER), lambda i: (i, 0)),
            pl.BlockSpec((tm, SSD_CONV_DIM), lambda i: (i, 0)),
            pl.BlockSpec((tm, LANES), lambda i: (i, 0)),
            pl.BlockSpec((tm, qw), lambda i: (i, 0)),
            a_specs[0], a_specs[1],
            pl.BlockSpec((tm, IN_GATE), lambda i: (i, 0)),
        ),
        scratch_shapes=[pltpu.VMEM((d // LANES, tm, LANES), F32)],
        compiler_params=_cparams(("arbitrary",)),
        name="in_proj_prompt",
    )(x, mod, norm_g, wz, wx, wd, wqs[0], wqs[1], wqs[2], wg)


def _in_small_kernel(x_ref, mod_ref, g_ref, w_ref, o_ref):
    sh = mod_ref[:, 0:D_MODEL]
    sc = mod_ref[:, D_MODEL:2 * D_MODEL]
    h = _rms_modulate(x_ref[...], g_ref[...], sh, sc)
    h_hi, h_lo = _split2(h)
    w_hi, w_lo = _split2(w_ref[...])
    o_ref[...] = _dot_nt(h_hi, w_hi) + (_dot_nt(h_lo, w_hi) + _dot_nt(h_hi, w_lo))


def _in_proj_sample(layer, x, mod, norm_g, w_in_t, tn=1024):
    m, d = x.shape
    width = w_in_t.shape[1]
    return pl.pallas_call(
        _in_small_kernel,
        out_shape=jax.ShapeDtypeStruct((m, width), F32),
        grid=(pl.cdiv(width, tn),),
        in_specs=[
            pl.BlockSpec((m, d), lambda j: (0, 0)),
            pl.BlockSpec((m, 2 * d), lambda j: (0, 0)),
            pl.BlockSpec((1, d), lambda j: (0, 0)),
            pl.BlockSpec((None, tn, d), lambda j: (layer, j, 0)),
        ],
        out_specs=pl.BlockSpec((m, tn), lambda j: (0, j)),
        compiler_params=_cparams(("arbitrary",)),
        name="in_proj_sample",
    )(x, mod, norm_g, w_in_t)


_SSD_SUB = 2


def _ssd_kernel(xbc_ref, z_ref, dt_ref, cw_ref, cb_ref, dtb_ref, alog_ref, dskip_ref, ng_ref,
                yn_ref, hout_ref, h_scr, xp_scr):
    q = SSD_CHUNK
    c = pl.program_id(1)

    @pl.when(c == 0)
    def _():
        h_scr[...] = jnp.zeros_like(h_scr)
        xp_scr[0:8, :] = jnp.zeros((8, SSD_CONV_DIM), F32)

    for sub in range(_SSD_SUB):
        rows = slice(sub * q, (sub + 1) * q)
        _ssd_chunk(xbc_ref.at[rows], z_ref.at[rows], dt_ref.at[rows], cw_ref, cb_ref, dtb_ref, alog_ref,
                   dskip_ref, ng_ref, yn_ref.at[rows], h_scr, xp_scr)

    @pl.when(c == pl.num_programs(1) - 1)
    def _():
        hout_ref[...] = h_scr[...]


def _ssd_chunk(xbc_ref, z_ref, dt_ref, cw_ref, cb_ref, dtb_ref, alog_ref, dskip_ref, ng_ref,
               yn_ref, h_scr, xp_scr):
    q = SSD_CHUNK
    xp_scr[8:8 + q, :] = xbc_ref[...]
    acc = cb_ref[...] + cw_ref[3:4, :] * xp_scr[8:8 + q, :]
    for k in range(SSD_CONV - 1):
        acc = acc + cw_ref[k:k + 1, :] * xp_scr[5 + k:5 + k + q, :]
    xp_scr[0:8, :] = xp_scr[q:q + 8, :]
    xc = _silu(acc)
    xs = xc[:, 0:SSD_INNER]
    xs_bf = xs.astype(BF16)

    lane_q = lax.broadcasted_iota(jnp.int32, (q, LANES), 1)
    row_q = lax.broadcasted_iota(jnp.int32, (q, LANES), 0)
    causal = row_q >= lane_q
    tri = jnp.where(causal, 1.0, 0.0).astype(BF16)
    tri_t = jnp.where(lane_q >= row_q, 1.0, 0.0).astype(BF16)
    e_heads = _head_expand_matrix(LANES, SSD_HEADS, SSD_HEAD_DIM)

    dt = _softplus(dt_ref[...] + dtb_ref[...])
    dt = jnp.where(lane_q < SSD_HEADS, dt, 0.0)
    a = -jnp.exp(alog_ref[...])
    d_a = dt * a
    cum = _mm01_left(tri, d_a)
    d_a_t = d_a.T
    dt_t = dt.T
    cum_t = _mm01(d_a_t, tri_t, 3)
    cum_last = cum[q - 1:q, :]
    exp_cum = jnp.exp(cum)
    dec_end = jnp.exp(cum_last - cum)
    stack = jnp.concatenate([exp_cum, dec_end * dt], axis=0)
    full = _mm01(stack, e_heads, 2)
    exp_cum_full = full[0:q]
    w_full = full[q:2 * q]
    chunk_dec_t = jnp.exp(cum_t[:, q - 1:q])

    lane_half = lax.broadcasted_iota(jnp.int32, (q, LANES), 1) < SSD_HEAD_DIM
    y_parts = []
    for g in range(SSD_GROUPS):
        b_off = SSD_INNER + g * SSD_STATE
        c_off = SSD_INNER + SSD_GROUPS * SSD_STATE + g * SSD_STATE
        bm = xc[:, b_off:b_off + SSD_STATE].astype(BF16)
        cm = xc[:, c_off:c_off + SSD_STATE].astype(BF16)
        cbm = _dot_nt(cm, bm)
        gw = SSD_HPG * SSD_HEAD_DIM
        g0 = g * gw
        yd = []
        for pair in range(SSD_HPG // 2):
            x_pair = xs_bf[:, g0 + pair * LANES:g0 + (pair + 1) * LANES]
            halves = []
            for hh in range(2):
                h = g * SSD_HPG + pair * 2 + hh
                seg = cum[:, h:h + 1] - cum_t[h:h + 1, :]
                dec = jnp.exp(jnp.where(causal, seg, -jnp.inf))
                m_h = (cbm * dec * dt_t[h:h + 1, :]).astype(BF16)
                halves.append(_dot(m_h, x_pair))
            yd.append(jnp.where(lane_half, halves[0], halves[1]))
        y_diag = jnp.concatenate(yd, axis=1)
        h_g = h_scr[g0:g0 + gw, :]
        y_off = _dot_nt(cm, h_g.astype(BF16)) * exp_cum_full[:, g0:g0 + gw]
        y_parts.append(y_diag + y_off)
        xw = (xs[:, g0:g0 + gw] * w_full[:, g0:g0 + gw])
        st = _dot(xw.T.astype(BF16), bm)
        for e in range(SSD_HPG):
            h = g * SSD_HPG + e
            r0 = g0 + e * SSD_HEAD_DIM
            h_scr[r0:r0 + SSD_HEAD_DIM, :] = (h_scr[r0:r0 + SSD_HEAD_DIM, :] * chunk_dec_t[h:h + 1, :]
                                              + st[e * SSD_HEAD_DIM:(e + 1) * SSD_HEAD_DIM, :])

    y = jnp.concatenate(y_parts, axis=1) + dskip_ref[...] * xs
    y = y * _silu(z_ref[...].astype(F32))
    outs = []
    for g in range(SSD_GROUPS):
        gw = SSD_HPG * SSD_HEAD_DIM
        yg = y[:, g * gw:(g + 1) * gw]
        ms = jnp.mean(yg * yg, axis=-1, keepdims=True)
        outs.append(yg * lax.rsqrt(ms + EPS) * ng_ref[:, g * gw:(g + 1) * gw])
    yn_ref[...] = jnp.concatenate(outs, axis=1).astype(yn_ref.dtype)


def _mm01_left(tri01, a):
    hi, mid, lo = _split3(a)
    return _dot(tri01, hi) + (_dot(tri01, mid) + _dot(tri01, lo))


def _ssd_prompt(xbc, z, dt_raw, conv_w, conv_b, dt_bias, a_log, d_skip_full, ssd_norm_g, bsz, seq):
    q = SSD_CHUNK * _SSD_SUB
    nc = seq // q
    row = lambda b, c: (b * nc + c, 0)
    const = lambda b, c: (0, 0)
    return pl.pallas_call(
        _ssd_kernel,
        out_shape=(
            jax.ShapeDtypeStruct((bsz * seq, SSD_INNER), BF16),
            jax.ShapeDtypeStruct((bsz, SSD_INNER, SSD_STATE), F32),
        ),
        grid=(bsz, nc),
        in_specs=[
            pl.BlockSpec((q, SSD_CONV_DIM), row),
            pl.BlockSpec((q, SSD_INNER), row),
            pl.BlockSpec((q, LANES), row),
            pl.BlockSpec((SSD_CONV, SSD_CONV_DIM), const),
            pl.BlockSpec((1, SSD_CONV_DIM), const),
            pl.BlockSpec((1, LANES), const),
            pl.BlockSpec((1, LANES), const),
            pl.BlockSpec((1, SSD_INNER), const),
            pl.BlockSpec((1, SSD_INNER), const),
        ],
        out_specs=(
            pl.BlockSpec((q, SSD_INNER), row),
            pl.BlockSpec((None, SSD_INNER, SSD_STATE), lambda b, c: (b, 0, 0)),
        ),
        scratch_shapes=[
            pltpu.VMEM((SSD_INNER, SSD_STATE), F32),
            pltpu.VMEM((SSD_CHUNK + 8, SSD_CONV_DIM), F32),
        ],
        compiler_params=_cparams(("arbitrary", "arbitrary")),
        name="ssd_prompt",
    )(xbc, z, dt_raw, conv_w, conv_b, dt_bias, a_log, d_skip_full, ssd_norm_g)


def _attn_kernel(q_ref, kc_ref, kp_ref, vc_ref, vp_ref, o_out_ref, lse_out_ref,
                 q_ref_s, kwin, vwin, o_ref, lse_ref, *, tq, band):
    blk = ATT_BLOCK
    j = pl.program_id(2)
    q_ref_s[...] = (q_ref[...] * (ATT_HEAD_DIM ** -0.5)).astype(BF16).reshape(tq, ATT_OUT)
    kwin[0:blk, :] = kp_ref[...].reshape(blk, ATT_OUT)
    kwin[blk:blk + tq, :] = kc_ref[...].reshape(tq, ATT_OUT)
    vwin[0:blk, :] = vp_ref[...].reshape(blk, ATT_OUT)
    vwin[blk:blk + tq, :] = vc_ref[...].reshape(tq, ATT_OUT)
    q_ref = q_ref_s

    qi = lax.broadcasted_iota(jnp.int32, (blk, 2 * blk), 0)
    ki = lax.broadcasted_iota(jnp.int32, (blk, 2 * blk), 1)
    dist = qi + blk - ki
    in_band = (dist >= 0) & (dist <= band)
    lane = lax.broadcasted_iota(jnp.int32, (blk, LANES), 1)
    lane_half = lane < ATT_HEAD_DIM
    scale = ATT_HEAD_DIM ** -0.5
    zero_bf = jnp.zeros((blk, LANES), BF16)

    for i in range(tq // blk):
        if i == 0:
            valid = in_band & ((ki >= blk) | (j > 0))
        else:
            valid = in_band
        lse_tile = jnp.zeros((blk, LANES), F32)
        for hp in range(ATT_HEADS // 2):
            c0 = hp * LANES
            q_pair = q_ref[i * blk:(i + 1) * blk, c0:c0 + LANES]
            k_pair = kwin[i * blk:(i + 2) * blk, c0:c0 + LANES]
            v_pair = vwin[i * blk:(i + 2) * blk, c0:c0 + LANES]
            halves = []
            for hh in range(2):
                q_m = jnp.where(lane_half if hh == 0 else jnp.logical_not(lane_half), q_pair, zero_bf)
                s = _dot_nt(q_m, k_pair)
                s = jnp.where(valid, s, -jnp.inf)
                m = jnp.max(s, axis=-1, keepdims=True)
                e = jnp.exp(s - m)
                den = jnp.sum(e, axis=-1, keepdims=True)
                pv = _dot(e.astype(BF16), v_pair)
                halves.append(pv / den)
                lse = m + jnp.log(den)
                head = hp * 2 + hh
                lse_tile = jnp.where((lane == head) | (lane == ATT_HEADS + head), lse, lse_tile)
            o_ref[i * blk:(i + 1) * blk, c0:c0 + LANES] = jnp.where(lane_half, halves[0], halves[1]).astype(o_ref.dtype)
        lse_ref[i * blk:(i + 1) * blk, :] = lse_tile
    o_out_ref[...] = o_ref[...].reshape(o_out_ref.shape)
    lse_out_ref[...] = lse_ref[...].reshape(lse_out_ref.shape)


def _attn_prompt(a_g, gi, bsz, seq):
    win, dil = ATT_PATTERNS[gi]
    band = win // dil
    blk = ATT_BLOCK
    rows = TOK_TILE // dil
    tiles = seq // TOK_TILE
    length = seq // dil
    tq = min(512, length)
    nj = length // tq
    tq_tiles = tq // rows
    cur = lambda which: pl.BlockSpec((None, tq_tiles, None, rows, ATT_OUT),
                                     lambda b, r, j: (b, j, r, 0, which))
    if rows >= blk:
        prev = lambda which: pl.BlockSpec(
            (None, None, None, blk, ATT_OUT),
            lambda b, r, j: (b, jnp.maximum(j * tq_tiles - 1, 0), r, rows // blk - 1, which))
    else:
        prev = lambda which: pl.BlockSpec(
            (None, blk // rows, None, rows, ATT_OUT),
            lambda b, r, j: (b, jnp.maximum(j * (tq // blk) - 1, 0), r, 0, which))
    return pl.pallas_call(
        functools.partial(_attn_kernel, tq=tq, band=band),
        out_shape=(
            jax.ShapeDtypeStruct((bsz, tiles, dil, rows, ATT_OUT), BF16),
            jax.ShapeDtypeStruct((bsz, tiles, dil, rows, LANES), F32),
        ),
        grid=(bsz, dil, nj),
        in_specs=[cur(0), cur(1), prev(1), cur(2), prev(2)],
        out_specs=(
            pl.BlockSpec((None, tq_tiles, None, rows, ATT_OUT), lambda b, r, j: (b, j, r, 0, 0)),
            pl.BlockSpec((None, tq_tiles, None, rows, LANES), lambda b, r, j: (b, j, r, 0, 0)),
        ),
        scratch_shapes=[
            pltpu.VMEM((tq, ATT_OUT), BF16),
            pltpu.VMEM((blk + tq, ATT_OUT), BF16),
            pltpu.VMEM((blk + tq, ATT_OUT), BF16),
            pltpu.VMEM((tq, ATT_OUT), BF16),
            pltpu.VMEM((tq, LANES), F32),
        ],
        compiler_params=_cparams(("arbitrary", "arbitrary", "arbitrary")),
        name=f"attn_prompt_w{win}",
    )(a_g, a_g, a_g, a_g, a_g)


def _router_gates(logits):
    shape = logits.shape
    lane = lax.broadcasted_iota(jnp.int32, shape, 1)
    big = jnp.int32(1 << 20)
    neg = -jnp.inf
    is_grp = (lane >= MOE_EXPERTS) & (lane < MOE_EXPERTS + MOE_GROUPS)
    lg = jnp.where(is_grp, logits, neg)
    gm = jnp.max(lg, axis=-1, keepdims=True)
    g_lane = jnp.min(jnp.where(lg == gm, lane, big), axis=-1, keepdims=True)
    g_sum = jnp.sum(jnp.exp(lg - gm), axis=-1, keepdims=True)
    g_w = 1.0 / g_sum
    lo = (g_lane - MOE_EXPERTS) * MOE_PER_GROUP
    in_grp = (lane >= lo) & (lane < lo + MOE_PER_GROUP)
    le = jnp.where(in_grp, logits, neg)
    m1 = jnp.max(le, axis=-1, keepdims=True)
    i1 = jnp.min(jnp.where(le == m1, lane, big), axis=-1, keepdims=True)
    le2 = jnp.where(lane == i1, neg, le)
    m2 = jnp.max(le2, axis=-1, keepdims=True)
    i2 = jnp.min(jnp.where(le2 == m2, lane, big), axis=-1, keepdims=True)
    t = jnp.exp(m2 - m1)
    w1 = 1.0 / (1.0 + t)
    w2 = t / (1.0 + t)
    gate = jnp.where(lane == i1, g_w * w1, jnp.where(lane == i2, g_w * w2, 0.0))
    e_lo = jnp.minimum(i1, i2)
    e_hi = jnp.maximum(i1, i2)
    lo_l = e_lo & (MOE_PER_GROUP - 1)
    hi_l = e_hi & (MOE_PER_GROUP - 1)
    cls = (e_lo >> 2) * MOE_PAIRS + ((lo_l * (7 - lo_l)) >> 1) + (hi_l - lo_l - 1)
    return gate, cls


def _post_kernel(*refs, passes, merge_attn):
    if merge_attn:
        (x_ref, yn_ref, o0_ref, o1_ref, o2_ref, l0_ref, l1_ref, l2_ref, gates_ref, mod_ref, n2_ref,
         wssd_ref, wattn_ref, wout_ref, wr_ref, br_ref, x1_ref, h2_ref, pos_ref, cnt_ref,
         o1_scr, o2_scr, l1_scr, l2_scr) = refs
        for src_ref, dst_ref in ((o1_ref, o1_scr), (o2_ref, o2_scr), (l1_ref, l1_scr), (l2_ref, l2_scr)):
            dil, r_len, width = src_ref.shape
            for r in range(dil):
                blk_r = src_ref[r].astype(F32)
                for cbk in range(width // LANES):
                    dst_ref[cbk, pl.ds(r, r_len, stride=dil), :] = blk_r[:, cbk * LANES:(cbk + 1) * LANES]
        o_nat = [o0_ref[...].astype(F32)] + [
            jnp.concatenate([scr[cbk] for cbk in range(ATT_OUT // LANES)], axis=1) for scr in (o1_scr, o2_scr)]
        l0, l1, l2 = l0_ref[...], l1_scr[0], l2_scr[0]
        mx = jnp.maximum(jnp.maximum(l0, l1), l2)
        e0, e1, e2 = jnp.exp(l0 - mx), jnp.exp(l1 - mx), jnp.exp(l2 - mx)
        inv = 1.0 / (e0 + e1 + e2)
        lane = lax.broadcasted_iota(jnp.int32, l0.shape, 1)
        e8 = _head_expand_matrix(LANES, ATT_HEADS, ATT_HEAD_DIM)
        r = lax.broadcasted_iota(jnp.int32, e8.shape, 0)
        c = lax.broadcasted_iota(jnp.int32, e8.shape, 1)
        e8 = jnp.where((c // ATT_HEAD_DIM) == (r - ATT_HEADS), 1.0, e8.astype(F32)).astype(BF16)
        o = None
        for e_g, o_g in zip((e0, e1, e2), o_nat):
            w = e_g * inv
            hi, lo = _split2(w)
            w_exp = _dot(jnp.where(lane < ATT_HEADS, hi, lo), e8)
            term = w_exp * o_g
            o = term if o is None else o + term
    else:
        (x_ref, yn_ref, o_ref, gates_ref, mod_ref, n2_ref,
         wssd_ref, wattn_ref, wout_ref, wr_ref, br_ref, x1_ref, h2_ref, gate_ref) = refs
        o = o_ref[...]
    d = D_MODEL
    g1 = mod_ref[:, 0:d]
    sh2 = mod_ref[:, d:2 * d]
    sc2 = mod_ref[:, 2 * d:3 * d]
    ssd_branch = _mm(yn_ref[...], wssd_ref[...], passes)
    attn_branch = _mm(o, wattn_ref[...], passes)
    ga = gates_ref[:, 0:d].astype(F32)
    gb = gates_ref[:, d:2 * d].astype(F32)
    mixed = _sigmoid(ga) * ssd_branch + _sigmoid(gb) * attn_branch
    x1 = x_ref[...] + g1 * _mm(mixed, wout_ref[...], passes)
    x1_ref[...] = x1
    h2 = _rms_modulate(x1, n2_ref[...], sh2, sc2)
    logits = _mm(h2, wr_ref[...], 3) + br_ref[...]
    gate, cls = _router_gates(logits)
    if not merge_attn:
        h2_ref[...] = h2
        gate_ref[...] = gate
        return
    tm = h2.shape[0]
    for cbk in range(ROW_SUB):
        h2_ref[pl.ds(cbk, tm, stride=ROW_SUB), :] = h2[:, cbk * LANES:(cbk + 1) * LANES]
    step = pl.program_id(0)

    @pl.when(step == 0)
    def _():
        cnt_ref[...] = jnp.zeros_like(cnt_ref)

    lane = lax.broadcasted_iota(jnp.int32, (tm, LANES), 1)
    onehot = lane == cls
    ri = lax.broadcasted_iota(jnp.int32, (tm, tm), 0)
    ci = lax.broadcasted_iota(jnp.int32, (tm, tm), 1)
    before = jnp.where(ci < ri, 1.0, 0.0).astype(BF16)
    seen = _dot(before, jnp.where(onehot, 1.0, 0.0).astype(BF16)) + cnt_ref[...]
    rank = jnp.sum(jnp.where(onehot, seen, 0.0), axis=-1, keepdims=True)
    n_tokens = tm * pl.num_programs(0)
    pos_ref[...] = jnp.broadcast_to(cls.astype(F32) * n_tokens + rank, (tm, LANES))
    cnt_ref[...] = cnt_ref[...] + jnp.sum(jnp.where(onehot, 1.0, 0.0), axis=0, keepdims=True)


def _post(x, yn, attn_in, gates, mod, norm2_g, wssd, wattn, wout, wr, br, *, passes, merge_attn,
          tm, rows_per_mod, h2_dtype):
    n, d = x.shape
    per_row_mod = rows_per_mod == 1
    if per_row_mod:
        mod_spec = pl.BlockSpec((tm, 3 * d), lambda i: (i, 0))
    else:
        tiles = rows_per_mod // tm
        mod_spec = pl.BlockSpec((None, 1, 3 * d), lambda i: (i // tiles, 0, 0))
    row = lambda w: pl.BlockSpec((tm, w), lambda i: (i, 0))
    const = lambda a: pl.BlockSpec(a.shape, lambda i: (0, 0), pipeline_mode=pl.Buffered(1))
    scratch = []
    if merge_attn:
        assert tm == TOK_TILE
        o_list, lse_list = attn_in
        attn_args = list(o_list) + list(lse_list)
        tiles = rows_per_mod // tm

        def tile_spec(a):
            _, _, dil, r_len, w = a.shape
            if dil == 1:
                return pl.BlockSpec((None, None, None, r_len, w), lambda i: (i // tiles, i % tiles, 0, 0, 0))
            return pl.BlockSpec((None, None, dil, r_len, w), lambda i: (i // tiles, i % tiles, 0, 0, 0))

        attn_specs = [tile_spec(a) for a in attn_args]
        scratch = [pltpu.VMEM((ATT_OUT // LANES, tm, LANES), F32), pltpu.VMEM((ATT_OUT // LANES, tm, LANES), F32),
                   pltpu.VMEM((1, tm, LANES), F32), pltpu.VMEM((1, tm, LANES), F32)]
    else:
        attn_args = [attn_in]
        attn_specs = [row(ATT_OUT)]
    if merge_attn:
        out_shape = (jax.ShapeDtypeStruct((n, d), F32), jax.ShapeDtypeStruct((n * ROW_SUB, LANES), F32),
                     jax.ShapeDtypeStruct((n, LANES), F32), jax.ShapeDtypeStruct((1, LANES), F32))
        out_specs = (row(d), pl.BlockSpec((tm * ROW_SUB, LANES), lambda i: (i, 0)), row(LANES),
                     pl.BlockSpec((1, LANES), lambda i: (0, 0)))
    else:
        out_shape = (jax.ShapeDtypeStruct((n, d), F32), jax.ShapeDtypeStruct((n, d), h2_dtype),
                     jax.ShapeDtypeStruct((n, LANES), F32))
        out_specs = (row(d), row(d), row(LANES))
    return pl.pallas_call(
        functools.partial(_post_kernel, passes=passes, merge_attn=merge_attn),
        out_shape=out_shape,
        grid=(n // tm,),
        in_specs=[row(d), row(SSD_INNER)] + attn_specs + [row(IN_GATE), mod_spec, const(norm2_g),
                                                          const(wssd), const(wattn), const(wout),
                                                          const(wr), const(br)],
        out_specs=out_specs,
        scratch_shapes=scratch,
        compiler_params=_cparams(("arbitrary",)),
        name="post_merge" if merge_attn else "post_sample",
    )(x, yn, *attn_args, gates, mod, norm2_g, wssd, wattn, wout, wr, br)


def _moe_kernel(h2_ref, gate_ref, x1_ref, g2_ref, wg_ref, wu_ref, wd_ref, x2_ref, acc_ref, *, passes):
    e = pl.program_id(1)

    @pl.when(e == 0)
    def _():
        acc_ref[...] = jnp.zeros_like(acc_ref)

    h2 = h2_ref[...]
    hg = _mm(h2, wg_ref[...], passes)
    hu = _mm(h2, wu_ref[...], passes)
    gate = gate_ref[...]
    lane = lax.broadcasted_iota(jnp.int32, gate.shape, 1)
    gcol = jnp.sum(jnp.where(lane == e, gate, 0.0), axis=-1, keepdims=True)
    act = _silu(hg) * hu * gcol
    acc_ref[...] += _mm(act, wd_ref[...], passes)

    @pl.when(e == pl.num_programs(1) - 1)
    def _():
        x2_ref[...] = x1_ref[...] + g2_ref[...] * acc_ref[...]


def _moe(layer, h2, gate, x1, g2, wg, wu, wd, *, passes, tm, rows_per_mod):
    n, d = x1.shape
    if rows_per_mod == 1:
        g2_spec = pl.BlockSpec((tm, d), lambda i, e: (i, 0))
    else:
        tiles = rows_per_mod // tm
        g2_spec = pl.BlockSpec((None, 1, d), lambda i, e: (i // tiles, 0, 0))
    row = lambda w: pl.BlockSpec((tm, w), lambda i, e: (i, 0))
    return pl.pallas_call(
        functools.partial(_moe_kernel, passes=passes),
        out_shape=jax.ShapeDtypeStruct((n, d), F32),
        grid=(n // tm, MOE_EXPERTS),
        in_specs=[
            row(d), row(LANES), row(d), g2_spec,
            pl.BlockSpec((None, None, d, MOE_FF), lambda i, e: (layer, e, 0, 0)),
            pl.BlockSpec((None, None, d, MOE_FF), lambda i, e: (layer, e, 0, 0)),
            pl.BlockSpec((None, None, MOE_FF, d), lambda i, e: (layer, e, 0, 0)),
        ],
        out_specs=row(d),
        scratch_shapes=[pltpu.VMEM((tm, d), F32)],
        compiler_params=_cparams(("arbitrary", "arbitrary")),
        name="moe_dense",
    )(h2, gate, x1, g2, wg, wu, wd)


_PAIR_LO = (0, 0, 0, 1, 1, 2)
_PAIR_HI = (1, 2, 3, 2, 3, 3)
_DMA_UNROLL = 8
_ROUTE_TILE = 2048


def _moe_plan(counts, n_tokens):
    tm = MOE_TILE
    blocks_per_class = n_tokens // tm
    ntile = (counts + tm - 1) // tm
    cum = jnp.cumsum(ntile)
    total = cum[-1]
    t_max = blocks_per_class + MOE_CLASSES
    t_eff = jnp.minimum(jnp.arange(t_max, dtype=jnp.int32), total - 1)
    cls = jnp.searchsorted(cum, t_eff, side="right").astype(jnp.int32)
    blk = cls * blocks_per_class + (t_eff - (cum - ntile)[cls])
    grp, pair = cls // MOE_PAIRS, cls % MOE_PAIRS
    e_lo = grp * MOE_PER_GROUP + jnp.asarray(_PAIR_LO, jnp.int32)[pair]
    e_hi = grp * MOE_PER_GROUP + jnp.asarray(_PAIR_HI, jnp.int32)[pair]
    z_need = ((counts % tm) != 0).astype(jnp.int32)
    z_blk = jnp.arange(MOE_CLASSES, dtype=jnp.int32) * blocks_per_class + jnp.maximum(ntile - 1, 0)
    return (blk.astype(jnp.int32), e_lo, e_hi, total.reshape(1).astype(jnp.int32), z_need, z_blk.astype(jnp.int32))


def _row_copy(src, src_row, dst, dst_row, sem):
    return pltpu.make_async_copy(src.at[pl.ds(pl.multiple_of(src_row * ROW_SUB, ROW_SUB), ROW_SUB), :],
                                 dst.at[pl.ds(pl.multiple_of(dst_row * ROW_SUB, ROW_SUB), ROW_SUB), :], sem)


def _dispatch_kernel(pos_ref, zneed_ref, zblk_ref, rows_ref, xs_ref, zero_scr, sem, zsem, *, tm):
    step = pl.program_id(0)
    tile_rows = MOE_TILE * ROW_SUB

    def zero_copy(c):
        start = pl.multiple_of(zblk_ref[c] * tile_rows, tile_rows)
        return pltpu.make_async_copy(zero_scr, xs_ref.at[pl.ds(start, tile_rows), :], zsem)

    @pl.when(step == 0)
    def _():
        zero_scr[...] = jnp.zeros_like(zero_scr)
        for c in range(MOE_CLASSES):
            @pl.when(zneed_ref[c] != 0)
            def _():
                zero_copy(c).start()
        for c in range(MOE_CLASSES):
            @pl.when(zneed_ref[c] != 0)
            def _():
                zero_copy(c).wait()

    base = step * tm

    def body(i8, carry):
        for u in range(_DMA_UNROLL):
            i = i8 * _DMA_UNROLL + u
            _row_copy(rows_ref, i, xs_ref, pos_ref[base + i], sem).start(priority=u % 2)
        return carry

    lax.fori_loop(0, tm // _DMA_UNROLL, body, 0)
    pltpu.make_async_copy(rows_ref, xs_ref.at[pl.ds(0, tm * ROW_SUB), :], sem).wait()


def _dispatch(pos, z_need, z_blk, rows, n_tokens, tm=_ROUTE_TILE):
    sorted_rows = MOE_CLASSES * n_tokens * ROW_SUB
    return pl.pallas_call(
        functools.partial(_dispatch_kernel, tm=tm),
        out_shape=jax.ShapeDtypeStruct((sorted_rows, LANES), F32),
        grid_spec=pltpu.PrefetchScalarGridSpec(
            num_scalar_prefetch=3,
            grid=(n_tokens // tm,),
            in_specs=[pl.BlockSpec((tm * ROW_SUB, LANES), lambda i, *_: (i, 0))],
            out_specs=pl.BlockSpec(memory_space=pl.ANY),
            scratch_shapes=[pltpu.VMEM((MOE_TILE * ROW_SUB, LANES), F32),
                            pltpu.SemaphoreType.DMA(()), pltpu.SemaphoreType.DMA(())],
        ),
        compiler_params=pltpu.CompilerParams(dimension_semantics=("arbitrary",), vmem_limit_bytes=VMEM_LIMIT,
                                             disable_bounds_checks=True),
        name="moe_dispatch",
    )(pos, z_need, z_blk, rows)


def _moe_sparse_kernel(blk_ref, elo_ref, ehi_ref, nact_ref, xs_ref, wr_ref, br_ref,
                       wg_lo_ref, wu_lo_ref, wd_lo_ref, wg_hi_ref, wu_hi_ref, wd_hi_ref, ys_ref):
    t = pl.program_id(0)
    tm = MOE_TILE

    @pl.when(t < nact_ref[0])
    def _():
        x = jnp.concatenate([xs_ref[pl.ds(c, tm, stride=ROW_SUB), :] for c in range(ROW_SUB)], axis=1)
        xb = x.astype(BF16)
        w_hi16, w_lo16 = _split2(wr_ref[...])
        logits = _dot(xb, w_hi16) + _dot(xb, w_lo16) + br_ref[...]
        lane = lax.broadcasted_iota(jnp.int32, logits.shape, 1)
        e_lo, e_hi = elo_ref[t], ehi_ref[t]
        g_lane = MOE_EXPERTS + (e_lo >> 2)
        pick = lambda ln: jnp.sum(jnp.where(lane == ln, logits, 0.0), axis=-1, keepdims=True)
        l_lo, l_hi, l_g = pick(e_lo), pick(e_hi), pick(g_lane)
        is_grp = (lane >= MOE_EXPERTS) & (lane < MOE_EXPERTS + MOE_GROUPS)
        g_w = 1.0 / jnp.sum(jnp.where(is_grp, jnp.exp(logits - l_g), 0.0), axis=-1, keepdims=True)
        w_a = g_w / (1.0 + jnp.exp(l_hi - l_lo))
        w_b = g_w / (1.0 + jnp.exp(l_lo - l_hi))
        bf = lambda w_ref: w_ref[...].astype(BF16)
        act_a = (_silu(_dot(xb, bf(wg_lo_ref))) * _dot(xb, bf(wu_lo_ref)) * w_a).astype(BF16)
        act_b = (_silu(_dot(xb, bf(wg_hi_ref))) * _dot(xb, bf(wu_hi_ref)) * w_b).astype(BF16)
        y = _dot(act_a, bf(wd_lo_ref)) + _dot(act_b, bf(wd_hi_ref))
        for c in range(ROW_SUB):
            ys_ref[pl.ds(c, tm, stride=ROW_SUB), :] = y[:, c * LANES:(c + 1) * LANES]


def _moe_sparse(layer, plan, xs, wr, br, wg, wu, wd, n_tokens):
    blk, e_lo, e_hi, n_act = plan
    tm = MOE_TILE
    t_max = blk.shape[0]
    d = D_MODEL
    tile = pl.BlockSpec((tm * ROW_SUB, LANES), lambda t, blk, lo, hi, na: (blk[t], 0))
    w_lo = lambda shape: pl.BlockSpec((None, None) + shape, lambda t, blk, lo, hi, na: (layer, lo[t], 0, 0))
    w_hi = lambda shape: pl.BlockSpec((None, None) + shape, lambda t, blk, lo, hi, na: (layer, hi[t], 0, 0))
    const = lambda a: pl.BlockSpec(a.shape, lambda t, *_: (0, 0))
    return pl.pallas_call(
        _moe_sparse_kernel,
        out_shape=jax.ShapeDtypeStruct(xs.shape, F32),
        grid_spec=pltpu.PrefetchScalarGridSpec(
            num_scalar_prefetch=4,
            grid=(t_max,),
            in_specs=[tile, const(wr), const(br),
                      w_lo((d, MOE_FF)), w_lo((d, MOE_FF)), w_lo((MOE_FF, d)),
                      w_hi((d, MOE_FF)), w_hi((d, MOE_FF)), w_hi((MOE_FF, d))],
            out_specs=tile,
        ),
        compiler_params=_cparams(("arbitrary",)),
        name="moe_sparse",
    )(blk, e_lo, e_hi, n_act, xs, wr, br, wg, wu, wd, wg, wu, wd)


def _combine_kernel(pos_ref, ys_ref, x1_ref, g2_ref, *rest, tm, final):
    if final:
        fg_ref, x2_ref, buf, sem = rest
    else:
        x2_ref, buf, sem = rest
    base = pl.program_id(0) * tm

    def body(i8, carry):
        for u in range(_DMA_UNROLL):
            i = i8 * _DMA_UNROLL + u
            _row_copy(ys_ref, pos_ref[base + i], buf, i, sem).start(priority=u % 2)
        return carry

    lax.fori_loop(0, tm // _DMA_UNROLL, body, 0)
    pltpu.make_async_copy(ys_ref.at[pl.ds(0, tm * ROW_SUB), :], buf, sem).wait()
    y = jnp.concatenate([buf[pl.ds(c, tm, stride=ROW_SUB), :] for c in range(ROW_SUB)], axis=1)
    x2 = x1_ref[...] + g2_ref[...] * y
    if final:
        ms = jnp.mean(x2 * x2, axis=-1, keepdims=True)
        x2 = x2 * lax.rsqrt(ms + EPS) * fg_ref[...]
    x2_ref[...] = x2


def _combine(pos, ys, x1, g2, rows_per_mod, final_g=None, tm=_ROUTE_TILE):
    n, d = x1.shape
    tiles = rows_per_mod // tm
    final = final_g is not None
    extra_specs = [pl.BlockSpec((1, d), lambda i, *_: (0, 0))] if final else []
    extra_args = [final_g] if final else []
    return pl.pallas_call(
        functools.partial(_combine_kernel, tm=tm, final=final),
        out_shape=jax.ShapeDtypeStruct((n, d), F32),
        grid_spec=pltpu.PrefetchScalarGridSpec(
            num_scalar_prefetch=1,
            grid=(n // tm,),
            in_specs=[pl.BlockSpec(memory_space=pl.ANY),
                      pl.BlockSpec((tm, d), lambda i, *_: (i, 0)),
                      pl.BlockSpec((None, 1, d), lambda i, *_: (i // tiles, 0, 0))] + extra_specs,
            out_specs=pl.BlockSpec((tm, d), lambda i, *_: (i, 0)),
            scratch_shapes=[pltpu.VMEM((tm * ROW_SUB, LANES), F32), pltpu.SemaphoreType.DMA(())],
        ),
        compiler_params=pltpu.CompilerParams(dimension_semantics=("arbitrary",), vmem_limit_bytes=VMEM_LIMIT,
                                             disable_bounds_checks=True),
        name="moe_combine",
    )(pos, ys, x1, g2, *extra_args)


def _final_kernel(x_ref, g_ref, o_ref):
    x = x_ref[...]
    ms = jnp.mean(x * x, axis=-1, keepdims=True)
    o_ref[...] = x * lax.rsqrt(ms + EPS) * g_ref[...]


def _final_norm(x, g, tm):
    n, d = x.shape
    return pl.pallas_call(
        _final_kernel,
        out_shape=jax.ShapeDtypeStruct((n, d), F32),
        grid=(n // tm,),
        in_specs=[pl.BlockSpec((tm, d), lambda i: (i, 0)), pl.BlockSpec((1, d), lambda i: (0, 0))],
        out_specs=pl.BlockSpec((tm, d), lambda i: (i, 0)),
        compiler_params=_cparams(("arbitrary",)),
        name="final_norm",
    )(x, g)


def _step_kernel(z_ref, xbc_ref, dt_ref, cst_ref, h_ref,
                 cw_ref, cb_ref, dtb_ref, alog_ref, dskip_ref, ng_ref, *rest):
    yn_ref, cnew_ref, hnew_ref, col_scr = rest[-4:]
    x_new = xbc_ref[...]
    acc = cb_ref[...] + cw_ref[3:4, :] * x_new
    for k in range(SSD_CONV - 1):
        acc = acc + cw_ref[k:k + 1, :] * cst_ref[k:k + 1, :]
    cnew_ref[0:1, :] = cst_ref[1:2, :]
    cnew_ref[1:2, :] = cst_ref[2:3, :]
    cnew_ref[2:3, :] = x_new
    xc = _silu(acc)
    xs = xc[:, 0:SSD_INNER]

    e_heads = _head_expand_matrix(LANES, SSD_HEADS, SSD_HEAD_DIM)
    dt_raw8 = jnp.broadcast_to(dt_ref[...], (8, LANES))
    dt_full = _softplus(_mm01(dt_raw8, e_heads, 3)[0:1, :] + dtb_ref[...])
    a_full = -jnp.exp(alog_ref[...])
    dec_full = jnp.exp(dt_full * a_full)
    xdt = xs * dt_full
    col_scr[...] = jnp.zeros_like(col_scr)
    col_scr[0:1, :] = xdt
    col_scr[1:2, :] = dec_full
    cols = col_scr[...].T
    y_parts = []
    gw = SSD_HPG * SSD_HEAD_DIM
    for g in range(SSD_GROUPS):
        b_off = SSD_INNER + g * SSD_STATE
        c_off = SSD_INNER + SSD_GROUPS * SSD_STATE + g * SSD_STATE
        bm = xc[:, b_off:b_off + SSD_STATE]
        cm = xc[:, c_off:c_off + SSD_STATE]
        g0 = g * gw
        hn = h_ref[g0:g0 + gw, :] * cols[g0:g0 + gw, 1:2] + cols[g0:g0 + gw, 0:1] * bm
        hnew_ref[g0:g0 + gw, :] = hn
        t = (hn * cm).T
        y_parts.append(jnp.sum(t, axis=0, keepdims=True))
    y = jnp.concatenate(y_parts, axis=1) + dskip_ref[...] * xs
    y = y * _silu(z_ref[...])
    outs = []
    for g in range(SSD_GROUPS):
        yg = y[:, g * gw:(g + 1) * gw]
        ms = jnp.mean(yg * yg, axis=-1, keepdims=True)
        outs.append(yg * lax.rsqrt(ms + EPS) * ng_ref[:, g * gw:(g + 1) * gw])
    yn_ref[...] = jnp.concatenate(outs, axis=1)


def _step_sample(layer, z, xbc, dt_raw, state_conv, state_ssm, prev_ssm,
                 conv_w, conv_b, dtb_full, alog_full, dskip_full, ssd_norm_g):
    bsz = z.shape[0]
    row3 = lambda w: pl.BlockSpec((None, 1, w), lambda b: (b, 0, 0))
    const = lambda a: pl.BlockSpec(a.shape, lambda b: (0,) * a.ndim)
    alias_args = [] if prev_ssm is None else [prev_ssm]
    n_in = 11
    return pl.pallas_call(
        _step_kernel,
        out_shape=(
            jax.ShapeDtypeStruct((bsz, 1, SSD_INNER), F32),
            jax.ShapeDtypeStruct((bsz, SSD_CONV - 1, SSD_CONV_DIM), F32),
            jax.ShapeDtypeStruct(state_ssm.shape, F32),
        ),
        grid=(bsz,),
        in_specs=[
            row3(SSD_INNER), row3(SSD_CONV_DIM), row3(LANES),
            pl.BlockSpec((None, None, SSD_CONV - 1, SSD_CONV_DIM), lambda b: (layer, b, 0, 0)),
            pl.BlockSpec((None, None, SSD_INNER, SSD_STATE), lambda b: (layer, b, 0, 0)),
            const(conv_w), const(conv_b), const(dtb_full), const(alog_full), const(dskip_full), const(ssd_norm_g),
        ] + [pl.BlockSpec(memory_space=pl.ANY)] * len(alias_args),
        out_specs=(
            row3(SSD_INNER),
            pl.BlockSpec((None, SSD_CONV - 1, SSD_CONV_DIM), lambda b: (b, 0, 0)),
            pl.BlockSpec((None, None, SSD_INNER, SSD_STATE), lambda b: (layer, b, 0, 0)),
        ),
        input_output_aliases={n_in: 2} if alias_args else {},
        scratch_shapes=[pltpu.VMEM((LANES, SSD_INNER), F32)],
        compiler_params=_cparams(("arbitrary",)),
        name="step_sample",
    )(z.reshape(bsz, 1, -1), xbc.reshape(bsz, 1, -1), dt_raw.reshape(bsz, 1, -1), state_conv, state_ssm,
      conv_w, conv_b, dtb_full, alog_full, dskip_full, ssd_norm_g, *alias_args)


_HEAD_SPLIT = 2
_HROWS = ATT_OUT // _HEAD_SPLIT


def _cache_attn_kernel(*refs, n_alias):
    qkv_ref = refs[0]
    cache_refs = refs[1:7]
    out_refs = refs[7 + n_alias:13 + n_alias]
    o_ref = refs[13 + n_alias]
    b = pl.program_id(0)
    nh = _HROWS // ATT_HEAD_DIM
    scale = ATT_HEAD_DIM ** -0.5
    qkv = qkv_ref[...]
    lane_b = lax.broadcasted_iota(jnp.int32, qkv.shape, 2)
    cols = jnp.sum(jnp.where(lane_b == b, qkv, 0.0), axis=-1, keepdims=True)

    def per_head_rows(v):
        return jnp.concatenate([jnp.broadcast_to(v[h:h + 1, :], (ATT_HEAD_DIM, 1)) for h in range(nh)], axis=0)

    o_g, lse_g = [], []
    for gi, (_, dil) in enumerate(ATT_PATTERNS):
        k_ref, v_ref = cache_refs[2 * gi], cache_refs[2 * gi + 1]
        ko_ref, vo_ref = out_refs[2 * gi], out_refs[2 * gi + 1]
        q = cols[gi]
        k_new = cols[ATT_GROUPS + gi]
        v_new = cols[2 * ATT_GROUPS + gi]
        kk = k_ref[...]
        vv = v_ref[...]
        length = kk.shape[1]
        lane = lax.broadcasted_iota(jnp.int32, (nh, length), 1)
        s = jnp.sum((kk * q).reshape(nh, ATT_HEAD_DIM, length), axis=1) * scale
        s = jnp.where((lane & (dil - 1)) == 0, s, -jnp.inf)
        s_new = jnp.sum((k_new * q).reshape(nh, ATT_HEAD_DIM, 1), axis=1) * scale
        m = jnp.maximum(jnp.max(s, axis=-1, keepdims=True), s_new)
        e = jnp.exp(s - m)
        e_new = jnp.exp(s_new - m)
        den = jnp.sum(e, axis=-1, keepdims=True) + e_new
        acc = jnp.sum(vv.reshape(nh, ATT_HEAD_DIM, length) * e[:, None, :], axis=-1, keepdims=True)
        acc = acc.reshape(_HROWS, 1) + per_head_rows(e_new) * v_new
        o_g.append(acc / per_head_rows(den))
        lse_g.append(per_head_rows(m + jnp.log(den)))
        lane_full = lax.broadcasted_iota(jnp.int32, kk.shape, 1)
        last = lane_full == length - 1
        ko_ref[...] = jnp.where(last, k_new, pltpu.roll(kk, length - 1, axis=1))
        vo_ref[...] = jnp.where(last, v_new, pltpu.roll(vv, length - 1, axis=1))
    mx = jnp.maximum(jnp.maximum(lse_g[0], lse_g[1]), lse_g[2])
    w = [jnp.exp(l - mx) for l in lse_g]
    tot = w[0] + w[1] + w[2]
    o_ref[...] = (w[0] / tot) * o_g[0] + (w[1] / tot) * o_g[1] + (w[2] / tot) * o_g[2]


def _cache_attn(layer, qkv_t, cache_views, prev_outs):
    depth, bsz, _, _ = cache_views[0].shape
    n_alias = 0 if prev_outs is None else len(prev_outs)
    q4 = qkv_t.reshape(3 * ATT_GROUPS, _HEAD_SPLIT, _HROWS, bsz)
    blk = lambda c: pl.BlockSpec((None, None, _HROWS, c.shape[3]), lambda b, hh: (layer, b, hh, 0))
    any_spec = pl.BlockSpec(memory_space=pl.ANY)
    args = [q4] + list(cache_views) + ([] if prev_outs is None else list(prev_outs))
    res = pl.pallas_call(
        functools.partial(_cache_attn_kernel, n_alias=n_alias),
        out_shape=tuple(jax.ShapeDtypeStruct(c.shape, c.dtype) for c in cache_views)
        + (jax.ShapeDtypeStruct((bsz, _HEAD_SPLIT, _HROWS, 1), F32),),
        grid=(bsz, _HEAD_SPLIT),
        in_specs=[pl.BlockSpec((3 * ATT_GROUPS, None, _HROWS, bsz), lambda b, hh: (0, hh, 0, 0))]
        + [blk(c) for c in cache_views] + [any_spec] * n_alias,
        out_specs=tuple(blk(c) for c in cache_views)
        + (pl.BlockSpec((None, None, _HROWS, 1), lambda b, hh: (b, hh, 0, 0)),),
        input_output_aliases={7 + i: i for i in range(n_alias)},
        compiler_params=_cparams(("arbitrary", "arbitrary")),
        name="cache_attn",
    )(*args)
    return list(res[:6]), res[6].reshape(bsz, ATT_OUT)


def _pad_lanes(v, width=LANES):
    return jnp.pad(v, [(0, 0)] * (v.ndim - 1) + [(0, width - v.shape[-1])])


def kernel(x_prompt, x_sample, c_prompt, c_sample, state_conv, state_ssm, cache_k_win128, cache_v_win128, cache_k_win512, cache_v_win512, cache_k_win2048, cache_v_win2048, norm1_g, w_ada, b_ada, w_in, conv_w, conv_b, dt_bias, a_log, d_skip, ssd_norm_g, w_ssd_proj, w_attn_proj, w_out, norm2_g, w_router_group, b_router_group, w_router_expert, b_router_expert, w_exp_gate, w_exp_up, w_exp_down, final_norm_g):
    depth = w_in.shape[0]
    bp, seq, d = x_prompt.shape
    bs = x_sample.shape[0]
    assert x_sample.shape[1] == 1 and d == D_MODEL and w_in.shape[2] == IN_WIDTH
    assert seq % (ATT_PATTERNS[-1][1] * ATT_BLOCK) == 0
    n_p = bp * seq
    caches = ((cache_k_win128, cache_v_win128), (cache_k_win512, cache_v_win512),
              (cache_k_win2048, cache_v_win2048))

    rows = bp + bs
    rows_pad = -(-rows // 8) * 8
    c_all = jnp.pad(jnp.concatenate([c_prompt, c_sample], axis=0), ((0, rows_pad - rows), (0, 0)))
    mods = _modulation(c_all, w_ada, b_ada)

    xp = x_prompt.reshape(n_p, d)
    xs = x_sample.reshape(bs, d)
    outs = {k: [] for k in ("conv_p", "conv_s", "ssm_p", "ssm_s")}
    kv_p = [[[], []] for _ in ATT_PATTERNS]
    tiles = seq // TOK_TILE
    assert bs <= LANES and all(c.shape[2] == win for (win, _), pair in zip(ATT_PATTERNS, caches) for c in pair)
    cache_views = [jnp.transpose(c, (0, 1, 3, 4, 2)).reshape(depth, bs, ATT_OUT, c.shape[2])
                   for pair in caches for c in pair]
    shifted = None
    ssm_s_all = None
    w_in_t = jnp.swapaxes(w_in, 1, 2)

    for l in range(depth):
        mod_p = mods[l, :bp]
        mod_s = mods[l, bp:bp + bs]
        w_in_l = w_in[l]
        wz = w_in_l[:, 0:OFF_XBC].astype(BF16)
        wx = w_in_l[:, OFF_XBC:OFF_DT].astype(BF16)
        wd = _pad_lanes(w_in_l[:, OFF_DT:OFF_QKV]).astype(BF16)
        wqs = [jnp.concatenate([w_in_l[:, OFF_QKV + (which * ATT_GROUPS + gi) * ATT_OUT:
                                          OFF_QKV + (which * ATT_GROUPS + gi + 1) * ATT_OUT]
                                for which in range(3)], axis=1).astype(BF16) for gi in range(ATT_GROUPS)]
        wgt = w_in_l[:, OFF_GATE:IN_WIDTH].astype(BF16)
        g1n = norm1_g[l].reshape(1, d)
        g2n = norm2_g[l].reshape(1, d)
        cw = conv_w[l]
        cb = conv_b[l].reshape(1, -1)
        dskip_full = jnp.repeat(d_skip[l], SSD_HEAD_DIM).reshape(1, -1)
        ssd_g = ssd_norm_g[l].reshape(1, -1)
        w_router = _pad_lanes(jnp.concatenate([w_router_expert[l], w_router_group[l]], axis=1))
        b_router = _pad_lanes(jnp.concatenate([b_router_expert[l], b_router_group[l]], axis=0).reshape(1, -1))

        z, xbc, dt_raw, a0, a1, a2, gates = _in_proj_prompt(
            xp, mod_p[:, 0:2 * d].reshape(bp, 1, 2 * d), g1n, wz, wx, wd, wqs, wgt, bp, seq)
        yn, ssm_new = _ssd_prompt(xbc, z, dt_raw, cw, cb, _pad_lanes(dt_bias[l].reshape(1, -1)),
                                  _pad_lanes(a_log[l].reshape(1, -1)), dskip_full, ssd_g, bp, seq)
        outs["ssm_p"].append(ssm_new.reshape(bp, SSD_GROUPS, SSD_HPG, SSD_HEAD_DIM, SSD_STATE))
        outs["conv_p"].append(xbc.reshape(bp, seq, SSD_CONV_DIM)[:, seq - (SSD_CONV - 1):])
        a_groups = [a0.reshape(bp, tiles, 1, TOK_TILE, 3 * ATT_OUT), a1, a2]
        o_list, lse_list = [], []
        for gi, (win, dil) in enumerate(ATT_PATTERNS):
            o_g, lse_g = _attn_prompt(a_groups[gi], gi, bp, seq)
            o_list.append(o_g)
            lse_list.append(lse_g)
            keep = min(win, seq)
            rows = TOK_TILE // dil
            if keep >= TOK_TILE:
                nt = keep // TOK_TILE
                tail = a_groups[gi][:, tiles - nt:, :, :, ATT_OUT:3 * ATT_OUT]
                tail = jnp.transpose(tail, (0, 1, 3, 2, 4)).reshape(bp, keep, 2 * ATT_OUT)
            else:
                assert dil == 1
                tail = a_groups[gi][:, tiles - 1, 0, TOK_TILE - keep:, ATT_OUT:3 * ATT_OUT]
            tail = tail.astype(F32)
            kv_p[gi][0].append(tail[:, :, 0:ATT_OUT].reshape(bp, keep, ATT_HEADS, ATT_HEAD_DIM))
            kv_p[gi][1].append(tail[:, :, ATT_OUT:2 * ATT_OUT].reshape(bp, keep, ATT_HEADS, ATT_HEAD_DIM))
        mod_post = jnp.concatenate([mod_p[:, 2 * d:3 * d], mod_p[:, 3 * d:5 * d]], axis=1).reshape(bp, 1, 3 * d)
        x1, rows, slot, cnt = _post(xp, yn, (o_list, lse_list), gates, mod_post, g2n,
                                    w_ssd_proj[l].astype(BF16), w_attn_proj[l].astype(BF16),
                                    w_out[l].astype(BF16), w_router, b_router, passes=1, merge_attn=True,
                                    tm=TOK_TILE, rows_per_mod=seq, h2_dtype=F32)
        pos = slot[:, 0].astype(jnp.int32)
        blk, e_lo, e_hi, n_act, z_need, z_blk = _moe_plan(cnt[0, :MOE_CLASSES].astype(jnp.int32), n_p)
        xs_sorted = _dispatch(pos, z_need, z_blk, rows, n_p)
        ys_sorted = _moe_sparse(l, (blk, e_lo, e_hi, n_act), xs_sorted, w_router, b_router,
                                w_exp_gate, w_exp_up, w_exp_down, n_p)
        xp = _combine(pos, ys_sorted, x1, mod_p[:, 5 * d:6 * d].reshape(bp, 1, d), seq,
                      final_g=final_norm_g.reshape(1, d) if l == depth - 1 else None)

        u = _in_proj_sample(l, xs, mod_s[:, 0:2 * d], g1n, w_in_t)
        z_s = u[:, 0:OFF_XBC]
        xbc_s = u[:, OFF_XBC:OFF_DT]
        dt_s = _pad_lanes(u[:, OFF_DT:OFF_QKV])
        qkv_s = u[:, OFF_QKV:OFF_GATE]
        gates_s = u[:, OFF_GATE:IN_WIDTH]
        yn_s, conv_new, ssm_s_all = _step_sample(
            l, z_s, xbc_s, dt_s, state_conv, state_ssm.reshape(depth, bs, SSD_INNER, SSD_STATE), ssm_s_all,
            cw, cb, jnp.repeat(dt_bias[l], SSD_HEAD_DIM).reshape(1, -1),
            jnp.repeat(a_log[l], SSD_HEAD_DIM).reshape(1, -1), dskip_full, ssd_g)
        outs["conv_s"].append(conv_new)
        shifted, o_s = _cache_attn(l, qkv_s.T, cache_views, shifted)
        mod_post_s = jnp.concatenate([mod_s[:, 2 * d:3 * d], mod_s[:, 3 * d:5 * d]], axis=1)
        x1_s, h2_s, gate_s = _post(xs, yn_s.reshape(bs, SSD_INNER), o_s, gates_s, mod_post_s,
                                   g2n, w_ssd_proj[l], w_attn_proj[l], w_out[l], w_router, b_router,
                                   passes=3, merge_attn=False, tm=bs, rows_per_mod=1, h2_dtype=F32)
        xs = _moe(l, h2_s, gate_s, x1_s, mod_s[:, 5 * d:6 * d], w_exp_gate, w_exp_up, w_exp_down,
                  passes=3, tm=bs, rows_per_mod=1)

    fg = final_norm_g.reshape(1, d)
    y_prompt = xp.reshape(bp, seq, d)
    y_sample = _final_norm(xs, fg, bs).reshape(bs, 1, d)

    shifted = [jnp.transpose(s.reshape(depth, bs, ATT_HEADS, ATT_HEAD_DIM, s.shape[3]), (0, 1, 4, 2, 3))
               for s in shifted]

    st = jnp.stack
    res = [y_prompt, y_sample, st(outs["conv_p"]), st(outs["conv_s"]), st(outs["ssm_p"]),
           ssm_s_all.reshape(state_ssm.shape)]
    for gi in range(ATT_GROUPS):
        res += [st(kv_p[gi][0]), shifted[2 * gi], st(kv_p[gi][1]), shifted[2 * gi + 1]]
    return tuple(res)
```

```python
import functools
import math

import jax
import jax.numpy as jnp
from jax import lax
from jax.experimental import pallas as pl
from jax.experimental.pallas import tpu as pltpu

F32 = jnp.float32
BF16 = jnp.bfloat16

D_MODEL = 1024
SSD_INNER = 1024
SSD_HEAD_DIM = 64
SSD_HEADS = 16
SSD_GROUPS = 2
SSD_HPG = 8
SSD_STATE = 128
SSD_CONV = 4
SSD_CHUNK = 128
SSD_CONV_DIM = SSD_INNER + 2 * SSD_GROUPS * SSD_STATE
ATT_PATTERNS = ((128, 1), (512, 4), (2048, 16))
ATT_GROUPS = 3
ATT_HEADS = 8
ATT_HEAD_DIM = 64
ATT_BLOCK = 128
ATT_OUT = ATT_HEADS * ATT_HEAD_DIM
MOE_GROUPS = 4
MOE_PER_GROUP = 4
MOE_EXPERTS = 16
MOE_FF = 512
MOE_PAIRS = 6
MOE_CLASSES = MOE_GROUPS * MOE_PAIRS
MOE_TILE = 256
ROW_SUB = 8
IN_QKV = 3 * ATT_GROUPS * ATT_OUT
IN_GATE = 2 * D_MODEL
OFF_XBC = SSD_INNER
OFF_DT = OFF_XBC + SSD_CONV_DIM
OFF_QKV = OFF_DT + SSD_HEADS
OFF_GATE = OFF_QKV + IN_QKV
IN_WIDTH = OFF_GATE + IN_GATE
EPS = 1e-6

LANES = 128
VMEM_LIMIT = 56 * 1024 * 1024


def _cparams(sem, vmem=VMEM_LIMIT):
    return pltpu.CompilerParams(dimension_semantics=sem, vmem_limit_bytes=vmem)


def _split2(a):
    hi = a.astype(BF16)
    lo = (a - hi.astype(F32)).astype(BF16)
    return hi, lo


def _split3(a):
    hi = a.astype(BF16)
    r = a - hi.astype(F32)
    mid = r.astype(BF16)
    lo = (r - mid.astype(F32)).astype(BF16)
    return hi, mid, lo


def _dot(a, b):
    return jnp.dot(a, b, preferred_element_type=F32)


def _dot_nt(a, b):
    return lax.dot_general(a, b, (((1,), (1,)), ((), ())), preferred_element_type=F32)


def _mm(a, w, passes):
    if passes == 1:
        return _dot(a.astype(BF16), w.astype(BF16))
    a = a.astype(F32)
    w = w.astype(F32)
    a_hi, a_lo = _split2(a)
    w_hi, w_lo = _split2(w)
    return _dot(a_hi, w_hi) + (_dot(a_lo, w_hi) + _dot(a_hi, w_lo))


def _mm01(a, e01, terms):
    parts = _split3(a) if terms == 3 else _split2(a)
    out = _dot(parts[0], e01)
    for p in parts[1:]:
        out = out + _dot(p, e01)
    return out


def _sigmoid(x):
    return 1.0 / (1.0 + jnp.exp(-x))


def _silu(x):
    return x * _sigmoid(x)


def _softplus(x):
    return jnp.maximum(x, 0.0) + jnp.log(1.0 + jnp.exp(-jnp.abs(x)))


def _head_expand_matrix(rows, n_heads, width):
    r = lax.broadcasted_iota(jnp.int32, (rows, n_heads * width), 0)
    c = lax.broadcasted_iota(jnp.int32, (rows, n_heads * width), 1)
    return jnp.where((c // width) == r, 1.0, 0.0).astype(BF16)


def _rms_modulate(x, g, sh, sc):
    ms = jnp.mean(x * x, axis=-1, keepdims=True)
    y = x * lax.rsqrt(ms + EPS) * g
    return y * (1.0 + sc) + sh


def _mod_kernel(c_ref, w_ref, b_ref, o_ref):
    a = _silu(c_ref[...])
    o_ref[...] = _mm(a, w_ref[...], 3) + b_ref[...]


def _modulation(c_all, w_ada, b_ada):
    depth, d, n6 = w_ada.shape
    rows = c_all.shape[0]
    tn = 1024
    return pl.pallas_call(
        _mod_kernel,
        out_shape=jax.ShapeDtypeStruct((depth, rows, n6), F32),
        grid=(depth, n6 // tn),
        in_specs=[
            pl.BlockSpec((rows, d), lambda l, j: (0, 0)),
            pl.BlockSpec((None, d, tn), lambda l, j: (l, 0, j)),
            pl.BlockSpec((None, 1, tn), lambda l, j: (l, 0, j)),
        ],
        out_specs=pl.BlockSpec((None, rows, tn), lambda l, j: (l, 0, j)),
        compiler_params=_cparams(("arbitrary", "arbitrary")),
        name="adaln_mod",
    )(c_all, w_ada, b_ada.reshape(depth, 1, n6))


_IN_CHUNK = 512
TOK_TILE = 512

_W_CHUNK_ROWS = (tuple(range(0, OFF_DT, _IN_CHUNK))
                 + tuple(range(OFF_QKV, OFF_GATE, _IN_CHUNK)) + tuple(range(OFF_GATE, IN_WIDTH, _IN_CHUNK)))
_CH_Z = (0, 1)
_CH_XBC = (2, 3, 4)
_CH_QKV0 = 5
_CH_GATES = (14, 15, 16, 17)


def _w_prep_kernel(starts_ref, w_ref, o_ref):
    del starts_ref
    o_ref[...] = w_ref[0].T.astype(BF16)


def _prep_w_in(w_in_t):
    depth, _, d = w_in_t.shape
    n_ch = len(_W_CHUNK_ROWS)
    starts = jnp.asarray(_W_CHUNK_ROWS, jnp.int32)
    return pl.pallas_call(
        _w_prep_kernel,
        out_shape=jax.ShapeDtypeStruct((depth, n_ch, d, _IN_CHUNK), BF16),
        grid_spec=pltpu.PrefetchScalarGridSpec(
            num_scalar_prefetch=1,
            grid=(depth, n_ch),
            in_specs=[pl.BlockSpec((pl.Element(1), pl.Element(_IN_CHUNK), pl.Element(d)),
                                   lambda l, j, st: (l, pl.multiple_of(st[j], 16), 0))],
            out_specs=pl.BlockSpec((None, None, d, _IN_CHUNK), lambda l, j, st: (l, j, 0, 0)),
        ),
        compiler_params=_cparams(("arbitrary", "arbitrary")),
        name="w_in_prep",
    )(starts, w_in_t)


def _in_kernel(x_ref, mod_ref, g_ref, w_ref, wd_ref,
               z_ref, xbc_ref, dt_ref, a0_ref, a1_ref, a2_ref, gates_ref, h_scr):
    sh = mod_ref[:, 0:D_MODEL]
    sc = mod_ref[:, D_MODEL:2 * D_MODEL]
    hf = _rms_modulate(x_ref[...], g_ref[...], sh, sc)
    n_cb = D_MODEL // LANES
    for cbk in range(n_cb):
        h_scr[cbk] = hf[:, cbk * LANES:(cbk + 1) * LANES]
    h = hf.astype(BF16)
    qkv_chunks = lambda gi: tuple(_CH_QKV0 + which * ATT_GROUPS + gi for which in range(3))
    for chunks, o_ref in ((_CH_Z, z_ref), (_CH_XBC, xbc_ref), (qkv_chunks(0), a0_ref), (_CH_GATES, gates_ref)):
        for i, ch in enumerate(chunks):
            o_ref[:, i * _IN_CHUNK:(i + 1) * _IN_CHUNK] = _dot(h, w_ref[ch]).astype(o_ref.dtype)
    dt_ref[...] = _dot(h, wd_ref[...])
    for gi, a_ref in ((1, a1_ref), (2, a2_ref)):
        d, r_len, _ = a_ref.shape
        hp = jnp.concatenate(
            [jnp.concatenate([h_scr[cbk, pl.ds(r, r_len, stride=d), :] for cbk in range(n_cb)], axis=1)
             for r in range(d)], axis=0).astype(BF16)
        for i, ch in enumerate(qkv_chunks(gi)):
            res = _dot(hp, w_ref[ch]).astype(BF16)
            a_ref[:, :, i * _IN_CHUNK:(i + 1) * _IN_CHUNK] = res.reshape(d, r_len, _IN_CHUNK)


def _in_proj_prompt(layer, x, mod, norm_g, w_chunks, wd, bsz, seq):
    n, d = x.shape
    tm = TOK_TILE
    tiles_per_seq = seq // tm
    qw = 3 * ATT_OUT

    def const(shape):
        return pl.BlockSpec(shape, lambda i: (0, 0), pipeline_mode=pl.Buffered(1))

    a_shapes, a_specs = [], []
    for _, dil in ATT_PATTERNS[1:]:
        a_shapes.append(jax.ShapeDtypeStruct((bsz, tiles_per_seq, dil, tm // dil, qw), BF16))
        a_specs.append(pl.BlockSpec((None, None, dil, tm // dil, qw),
                                    lambda i: (i // tiles_per_seq, i % tiles_per_seq, 0, 0, 0)))
    return pl.pallas_call(
        _in_kernel,
        out_shape=(
            jax.ShapeDtypeStruct((n, SSD_INNER), BF16),
            jax.ShapeDtypeStruct((n, SSD_CONV_DIM), F32),
            jax.ShapeDtypeStruct((n, LANES), F32),
            jax.ShapeDtypeStruct((n, qw), BF16),
            a_shapes[0], a_shapes[1],
            jax.ShapeDtypeStruct((n, IN_GATE), BF16),
        ),
        grid=(n // tm,),
        in_specs=[
            pl.BlockSpec((tm, d), lambda i: (i, 0)),
            pl.BlockSpec((None, 1, 2 * d), lambda i: (i // tiles_per_seq, 0, 0)),
            const((1, d)),
            pl.BlockSpec((None,) + w_chunks.shape[1:], lambda i: (layer, 0, 0, 0), pipeline_mode=pl.Buffered(1)),
            const(wd.shape),
        ],
        out_specs=(
            pl.BlockSpec((tm, SSD_INNER), lambda i: (i, 0)),
            pl.BlockSpec((tm, SSD_CONV_DIM), lambda i: (i, 0)),
            pl.BlockSpec((tm, LANES), lambda i: (i, 0)),
            pl.BlockSpec((tm, qw), lambda i: (i, 0)),
            a_specs[0], a_specs[1],
            pl.BlockSpec((tm, IN_GATE), lambda i: (i, 0)),
        ),
        scratch_shapes=[pltpu.VMEM((d // LANES, tm, LANES), F32)],
        compiler_params=_cparams(("arbitrary",)),
        name="in_proj_prompt",
    )(x, mod, norm_g, w_chunks, wd)


def _in_small_kernel(x_ref, mod_ref, g_ref, w_ref, o_ref):
    sh = mod_ref[:, 0:D_MODEL]
    sc = mod_ref[:, D_MODEL:2 * D_MODEL]
    h = _rms_modulate(x_ref[...], g_ref[...], sh, sc)
    h_hi, h_lo = _split2(h)
    w_hi, w_lo = _split2(w_ref[...])
    o_ref[...] = _dot_nt(h_hi, w_hi) + (_dot_nt(h_lo, w_hi) + _dot_nt(h_hi, w_lo))


def _in_proj_sample(layer, x, mod, norm_g, w_in_t, tn=1024):
    m, d = x.shape
    width = w_in_t.shape[1]
    return pl.pallas_call(
        _in_small_kernel,
        out_shape=jax.ShapeDtypeStruct((m, width), F32),
        grid=(pl.cdiv(width, tn),),
        in_specs=[
            pl.BlockSpec((m, d), lambda j: (0, 0)),
            pl.BlockSpec((m, 2 * d), lambda j: (0, 0)),
            pl.BlockSpec((1, d), lambda j: (0, 0)),
            pl.BlockSpec((None, tn, d), lambda j: (layer, j, 0)),
        ],
        out_specs=pl.BlockSpec((m, tn), lambda j: (0, j)),
        compiler_params=_cparams(("arbitrary",)),
        name="in_proj_sample",
    )(x, mod, norm_g, w_in_t)


_SSD_SUB = 2


def _ssd_kernel(xbc_ref, z_ref, dt_ref, cw_ref, cb_ref, dtb_ref, alog_ref, dskip_ref, ng_ref,
                yn_ref, hout_ref, h_scr, xp_scr):
    q = SSD_CHUNK
    c = pl.program_id(1)

    @pl.when(c == 0)
    def _():
        h_scr[...] = jnp.zeros_like(h_scr)
        xp_scr[0:8, :] = jnp.zeros((8, SSD_CONV_DIM), F32)

    for sub in range(_SSD_SUB):
        rows = slice(sub * q, (sub + 1) * q)
        _ssd_chunk(xbc_ref.at[rows], z_ref.at[rows], dt_ref.at[rows], cw_ref, cb_ref, dtb_ref, alog_ref,
                   dskip_ref, ng_ref, yn_ref.at[rows], h_scr, xp_scr)

    @pl.when(c == pl.num_programs(1) - 1)
    def _():
        hout_ref[...] = h_scr[...]


def _ssd_chunk(xbc_ref, z_ref, dt_ref, cw_ref, cb_ref, dtb_ref, alog_ref, dskip_ref, ng_ref,
               yn_ref, h_scr, xp_scr):
    q = SSD_CHUNK
    xp_scr[8:8 + q, :] = xbc_ref[...]
    acc = cb_ref[...] + cw_ref[3:4, :] * xp_scr[8:8 + q, :]
    for k in range(SSD_CONV - 1):
        acc = acc + cw_ref[k:k + 1, :] * xp_scr[5 + k:5 + k + q, :]
    xp_scr[0:8, :] = xp_scr[q:q + 8, :]
    xc = _silu(acc)
    xs = xc[:, 0:SSD_INNER]
    xs_bf = xs.astype(BF16)

    lane_q = lax.broadcasted_iota(jnp.int32, (q, LANES), 1)
    row_q = lax.broadcasted_iota(jnp.int32, (q, LANES), 0)
    causal = row_q >= lane_q
    tri = jnp.where(causal, 1.0, 0.0).astype(BF16)
    tri_t = jnp.where(lane_q >= row_q, 1.0, 0.0).astype(BF16)
    e_heads = _head_expand_matrix(LANES, SSD_HEADS, SSD_HEAD_DIM)

    dt = _softplus(dt_ref[...] + dtb_ref[...])
    dt = jnp.where(lane_q < SSD_HEADS, dt, 0.0)
    a = -jnp.exp(alog_ref[...])
    d_a = dt * a
    cum = _mm01_left(tri, d_a)
    d_a_t = d_a.T
    dt_t = dt.T
    cum_t = _mm01(d_a_t, tri_t, 3)
    cum_last = cum[q - 1:q, :]
    exp_cum = jnp.exp(cum)
    dec_end = jnp.exp(cum_last - cum)
    stack = jnp.concatenate([exp_cum, dec_end * dt], axis=0)
    full = _mm01(stack, e_heads, 2)
    exp_cum_full = full[0:q]
    w_full = full[q:2 * q]
    chunk_dec_t = jnp.exp(cum_t[:, q - 1:q])

    lane_half = lax.broadcasted_iota(jnp.int32, (q, LANES), 1) < SSD_HEAD_DIM
    y_parts = []
    for g in range(SSD_GROUPS):
        b_off = SSD_INNER + g * SSD_STATE
        c_off = SSD_INNER + SSD_GROUPS * SSD_STATE + g * SSD_STATE
        bm = xc[:, b_off:b_off + SSD_STATE].astype(BF16)
        cm = xc[:, c_off:c_off + SSD_STATE].astype(BF16)
        cbm = _dot_nt(cm, bm)
        gw = SSD_HPG * SSD_HEAD_DIM
        g0 = g * gw
        yd = []
        for pair in range(SSD_HPG // 2):
            x_pair = xs_bf[:, g0 + pair * LANES:g0 + (pair + 1) * LANES]
            halves = []
            for hh in range(2):
                h = g * SSD_HPG + pair * 2 + hh
                seg = cum[:, h:h + 1] - cum_t[h:h + 1, :]
                dec = jnp.exp(jnp.where(causal, seg, -jnp.inf))
                m_h = (cbm * dec * dt_t[h:h + 1, :]).astype(BF16)
                halves.append(_dot(m_h, x_pair))
            yd.append(jnp.where(lane_half, halves[0], halves[1]))
        y_diag = jnp.concatenate(yd, axis=1)
        h_g = h_scr[g0:g0 + gw, :]
        y_off = _dot_nt(cm, h_g.astype(BF16)) * exp_cum_full[:, g0:g0 + gw]
        y_parts.append(y_diag + y_off)
        xw = (xs[:, g0:g0 + gw] * w_full[:, g0:g0 + gw])
        st = _dot(xw.T.astype(BF16), bm)
        for e in range(SSD_HPG):
            h = g * SSD_HPG + e
            r0 = g0 + e * SSD_HEAD_DIM
            h_scr[r0:r0 + SSD_HEAD_DIM, :] = (h_scr[r0:r0 + SSD_HEAD_DIM, :] * chunk_dec_t[h:h + 1, :]
                                              + st[e * SSD_HEAD_DIM:(e + 1) * SSD_HEAD_DIM, :])

    y = jnp.concatenate(y_parts, axis=1) + dskip_ref[...] * xs
    y = y * _silu(z_ref[...].astype(F32))
    outs = []
    for g in range(SSD_GROUPS):
        gw = SSD_HPG * SSD_HEAD_DIM
        yg = y[:, g * gw:(g + 1) * gw]
        ms = jnp.mean(yg * yg, axis=-1, keepdims=True)
        outs.append(yg * lax.rsqrt(ms + EPS) * ng_ref[:, g * gw:(g + 1) * gw])
    yn_ref[...] = jnp.concatenate(outs, axis=1).astype(yn_ref.dtype)


def _mm01_left(tri01, a):
    hi, mid, lo = _split3(a)
    return _dot(tri01, hi) + (_dot(tri01, mid) + _dot(tri01, lo))


def _ssd_prompt(xbc, z, dt_raw, conv_w, conv_b, dt_bias, a_log, d_skip_full, ssd_norm_g, bsz, seq):
    q = SSD_CHUNK * _SSD_SUB
    nc = seq // q
    row = lambda b, c: (b * nc + c, 0)
    const = lambda b, c: (0, 0)
    return pl.pallas_call(
        _ssd_kernel,
        out_shape=(
            jax.ShapeDtypeStruct((bsz * seq, SSD_INNER), BF16),
            jax.ShapeDtypeStruct((bsz, SSD_INNER, SSD_STATE), F32),
        ),
        grid=(bsz, nc),
        in_specs=[
            pl.BlockSpec((q, SSD_CONV_DIM), row),
            pl.BlockSpec((q, SSD_INNER), row),
            pl.BlockSpec((q, LANES), row),
            pl.BlockSpec((SSD_CONV, SSD_CONV_DIM), const),
            pl.BlockSpec((1, SSD_CONV_DIM), const),
            pl.BlockSpec((1, LANES), const),
            pl.BlockSpec((1, LANES), const),
            pl.BlockSpec((1, SSD_INNER), const),
            pl.BlockSpec((1, SSD_INNER), const),
        ],
        out_specs=(
            pl.BlockSpec((q, SSD_INNER), row),
            pl.BlockSpec((None, SSD_INNER, SSD_STATE), lambda b, c: (b, 0, 0)),
        ),
        scratch_shapes=[
            pltpu.VMEM((SSD_INNER, SSD_STATE), F32),
            pltpu.VMEM((SSD_CHUNK + 8, SSD_CONV_DIM), F32),
        ],
        compiler_params=_cparams(("arbitrary", "arbitrary")),
        name="ssd_prompt",
    )(xbc, z, dt_raw, conv_w, conv_b, dt_bias, a_log, d_skip_full, ssd_norm_g)


def _attn_kernel(q_ref, kc_ref, kp_ref, vc_ref, vp_ref, o_out_ref, lse_out_ref,
                 q_ref_s, kwin, vwin, o_ref, lse_ref, *, tq, band):
    blk = ATT_BLOCK
    j = pl.program_id(2)
    q_ref_s[...] = (q_ref[...] * (ATT_HEAD_DIM ** -0.5)).astype(BF16).reshape(tq, ATT_OUT)
    kwin[0:blk, :] = kp_ref[...].reshape(blk, ATT_OUT)
    kwin[blk:blk + tq, :] = kc_ref[...].reshape(tq, ATT_OUT)
    vwin[0:blk, :] = vp_ref[...].reshape(blk, ATT_OUT)
    vwin[blk:blk + tq, :] = vc_ref[...].reshape(tq, ATT_OUT)
    q_ref = q_ref_s

    qi = lax.broadcasted_iota(jnp.int32, (blk, 2 * blk), 0)
    ki = lax.broadcasted_iota(jnp.int32, (blk, 2 * blk), 1)
    dist = qi + blk - ki
    in_band = (dist >= 0) & (dist <= band)
    lane = lax.broadcasted_iota(jnp.int32, (blk, LANES), 1)
    lane_half = lane < ATT_HEAD_DIM
    scale = ATT_HEAD_DIM ** -0.5
    zero_bf = jnp.zeros((blk, LANES), BF16)

    for i in range(tq // blk):
        if i == 0:
            valid = in_band & ((ki >= blk) | (j > 0))
        else:
            valid = in_band
        lse_tile = jnp.zeros((blk, LANES), F32)
        for hp in range(ATT_HEADS // 2):
            c0 = hp * LANES
            q_pair = q_ref[i * blk:(i + 1) * blk, c0:c0 + LANES]
            k_pair = kwin[i * blk:(i + 2) * blk, c0:c0 + LANES]
            v_pair = vwin[i * blk:(i + 2) * blk, c0:c0 + LANES]
            halves = []
            for hh in range(2):
                q_m = jnp.where(lane_half if hh == 0 else jnp.logical_not(lane_half), q_pair, zero_bf)
                s = _dot_nt(q_m, k_pair)
                s = jnp.where(valid, s, -jnp.inf)
                m = jnp.max(s, axis=-1, keepdims=True)
                e = jnp.exp(s - m)
                den = jnp.sum(e, axis=-1, keepdims=True)
                pv = _dot(e.astype(BF16), v_pair)
                halves.append(pv / den)
                lse = m + jnp.log(den)
                head = hp * 2 + hh
                lse_tile = jnp.where((lane == head) | (lane == ATT_HEADS + head), lse, lse_tile)
            o_ref[i * blk:(i + 1) * blk, c0:c0 + LANES] = jnp.where(lane_half, halves[0], halves[1]).astype(o_ref.dtype)
        lse_ref[i * blk:(i + 1) * blk, :] = lse_tile
    o_out_ref[...] = o_ref[...].reshape(o_out_ref.shape)
    lse_out_ref[...] = lse_ref[...].reshape(lse_out_ref.shape)


def _attn_prompt(a_g, gi, bsz, seq):
    win, dil = ATT_PATTERNS[gi]
    band = win // dil
    blk = ATT_BLOCK
    rows = TOK_TILE // dil
    tiles = seq // TOK_TILE
    length = seq // dil
    tq = min(512, length)
    nj = length // tq
    tq_tiles = tq // rows
    cur = lambda which: pl.BlockSpec((None, tq_tiles, None, rows, ATT_OUT),
                                     lambda b, r, j: (b, j, r, 0, which))
    if rows >= blk:
        prev = lambda which: pl.BlockSpec(
            (None, None, None, blk, ATT_OUT),
            lambda b, r, j: (b, jnp.maximum(j * tq_tiles - 1, 0), r, rows // blk - 1, which))
    else:
        prev = lambda which: pl.BlockSpec(
            (None, blk // rows, None, rows, ATT_OUT),
            lambda b, r, j: (b, jnp.maximum(j * (tq // blk) - 1, 0), r, 0, which))
    return pl.pallas_call(
        functools.partial(_attn_kernel, tq=tq, band=band),
        out_shape=(
            jax.ShapeDtypeStruct((bsz, tiles, dil, rows, ATT_OUT), BF16),
            jax.ShapeDtypeStruct((bsz, tiles, dil, rows, LANES), F32),
        ),
        grid=(bsz, dil, nj),
        in_specs=[cur(0), cur(1), prev(1), cur(2), prev(2)],
        out_specs=(
            pl.BlockSpec((None, tq_tiles, None, rows, ATT_OUT), lambda b, r, j: (b, j, r, 0, 0)),
            pl.BlockSpec((None, tq_tiles, None, rows, LANES), lambda b, r, j: (b, j, r, 0, 0)),
        ),
        scratch_shapes=[
            pltpu.VMEM((tq, ATT_OUT), BF16),
            pltpu.VMEM((blk + tq, ATT_OUT), BF16),
            pltpu.VMEM((blk + tq, ATT_OUT), BF16),
            pltpu.VMEM((tq, ATT_OUT), BF16),
            pltpu.VMEM((tq, LANES), F32),
        ],
        compiler_params=_cparams(("arbitrary", "arbitrary", "arbitrary")),
        name=f"attn_prompt_w{win}",
    )(a_g, a_g, a_g, a_g, a_g)


def _router_gates(logits):
    shape = logits.shape
    lane = lax.broadcasted_iota(jnp.int32, shape, 1)
    big = jnp.int32(1 << 20)
    neg = -jnp.inf
    is_grp = (lane >= MOE_EXPERTS) & (lane < MOE_EXPERTS + MOE_GROUPS)
    lg = jnp.where(is_grp, logits, neg)
    gm = jnp.max(lg, axis=-1, keepdims=True)
    g_lane = jnp.min(jnp.where(lg == gm, lane, big), axis=-1, keepdims=True)
    g_sum = jnp.sum(jnp.exp(lg - gm), axis=-1, keepdims=True)
    g_w = 1.0 / g_sum
    lo = (g_lane - MOE_EXPERTS) * MOE_PER_GROUP
    in_grp = (lane >= lo) & (lane < lo + MOE_PER_GROUP)
    le = jnp.where(in_grp, logits, neg)
    m1 = jnp.max(le, axis=-1, keepdims=True)
    i1 = jnp.min(jnp.where(le == m1, lane, big), axis=-1, keepdims=True)
    le2 = jnp.where(lane == i1, neg, le)
    m2 = jnp.max(le2, axis=-1, keepdims=True)
    i2 = jnp.min(jnp.where(le2 == m2, lane, big), axis=-1, keepdims=True)
    t = jnp.exp(m2 - m1)
    w1 = 1.0 / (1.0 + t)
    w2 = t / (1.0 + t)
    gate = jnp.where(lane == i1, g_w * w1, jnp.where(lane == i2, g_w * w2, 0.0))
    e_lo = jnp.minimum(i1, i2)
    e_hi = jnp.maximum(i1, i2)
    lo_l = e_lo & (MOE_PER_GROUP - 1)
    hi_l = e_hi & (MOE_PER_GROUP - 1)
    cls = (e_lo >> 2) * MOE_PAIRS + ((lo_l * (7 - lo_l)) >> 1) + (hi_l - lo_l - 1)
    return gate, cls


def _post_kernel(*refs, passes, merge_attn):
    if merge_attn:
        (x_ref, yn_ref, o0_ref, o1_ref, o2_ref, l0_ref, l1_ref, l2_ref, gates_ref, mod_ref, n2_ref,
         wssd_ref, wattn_ref, wout_ref, wr_ref, br_ref, x1_ref, h2_ref, pos_ref, cnt_ref,
         o1_scr, o2_scr, l1_scr, l2_scr) = refs
        for src_ref, dst_ref in ((o1_ref, o1_scr), (o2_ref, o2_scr), (l1_ref, l1_scr), (l2_ref, l2_scr)):
            dil, r_len, width = src_ref.shape
            for r in range(dil):
                blk_r = src_ref[r].astype(F32)
                for cbk in range(width // LANES):
                    dst_ref[cbk, pl.ds(r, r_len, stride=dil), :] = blk_r[:, cbk * LANES:(cbk + 1) * LANES]
        o_nat = [o0_ref[...].astype(F32)] + [
            jnp.concatenate([scr[cbk] for cbk in range(ATT_OUT // LANES)], axis=1) for scr in (o1_scr, o2_scr)]
        l0, l1, l2 = l0_ref[...], l1_scr[0], l2_scr[0]
        mx = jnp.maximum(jnp.maximum(l0, l1), l2)
        e0, e1, e2 = jnp.exp(l0 - mx), jnp.exp(l1 - mx), jnp.exp(l2 - mx)
        inv = 1.0 / (e0 + e1 + e2)
        lane = lax.broadcasted_iota(jnp.int32, l0.shape, 1)
        e8 = _head_expand_matrix(LANES, ATT_HEADS, ATT_HEAD_DIM)
        r = lax.broadcasted_iota(jnp.int32, e8.shape, 0)
        c = lax.broadcasted_iota(jnp.int32, e8.shape, 1)
        e8 = jnp.where((c // ATT_HEAD_DIM) == (r - ATT_HEADS), 1.0, e8.astype(F32)).astype(BF16)
        o = None
        for e_g, o_g in zip((e0, e1, e2), o_nat):
            w = e_g * inv
            hi, lo = _split2(w)
            w_exp = _dot(jnp.where(lane < ATT_HEADS, hi, lo), e8)
            term = w_exp * o_g
            o = term if o is None else o + term
    else:
        (x_ref, yn_ref, o_ref, gates_ref, mod_ref, n2_ref,
         wssd_ref, wattn_ref, wout_ref, wr_ref, br_ref, x1_ref, h2_ref, gate_ref) = refs
        o = o_ref[...]
    d = D_MODEL
    g1 = mod_ref[:, 0:d]
    sh2 = mod_ref[:, d:2 * d]
    sc2 = mod_ref[:, 2 * d:3 * d]
    ssd_branch = _mm(yn_ref[...], wssd_ref[...], passes)
    attn_branch = _mm(o, wattn_ref[...], passes)
    ga = gates_ref[:, 0:d].astype(F32)
    gb = gates_ref[:, d:2 * d].astype(F32)
    mixed = _sigmoid(ga) * ssd_branch + _sigmoid(gb) * attn_branch
    x1 = x_ref[...] + g1 * _mm(mixed, wout_ref[...], passes)
    x1_ref[...] = x1
    h2 = _rms_modulate(x1, n2_ref[...], sh2, sc2)
    logits = _mm(h2, wr_ref[...], 3) + br_ref[...]
    gate, cls = _router_gates(logits)
    if not merge_attn:
        h2_ref[...] = h2
        gate_ref[...] = gate
        return
    tm = h2.shape[0]
    for cbk in range(ROW_SUB):
        h2_ref[pl.ds(cbk, tm, stride=ROW_SUB), :] = h2[:, cbk * LANES:(cbk + 1) * LANES]
    step = pl.program_id(0)

    @pl.when(step == 0)
    def _():
        cnt_ref[...] = jnp.zeros_like(cnt_ref)

    lane = lax.broadcasted_iota(jnp.int32, (tm, LANES), 1)
    onehot = lane == cls
    ri = lax.broadcasted_iota(jnp.int32, (tm, tm), 0)
    ci = lax.broadcasted_iota(jnp.int32, (tm, tm), 1)
    before = jnp.where(ci < ri, 1.0, 0.0).astype(BF16)
    seen = _dot(before, jnp.where(onehot, 1.0, 0.0).astype(BF16)) + cnt_ref[...]
    rank = jnp.sum(jnp.where(onehot, seen, 0.0), axis=-1, keepdims=True)
    n_tokens = tm * pl.num_programs(0)
    pos_ref[...] = jnp.broadcast_to(cls.astype(F32) * n_tokens + rank, (tm, LANES))
    cnt_ref[...] = cnt_ref[...] + jnp.sum(jnp.where(onehot, 1.0, 0.0), axis=0, keepdims=True)


def _post(x, yn, attn_in, gates, mod, norm2_g, wssd, wattn, wout, wr, br, *, passes, merge_attn,
          tm, rows_per_mod, h2_dtype):
    n, d = x.shape
    per_row_mod = rows_per_mod == 1
    if per_row_mod:
        mod_spec = pl.BlockSpec((tm, 3 * d), lambda i: (i, 0))
    else:
        tiles = rows_per_mod // tm
        mod_spec = pl.BlockSpec((None, 1, 3 * d), lambda i: (i // tiles, 0, 0))
    row = lambda w: pl.BlockSpec((tm, w), lambda i: (i, 0))
    const = lambda a: pl.BlockSpec(a.shape, lambda i: (0, 0), pipeline_mode=pl.Buffered(1))
    scratch = []
    if merge_attn:
        assert tm == TOK_TILE
        o_list, lse_list = attn_in
        attn_args = list(o_list) + list(lse_list)
        tiles = rows_per_mod // tm

        def tile_spec(a):
            _, _, dil, r_len, w = a.shape
            if dil == 1:
                return pl.BlockSpec((None, None, None, r_len, w), lambda i: (i // tiles, i % tiles, 0, 0, 0))
            return pl.BlockSpec((None, None, dil, r_len, w), lambda i: (i // tiles, i % tiles, 0, 0, 0))

        attn_specs = [tile_spec(a) for a in attn_args]
        scratch = [pltpu.VMEM((ATT_OUT // LANES, tm, LANES), F32), pltpu.VMEM((ATT_OUT // LANES, tm, LANES), F32),
                   pltpu.VMEM((1, tm, LANES), F32), pltpu.VMEM((1, tm, LANES), F32)]
    else:
        attn_args = [attn_in]
        attn_specs = [row(ATT_OUT)]
    if merge_attn:
        out_shape = (jax.ShapeDtypeStruct((n, d), F32), jax.ShapeDtypeStruct((n * ROW_SUB, LANES), F32),
                     jax.ShapeDtypeStruct((n, LANES), F32), jax.ShapeDtypeStruct((1, LANES), F32))
        out_specs = (row(d), pl.BlockSpec((tm * ROW_SUB, LANES), lambda i: (i, 0)), row(LANES),
                     pl.BlockSpec((1, LANES), lambda i: (0, 0)))
    else:
        out_shape = (jax.ShapeDtypeStruct((n, d), F32), jax.ShapeDtypeStruct((n, d), h2_dtype),
                     jax.ShapeDtypeStruct((n, LANES), F32))
        out_specs = (row(d), row(d), row(LANES))
    return pl.pallas_call(
        functools.partial(_post_kernel, passes=passes, merge_attn=merge_attn),
        out_shape=out_shape,
        grid=(n // tm,),
        in_specs=[row(d), row(SSD_INNER)] + attn_specs + [row(IN_GATE), mod_spec, const(norm2_g),
                                                          const(wssd), const(wattn), const(wout),
                                                          const(wr), const(br)],
        out_specs=out_specs,
        scratch_shapes=scratch,
        compiler_params=_cparams(("arbitrary",)),
        name="post_merge" if merge_attn else "post_sample",
    )(x, yn, *attn_args, gates, mod, norm2_g, wssd, wattn, wout, wr, br)


def _moe_kernel(h2_ref, gate_ref, x1_ref, g2_ref, wg_ref, wu_ref, wd_ref, x2_ref, acc_ref, *, passes):
    e = pl.program_id(1)

    @pl.when(e == 0)
    def _():
        acc_ref[...] = jnp.zeros_like(acc_ref)

    h2 = h2_ref[...]
    hg = _mm(h2, wg_ref[...], passes)
    hu = _mm(h2, wu_ref[...], passes)
    gate = gate_ref[...]
    lane = lax.broadcasted_iota(jnp.int32, gate.shape, 1)
    gcol = jnp.sum(jnp.where(lane == e, gate, 0.0), axis=-1, keepdims=True)
    act = _silu(hg) * hu * gcol
    acc_ref[...] += _mm(act, wd_ref[...], passes)

    @pl.when(e == pl.num_programs(1) - 1)
    def _():
        x2_ref[...] = x1_ref[...] + g2_ref[...] * acc_ref[...]


def _moe(layer, h2, gate, x1, g2, wg, wu, wd, *, passes, tm, rows_per_mod):
    n, d = x1.shape
    if rows_per_mod == 1:
        g2_spec = pl.BlockSpec((tm, d), lambda i, e: (i, 0))
    else:
        tiles = rows_per_mod // tm
        g2_spec = pl.BlockSpec((None, 1, d), lambda i, e: (i // tiles, 0, 0))
    row = lambda w: pl.BlockSpec((tm, w), lambda i, e: (i, 0))
    return pl.pallas_call(
        functools.partial(_moe_kernel, passes=passes),
        out_shape=jax.ShapeDtypeStruct((n, d), F32),
        grid=(n // tm, MOE_EXPERTS),
        in_specs=[
            row(d), row(LANES), row(d), g2_spec,
            pl.BlockSpec((None, None, d, MOE_FF), lambda i, e: (layer, e, 0, 0)),
            pl.BlockSpec((None, None, d, MOE_FF), lambda i, e: (layer, e, 0, 0)),
            pl.BlockSpec((None, None, MOE_FF, d), lambda i, e: (layer, e, 0, 0)),
        ],
        out_specs=row(d),
        scratch_shapes=[pltpu.VMEM((tm, d), F32)],
        compiler_params=_cparams(("arbitrary", "arbitrary")),
        name="moe_dense",
    )(h2, gate, x1, g2, wg, wu, wd)


_PAIR_LO = (0, 0, 0, 1, 1, 2)
_PAIR_HI = (1, 2, 3, 2, 3, 3)
_DMA_UNROLL = 8
_ROUTE_TILE = 2048


def _moe_plan(counts, n_tokens):
    tm = MOE_TILE
    blocks_per_class = n_tokens // tm
    ntile = (counts + tm - 1) // tm
    cum = jnp.cumsum(ntile)
    total = cum[-1]
    t_max = blocks_per_class + MOE_CLASSES
    t_eff = jnp.minimum(jnp.arange(t_max, dtype=jnp.int32), total - 1)
    cls = jnp.searchsorted(cum, t_eff, side="right").astype(jnp.int32)
    blk = cls * blocks_per_class + (t_eff - (cum - ntile)[cls])
    grp, pair = cls // MOE_PAIRS, cls % MOE_PAIRS
    e_lo = grp * MOE_PER_GROUP + jnp.asarray(_PAIR_LO, jnp.int32)[pair]
    e_hi = grp * MOE_PER_GROUP + jnp.asarray(_PAIR_HI, jnp.int32)[pair]
    z_need = ((counts % tm) != 0).astype(jnp.int32)
    z_blk = jnp.arange(MOE_CLASSES, dtype=jnp.int32) * blocks_per_class + jnp.maximum(ntile - 1, 0)
    return (blk.astype(jnp.int32), e_lo, e_hi, total.reshape(1).astype(jnp.int32), z_need, z_blk.astype(jnp.int32))


def _row_copy(src, src_row, dst, dst_row, sem):
    return pltpu.make_async_copy(src.at[pl.ds(pl.multiple_of(src_row * ROW_SUB, ROW_SUB), ROW_SUB), :],
                                 dst.at[pl.ds(pl.multiple_of(dst_row * ROW_SUB, ROW_SUB), ROW_SUB), :], sem)


def _dispatch_kernel(pos_ref, zneed_ref, zblk_ref, rows_ref, xs_ref, zero_scr, sem, zsem, *, tm):
    step = pl.program_id(0)
    tile_rows = MOE_TILE * ROW_SUB

    def zero_copy(c):
        start = pl.multiple_of(zblk_ref[c] * tile_rows, tile_rows)
        return pltpu.make_async_copy(zero_scr, xs_ref.at[pl.ds(start, tile_rows), :], zsem)

    @pl.when(step == 0)
    def _():
        zero_scr[...] = jnp.zeros_like(zero_scr)
        for c in range(MOE_CLASSES):
            @pl.when(zneed_ref[c] != 0)
            def _():
                zero_copy(c).start()
        for c in range(MOE_CLASSES):
            @pl.when(zneed_ref[c] != 0)
            def _():
                zero_copy(c).wait()

    base = step * tm

    def body(i8, carry):
        for u in range(_DMA_UNROLL):
            i = i8 * _DMA_UNROLL + u
            _row_copy(rows_ref, i, xs_ref, pos_ref[base + i], sem).start(priority=u % 2)
        return carry

    lax.fori_loop(0, tm // _DMA_UNROLL, body, 0)
    pltpu.make_async_copy(rows_ref, xs_ref.at[pl.ds(0, tm * ROW_SUB), :], sem).wait()


def _dispatch(pos, z_need, z_blk, rows, n_tokens, tm=_ROUTE_TILE):
    sorted_rows = MOE_CLASSES * n_tokens * ROW_SUB
    return pl.pallas_call(
        functools.partial(_dispatch_kernel, tm=tm),
        out_shape=jax.ShapeDtypeStruct((sorted_rows, LANES), F32),
        grid_spec=pltpu.PrefetchScalarGridSpec(
            num_scalar_prefetch=3,
            grid=(n_tokens // tm,),
            in_specs=[pl.BlockSpec((tm * ROW_SUB, LANES), lambda i, *_: (i, 0))],
            out_specs=pl.BlockSpec(memory_space=pl.ANY),
            scratch_shapes=[pltpu.VMEM((MOE_TILE * ROW_SUB, LANES), F32),
                            pltpu.SemaphoreType.DMA(()), pltpu.SemaphoreType.DMA(())],
        ),
        compiler_params=pltpu.CompilerParams(dimension_semantics=("arbitrary",), vmem_limit_bytes=VMEM_LIMIT,
                                             disable_bounds_checks=True),
        name="moe_dispatch",
    )(pos, z_need, z_blk, rows)


def _moe_sparse_kernel(blk_ref, elo_ref, ehi_ref, nact_ref, xs_ref, wr_ref, br_ref,
                       wg_lo_ref, wu_lo_ref, wd_lo_ref, wg_hi_ref, wu_hi_ref, wd_hi_ref, ys_ref):
    t = pl.program_id(0)
    tm = MOE_TILE

    @pl.when(t < nact_ref[0])
    def _():
        x = jnp.concatenate([xs_ref[pl.ds(c, tm, stride=ROW_SUB), :] for c in range(ROW_SUB)], axis=1)
        xb = x.astype(BF16)
        w_hi16, w_lo16 = _split2(wr_ref[...])
        logits = _dot(xb, w_hi16) + _dot(xb, w_lo16) + br_ref[...]
        lane = lax.broadcasted_iota(jnp.int32, logits.shape, 1)
        e_lo, e_hi = elo_ref[t], ehi_ref[t]
        g_lane = MOE_EXPERTS + (e_lo >> 2)
        pick = lambda ln: jnp.sum(jnp.where(lane == ln, logits, 0.0), axis=-1, keepdims=True)
        l_lo, l_hi, l_g = pick(e_lo), pick(e_hi), pick(g_lane)
        is_grp = (lane >= MOE_EXPERTS) & (lane < MOE_EXPERTS + MOE_GROUPS)
        g_w = 1.0 / jnp.sum(jnp.where(is_grp, jnp.exp(logits - l_g), 0.0), axis=-1, keepdims=True)
        w_a = g_w / (1.0 + jnp.exp(l_hi - l_lo))
        w_b = g_w / (1.0 + jnp.exp(l_lo - l_hi))
        bf = lambda w_ref: w_ref[...].astype(BF16)
        act_a = (_silu(_dot(xb, bf(wg_lo_ref))) * _dot(xb, bf(wu_lo_ref)) * w_a).astype(BF16)
        act_b = (_silu(_dot(xb, bf(wg_hi_ref))) * _dot(xb, bf(wu_hi_ref)) * w_b).astype(BF16)
        y = _dot(act_a, bf(wd_lo_ref)) + _dot(act_b, bf(wd_hi_ref))
        for c in range(ROW_SUB):
            ys_ref[pl.ds(c, tm, stride=ROW_SUB), :] = y[:, c * LANES:(c + 1) * LANES]


def _moe_sparse(layer, plan, xs, wr, br, wg, wu, wd, n_tokens):
    blk, e_lo, e_hi, n_act = plan
    tm = MOE_TILE
    t_max = blk.shape[0]
    d = D_MODEL
    tile = pl.BlockSpec((tm * ROW_SUB, LANES), lambda t, blk, lo, hi, na: (blk[t], 0))
    w_lo = lambda shape: pl.BlockSpec((None, None) + shape, lambda t, blk, lo, hi, na: (layer, lo[t], 0, 0))
    w_hi = lambda shape: pl.BlockSpec((None, None) + shape, lambda t, blk, lo, hi, na: (layer, hi[t], 0, 0))
    const = lambda a: pl.BlockSpec(a.shape, lambda t, *_: (0, 0))
    return pl.pallas_call(
        _moe_sparse_kernel,
        out_shape=jax.ShapeDtypeStruct(xs.shape, F32),
        grid_spec=pltpu.PrefetchScalarGridSpec(
            num_scalar_prefetch=4,
            grid=(t_max,),
            in_specs=[tile, const(wr), const(br),
                      w_lo((d, MOE_FF)), w_lo((d, MOE_FF)), w_lo((MOE_FF, d)),
                      w_hi((d, MOE_FF)), w_hi((d, MOE_FF)), w_hi((MOE_FF, d))],
            out_specs=tile,
        ),
        compiler_params=_cparams(("arbitrary",)),
        name="moe_sparse",
    )(blk, e_lo, e_hi, n_act, xs, wr, br, wg, wu, wd, wg, wu, wd)


def _combine_kernel(pos_ref, ys_ref, x1_ref, g2_ref, *rest, tm, final):
    if final:
        fg_ref, x2_ref, buf, sem = rest
    else:
        x2_ref, buf, sem = rest
    base = pl.program_id(0) * tm

    def body(i8, carry):
        for u in range(_DMA_UNROLL):
            i = i8 * _DMA_UNROLL + u
            _row_copy(ys_ref, pos_ref[base + i], buf, i, sem).start(priority=u % 2)
        return carry

    lax.fori_loop(0, tm // _DMA_UNROLL, body, 0)
    pltpu.make_async_copy(ys_ref.at[pl.ds(0, tm * ROW_SUB), :], buf, sem).wait()
    y = jnp.concatenate([buf[pl.ds(c, tm, stride=ROW_SUB), :] for c in range(ROW_SUB)], axis=1)
    x2 = x1_ref[...] + g2_ref[...] * y
    if final:
        ms = jnp.mean(x2 * x2, axis=-1, keepdims=True)
        x2 = x2 * lax.rsqrt(ms + EPS) * fg_ref[...]
    x2_ref[...] = x2


def _combine(pos, ys, x1, g2, rows_per_mod, final_g=None, tm=_ROUTE_TILE):
    n, d = x1.shape
    tiles = rows_per_mod // tm
    final = final_g is not None
    extra_specs = [pl.BlockSpec((1, d), lambda i, *_: (0, 0))] if final else []
    extra_args = [final_g] if final else []
    return pl.pallas_call(
        functools.partial(_combine_kernel, tm=tm, final=final),
        out_shape=jax.ShapeDtypeStruct((n, d), F32),
        grid_spec=pltpu.PrefetchScalarGridSpec(
            num_scalar_prefetch=1,
            grid=(n // tm,),
            in_specs=[pl.BlockSpec(memory_space=pl.ANY),
                      pl.BlockSpec((tm, d), lambda i, *_: (i, 0)),
                      pl.BlockSpec((None, 1, d), lambda i, *_: (i // tiles, 0, 0))] + extra_specs,
            out_specs=pl.BlockSpec((tm, d), lambda i, *_: (i, 0)),
            scratch_shapes=[pltpu.VMEM((tm * ROW_SUB, LANES), F32), pltpu.SemaphoreType.DMA(())],
        ),
        compiler_params=pltpu.CompilerParams(dimension_semantics=("arbitrary",), vmem_limit_bytes=VMEM_LIMIT,
                                             disable_bounds_checks=True),
        name="moe_combine",
    )(pos, ys, x1, g2, *extra_args)


def _final_kernel(x_ref, g_ref, o_ref):
    x = x_ref[...]
    ms = jnp.mean(x * x, axis=-1, keepdims=True)
    o_ref[...] = x * lax.rsqrt(ms + EPS) * g_ref[...]


def _final_norm(x, g, tm):
    n, d = x.shape
    return pl.pallas_call(
        _final_kernel,
        out_shape=jax.ShapeDtypeStruct((n, d), F32),
        grid=(n // tm,),
        in_specs=[pl.BlockSpec((tm, d), lambda i: (i, 0)), pl.BlockSpec((1, d), lambda i: (0, 0))],
        out_specs=pl.BlockSpec((tm, d), lambda i: (i, 0)),
        compiler_params=_cparams(("arbitrary",)),
        name="final_norm",
    )(x, g)


def _step_kernel(z_ref, xbc_ref, dt_ref, cst_ref, h_ref,
                 cw_ref, cb_ref, dtb_ref, alog_ref, dskip_ref, ng_ref, *rest):
    yn_ref, cnew_ref, hnew_ref, col_scr = rest[-4:]
    x_new = xbc_ref[...]
    acc = cb_ref[...] + cw_ref[3:4, :] * x_new
    for k in range(SSD_CONV - 1):
        acc = acc + cw_ref[k:k + 1, :] * cst_ref[k:k + 1, :]
    cnew_ref[0:1, :] = cst_ref[1:2, :]
    cnew_ref[1:2, :] = cst_ref[2:3, :]
    cnew_ref[2:3, :] = x_new
    xc = _silu(acc)
    xs = xc[:, 0:SSD_INNER]

    e_heads = _head_expand_matrix(LANES, SSD_HEADS, SSD_HEAD_DIM)
    dt_raw8 = jnp.broadcast_to(dt_ref[...], (8, LANES))
    dt_full = _softplus(_mm01(dt_raw8, e_heads, 3)[0:1, :] + dtb_ref[...])
    a_full = -jnp.exp(alog_ref[...])
    dec_full = jnp.exp(dt_full * a_full)
    xdt = xs * dt_full
    col_scr[...] = jnp.zeros_like(col_scr)
    col_scr[0:1, :] = xdt
    col_scr[1:2, :] = dec_full
    cols = col_scr[...].T
    y_parts = []
    gw = SSD_HPG * SSD_HEAD_DIM
    for g in range(SSD_GROUPS):
        b_off = SSD_INNER + g * SSD_STATE
        c_off = SSD_INNER + SSD_GROUPS * SSD_STATE + g * SSD_STATE
        bm = xc[:, b_off:b_off + SSD_STATE]
        cm = xc[:, c_off:c_off + SSD_STATE]
        g0 = g * gw
        hn = h_ref[g0:g0 + gw, :] * cols[g0:g0 + gw, 1:2] + cols[g0:g0 + gw, 0:1] * bm
        hnew_ref[g0:g0 + gw, :] = hn
        t = (hn * cm).T
        y_parts.append(jnp.sum(t, axis=0, keepdims=True))
    y = jnp.concatenate(y_parts, axis=1) + dskip_ref[...] * xs
    y = y * _silu(z_ref[...])
    outs = []
    for g in range(SSD_GROUPS):
        yg = y[:, g * gw:(g + 1) * gw]
        ms = jnp.mean(yg * yg, axis=-1, keepdims=True)
        outs.append(yg * lax.rsqrt(ms + EPS) * ng_ref[:, g * gw:(g + 1) * gw])
    yn_ref[...] = jnp.concatenate(outs, axis=1)


def _step_sample(layer, z, xbc, dt_raw, state_conv, state_ssm, prev_ssm,
                 conv_w, conv_b, dtb_full, alog_full, dskip_full, ssd_norm_g):
    bsz = z.shape[0]
    row3 = lambda w: pl.BlockSpec((None, 1, w), lambda b: (b, 0, 0))
    const = lambda a: pl.BlockSpec(a.shape, lambda b: (0,) * a.ndim)
    alias_args = [] if prev_ssm is None else [prev_ssm]
    n_in = 11
    return pl.pallas_call(
        _step_kernel,
        out_shape=(
            jax.ShapeDtypeStruct((bsz, 1, SSD_INNER), F32),
            jax.ShapeDtypeStruct((bsz, SSD_CONV - 1, SSD_CONV_DIM), F32),
            jax.ShapeDtypeStruct(state_ssm.shape, F32),
        ),
        grid=(bsz,),
        in_specs=[
            row3(SSD_INNER), row3(SSD_CONV_DIM), row3(LANES),
            pl.BlockSpec((None, None, SSD_CONV - 1, SSD_CONV_DIM), lambda b: (layer, b, 0, 0)),
            pl.BlockSpec((None, None, SSD_INNER, SSD_STATE), lambda b: (layer, b, 0, 0)),
            const(conv_w), const(conv_b), const(dtb_full), const(alog_full), const(dskip_full), const(ssd_norm_g),
        ] + [pl.BlockSpec(memory_space=pl.ANY)] * len(alias_args),
        out_specs=(
            row3(SSD_INNER),
            pl.BlockSpec((None, SSD_CONV - 1, SSD_CONV_DIM), lambda b: (b, 0, 0)),
            pl.BlockSpec((None, None, SSD_INNER, SSD_STATE), lambda b: (layer, b, 0, 0)),
        ),
        input_output_aliases={n_in: 2} if alias_args else {},
        scratch_shapes=[pltpu.VMEM((LANES, SSD_INNER), F32)],
        compiler_params=_cparams(("arbitrary",)),
        name="step_sample",
    )(z.reshape(bsz, 1, -1), xbc.reshape(bsz, 1, -1), dt_raw.reshape(bsz, 1, -1), state_conv, state_ssm,
      conv_w, conv_b, dtb_full, alog_full, dskip_full, ssd_norm_g, *alias_args)


_HEAD_SPLIT = 2
_HROWS = ATT_OUT // _HEAD_SPLIT


def _cache_attn_kernel(*refs, n_alias):
    qkv_ref = refs[0]
    cache_refs = refs[1:7]
    out_refs = refs[7 + n_alias:13 + n_alias]
    o_ref = refs[13 + n_alias]
    b = pl.program_id(0)
    nh = _HROWS // ATT_HEAD_DIM
    scale = ATT_HEAD_DIM ** -0.5
    qkv = qkv_ref[...]
    lane_b = lax.broadcasted_iota(jnp.int32, qkv.shape, 2)
    cols = jnp.sum(jnp.where(lane_b == b, qkv, 0.0), axis=-1, keepdims=True)

    def per_head_rows(v):
        return jnp.concatenate([jnp.broadcast_to(v[h:h + 1, :], (ATT_HEAD_DIM, 1)) for h in range(nh)], axis=0)

    o_g, lse_g = [], []
    for gi, (_, dil) in enumerate(ATT_PATTERNS):
        k_ref, v_ref = cache_refs[2 * gi], cache_refs[2 * gi + 1]
        ko_ref, vo_ref = out_refs[2 * gi], out_refs[2 * gi + 1]
        q = cols[gi]
        k_new = cols[ATT_GROUPS + gi]
        v_new = cols[2 * ATT_GROUPS + gi]
        kk = k_ref[...]
        vv = v_ref[...]
        length = kk.shape[1]
        lane = lax.broadcasted_iota(jnp.int32, (nh, length), 1)
        s = jnp.sum((kk * q).reshape(nh, ATT_HEAD_DIM, length), axis=1) * scale
        s = jnp.where((lane & (dil - 1)) == 0, s, -jnp.inf)
        s_new = jnp.sum((k_new * q).reshape(nh, ATT_HEAD_DIM, 1), axis=1) * scale
        m = jnp.maximum(jnp.max(s, axis=-1, keepdims=True), s_new)
        e = jnp.exp(s - m)
        e_new = jnp.exp(s_new - m)
        den = jnp.sum(e, axis=-1, keepdims=True) + e_new
        acc = jnp.sum(vv.reshape(nh, ATT_HEAD_DIM, length) * e[:, None, :], axis=-1, keepdims=True)
        acc = acc.reshape(_HROWS, 1) + per_head_rows(e_new) * v_new
        o_g.append(acc / per_head_rows(den))
        lse_g.append(per_head_rows(m + jnp.log(den)))
        lane_full = lax.broadcasted_iota(jnp.int32, kk.shape, 1)
        last = lane_full == length - 1
        ko_ref[...] = jnp.where(last, k_new, pltpu.roll(kk, length - 1, axis=1))
        vo_ref[...] = jnp.where(last, v_new, pltpu.roll(vv, length - 1, axis=1))
    mx = jnp.maximum(jnp.maximum(lse_g[0], lse_g[1]), lse_g[2])
    w = [jnp.exp(l - mx) for l in lse_g]
    tot = w[0] + w[1] + w[2]
    o_ref[...] = (w[0] / tot) * o_g[0] + (w[1] / tot) * o_g[1] + (w[2] / tot) * o_g[2]


def _cache_attn(layer, qkv_t, cache_views, prev_outs):
    depth, bsz, _, _ = cache_views[0].shape
    n_alias = 0 if prev_outs is None else len(prev_outs)
    q4 = qkv_t.reshape(3 * ATT_GROUPS, _HEAD_SPLIT, _HROWS, bsz)
    blk = lambda c: pl.BlockSpec((None, None, _HROWS, c.shape[3]), lambda b, hh: (layer, b, hh, 0))
    any_spec = pl.BlockSpec(memory_space=pl.ANY)
    args = [q4] + list(cache_views) + ([] if prev_outs is None else list(prev_outs))
    res = pl.pallas_call(
        functools.partial(_cache_attn_kernel, n_alias=n_alias),
        out_shape=tuple(jax.ShapeDtypeStruct(c.shape, c.dtype) for c in cache_views)
        + (jax.ShapeDtypeStruct((bsz, _HEAD_SPLIT, _HROWS, 1), F32),),
        grid=(bsz, _HEAD_SPLIT),
        in_specs=[pl.BlockSpec((3 * ATT_GROUPS, None, _HROWS, bsz), lambda b, hh: (0, hh, 0, 0))]
        + [blk(c) for c in cache_views] + [any_spec] * n_alias,
        out_specs=tuple(blk(c) for c in cache_views)
        + (pl.BlockSpec((None, None, _HROWS, 1), lambda b, hh: (b, hh, 0, 0)),),
        input_output_aliases={7 + i: i for i in range(n_alias)},
        compiler_params=_cparams(("arbitrary", "arbitrary")),
        name="cache_attn",
    )(*args)
    return list(res[:6]), res[6].reshape(bsz, ATT_OUT)


def _pad_lanes(v, width=LANES):
    return jnp.pad(v, [(0, 0)] * (v.ndim - 1) + [(0, width - v.shape[-1])])


def kernel(x_prompt, x_sample, c_prompt, c_sample, state_conv, state_ssm, cache_k_win128, cache_v_win128, cache_k_win512, cache_v_win512, cache_k_win2048, cache_v_win2048, norm1_g, w_ada, b_ada, w_in, conv_w, conv_b, dt_bias, a_log, d_skip, ssd_norm_g, w_ssd_proj, w_attn_proj, w_out, norm2_g, w_router_group, b_router_group, w_router_expert, b_router_expert, w_exp_gate, w_exp_up, w_exp_down, final_norm_g):
    depth = w_in.shape[0]
    bp, seq, d = x_prompt.shape
    bs = x_sample.shape[0]
    assert x_sample.shape[1] == 1 and d == D_MODEL and w_in.shape[2] == IN_WIDTH
    assert seq % (ATT_PATTERNS[-1][1] * ATT_BLOCK) == 0
    n_p = bp * seq
    caches = ((cache_k_win128, cache_v_win128), (cache_k_win512, cache_v_win512),
              (cache_k_win2048, cache_v_win2048))

    rows = bp + bs
    rows_pad = -(-rows // 8) * 8
    c_all = jnp.pad(jnp.concatenate([c_prompt, c_sample], axis=0), ((0, rows_pad - rows), (0, 0)))
    mods = _modulation(c_all, w_ada, b_ada)

    xp = x_prompt.reshape(n_p, d)
    xs = x_sample.reshape(bs, d)
    outs = {k: [] for k in ("conv_p", "conv_s", "ssm_p", "ssm_s")}
    kv_p = [[[], []] for _ in ATT_PATTERNS]
    tiles = seq // TOK_TILE
    assert bs <= LANES and all(c.shape[2] == win for (win, _), pair in zip(ATT_PATTERNS, caches) for c in pair)
    cache_views = [jnp.transpose(c, (0, 1, 3, 4, 2)).reshape(depth, bs, ATT_OUT, c.shape[2])
                   for pair in caches for c in pair]
    shifted = None
    ssm_s_all = None
    w_in_t = jnp.swapaxes(w_in, 1, 2)
    w_chunks = _prep_w_in(w_in_t)

    for l in range(depth):
        mod_p = mods[l, :bp]
        mod_s = mods[l, bp:bp + bs]
        wd = _pad_lanes(w_in_t[l, OFF_DT:OFF_QKV].T).astype(BF16)
        g1n = norm1_g[l].reshape(1, d)
        g2n = norm2_g[l].reshape(1, d)
        cw = conv_w[l]
        cb = conv_b[l].reshape(1, -1)
        dskip_full = jnp.repeat(d_skip[l], SSD_HEAD_DIM).reshape(1, -1)
        ssd_g = ssd_norm_g[l].reshape(1, -1)
        w_router = _pad_lanes(jnp.concatenate([w_router_expert[l], w_router_group[l]], axis=1))
        b_router = _pad_lanes(jnp.concatenate([b_router_expert[l], b_router_group[l]], axis=0).reshape(1, -1))

        z, xbc, dt_raw, a0, a1, a2, gates = _in_proj_prompt(
            l, xp, mod_p[:, 0:2 * d].reshape(bp, 1, 2 * d), g1n, w_chunks, wd, bp, seq)
        yn, ssm_new = _ssd_prompt(xbc, z, dt_raw, cw, cb, _pad_lanes(dt_bias[l].reshape(1, -1)),
                                  _pad_lanes(a_log[l].reshape(1, -1)), dskip_full, ssd_g, bp, seq)
        outs["ssm_p"].append(ssm_new.reshape(bp, SSD_GROUPS, SSD_HPG, SSD_HEAD_DIM, SSD_STATE))
        outs["conv_p"].append(xbc.reshape(bp, seq, SSD_CONV_DIM)[:, seq - (SSD_CONV - 1):])
        a_groups = [a0.reshape(bp, tiles, 1, TOK_TILE, 3 * ATT_OUT), a1, a2]
        o_list, lse_list = [], []
        for gi, (win, dil) in enumerate(ATT_PATTERNS):
            o_g, lse_g = _attn_prompt(a_groups[gi], gi, bp, seq)
            o_list.append(o_g)
            lse_list.append(lse_g)
            keep = min(win, seq)
            rows = TOK_TILE // dil
            if keep >= TOK_TILE:
                nt = keep // TOK_TILE
                tail = a_groups[gi][:, tiles - nt:, :, :, ATT_OUT:3 * ATT_OUT]
                tail = jnp.transpose(tail, (0, 1, 3, 2, 4)).reshape(bp, keep, 2 * ATT_OUT)
            else:
                assert dil == 1
                tail = a_groups[gi][:, tiles - 1, 0, TOK_TILE - keep:, ATT_OUT:3 * ATT_OUT]
            tail = tail.astype(F32)
            kv_p[gi][0].append(tail[:, :, 0:ATT_OUT].reshape(bp, keep, ATT_HEADS, ATT_HEAD_DIM))
            kv_p[gi][1].append(tail[:, :, ATT_OUT:2 * ATT_OUT].reshape(bp, keep, ATT_HEADS, ATT_HEAD_DIM))
        mod_post = jnp.concatenate([mod_p[:, 2 * d:3 * d], mod_p[:, 3 * d:5 * d]], axis=1).reshape(bp, 1, 3 * d)
        x1, rows, slot, cnt = _post(xp, yn, (o_list, lse_list), gates, mod_post, g2n,
                                    w_ssd_proj[l].astype(BF16), w_attn_proj[l].astype(BF16),
                                    w_out[l].astype(BF16), w_router, b_router, passes=1, merge_attn=True,
                                    tm=TOK_TILE, rows_per_mod=seq, h2_dtype=F32)
        pos = slot[:, 0].astype(jnp.int32)
        blk, e_lo, e_hi, n_act, z_need, z_blk = _moe_plan(cnt[0, :MOE_CLASSES].astype(jnp.int32), n_p)
        xs_sorted = _dispatch(pos, z_need, z_blk, rows, n_p)
        ys_sorted = _moe_sparse(l, (blk, e_lo, e_hi, n_act), xs_sorted, w_router, b_router,
                                w_exp_gate, w_exp_up, w_exp_down, n_p)
        xp = _combine(pos, ys_sorted, x1, mod_p[:, 5 * d:6 * d].reshape(bp, 1, d), seq,
                      final_g=final_norm_g.reshape(1, d) if l == depth - 1 else None)

        u = _in_proj_sample(l, xs, mod_s[:, 0:2 * d], g1n, w_in_t)
        z_s = u[:, 0:OFF_XBC]
        xbc_s = u[:, OFF_XBC:OFF_DT]
        dt_s = _pad_lanes(u[:, OFF_DT:OFF_QKV])
        qkv_s = u[:, OFF_QKV:OFF_GATE]
        gates_s = u[:, OFF_GATE:IN_WIDTH]
        yn_s, conv_new, ssm_s_all = _step_sample(
            l, z_s, xbc_s, dt_s, state_conv, state_ssm.reshape(depth, bs, SSD_INNER, SSD_STATE), ssm_s_all,
            cw, cb, jnp.repeat(dt_bias[l], SSD_HEAD_DIM).reshape(1, -1),
            jnp.repeat(a_log[l], SSD_HEAD_DIM).reshape(1, -1), dskip_full, ssd_g)
        outs["conv_s"].append(conv_new)
        shifted, o_s = _cache_attn(l, qkv_s.T, cache_views, shifted)
        mod_post_s = jnp.concatenate([mod_s[:, 2 * d:3 * d], mod_s[:, 3 * d:5 * d]], axis=1)
        x1_s, h2_s, gate_s = _post(xs, yn_s.reshape(bs, SSD_INNER), o_s, gates_s, mod_post_s,
                                   g2n, w_ssd_proj[l], w_attn_proj[l], w_out[l], w_router, b_router,
                                   passes=3, merge_attn=False, tm=bs, rows_per_mod=1, h2_dtype=F32)
        xs = _moe(l, h2_s, gate_s, x1_s, mod_s[:, 5 * d:6 * d], w_exp_gate, w_exp_up, w_exp_down,
                  passes=3, tm=bs, rows_per_mod=1)

    fg = final_norm_g.reshape(1, d)
    y_prompt = xp.reshape(bp, seq, d)
    y_sample = _final_norm(xs, fg, bs).reshape(bs, 1, d)

    shifted = [jnp.transpose(s.reshape(depth, bs, ATT_HEADS, ATT_HEAD_DIM, s.shape[3]), (0, 1, 4, 2, 3))
               for s in shifted]

    st = jnp.stack
    res = [y_prompt, y_sample, st(outs["conv_p"]), st(outs["conv_s"]), st(outs["ssm_p"]),
           ssm_s_all.reshape(state_ssm.shape)]
    for gi in range(ATT_GROUPS):
        res += [st(kv_p[gi][0]), shifted[2 * gi], st(kv_p[gi][1]), shifted[2 * gi + 1]]
    return tuple(res)
```

```python
import functools
import math

import jax
import jax.numpy as jnp
from jax import lax
from jax.experimental import pallas as pl
from jax.experimental.pallas import tpu as pltpu

F32 = jnp.float32
BF16 = jnp.bfloat16

D_MODEL = 1024
SSD_INNER = 1024
SSD_HEAD_DIM = 64
SSD_HEADS = 16
SSD_GROUPS = 2
SSD_HPG = 8
SSD_STATE = 128
SSD_CONV = 4
SSD_CHUNK = 128
SSD_CONV_DIM = SSD_INNER + 2 * SSD_GROUPS * SSD_STATE
ATT_PATTERNS = ((128, 1), (512, 4), (2048, 16))
ATT_GROUPS = 3
ATT_HEADS = 8
ATT_HEAD_DIM = 64
ATT_BLOCK = 128
ATT_OUT = ATT_HEADS * ATT_HEAD_DIM
MOE_GROUPS = 4
MOE_PER_GROUP = 4
MOE_EXPERTS = 16
MOE_FF = 512
MOE_PAIRS = 6
MOE_CLASSES = MOE_GROUPS * MOE_PAIRS
MOE_TILE = 512
ROW_SUB = 8
IN_QKV = 3 * ATT_GROUPS * ATT_OUT
IN_GATE = 2 * D_MODEL
OFF_XBC = SSD_INNER
OFF_DT = OFF_XBC + SSD_CONV_DIM
OFF_QKV = OFF_DT + SSD_HEADS
OFF_GATE = OFF_QKV + IN_QKV
IN_WIDTH = OFF_GATE + IN_GATE
EPS = 1e-6

LANES = 128
VMEM_LIMIT = 56 * 1024 * 1024


def _cparams(sem, vmem=VMEM_LIMIT):
    return pltpu.CompilerParams(dimension_semantics=sem, vmem_limit_bytes=vmem)


def _split2(a):
    hi = a.astype(BF16)
    lo = (a - hi.astype(F32)).astype(BF16)
    return hi, lo


def _split3(a):
    hi = a.astype(BF16)
    r = a - hi.astype(F32)
    mid = r.astype(BF16)
    lo = (r - mid.astype(F32)).astype(BF16)
    return hi, mid, lo


def _dot(a, b):
    return jnp.dot(a, b, preferred_element_type=F32)


def _dot_nt(a, b):
    return lax.dot_general(a, b, (((1,), (1,)), ((), ())), preferred_element_type=F32)


def _mm(a, w, passes):
    if passes == 1:
        return _dot(a.astype(BF16), w.astype(BF16))
    a = a.astype(F32)
    w = w.astype(F32)
    a_hi, a_lo = _split2(a)
    w_hi, w_lo = _split2(w)
    return _dot(a_hi, w_hi) + (_dot(a_lo, w_hi) + _dot(a_hi, w_lo))


def _mm01(a, e01, terms):
    parts = _split3(a) if terms == 3 else _split2(a)
    out = _dot(parts[0], e01)
    for p in parts[1:]:
        out = out + _dot(p, e01)
    return out


def _sigmoid(x):
    return 1.0 / (1.0 + jnp.exp(-x))


def _silu(x):
    return x * _sigmoid(x)


def _softplus(x):
    return jnp.maximum(x, 0.0) + jnp.log(1.0 + jnp.exp(-jnp.abs(x)))


def _head_expand_matrix(rows, n_heads, width):
    r = lax.broadcasted_iota(jnp.int32, (rows, n_heads * width), 0)
    c = lax.broadcasted_iota(jnp.int32, (rows, n_heads * width), 1)
    return jnp.where((c // width) == r, 1.0, 0.0).astype(BF16)


def _rms_modulate(x, g, sh, sc):
    ms = jnp.mean(x * x, axis=-1, keepdims=True)
    y = x * lax.rsqrt(ms + EPS) * g
    return y * (1.0 + sc) + sh


def _mod_kernel(c_ref, w_ref, b_ref, o_ref):
    a = _silu(c_ref[...])
    o_ref[...] = _mm(a, w_ref[...], 3) + b_ref[...]


def _modulation(c_all, w_ada, b_ada):
    depth, d, n6 = w_ada.shape
    rows = c_all.shape[0]
    tn = 1024
    return pl.pallas_call(
        _mod_kernel,
        out_shape=jax.ShapeDtypeStruct((depth, rows, n6), F32),
        grid=(depth, n6 // tn),
        in_specs=[
            pl.BlockSpec((rows, d), lambda l, j: (0, 0)),
            pl.BlockSpec((None, d, tn), lambda l, j: (l, 0, j)),
            pl.BlockSpec((None, 1, tn), lambda l, j: (l, 0, j)),
        ],
        out_specs=pl.BlockSpec((None, rows, tn), lambda l, j: (l, 0, j)),
        compiler_params=_cparams(("arbitrary", "arbitrary")),
        name="adaln_mod",
    )(c_all, w_ada, b_ada.reshape(depth, 1, n6))


_IN_CHUNK = 512
TOK_TILE = 512

_W_CHUNK_ROWS = (tuple(range(0, OFF_DT, _IN_CHUNK))
                 + tuple(range(OFF_QKV, OFF_GATE, _IN_CHUNK)) + tuple(range(OFF_GATE, IN_WIDTH, _IN_CHUNK)))
_CH_Z = (0, 1)
_CH_XBC = (2, 3, 4)
_CH_QKV0 = 5
_CH_GATES = (14, 15, 16, 17)


def _w_prep_kernel(starts_ref, w_ref, o_ref):
    del starts_ref
    o_ref[...] = w_ref[0].T.astype(BF16)


def _prep_w_in(w_in_t):
    depth, _, d = w_in_t.shape
    n_ch = len(_W_CHUNK_ROWS)
    starts = jnp.asarray(_W_CHUNK_ROWS, jnp.int32)
    return pl.pallas_call(
        _w_prep_kernel,
        out_shape=jax.ShapeDtypeStruct((depth, n_ch, d, _IN_CHUNK), BF16),
        grid_spec=pltpu.PrefetchScalarGridSpec(
            num_scalar_prefetch=1,
            grid=(depth, n_ch),
            in_specs=[pl.BlockSpec((pl.Element(1), pl.Element(_IN_CHUNK), pl.Element(d)),
                                   lambda l, j, st: (l, pl.multiple_of(st[j], 16), 0))],
            out_specs=pl.BlockSpec((None, None, d, _IN_CHUNK), lambda l, j, st: (l, j, 0, 0)),
        ),
        compiler_params=_cparams(("arbitrary", "arbitrary")),
        name="w_in_prep",
    )(starts, w_in_t)


def _in_kernel(x_ref, mod_ref, g_ref, w_ref, wd_ref,
               z_ref, xbc_ref, dt_ref, a0_ref, a1_ref, a2_ref, gates_ref, h_scr):
    sh = mod_ref[:, 0:D_MODEL]
    sc = mod_ref[:, D_MODEL:2 * D_MODEL]
    hf = _rms_modulate(x_ref[...], g_ref[...], sh, sc)
    n_cb = D_MODEL // LANES
    for cbk in range(n_cb):
        h_scr[cbk] = hf[:, cbk * LANES:(cbk + 1) * LANES]
    h = hf.astype(BF16)
    qkv_chunks = lambda gi: tuple(_CH_QKV0 + which * ATT_GROUPS + gi for which in range(3))
    for chunks, o_ref in ((_CH_Z, z_ref), (_CH_XBC, xbc_ref), (qkv_chunks(0), a0_ref), (_CH_GATES, gates_ref)):
        for i, ch in enumerate(chunks):
            o_ref[:, i * _IN_CHUNK:(i + 1) * _IN_CHUNK] = _dot(h, w_ref[ch]).astype(o_ref.dtype)
    dt_ref[...] = _dot(h, wd_ref[...].astype(BF16))
    for gi, a_ref in ((1, a1_ref), (2, a2_ref)):
        d, r_len, _ = a_ref.shape
        hp = jnp.concatenate(
            [jnp.concatenate([h_scr[cbk, pl.ds(r, r_len, stride=d), :] for cbk in range(n_cb)], axis=1)
             for r in range(d)], axis=0).astype(BF16)
        for i, ch in enumerate(qkv_chunks(gi)):
            res = _dot(hp, w_ref[ch]).astype(BF16)
            a_ref[:, :, i * _IN_CHUNK:(i + 1) * _IN_CHUNK] = res.reshape(d, r_len, _IN_CHUNK)


def _in_proj_prompt(layer, x, mod, norm_g, w_chunks, wd, bsz, seq):
    n, d = x.shape
    tm = TOK_TILE
    tiles_per_seq = seq // tm
    qw = 3 * ATT_OUT

    def const(shape):
        return pl.BlockSpec(shape, lambda i: (0, 0), pipeline_mode=pl.Buffered(1))

    a_shapes, a_specs = [], []
    for _, dil in ATT_PATTERNS[1:]:
        a_shapes.append(jax.ShapeDtypeStruct((bsz, tiles_per_seq, dil, tm // dil, qw), BF16))
        a_specs.append(pl.BlockSpec((None, None, dil, tm // dil, qw),
                                    lambda i: (i // tiles_per_seq, i % tiles_per_seq, 0, 0, 0)))
    return pl.pallas_call(
        _in_kernel,
        out_shape=(
            jax.ShapeDtypeStruct((n, SSD_INNER), BF16),
            jax.ShapeDtypeStruct((n, SSD_CONV_DIM), F32),
            jax.ShapeDtypeStruct((n, LANES), F32),
            jax.ShapeDtypeStruct((n, qw), BF16),
            a_shapes[0], a_shapes[1],
            jax.ShapeDtypeStruct((n, IN_GATE), BF16),
        ),
        grid=(n // tm,),
        in_specs=[
            pl.BlockSpec((tm, d), lambda i: (i, 0)),
            pl.BlockSpec((None, 1, 2 * d), lambda i: (i // tiles_per_seq, 0, 0)),
            const((1, d)),
            pl.BlockSpec((None,) + w_chunks.shape[1:], lambda i: (layer, 0, 0, 0), pipeline_mode=pl.Buffered(1)),
            const(wd.shape),
        ],
        out_specs=(
            pl.BlockSpec((tm, SSD_INNER), lambda i: (i, 0)),
            pl.BlockSpec((tm, SSD_CONV_DIM), lambda i: (i, 0)),
            pl.BlockSpec((tm, LANES), lambda i: (i, 0)),
            pl.BlockSpec((tm, qw), lambda i: (i, 0)),
            a_specs[0], a_specs[1],
            pl.BlockSpec((tm, IN_GATE), lambda i: (i, 0)),
        ),
        scratch_shapes=[pltpu.VMEM((d // LANES, tm, LANES), F32)],
        compiler_params=_cparams(("arbitrary",)),
        name="in_proj_prompt",
    )(x, mod, norm_g, w_chunks, wd)


def _in_small_kernel(x_ref, mod_ref, g_ref, w_ref, o_ref):
    sh = mod_ref[:, 0:D_MODEL]
    sc = mod_ref[:, D_MODEL:2 * D_MODEL]
    h = _rms_modulate(x_ref[...], g_ref[...], sh, sc)
    h_hi, h_lo = _split2(h)
    w_hi, w_lo = _split2(w_ref[...])
    o_ref[...] = _dot_nt(h_hi, w_hi) + (_dot_nt(h_lo, w_hi) + _dot_nt(h_hi, w_lo))


def _in_proj_sample(layer, x, mod, norm_g, w_in_t, tn=1024):
    m, d = x.shape
    width = w_in_t.shape[1]
    return pl.pallas_call(
        _in_small_kernel,
        out_shape=jax.ShapeDtypeStruct((m, width), F32),
        grid=(pl.cdiv(width, tn),),
        in_specs=[
            pl.BlockSpec((m, d), lambda j: (0, 0)),
            pl.BlockSpec((m, 2 * d), lambda j: (0, 0)),
            pl.BlockSpec((1, d), lambda j: (0, 0)),
            pl.BlockSpec((None, tn, d), lambda j: (layer, j, 0)),
        ],
        out_specs=pl.BlockSpec((m, tn), lambda j: (0, j)),
        compiler_params=_cparams(("arbitrary",)),
        name="in_proj_sample",
    )(x, mod, norm_g, w_in_t)


_SSD_SUB = 2


def _ssd_kernel(xbc_ref, z_ref, dt_ref, cw_ref, cb_ref, dtb_ref, alog_ref, dskip_ref, ng_ref,
                yn_ref, hout_ref, h_scr, xp_scr):
    q = SSD_CHUNK
    c = pl.program_id(1)

    @pl.when(c == 0)
    def _():
        h_scr[...] = jnp.zeros_like(h_scr)
        xp_scr[0:8, :] = jnp.zeros((8, SSD_CONV_DIM), F32)

    for sub in range(_SSD_SUB):
        rows = slice(sub * q, (sub + 1) * q)
        _ssd_chunk(xbc_ref.at[rows], z_ref.at[rows], dt_ref.at[rows], cw_ref, cb_ref, dtb_ref, alog_ref,
                   dskip_ref, ng_ref, yn_ref.at[rows], h_scr, xp_scr)

    @pl.when(c == pl.num_programs(1) - 1)
    def _():
        hout_ref[...] = h_scr[...]


def _ssd_chunk(xbc_ref, z_ref, dt_ref, cw_ref, cb_ref, dtb_ref, alog_ref, dskip_ref, ng_ref,
               yn_ref, h_scr, xp_scr):
    q = SSD_CHUNK
    xp_scr[8:8 + q, :] = xbc_ref[...]
    acc = cb_ref[...] + cw_ref[3:4, :] * xp_scr[8:8 + q, :]
    for k in range(SSD_CONV - 1):
        acc = acc + cw_ref[k:k + 1, :] * xp_scr[5 + k:5 + k + q, :]
    xp_scr[0:8, :] = xp_scr[q:q + 8, :]
    xc = _silu(acc)
    xs = xc[:, 0:SSD_INNER]
    xs_bf = xs.astype(BF16)

    lane_q = lax.broadcasted_iota(jnp.int32, (q, LANES), 1)
    row_q = lax.broadcasted_iota(jnp.int32, (q, LANES), 0)
    causal = row_q >= lane_q
    tri = jnp.where(causal, 1.0, 0.0).astype(BF16)
    tri_t = jnp.where(lane_q >= row_q, 1.0, 0.0).astype(BF16)
    e_heads = _head_expand_matrix(LANES, SSD_HEADS, SSD_HEAD_DIM)

    dt = _softplus(dt_ref[...] + dtb_ref[...])
    dt = jnp.where(lane_q < SSD_HEADS, dt, 0.0)
    a = -jnp.exp(alog_ref[...])
    d_a = dt * a
    cum = _mm01_left(tri, d_a)
    d_a_t = d_a.T
    dt_t = dt.T
    cum_t = _mm01(d_a_t, tri_t, 3)
    cum_last = cum[q - 1:q, :]
    exp_cum = jnp.exp(cum)
    dec_end = jnp.exp(cum_last - cum)
    stack = jnp.concatenate([exp_cum, dec_end * dt], axis=0)
    full = _mm01(stack, e_heads, 2)
    exp_cum_full = full[0:q]
    w_full = full[q:2 * q]
    chunk_dec_t = jnp.exp(cum_t[:, q - 1:q])

    lane_half = lax.broadcasted_iota(jnp.int32, (q, LANES), 1) < SSD_HEAD_DIM
    y_parts = []
    for g in range(SSD_GROUPS):
        b_off = SSD_INNER + g * SSD_STATE
        c_off = SSD_INNER + SSD_GROUPS * SSD_STATE + g * SSD_STATE
        bm = xc[:, b_off:b_off + SSD_STATE].astype(BF16)
        cm = xc[:, c_off:c_off + SSD_STATE].astype(BF16)
        cbm = _dot_nt(cm, bm)
        gw = SSD_HPG * SSD_HEAD_DIM
        g0 = g * gw
        yd = []
        for pair in range(SSD_HPG // 2):
            x_pair = xs_bf[:, g0 + pair * LANES:g0 + (pair + 1) * LANES]
            halves = []
            for hh in range(2):
                h = g * SSD_HPG + pair * 2 + hh
                seg = cum[:, h:h + 1] - cum_t[h:h + 1, :]
                dec = jnp.exp(jnp.where(causal, seg, -jnp.inf))
                m_h = (cbm * dec * dt_t[h:h + 1, :]).astype(BF16)
                halves.append(_dot(m_h, x_pair))
            yd.append(jnp.where(lane_half, halves[0], halves[1]))
        y_diag = jnp.concatenate(yd, axis=1)
        h_g = h_scr[g0:g0 + gw, :]
        y_off = _dot_nt(cm, h_g.astype(BF16)) * exp_cum_full[:, g0:g0 + gw]
        y_parts.append(y_diag + y_off)
        xw = (xs[:, g0:g0 + gw] * w_full[:, g0:g0 + gw])
        st = _dot(xw.T.astype(BF16), bm)
        for e in range(SSD_HPG):
            h = g * SSD_HPG + e
            r0 = g0 + e * SSD_HEAD_DIM
            h_scr[r0:r0 + SSD_HEAD_DIM, :] = (h_scr[r0:r0 + SSD_HEAD_DIM, :] * chunk_dec_t[h:h + 1, :]
                                              + st[e * SSD_HEAD_DIM:(e + 1) * SSD_HEAD_DIM, :])

    y = jnp.concatenate(y_parts, axis=1) + dskip_ref[...] * xs
    y = y * _silu(z_ref[...].astype(F32))
    outs = []
    for g in range(SSD_GROUPS):
        gw = SSD_HPG * SSD_HEAD_DIM
        yg = y[:, g * gw:(g + 1) * gw]
        ms = jnp.mean(yg * yg, axis=-1, keepdims=True)
        outs.append(yg * lax.rsqrt(ms + EPS) * ng_ref[:, g * gw:(g + 1) * gw])
    yn_ref[...] = jnp.concatenate(outs, axis=1).astype(yn_ref.dtype)


def _mm01_left(tri01, a):
    hi, mid, lo = _split3(a)
    return _dot(tri01, hi) + (_dot(tri01, mid) + _dot(tri01, lo))


def _ssd_prompt(xbc, z, dt_raw, conv_w, conv_b, dt_bias, a_log, d_skip_full, ssd_norm_g, bsz, seq):
    q = SSD_CHUNK * _SSD_SUB
    nc = seq // q
    row = lambda b, c: (b * nc + c, 0)
    const = lambda b, c: (0, 0)
    return pl.pallas_call(
        _ssd_kernel,
        out_shape=(
            jax.ShapeDtypeStruct((bsz * seq, SSD_INNER), BF16),
            jax.ShapeDtypeStruct((bsz, SSD_INNER, SSD_STATE), F32),
        ),
        grid=(bsz, nc),
        in_specs=[
            pl.BlockSpec((q, SSD_CONV_DIM), row),
            pl.BlockSpec((q, SSD_INNER), row),
            pl.BlockSpec((q, LANES), row),
            pl.BlockSpec((SSD_CONV, SSD_CONV_DIM), const),
            pl.BlockSpec((1, SSD_CONV_DIM), const),
            pl.BlockSpec((1, LANES), const),
            pl.BlockSpec((1, LANES), const),
            pl.BlockSpec((1, SSD_INNER), const),
            pl.BlockSpec((1, SSD_INNER), const),
        ],
        out_specs=(
            pl.BlockSpec((q, SSD_INNER), row),
            pl.BlockSpec((None, SSD_INNER, SSD_STATE), lambda b, c: (b, 0, 0)),
        ),
        scratch_shapes=[
            pltpu.VMEM((SSD_INNER, SSD_STATE), F32),
            pltpu.VMEM((SSD_CHUNK + 8, SSD_CONV_DIM), F32),
        ],
        compiler_params=_cparams(("arbitrary", "arbitrary")),
        name="ssd_prompt",
    )(xbc, z, dt_raw, conv_w, conv_b, dt_bias, a_log, d_skip_full, ssd_norm_g)


def _attn_kernel(q_ref, kc_ref, kp_ref, vc_ref, vp_ref, o_out_ref, lse_out_ref,
                 q_ref_s, kwin, vwin, o_ref, lse_ref, *, tq, band):
    blk = ATT_BLOCK
    j = pl.program_id(2)
    q_ref_s[...] = (q_ref[...] * (ATT_HEAD_DIM ** -0.5)).astype(BF16).reshape(tq, ATT_OUT)
    kwin[0:blk, :] = kp_ref[...].reshape(blk, ATT_OUT)
    kwin[blk:blk + tq, :] = kc_ref[...].reshape(tq, ATT_OUT)
    vwin[0:blk, :] = vp_ref[...].reshape(blk, ATT_OUT)
    vwin[blk:blk + tq, :] = vc_ref[...].reshape(tq, ATT_OUT)
    q_ref = q_ref_s

    qi = lax.broadcasted_iota(jnp.int32, (blk, 2 * blk), 0)
    ki = lax.broadcasted_iota(jnp.int32, (blk, 2 * blk), 1)
    dist = qi + blk - ki
    in_band = (dist >= 0) & (dist <= band)
    lane = lax.broadcasted_iota(jnp.int32, (blk, LANES), 1)
    lane_half = lane < ATT_HEAD_DIM
    scale = ATT_HEAD_DIM ** -0.5
    zero_bf = jnp.zeros((blk, LANES), BF16)

    for i in range(tq // blk):
        if i == 0:
            valid = in_band & ((ki >= blk) | (j > 0))
        else:
            valid = in_band
        lse_tile = jnp.zeros((blk, LANES), F32)
        for hp in range(ATT_HEADS // 2):
            c0 = hp * LANES
            q_pair = q_ref[i * blk:(i + 1) * blk, c0:c0 + LANES]
            k_pair = kwin[i * blk:(i + 2) * blk, c0:c0 + LANES]
            v_pair = vwin[i * blk:(i + 2) * blk, c0:c0 + LANES]
            halves = []
            for hh in range(2):
                q_m = jnp.where(lane_half if hh == 0 else jnp.logical_not(lane_half), q_pair, zero_bf)
                s = _dot_nt(q_m, k_pair)
                s = jnp.where(valid, s, -jnp.inf)
                m = jnp.max(s, axis=-1, keepdims=True)
                e = jnp.exp(s - m)
                den = jnp.sum(e, axis=-1, keepdims=True)
                pv = _dot(e.astype(BF16), v_pair)
                halves.append(pv / den)
                lse = m + jnp.log(den)
                head = hp * 2 + hh
                lse_tile = jnp.where((lane == head) | (lane == ATT_HEADS + head), lse, lse_tile)
            o_ref[i * blk:(i + 1) * blk, c0:c0 + LANES] = jnp.where(lane_half, halves[0], halves[1]).astype(o_ref.dtype)
        lse_ref[i * blk:(i + 1) * blk, :] = lse_tile
    o_out_ref[...] = o_ref[...].reshape(o_out_ref.shape)
    lse_out_ref[...] = lse_ref[...].reshape(lse_out_ref.shape)


def _attn_prompt(a_g, gi, bsz, seq):
    win, dil = ATT_PATTERNS[gi]
    band = win // dil
    blk = ATT_BLOCK
    rows = TOK_TILE // dil
    tiles = seq // TOK_TILE
    length = seq // dil
    tq = min(512, length)
    nj = length // tq
    tq_tiles = tq // rows
    cur = lambda which: pl.BlockSpec((None, tq_tiles, None, rows, ATT_OUT),
                                     lambda b, r, j: (b, j, r, 0, which))
    if rows >= blk:
        prev = lambda which: pl.BlockSpec(
            (None, None, None, blk, ATT_OUT),
            lambda b, r, j: (b, jnp.maximum(j * tq_tiles - 1, 0), r, rows // blk - 1, which))
    else:
        prev = lambda which: pl.BlockSpec(
            (None, blk // rows, None, rows, ATT_OUT),
            lambda b, r, j: (b, jnp.maximum(j * (tq // blk) - 1, 0), r, 0, which))
    return pl.pallas_call(
        functools.partial(_attn_kernel, tq=tq, band=band),
        out_shape=(
            jax.ShapeDtypeStruct((bsz, tiles, dil, rows, ATT_OUT), BF16),
            jax.ShapeDtypeStruct((bsz, tiles, dil, rows, LANES), F32),
        ),
        grid=(bsz, dil, nj),
        in_specs=[cur(0), cur(1), prev(1), cur(2), prev(2)],
        out_specs=(
            pl.BlockSpec((None, tq_tiles, None, rows, ATT_OUT), lambda b, r, j: (b, j, r, 0, 0)),
            pl.BlockSpec((None, tq_tiles, None, rows, LANES), lambda b, r, j: (b, j, r, 0, 0)),
        ),
        scratch_shapes=[
            pltpu.VMEM((tq, ATT_OUT), BF16),
            pltpu.VMEM((blk + tq, ATT_OUT), BF16),
            pltpu.VMEM((blk + tq, ATT_OUT), BF16),
            pltpu.VMEM((tq, ATT_OUT), BF16),
            pltpu.VMEM((tq, LANES), F32),
        ],
        compiler_params=_cparams(("arbitrary", "arbitrary", "arbitrary")),
        name=f"attn_prompt_w{win}",
    )(a_g, a_g, a_g, a_g, a_g)


def _router_gates(logits):
    shape = logits.shape
    lane = lax.broadcasted_iota(jnp.int32, shape, 1)
    big = jnp.int32(1 << 20)
    neg = -jnp.inf
    is_grp = (lane >= MOE_EXPERTS) & (lane < MOE_EXPERTS + MOE_GROUPS)
    lg = jnp.where(is_grp, logits, neg)
    gm = jnp.max(lg, axis=-1, keepdims=True)
    g_lane = jnp.min(jnp.where(lg == gm, lane, big), axis=-1, keepdims=True)
    g_sum = jnp.sum(jnp.exp(lg - gm), axis=-1, keepdims=True)
    g_w = 1.0 / g_sum
    lo = (g_lane - MOE_EXPERTS) * MOE_PER_GROUP
    in_grp = (lane >= lo) & (lane < lo + MOE_PER_GROUP)
    le = jnp.where(in_grp, logits, neg)
    m1 = jnp.max(le, axis=-1, keepdims=True)
    i1 = jnp.min(jnp.where(le == m1, lane, big), axis=-1, keepdims=True)
    le2 = jnp.where(lane == i1, neg, le)
    m2 = jnp.max(le2, axis=-1, keepdims=True)
    i2 = jnp.min(jnp.where(le2 == m2, lane, big), axis=-1, keepdims=True)
    t = jnp.exp(m2 - m1)
    w1 = 1.0 / (1.0 + t)
    w2 = t / (1.0 + t)
    gate = jnp.where(lane == i1, g_w * w1, jnp.where(lane == i2, g_w * w2, 0.0))
    e_lo = jnp.minimum(i1, i2)
    e_hi = jnp.maximum(i1, i2)
    lo_l = e_lo & (MOE_PER_GROUP - 1)
    hi_l = e_hi & (MOE_PER_GROUP - 1)
    cls = (e_lo >> 2) * MOE_PAIRS + ((lo_l * (7 - lo_l)) >> 1) + (hi_l - lo_l - 1)
    return gate, cls


def _post_kernel(*refs, passes, merge_attn):
    if merge_attn:
        (x_ref, yn_ref, o0_ref, o1_ref, o2_ref, l0_ref, l1_ref, l2_ref, gates_ref, mod_ref, n2_ref,
         wssd_ref, wattn_ref, wout_ref, wr_ref, br_ref, x1_ref, h2_ref, pos_ref, cnt_ref,
         o1_scr, o2_scr, l1_scr, l2_scr) = refs
        for src_ref, dst_ref in ((o1_ref, o1_scr), (o2_ref, o2_scr), (l1_ref, l1_scr), (l2_ref, l2_scr)):
            dil, r_len, width = src_ref.shape
            for r in range(dil):
                blk_r = src_ref[r].astype(F32)
                for cbk in range(width // LANES):
                    dst_ref[cbk, pl.ds(r, r_len, stride=dil), :] = blk_r[:, cbk * LANES:(cbk + 1) * LANES]
        o_nat = [o0_ref[...].astype(F32)] + [
            jnp.concatenate([scr[cbk] for cbk in range(ATT_OUT // LANES)], axis=1) for scr in (o1_scr, o2_scr)]
        l0, l1, l2 = l0_ref[...], l1_scr[0], l2_scr[0]
        mx = jnp.maximum(jnp.maximum(l0, l1), l2)
        e0, e1, e2 = jnp.exp(l0 - mx), jnp.exp(l1 - mx), jnp.exp(l2 - mx)
        inv = 1.0 / (e0 + e1 + e2)
        lane = lax.broadcasted_iota(jnp.int32, l0.shape, 1)
        e8 = _head_expand_matrix(LANES, ATT_HEADS, ATT_HEAD_DIM)
        r = lax.broadcasted_iota(jnp.int32, e8.shape, 0)
        c = lax.broadcasted_iota(jnp.int32, e8.shape, 1)
        e8 = jnp.where((c // ATT_HEAD_DIM) == (r - ATT_HEADS), 1.0, e8.astype(F32)).astype(BF16)
        o = None
        for e_g, o_g in zip((e0, e1, e2), o_nat):
            w = e_g * inv
            hi, lo = _split2(w)
            w_exp = _dot(jnp.where(lane < ATT_HEADS, hi, lo), e8)
            term = w_exp * o_g
            o = term if o is None else o + term
    else:
        (x_ref, yn_ref, o_ref, gates_ref, mod_ref, n2_ref,
         wssd_ref, wattn_ref, wout_ref, wr_ref, br_ref, x1_ref, h2_ref, gate_ref) = refs
        o = o_ref[...]
    d = D_MODEL
    g1 = mod_ref[:, 0:d]
    sh2 = mod_ref[:, d:2 * d]
    sc2 = mod_ref[:, 2 * d:3 * d]
    ssd_branch = _mm(yn_ref[...], wssd_ref[...], passes)
    attn_branch = _mm(o, wattn_ref[...], passes)
    ga = gates_ref[:, 0:d].astype(F32)
    gb = gates_ref[:, d:2 * d].astype(F32)
    mixed = _sigmoid(ga) * ssd_branch + _sigmoid(gb) * attn_branch
    x1 = x_ref[...] + g1 * _mm(mixed, wout_ref[...], passes)
    x1_ref[...] = x1
    h2 = _rms_modulate(x1, n2_ref[...], sh2, sc2)
    logits = _mm(h2, wr_ref[...], 3) + br_ref[...]
    gate, cls = _router_gates(logits)
    if not merge_attn:
        h2_ref[...] = h2
        gate_ref[...] = gate
        return
    tm = h2.shape[0]
    for cbk in range(ROW_SUB):
        h2_ref[pl.ds(cbk, tm, stride=ROW_SUB), :] = h2[:, cbk * LANES:(cbk + 1) * LANES]
    step = pl.program_id(0)

    @pl.when(step == 0)
    def _():
        cnt_ref[...] = jnp.zeros_like(cnt_ref)

    lane = lax.broadcasted_iota(jnp.int32, (tm, LANES), 1)
    onehot = lane == cls
    ri = lax.broadcasted_iota(jnp.int32, (tm, tm), 0)
    ci = lax.broadcasted_iota(jnp.int32, (tm, tm), 1)
    before = jnp.where(ci < ri, 1.0, 0.0).astype(BF16)
    seen = _dot(before, jnp.where(onehot, 1.0, 0.0).astype(BF16)) + cnt_ref[...]
    rank = jnp.sum(jnp.where(onehot, seen, 0.0), axis=-1, keepdims=True)
    n_tokens = tm * pl.num_programs(0)
    pos_ref[...] = jnp.broadcast_to(cls.astype(F32) * n_tokens + rank, (tm, LANES))
    cnt_ref[...] = cnt_ref[...] + jnp.sum(jnp.where(onehot, 1.0, 0.0), axis=0, keepdims=True)


def _post(x, yn, attn_in, gates, mod, norm2_g, wssd, wattn, wout, wr, br, *, passes, merge_attn,
          tm, rows_per_mod, h2_dtype):
    n, d = x.shape
    per_row_mod = rows_per_mod == 1
    if per_row_mod:
        mod_spec = pl.BlockSpec((tm, 3 * d), lambda i: (i, 0))
    else:
        tiles = rows_per_mod // tm
        mod_spec = pl.BlockSpec((None, 1, 3 * d), lambda i: (i // tiles, 0, 0))
    row = lambda w: pl.BlockSpec((tm, w), lambda i: (i, 0))
    const = lambda a: pl.BlockSpec(a.shape, lambda i: (0, 0), pipeline_mode=pl.Buffered(1))
    scratch = []
    if merge_attn:
        assert tm == TOK_TILE
        o_list, lse_list = attn_in
        attn_args = list(o_list) + list(lse_list)
        tiles = rows_per_mod // tm

        def tile_spec(a):
            _, _, dil, r_len, w = a.shape
            if dil == 1:
                return pl.BlockSpec((None, None, None, r_len, w), lambda i: (i // tiles, i % tiles, 0, 0, 0))
            return pl.BlockSpec((None, None, dil, r_len, w), lambda i: (i // tiles, i % tiles, 0, 0, 0))

        attn_specs = [tile_spec(a) for a in attn_args]
        scratch = [pltpu.VMEM((ATT_OUT // LANES, tm, LANES), F32), pltpu.VMEM((ATT_OUT // LANES, tm, LANES), F32),
                   pltpu.VMEM((1, tm, LANES), F32), pltpu.VMEM((1, tm, LANES), F32)]
    else:
        attn_args = [attn_in]
        attn_specs = [row(ATT_OUT)]
    if merge_attn:
        out_shape = (jax.ShapeDtypeStruct((n, d), F32), jax.ShapeDtypeStruct((n * ROW_SUB, LANES), F32),
                     jax.ShapeDtypeStruct((n, LANES), F32), jax.ShapeDtypeStruct((1, LANES), F32))
        out_specs = (row(d), pl.BlockSpec((tm * ROW_SUB, LANES), lambda i: (i, 0)), row(LANES),
                     pl.BlockSpec((1, LANES), lambda i: (0, 0)))
    else:
        out_shape = (jax.ShapeDtypeStruct((n, d), F32), jax.ShapeDtypeStruct((n, d), h2_dtype),
                     jax.ShapeDtypeStruct((n, LANES), F32))
        out_specs = (row(d), row(d), row(LANES))
    return pl.pallas_call(
        functools.partial(_post_kernel, passes=passes, merge_attn=merge_attn),
        out_shape=out_shape,
        grid=(n // tm,),
        in_specs=[row(d), row(SSD_INNER)] + attn_specs + [row(IN_GATE), mod_spec, const(norm2_g),
                                                          const(wssd), const(wattn), const(wout),
                                                          const(wr), const(br)],
        out_specs=out_specs,
        scratch_shapes=scratch,
        compiler_params=_cparams(("arbitrary",)),
        name="post_merge" if merge_attn else "post_sample",
    )(x, yn, *attn_args, gates, mod, norm2_g, wssd, wattn, wout, wr, br)


def _moe_kernel(h2_ref, gate_ref, x1_ref, g2_ref, wg_ref, wu_ref, wd_ref, x2_ref, acc_ref, *, passes):
    e = pl.program_id(1)

    @pl.when(e == 0)
    def _():
        acc_ref[...] = jnp.zeros_like(acc_ref)

    h2 = h2_ref[...]
    hg = _mm(h2, wg_ref[...], passes)
    hu = _mm(h2, wu_ref[...], passes)
    gate = gate_ref[...]
    lane = lax.broadcasted_iota(jnp.int32, gate.shape, 1)
    gcol = jnp.sum(jnp.where(lane == e, gate, 0.0), axis=-1, keepdims=True)
    act = _silu(hg) * hu * gcol
    acc_ref[...] += _mm(act, wd_ref[...], passes)

    @pl.when(e == pl.num_programs(1) - 1)
    def _():
        x2_ref[...] = x1_ref[...] + g2_ref[...] * acc_ref[...]


def _moe(layer, h2, gate, x1, g2, wg, wu, wd, *, passes, tm, rows_per_mod):
    n, d = x1.shape
    if rows_per_mod == 1:
        g2_spec = pl.BlockSpec((tm, d), lambda i, e: (i, 0))
    else:
        tiles = rows_per_mod // tm
        g2_spec = pl.BlockSpec((None, 1, d), lambda i, e: (i // tiles, 0, 0))
    row = lambda w: pl.BlockSpec((tm, w), lambda i, e: (i, 0))
    return pl.pallas_call(
        functools.partial(_moe_kernel, passes=passes),
        out_shape=jax.ShapeDtypeStruct((n, d), F32),
        grid=(n // tm, MOE_EXPERTS),
        in_specs=[
            row(d), row(LANES), row(d), g2_spec,
            pl.BlockSpec((None, None, d, MOE_FF), lambda i, e: (layer, e, 0, 0)),
            pl.BlockSpec((None, None, d, MOE_FF), lambda i, e: (layer, e, 0, 0)),
            pl.BlockSpec((None, None, MOE_FF, d), lambda i, e: (layer, e, 0, 0)),
        ],
        out_specs=row(d),
        scratch_shapes=[pltpu.VMEM((tm, d), F32)],
        compiler_params=_cparams(("arbitrary", "arbitrary")),
        name="moe_dense",
    )(h2, gate, x1, g2, wg, wu, wd)


_PAIR_LO = (0, 0, 0, 1, 1, 2)
_PAIR_HI = (1, 2, 3, 2, 3, 3)
_DMA_UNROLL = 8
_ROUTE_TILE = 2048


def _moe_plan(counts, n_tokens):
    tm = MOE_TILE
    blocks_per_class = n_tokens // tm
    ntile = (counts + tm - 1) // tm
    cum = jnp.cumsum(ntile)
    total = cum[-1]
    t_max = blocks_per_class + MOE_CLASSES
    t_eff = jnp.minimum(jnp.arange(t_max, dtype=jnp.int32), total - 1)
    cls = jnp.searchsorted(cum, t_eff, side="right").astype(jnp.int32)
    blk = cls * blocks_per_class + (t_eff - (cum - ntile)[cls])
    grp, pair = cls // MOE_PAIRS, cls % MOE_PAIRS
    e_lo = grp * MOE_PER_GROUP + jnp.asarray(_PAIR_LO, jnp.int32)[pair]
    e_hi = grp * MOE_PER_GROUP + jnp.asarray(_PAIR_HI, jnp.int32)[pair]
    z_need = ((counts % tm) != 0).astype(jnp.int32)
    z_blk = jnp.arange(MOE_CLASSES, dtype=jnp.int32) * blocks_per_class + jnp.maximum(ntile - 1, 0)
    return (blk.astype(jnp.int32), e_lo, e_hi, total.reshape(1).astype(jnp.int32), z_need, z_blk.astype(jnp.int32))


def _row_copy(src, src_row, dst, dst_row, sem):
    return pltpu.make_async_copy(src.at[pl.ds(pl.multiple_of(src_row * ROW_SUB, ROW_SUB), ROW_SUB), :],
                                 dst.at[pl.ds(pl.multiple_of(dst_row * ROW_SUB, ROW_SUB), ROW_SUB), :], sem)


def _dispatch_kernel(pos_ref, zneed_ref, zblk_ref, rows_ref, xs_ref, zero_scr, sem, zsem, *, tm):
    step = pl.program_id(0)
    tile_rows = MOE_TILE * ROW_SUB

    def zero_copy(c):
        start = pl.multiple_of(zblk_ref[c] * tile_rows, tile_rows)
        return pltpu.make_async_copy(zero_scr, xs_ref.at[pl.ds(start, tile_rows), :], zsem)

    @pl.when(step == 0)
    def _():
        zero_scr[...] = jnp.zeros_like(zero_scr)
        for c in range(MOE_CLASSES):
            @pl.when(zneed_ref[c] != 0)
            def _():
                zero_copy(c).start()
        for c in range(MOE_CLASSES):
            @pl.when(zneed_ref[c] != 0)
            def _():
                zero_copy(c).wait()

    base = step * tm

    def body(i8, carry):
        for u in range(_DMA_UNROLL):
            i = i8 * _DMA_UNROLL + u
            _row_copy(rows_ref, i, xs_ref, pos_ref[base + i], sem).start(priority=u % 2)
        return carry

    lax.fori_loop(0, tm // _DMA_UNROLL, body, 0)
    pltpu.make_async_copy(rows_ref, xs_ref.at[pl.ds(0, tm * ROW_SUB), :], sem).wait()


def _dispatch(pos, z_need, z_blk, rows, n_tokens, tm=_ROUTE_TILE):
    sorted_rows = MOE_CLASSES * n_tokens * ROW_SUB
    return pl.pallas_call(
        functools.partial(_dispatch_kernel, tm=tm),
        out_shape=jax.ShapeDtypeStruct((sorted_rows, LANES), F32),
        grid_spec=pltpu.PrefetchScalarGridSpec(
            num_scalar_prefetch=3,
            grid=(n_tokens // tm,),
            in_specs=[pl.BlockSpec((tm * ROW_SUB, LANES), lambda i, *_: (i, 0))],
            out_specs=pl.BlockSpec(memory_space=pl.ANY),
            scratch_shapes=[pltpu.VMEM((MOE_TILE * ROW_SUB, LANES), F32),
                            pltpu.SemaphoreType.DMA(()), pltpu.SemaphoreType.DMA(())],
        ),
        compiler_params=pltpu.CompilerParams(dimension_semantics=("arbitrary",), vmem_limit_bytes=VMEM_LIMIT,
                                             disable_bounds_checks=True),
        name="moe_dispatch",
    )(pos, z_need, z_blk, rows)


def _moe_sparse_kernel(blk_ref, elo_ref, ehi_ref, nact_ref, xs_ref, wr_ref, br_ref,
                       wg_lo_ref, wu_lo_ref, wd_lo_ref, wg_hi_ref, wu_hi_ref, wd_hi_ref, ys_ref):
    t = pl.program_id(0)
    tm = MOE_TILE

    @pl.when(t < nact_ref[0])
    def _():
        x = jnp.concatenate([xs_ref[pl.ds(c, tm, stride=ROW_SUB), :] for c in range(ROW_SUB)], axis=1)
        xb = x.astype(BF16)
        w_hi16, w_lo16 = _split2(wr_ref[...])
        logits = _dot(xb, w_hi16) + _dot(xb, w_lo16) + br_ref[...]
        lane = lax.broadcasted_iota(jnp.int32, logits.shape, 1)
        e_lo, e_hi = elo_ref[t], ehi_ref[t]
        g_lane = MOE_EXPERTS + (e_lo >> 2)
        pick = lambda ln: jnp.sum(jnp.where(lane == ln, logits, 0.0), axis=-1, keepdims=True)
        l_lo, l_hi, l_g = pick(e_lo), pick(e_hi), pick(g_lane)
        is_grp = (lane >= MOE_EXPERTS) & (lane < MOE_EXPERTS + MOE_GROUPS)
        g_w = 1.0 / jnp.sum(jnp.where(is_grp, jnp.exp(logits - l_g), 0.0), axis=-1, keepdims=True)
        w_a = g_w / (1.0 + jnp.exp(l_hi - l_lo))
        w_b = g_w / (1.0 + jnp.exp(l_lo - l_hi))
        bf = lambda w_ref: w_ref[...].astype(BF16)
        act_a = (_silu(_dot(xb, bf(wg_lo_ref))) * _dot(xb, bf(wu_lo_ref)) * w_a).astype(BF16)
        act_b = (_silu(_dot(xb, bf(wg_hi_ref))) * _dot(xb, bf(wu_hi_ref)) * w_b).astype(BF16)
        y = _dot(act_a, bf(wd_lo_ref)) + _dot(act_b, bf(wd_hi_ref))
        for c in range(ROW_SUB):
            ys_ref[pl.ds(c, tm, stride=ROW_SUB), :] = y[:, c * LANES:(c + 1) * LANES]


def _moe_sparse(layer, plan, xs, wr, br, wg, wu, wd, n_tokens):
    blk, e_lo, e_hi, n_act = plan
    tm = MOE_TILE
    t_max = blk.shape[0]
    d = D_MODEL
    tile = pl.BlockSpec((tm * ROW_SUB, LANES), lambda t, blk, lo, hi, na: (blk[t], 0))
    w_lo = lambda shape: pl.BlockSpec((None, None) + shape, lambda t, blk, lo, hi, na: (layer, lo[t], 0, 0))
    w_hi = lambda shape: pl.BlockSpec((None, None) + shape, lambda t, blk, lo, hi, na: (layer, hi[t], 0, 0))
    const = lambda a: pl.BlockSpec(a.shape, lambda t, *_: (0, 0))
    return pl.pallas_call(
        _moe_sparse_kernel,
        out_shape=jax.ShapeDtypeStruct(xs.shape, F32),
        grid_spec=pltpu.PrefetchScalarGridSpec(
            num_scalar_prefetch=4,
            grid=(t_max,),
            in_specs=[tile, const(wr), const(br),
                      w_lo((d, MOE_FF)), w_lo((d, MOE_FF)), w_lo((MOE_FF, d)),
                      w_hi((d, MOE_FF)), w_hi((d, MOE_FF)), w_hi((MOE_FF, d))],
            out_specs=tile,
        ),
        compiler_params=_cparams(("arbitrary",)),
        name="moe_sparse",
    )(blk, e_lo, e_hi, n_act, xs, wr, br, wg, wu, wd, wg, wu, wd)


def _combine_kernel(pos_ref, ys_ref, x1_ref, g2_ref, *rest, tm, final):
    if final:
        fg_ref, x2_ref, buf, sem = rest
    else:
        x2_ref, buf, sem = rest
    base = pl.program_id(0) * tm

    def body(i8, carry):
        for u in range(_DMA_UNROLL):
            i = i8 * _DMA_UNROLL + u
            _row_copy(ys_ref, pos_ref[base + i], buf, i, sem).start(priority=u % 2)
        return carry

    lax.fori_loop(0, tm // _DMA_UNROLL, body, 0)
    pltpu.make_async_copy(ys_ref.at[pl.ds(0, tm * ROW_SUB), :], buf, sem).wait()
    y = jnp.concatenate([buf[pl.ds(c, tm, stride=ROW_SUB), :] for c in range(ROW_SUB)], axis=1)
    x2 = x1_ref[...] + g2_ref[...] * y
    if final:
        ms = jnp.mean(x2 * x2, axis=-1, keepdims=True)
        x2 = x2 * lax.rsqrt(ms + EPS) * fg_ref[...]
    x2_ref[...] = x2


def _combine(pos, ys, x1, g2, rows_per_mod, final_g=None, tm=_ROUTE_TILE):
    n, d = x1.shape
    tiles = rows_per_mod // tm
    final = final_g is not None
    extra_specs = [pl.BlockSpec((1, d), lambda i, *_: (0, 0))] if final else []
    extra_args = [final_g] if final else []
    return pl.pallas_call(
        functools.partial(_combine_kernel, tm=tm, final=final),
        out_shape=jax.ShapeDtypeStruct((n, d), F32),
        grid_spec=pltpu.PrefetchScalarGridSpec(
            num_scalar_prefetch=1,
            grid=(n // tm,),
            in_specs=[pl.BlockSpec(memory_space=pl.ANY),
                      pl.BlockSpec((tm, d), lambda i, *_: (i, 0)),
                      pl.BlockSpec((None, 1, d), lambda i, *_: (i // tiles, 0, 0))] + extra_specs,
            out_specs=pl.BlockSpec((tm, d), lambda i, *_: (i, 0)),
            scratch_shapes=[pltpu.VMEM((tm * ROW_SUB, LANES), F32), pltpu.SemaphoreType.DMA(())],
        ),
        compiler_params=pltpu.CompilerParams(dimension_semantics=("arbitrary",), vmem_limit_bytes=VMEM_LIMIT,
                                             disable_bounds_checks=True),
        name="moe_combine",
    )(pos, ys, x1, g2, *extra_args)


def _final_kernel(x_ref, g_ref, o_ref):
    x = x_ref[...]
    ms = jnp.mean(x * x, axis=-1, keepdims=True)
    o_ref[...] = x * lax.rsqrt(ms + EPS) * g_ref[...]


def _final_norm(x, g, tm):
    n, d = x.shape
    return pl.pallas_call(
        _final_kernel,
        out_shape=jax.ShapeDtypeStruct((n, d), F32),
        grid=(n // tm,),
        in_specs=[pl.BlockSpec((tm, d), lambda i: (i, 0)), pl.BlockSpec((1, d), lambda i: (0, 0))],
        out_specs=pl.BlockSpec((tm, d), lambda i: (i, 0)),
        compiler_params=_cparams(("arbitrary",)),
        name="final_norm",
    )(x, g)


def _step_kernel(z_ref, xbc_ref, dt_ref, cst_ref, h_ref,
                 cw_ref, cb_ref, dtb_ref, alog_ref, dskip_ref, ng_ref, *rest):
    yn_ref, cnew_ref, hnew_ref, col_scr = rest[-4:]
    x_new = xbc_ref[...]
    acc = cb_ref[...] + cw_ref[3:4, :] * x_new
    for k in range(SSD_CONV - 1):
        acc = acc + cw_ref[k:k + 1, :] * cst_ref[k:k + 1, :]
    cnew_ref[0:1, :] = cst_ref[1:2, :]
    cnew_ref[1:2, :] = cst_ref[2:3, :]
    cnew_ref[2:3, :] = x_new
    xc = _silu(acc)
    xs = xc[:, 0:SSD_INNER]

    e_heads = _head_expand_matrix(LANES, SSD_HEADS, SSD_HEAD_DIM)
    dt_raw8 = jnp.broadcast_to(dt_ref[...], (8, LANES))
    dt_full = _softplus(_mm01(dt_raw8, e_heads, 3)[0:1, :] + dtb_ref[...])
    a_full = -jnp.exp(alog_ref[...])
    dec_full = jnp.exp(dt_full * a_full)
    xdt = xs * dt_full
    col_scr[...] = jnp.zeros_like(col_scr)
    col_scr[0:1, :] = xdt
    col_scr[1:2, :] = dec_full
    cols = col_scr[...].T
    y_parts = []
    gw = SSD_HPG * SSD_HEAD_DIM
    for g in range(SSD_GROUPS):
        b_off = SSD_INNER + g * SSD_STATE
        c_off = SSD_INNER + SSD_GROUPS * SSD_STATE + g * SSD_STATE
        bm = xc[:, b_off:b_off + SSD_STATE]
        cm = xc[:, c_off:c_off + SSD_STATE]
        g0 = g * gw
        hn = h_ref[g0:g0 + gw, :] * cols[g0:g0 + gw, 1:2] + cols[g0:g0 + gw, 0:1] * bm
        hnew_ref[g0:g0 + gw, :] = hn
        t = (hn * cm).T
        y_parts.append(jnp.sum(t, axis=0, keepdims=True))
    y = jnp.concatenate(y_parts, axis=1) + dskip_ref[...] * xs
    y = y * _silu(z_ref[...])
    outs = []
    for g in range(SSD_GROUPS):
        yg = y[:, g * gw:(g + 1) * gw]
        ms = jnp.mean(yg * yg, axis=-1, keepdims=True)
        outs.append(yg * lax.rsqrt(ms + EPS) * ng_ref[:, g * gw:(g + 1) * gw])
    yn_ref[...] = jnp.concatenate(outs, axis=1)


def _step_sample(layer, z, xbc, dt_raw, state_conv, state_ssm, prev_ssm,
                 conv_w, conv_b, dtb_full, alog_full, dskip_full, ssd_norm_g):
    bsz = z.shape[0]
    row3 = lambda w: pl.BlockSpec((None, 1, w), lambda b: (b, 0, 0))
    const = lambda a: pl.BlockSpec(a.shape, lambda b: (0,) * a.ndim)
    alias_args = [] if prev_ssm is None else [prev_ssm]
    n_in = 11
    return pl.pallas_call(
        _step_kernel,
        out_shape=(
            jax.ShapeDtypeStruct((bsz, 1, SSD_INNER), F32),
            jax.ShapeDtypeStruct((bsz, SSD_CONV - 1, SSD_CONV_DIM), F32),
            jax.ShapeDtypeStruct(state_ssm.shape, F32),
        ),
        grid=(bsz,),
        in_specs=[
            row3(SSD_INNER), row3(SSD_CONV_DIM), row3(LANES),
            pl.BlockSpec((None, None, SSD_CONV - 1, SSD_CONV_DIM), lambda b: (layer, b, 0, 0)),
            pl.BlockSpec((None, None, SSD_INNER, SSD_STATE), lambda b: (layer, b, 0, 0)),
            const(conv_w), const(conv_b), const(dtb_full), const(alog_full), const(dskip_full), const(ssd_norm_g),
        ] + [pl.BlockSpec(memory_space=pl.ANY)] * len(alias_args),
        out_specs=(
            row3(SSD_INNER),
            pl.BlockSpec((None, SSD_CONV - 1, SSD_CONV_DIM), lambda b: (b, 0, 0)),
            pl.BlockSpec((None, None, SSD_INNER, SSD_STATE), lambda b: (layer, b, 0, 0)),
        ),
        input_output_aliases={n_in: 2} if alias_args else {},
        scratch_shapes=[pltpu.VMEM((LANES, SSD_INNER), F32)],
        compiler_params=_cparams(("arbitrary",)),
        name="step_sample",
    )(z.reshape(bsz, 1, -1), xbc.reshape(bsz, 1, -1), dt_raw.reshape(bsz, 1, -1), state_conv, state_ssm,
      conv_w, conv_b, dtb_full, alog_full, dskip_full, ssd_norm_g, *alias_args)


_HEAD_SPLIT = 2
_HROWS = ATT_OUT // _HEAD_SPLIT


def _cache_attn_kernel(*refs, n_alias):
    qkv_ref = refs[0]
    cache_refs = refs[1:7]
    out_refs = refs[7 + n_alias:13 + n_alias]
    o_ref = refs[13 + n_alias]
    b = pl.program_id(0)
    nh = _HROWS // ATT_HEAD_DIM
    scale = ATT_HEAD_DIM ** -0.5
    qkv = qkv_ref[...]
    lane_b = lax.broadcasted_iota(jnp.int32, qkv.shape, 2)
    cols = jnp.sum(jnp.where(lane_b == b, qkv, 0.0), axis=-1, keepdims=True)

    def per_head_rows(v):
        return jnp.concatenate([jnp.broadcast_to(v[h:h + 1, :], (ATT_HEAD_DIM, 1)) for h in range(nh)], axis=0)

    o_g, lse_g = [], []
    for gi, (_, dil) in enumerate(ATT_PATTERNS):
        k_ref, v_ref = cache_refs[2 * gi], cache_refs[2 * gi + 1]
        ko_ref, vo_ref = out_refs[2 * gi], out_refs[2 * gi + 1]
        q = cols[gi]
        k_new = cols[ATT_GROUPS + gi]
        v_new = cols[2 * ATT_GROUPS + gi]
        kk = k_ref[...]
        vv = v_ref[...]
        length = kk.shape[1]
        lane = lax.broadcasted_iota(jnp.int32, (nh, length), 1)
        s = jnp.sum((kk * q).reshape(nh, ATT_HEAD_DIM, length), axis=1) * scale
        s = jnp.where((lane & (dil - 1)) == 0, s, -jnp.inf)
        s_new = jnp.sum((k_new * q).reshape(nh, ATT_HEAD_DIM, 1), axis=1) * scale
        m = jnp.maximum(jnp.max(s, axis=-1, keepdims=True), s_new)
        e = jnp.exp(s - m)
        e_new = jnp.exp(s_new - m)
        den = jnp.sum(e, axis=-1, keepdims=True) + e_new
        acc = jnp.sum(vv.reshape(nh, ATT_HEAD_DIM, length) * e[:, None, :], axis=-1, keepdims=True)
        acc = acc.reshape(_HROWS, 1) + per_head_rows(e_new) * v_new
        o_g.append(acc / per_head_rows(den))
        lse_g.append(per_head_rows(m + jnp.log(den)))
        lane_full = lax.broadcasted_iota(jnp.int32, kk.shape, 1)
        last = lane_full == length - 1
        ko_ref[...] = jnp.where(last, k_new, pltpu.roll(kk, length - 1, axis=1))
        vo_ref[...] = jnp.where(last, v_new, pltpu.roll(vv, length - 1, axis=1))
    mx = jnp.maximum(jnp.maximum(lse_g[0], lse_g[1]), lse_g[2])
    w = [jnp.exp(l - mx) for l in lse_g]
    tot = w[0] + w[1] + w[2]
    o_ref[...] = (w[0] / tot) * o_g[0] + (w[1] / tot) * o_g[1] + (w[2] / tot) * o_g[2]


def _cache_attn(layer, qkv_t, cache_views, prev_outs):
    depth, bsz, _, _ = cache_views[0].shape
    n_alias = 0 if prev_outs is None else len(prev_outs)
    q4 = qkv_t.reshape(3 * ATT_GROUPS, _HEAD_SPLIT, _HROWS, bsz)
    blk = lambda c: pl.BlockSpec((None, None, _HROWS, c.shape[3]), lambda b, hh: (layer, b, hh, 0))
    any_spec = pl.BlockSpec(memory_space=pl.ANY)
    args = [q4] + list(cache_views) + ([] if prev_outs is None else list(prev_outs))
    res = pl.pallas_call(
        functools.partial(_cache_attn_kernel, n_alias=n_alias),
        out_shape=tuple(jax.ShapeDtypeStruct(c.shape, c.dtype) for c in cache_views)
        + (jax.ShapeDtypeStruct((bsz, _HEAD_SPLIT, _HROWS, 1), F32),),
        grid=(bsz, _HEAD_SPLIT),
        in_specs=[pl.BlockSpec((3 * ATT_GROUPS, None, _HROWS, bsz), lambda b, hh: (0, hh, 0, 0))]
        + [blk(c) for c in cache_views] + [any_spec] * n_alias,
        out_specs=tuple(blk(c) for c in cache_views)
        + (pl.BlockSpec((None, None, _HROWS, 1), lambda b, hh: (b, hh, 0, 0)),),
        input_output_aliases={7 + i: i for i in range(n_alias)},
        compiler_params=_cparams(("arbitrary", "arbitrary")),
        name="cache_attn",
    )(*args)
    return list(res[:6]), res[6].reshape(bsz, ATT_OUT)


def _pad_lanes(v, width=LANES):
    return jnp.pad(v, [(0, 0)] * (v.ndim - 1) + [(0, width - v.shape[-1])])


def kernel(x_prompt, x_sample, c_prompt, c_sample, state_conv, state_ssm, cache_k_win128, cache_v_win128, cache_k_win512, cache_v_win512, cache_k_win2048, cache_v_win2048, norm1_g, w_ada, b_ada, w_in, conv_w, conv_b, dt_bias, a_log, d_skip, ssd_norm_g, w_ssd_proj, w_attn_proj, w_out, norm2_g, w_router_group, b_router_group, w_router_expert, b_router_expert, w_exp_gate, w_exp_up, w_exp_down, final_norm_g):
    depth = w_in.shape[0]
    bp, seq, d = x_prompt.shape
    bs = x_sample.shape[0]
    assert x_sample.shape[1] == 1 and d == D_MODEL and w_in.shape[2] == IN_WIDTH
    assert seq % (ATT_PATTERNS[-1][1] * ATT_BLOCK) == 0
    n_p = bp * seq
    caches = ((cache_k_win128, cache_v_win128), (cache_k_win512, cache_v_win512),
              (cache_k_win2048, cache_v_win2048))

    rows = bp + bs
    rows_pad = -(-rows // 8) * 8
    c_all = jnp.pad(jnp.concatenate([c_prompt, c_sample], axis=0), ((0, rows_pad - rows), (0, 0)))
    mods = _modulation(c_all, w_ada, b_ada)

    xp = x_prompt.reshape(n_p, d)
    xs = x_sample.reshape(bs, d)
    outs = {k: [] for k in ("conv_p", "conv_s", "ssm_p", "ssm_s")}
    kv_p = [[[], []] for _ in ATT_PATTERNS]
    tiles = seq // TOK_TILE
    assert bs <= LANES and all(c.shape[2] == win for (win, _), pair in zip(ATT_PATTERNS, caches) for c in pair)
    cache_views = [jnp.transpose(c, (0, 1, 3, 4, 2)).reshape(depth, bs, ATT_OUT, c.shape[2])
                   for pair in caches for c in pair]
    shifted = None
    ssm_s_all = None
    w_in_t = jnp.swapaxes(w_in, 1, 2)
    w_chunks = _prep_w_in(w_in_t)

    for l in range(depth):
        mod_p = mods[l, :bp]
        mod_s = mods[l, bp:bp + bs]
        wd = _pad_lanes(w_in_t[l, OFF_DT:OFF_QKV].T)
        g1n = norm1_g[l].reshape(1, d)
        g2n = norm2_g[l].reshape(1, d)
        cw = conv_w[l]
        cb = conv_b[l].reshape(1, -1)
        dskip_full = jnp.repeat(d_skip[l], SSD_HEAD_DIM).reshape(1, -1)
        ssd_g = ssd_norm_g[l].reshape(1, -1)
        w_router = _pad_lanes(jnp.concatenate([w_router_expert[l], w_router_group[l]], axis=1))
        b_router = _pad_lanes(jnp.concatenate([b_router_expert[l], b_router_group[l]], axis=0).reshape(1, -1))

        z, xbc, dt_raw, a0, a1, a2, gates = _in_proj_prompt(
            l, xp, mod_p[:, 0:2 * d].reshape(bp, 1, 2 * d), g1n, w_chunks, wd, bp, seq)
        yn, ssm_new = _ssd_prompt(xbc, z, dt_raw, cw, cb, _pad_lanes(dt_bias[l].reshape(1, -1)),
                                  _pad_lanes(a_log[l].reshape(1, -1)), dskip_full, ssd_g, bp, seq)
        outs["ssm_p"].append(ssm_new.reshape(bp, SSD_GROUPS, SSD_HPG, SSD_HEAD_DIM, SSD_STATE))
        outs["conv_p"].append(xbc.reshape(bp, seq, SSD_CONV_DIM)[:, seq - (SSD_CONV - 1):])
        a_groups = [a0.reshape(bp, tiles, 1, TOK_TILE, 3 * ATT_OUT), a1, a2]
        o_list, lse_list = [], []
        for gi, (win, dil) in enumerate(ATT_PATTERNS):
            o_g, lse_g = _attn_prompt(a_groups[gi], gi, bp, seq)
            o_list.append(o_g)
            lse_list.append(lse_g)
            keep = min(win, seq)
            rows = TOK_TILE // dil
            if keep >= TOK_TILE:
                nt = keep // TOK_TILE
                tail = a_groups[gi][:, tiles - nt:, :, :, ATT_OUT:3 * ATT_OUT]
                tail = jnp.transpose(tail, (0, 1, 3, 2, 4)).reshape(bp, keep, 2 * ATT_OUT)
            else:
                assert dil == 1
                tail = a_groups[gi][:, tiles - 1, 0, TOK_TILE - keep:, ATT_OUT:3 * ATT_OUT]
            tail = tail.astype(F32)
            kv_p[gi][0].append(tail[:, :, 0:ATT_OUT].reshape(bp, keep, ATT_HEADS, ATT_HEAD_DIM))
            kv_p[gi][1].append(tail[:, :, ATT_OUT:2 * ATT_OUT].reshape(bp, keep, ATT_HEADS, ATT_HEAD_DIM))
        mod_post = jnp.concatenate([mod_p[:, 2 * d:3 * d], mod_p[:, 3 * d:5 * d]], axis=1).reshape(bp, 1, 3 * d)
        x1, rows, slot, cnt = _post(xp, yn, (o_list, lse_list), gates, mod_post, g2n,
                                    w_ssd_proj[l].astype(BF16), w_attn_proj[l].astype(BF16),
                                    w_out[l].astype(BF16), w_router, b_router, passes=1, merge_attn=True,
                                    tm=TOK_TILE, rows_per_mod=seq, h2_dtype=F32)
        pos = slot[:, 0].astype(jnp.int32)
        blk, e_lo, e_hi, n_act, z_need, z_blk = _moe_plan(cnt[0, :MOE_CLASSES].astype(jnp.int32), n_p)
        xs_sorted = _dispatch(pos, z_need, z_blk, rows, n_p)
        ys_sorted = _moe_sparse(l, (blk, e_lo, e_hi, n_act), xs_sorted, w_router, b_router,
                                w_exp_gate, w_exp_up, w_exp_down, n_p)
        xp = _combine(pos, ys_sorted, x1, mod_p[:, 5 * d:6 * d].reshape(bp, 1, d), seq,
                      final_g=final_norm_g.reshape(1, d) if l == depth - 1 else None)

        u = _in_proj_sample(l, xs, mod_s[:, 0:2 * d], g1n, w_in_t)
        z_s = u[:, 0:OFF_XBC]
        xbc_s = u[:, OFF_XBC:OFF_DT]
        dt_s = _pad_lanes(u[:, OFF_DT:OFF_QKV])
        qkv_s = u[:, OFF_QKV:OFF_GATE]
        gates_s = u[:, OFF_GATE:IN_WIDTH]
        yn_s, conv_new, ssm_s_all = _step_sample(
            l, z_s, xbc_s, dt_s, state_conv, state_ssm.reshape(depth, bs, SSD_INNER, SSD_STATE), ssm_s_all,
            cw, cb, jnp.repeat(dt_bias[l], SSD_HEAD_DIM).reshape(1, -1),
            jnp.repeat(a_log[l], SSD_HEAD_DIM).reshape(1, -1), dskip_full, ssd_g)
        outs["conv_s"].append(conv_new)
        shifted, o_s = _cache_attn(l, qkv_s.T, cache_views, shifted)
        mod_post_s = jnp.concatenate([mod_s[:, 2 * d:3 * d], mod_s[:, 3 * d:5 * d]], axis=1)
        x1_s, h2_s, gate_s = _post(xs, yn_s.reshape(bs, SSD_INNER), o_s, gates_s, mod_post_s,
                                   g2n, w_ssd_proj[l], w_attn_proj[l], w_out[l], w_router, b_router,
                                   passes=3, merge_attn=False, tm=bs, rows_per_mod=1, h2_dtype=F32)
        xs = _moe(l, h2_s, gate_s, x1_s, mod_s[:, 5 * d:6 * d], w_exp_gate, w_exp_up, w_exp_down,
                  passes=3, tm=bs, rows_per_mod=1)

    fg = final_norm_g.reshape(1, d)
    y_prompt = xp.reshape(bp, seq, d)
    y_sample = _final_norm(xs, fg, bs).reshape(bs, 1, d)

    shifted = [jnp.transpose(s.reshape(depth, bs, ATT_HEADS, ATT_HEAD_DIM, s.shape[3]), (0, 1, 4, 2, 3))
               for s in shifted]

    st = jnp.stack
    res = [y_prompt, y_sample, st(outs["conv_p"]), st(outs["conv_s"]), st(outs["ssm_p"]),
           ssm_s_all.reshape(state_ssm.shape)]
    for gi in range(ATT_GROUPS):
        res += [st(kv_p[gi][0]), shifted[2 * gi], st(kv_p[gi][1]), shifted[2 * gi + 1]]
    return tuple(res)
```

```python
import functools
import math

import jax
import jax.numpy as jnp
from jax import lax
from jax.experimental import pallas as pl
from jax.experimental.pallas import tpu as pltpu

F32 = jnp.float32
BF16 = jnp.bfloat16

D_MODEL = 1024
SSD_INNER = 1024
SSD_HEAD_DIM = 64
SSD_HEADS = 16
SSD_GROUPS = 2
SSD_HPG = 8
SSD_STATE = 128
SSD_CONV = 4
SSD_CHUNK = 128
SSD_CONV_DIM = SSD_INNER + 2 * SSD_GROUPS * SSD_STATE
ATT_PATTERNS = ((128, 1), (512, 4), (2048, 16))
ATT_GROUPS = 3
ATT_HEADS = 8
ATT_HEAD_DIM = 64
ATT_BLOCK = 128
ATT_OUT = ATT_HEADS * ATT_HEAD_DIM
MOE_GROUPS = 4
MOE_PER_GROUP = 4
MOE_EXPERTS = 16
MOE_FF = 512
MOE_PAIRS = 6
MOE_CLASSES = MOE_GROUPS * MOE_PAIRS
MOE_TILE = 512
ROW_SUB = 8
IN_QKV = 3 * ATT_GROUPS * ATT_OUT
IN_GATE = 2 * D_MODEL
OFF_XBC = SSD_INNER
OFF_DT = OFF_XBC + SSD_CONV_DIM
OFF_QKV = OFF_DT + SSD_HEADS
OFF_GATE = OFF_QKV + IN_QKV
IN_WIDTH = OFF_GATE + IN_GATE
EPS = 1e-6

LANES = 128
VMEM_LIMIT = 56 * 1024 * 1024


def _cparams(sem, vmem=VMEM_LIMIT):
    return pltpu.CompilerParams(dimension_semantics=sem, vmem_limit_bytes=vmem)


def _split2(a):
    hi = a.astype(BF16)
    lo = (a - hi.astype(F32)).astype(BF16)
    return hi, lo


def _split3(a):
    hi = a.astype(BF16)
    r = a - hi.astype(F32)
    mid = r.astype(BF16)
    lo = (r - mid.astype(F32)).astype(BF16)
    return hi, mid, lo


def _dot(a, b):
    return jnp.dot(a, b, preferred_element_type=F32)


def _dot_nt(a, b):
    return lax.dot_general(a, b, (((1,), (1,)), ((), ())), preferred_element_type=F32)


def _mm(a, w, passes):
    if passes == 1:
        return _dot(a.astype(BF16), w.astype(BF16))
    a = a.astype(F32)
    w = w.astype(F32)
    a_hi, a_lo = _split2(a)
    w_hi, w_lo = _split2(w)
    return _dot(a_hi, w_hi) + (_dot(a_lo, w_hi) + _dot(a_hi, w_lo))


def _mm01(a, e01, terms):
    parts = _split3(a) if terms == 3 else _split2(a)
    out = _dot(parts[0], e01)
    for p in parts[1:]:
        out = out + _dot(p, e01)
    return out


def _sigmoid(x):
    return 1.0 / (1.0 + jnp.exp(-x))


def _silu(x):
    return x * _sigmoid(x)


def _softplus(x):
    return jnp.maximum(x, 0.0) + jnp.log(1.0 + jnp.exp(-jnp.abs(x)))


def _head_expand_matrix(rows, n_heads, width):
    r = lax.broadcasted_iota(jnp.int32, (rows, n_heads * width), 0)
    c = lax.broadcasted_iota(jnp.int32, (rows, n_heads * width), 1)
    return jnp.where((c // width) == r, 1.0, 0.0).astype(BF16)


def _rms_modulate(x, g, sh, sc):
    ms = jnp.mean(x * x, axis=-1, keepdims=True)
    y = x * lax.rsqrt(ms + EPS) * g
    return y * (1.0 + sc) + sh


def _mod_kernel(c_ref, w_ref, b_ref, o_ref):
    a = _silu(c_ref[...])
    o_ref[...] = _mm(a, w_ref[...], 3) + b_ref[...]


def _modulation(c_all, w_ada, b_ada):
    depth, d, n6 = w_ada.shape
    rows = c_all.shape[0]
    tn = 1024
    return pl.pallas_call(
        _mod_kernel,
        out_shape=jax.ShapeDtypeStruct((depth, rows, n6), F32),
        grid=(depth, n6 // tn),
        in_specs=[
            pl.BlockSpec((rows, d), lambda l, j: (0, 0)),
            pl.BlockSpec((None, d, tn), lambda l, j: (l, 0, j)),
            pl.BlockSpec((None, 1, tn), lambda l, j: (l, 0, j)),
        ],
        out_specs=pl.BlockSpec((None, rows, tn), lambda l, j: (l, 0, j)),
        compiler_params=_cparams(("arbitrary", "arbitrary")),
        name="adaln_mod",
    )(c_all, w_ada, b_ada.reshape(depth, 1, n6))


_IN_CHUNK = 512
TOK_TILE = 512

_W_CHUNK_ROWS = (tuple(range(0, OFF_DT, _IN_CHUNK))
                 + tuple(range(OFF_QKV, OFF_GATE, _IN_CHUNK)) + tuple(range(OFF_GATE, IN_WIDTH, _IN_CHUNK)))
_CH_Z = (0, 1)
_CH_XBC = (2, 3, 4)
_CH_QKV0 = 5
_CH_GATES = (14, 15, 16, 17)


def _w_prep_kernel(starts_ref, w_ref, o_ref):
    del starts_ref
    o_ref[...] = w_ref[0].T.astype(BF16)


def _prep_w_in(w_in_t):
    depth, _, d = w_in_t.shape
    n_ch = len(_W_CHUNK_ROWS)
    starts = jnp.asarray(_W_CHUNK_ROWS, jnp.int32)
    return pl.pallas_call(
        _w_prep_kernel,
        out_shape=jax.ShapeDtypeStruct((depth, n_ch, d, _IN_CHUNK), BF16),
        grid_spec=pltpu.PrefetchScalarGridSpec(
            num_scalar_prefetch=1,
            grid=(depth, n_ch),
            in_specs=[pl.BlockSpec((pl.Element(1), pl.Element(_IN_CHUNK), pl.Element(d)),
                                   lambda l, j, st: (l, pl.multiple_of(st[j], 16), 0))],
            out_specs=pl.BlockSpec((None, None, d, _IN_CHUNK), lambda l, j, st: (l, j, 0, 0)),
        ),
        compiler_params=_cparams(("arbitrary", "arbitrary")),
        name="w_in_prep",
    )(starts, w_in_t)


def _in_kernel(x_ref, mod_ref, g_ref, w_ref, wd_ref,
               z_ref, xbc_ref, dt_ref, a0_ref, a1_ref, a2_ref, gates_ref, h_scr):
    sh = mod_ref[:, 0:D_MODEL]
    sc = mod_ref[:, D_MODEL:2 * D_MODEL]
    hf = _rms_modulate(x_ref[...], g_ref[...], sh, sc)
    n_cb = D_MODEL // LANES
    for cbk in range(n_cb):
        h_scr[cbk] = hf[:, cbk * LANES:(cbk + 1) * LANES]
    h = hf.astype(BF16)
    qkv_chunks = lambda gi: tuple(_CH_QKV0 + which * ATT_GROUPS + gi for which in range(3))
    for chunks, o_ref in ((_CH_Z, z_ref), (_CH_XBC, xbc_ref), (qkv_chunks(0), a0_ref), (_CH_GATES, gates_ref)):
        for i, ch in enumerate(chunks):
            o_ref[:, i * _IN_CHUNK:(i + 1) * _IN_CHUNK] = _dot(h, w_ref[ch]).astype(o_ref.dtype)
    dt_ref[...] = _dot(h, wd_ref[...].astype(BF16))
    for gi, a_ref in ((1, a1_ref), (2, a2_ref)):
        d, r_len, _ = a_ref.shape
        hp = jnp.concatenate(
            [jnp.concatenate([h_scr[cbk, pl.ds(r, r_len, stride=d), :] for cbk in range(n_cb)], axis=1)
             for r in range(d)], axis=0).astype(BF16)
        for i, ch in enumerate(qkv_chunks(gi)):
            res = _dot(hp, w_ref[ch]).astype(BF16)
            a_ref[:, :, i * _IN_CHUNK:(i + 1) * _IN_CHUNK] = res.reshape(d, r_len, _IN_CHUNK)


def _in_proj_prompt(layer, x, mod, norm_g, w_chunks, wd, bsz, seq):
    n, d = x.shape
    tm = TOK_TILE
    tiles_per_seq = seq // tm
    qw = 3 * ATT_OUT

    def const(shape):
        return pl.BlockSpec(shape, lambda i: (0, 0), pipeline_mode=pl.Buffered(1))

    a_shapes, a_specs = [], []
    for _, dil in ATT_PATTERNS[1:]:
        a_shapes.append(jax.ShapeDtypeStruct((bsz, tiles_per_seq, dil, tm // dil, qw), BF16))
        a_specs.append(pl.BlockSpec((None, None, dil, tm // dil, qw),
                                    lambda i: (i // tiles_per_seq, i % tiles_per_seq, 0, 0, 0)))
    return pl.pallas_call(
        _in_kernel,
        out_shape=(
            jax.ShapeDtypeStruct((n, SSD_INNER), BF16),
            jax.ShapeDtypeStruct((n, SSD_CONV_DIM), F32),
            jax.ShapeDtypeStruct((n, LANES), F32),
            jax.ShapeDtypeStruct((n, qw), BF16),
            a_shapes[0], a_shapes[1],
            jax.ShapeDtypeStruct((n, IN_GATE), BF16),
        ),
        grid=(n // tm,),
        in_specs=[
            pl.BlockSpec((tm, d), lambda i: (i, 0)),
            pl.BlockSpec((None, 1, 2 * d), lambda i: (i // tiles_per_seq, 0, 0)),
            const((1, d)),
            pl.BlockSpec((None,) + w_chunks.shape[1:], lambda i: (layer, 0, 0, 0), pipeline_mode=pl.Buffered(1)),
            const(wd.shape),
        ],
        out_specs=(
            pl.BlockSpec((tm, SSD_INNER), lambda i: (i, 0)),
            pl.BlockSpec((tm, SSD_CONV_DIM), lambda i: (i, 0)),
            pl.BlockSpec((tm, LANES), lambda i: (i, 0)),
            pl.BlockSpec((tm, qw), lambda i: (i, 0)),
            a_specs[0], a_specs[1],
            pl.BlockSpec((tm, IN_GATE), lambda i: (i, 0)),
        ),
        scratch_shapes=[pltpu.VMEM((d // LANES, tm, LANES), F32)],
        compiler_params=_cparams(("arbitrary",)),
        name="in_proj_prompt",
    )(x, mod, norm_g, w_chunks, wd)


def _in_small_kernel(x_ref, mod_ref, g_ref, w_ref, o_ref):
    sh = mod_ref[:, 0:D_MODEL]
    sc = mod_ref[:, D_MODEL:2 * D_MODEL]
    h = _rms_modulate(x_ref[...], g_ref[...], sh, sc)
    h_hi, h_lo = _split2(h)
    w_hi, w_lo = _split2(w_ref[...])
    o_ref[...] = _dot_nt(h_hi, w_hi) + (_dot_nt(h_lo, w_hi) + _dot_nt(h_hi, w_lo))


def _in_proj_sample(layer, x, mod, norm_g, w_in_t, tn=1024):
    m, d = x.shape
    width = w_in_t.shape[1]
    return pl.pallas_call(
        _in_small_kernel,
        out_shape=jax.ShapeDtypeStruct((m, width), F32),
        grid=(pl.cdiv(width, tn),),
        in_specs=[
            pl.BlockSpec((m, d), lambda j: (0, 0)),
            pl.BlockSpec((m, 2 * d), lambda j: (0, 0)),
            pl.BlockSpec((1, d), lambda j: (0, 0)),
            pl.BlockSpec((None, tn, d), lambda j: (layer, j, 0)),
        ],
        out_specs=pl.BlockSpec((m, tn), lambda j: (0, j)),
        compiler_params=_cparams(("arbitrary",)),
        name="in_proj_sample",
    )(x, mod, norm_g, w_in_t)


_SSD_SUB = 2


def _ssd_kernel(xbc_ref, z_ref, dt_ref, cw_ref, cb_ref, dtb_ref, alog_ref, dskip_ref, ng_ref,
                yn_ref, hout_ref, h_scr, xp_scr):
    q = SSD_CHUNK
    c = pl.program_id(1)

    @pl.when(c == 0)
    def _():
        h_scr[...] = jnp.zeros_like(h_scr)
        xp_scr[0:8, :] = jnp.zeros((8, SSD_CONV_DIM), F32)

    for sub in range(_SSD_SUB):
        rows = slice(sub * q, (sub + 1) * q)
        _ssd_chunk(xbc_ref.at[rows], z_ref.at[rows], dt_ref.at[rows], cw_ref, cb_ref, dtb_ref, alog_ref,
                   dskip_ref, ng_ref, yn_ref.at[rows], h_scr, xp_scr)

    @pl.when(c == pl.num_programs(1) - 1)
    def _():
        hout_ref[...] = h_scr[...]


def _ssd_chunk(xbc_ref, z_ref, dt_ref, cw_ref, cb_ref, dtb_ref, alog_ref, dskip_ref, ng_ref,
               yn_ref, h_scr, xp_scr):
    q = SSD_CHUNK
    xp_scr[8:8 + q, :] = xbc_ref[...]
    acc = cb_ref[...] + cw_ref[3:4, :] * xp_scr[8:8 + q, :]
    for k in range(SSD_CONV - 1):
        acc = acc + cw_ref[k:k + 1, :] * xp_scr[5 + k:5 + k + q, :]
    xp_scr[0:8, :] = xp_scr[q:q + 8, :]
    xc = _silu(acc)
    xs = xc[:, 0:SSD_INNER]
    xs_bf = xs.astype(BF16)

    lane_q = lax.broadcasted_iota(jnp.int32, (q, LANES), 1)
    row_q = lax.broadcasted_iota(jnp.int32, (q, LANES), 0)
    causal = row_q >= lane_q
    tri = jnp.where(causal, 1.0, 0.0).astype(BF16)
    tri_t = jnp.where(lane_q >= row_q, 1.0, 0.0).astype(BF16)
    e_heads = _head_expand_matrix(LANES, SSD_HEADS, SSD_HEAD_DIM)

    dt = _softplus(dt_ref[...] + dtb_ref[...])
    dt = jnp.where(lane_q < SSD_HEADS, dt, 0.0)
    a = -jnp.exp(alog_ref[...])
    d_a = dt * a
    cum = _mm01_left(tri, d_a)
    d_a_t = d_a.T
    dt_t = dt.T
    cum_t = _mm01(d_a_t, tri_t, 3)
    cum_last = cum[q - 1:q, :]
    exp_cum = jnp.exp(cum)
    dec_end = jnp.exp(cum_last - cum)
    stack = jnp.concatenate([exp_cum, dec_end * dt], axis=0)
    full = _mm01(stack, e_heads, 2)
    exp_cum_full = full[0:q]
    w_full = full[q:2 * q]
    chunk_dec_t = jnp.exp(cum_t[:, q - 1:q])

    lane_half = lax.broadcasted_iota(jnp.int32, (q, LANES), 1) < SSD_HEAD_DIM
    y_parts = []
    for g in range(SSD_GROUPS):
        b_off = SSD_INNER + g * SSD_STATE
        c_off = SSD_INNER + SSD_GROUPS * SSD_STATE + g * SSD_STATE
        bm = xc[:, b_off:b_off + SSD_STATE].astype(BF16)
        cm = xc[:, c_off:c_off + SSD_STATE].astype(BF16)
        cbm = _dot_nt(cm, bm)
        gw = SSD_HPG * SSD_HEAD_DIM
        g0 = g * gw
        yd = []
        for pair in range(SSD_HPG // 2):
            x_pair = xs_bf[:, g0 + pair * LANES:g0 + (pair + 1) * LANES]
            halves = []
            for hh in range(2):
                h = g * SSD_HPG + pair * 2 + hh
                seg = cum[:, h:h + 1] - cum_t[h:h + 1, :]
                dec = jnp.exp(jnp.where(causal, seg, -jnp.inf))
                m_h = (cbm * dec * dt_t[h:h + 1, :]).astype(BF16)
                halves.append(_dot(m_h, x_pair))
            yd.append(jnp.where(lane_half, halves[0], halves[1]))
        y_diag = jnp.concatenate(yd, axis=1)
        h_g = h_scr[g0:g0 + gw, :]
        y_off = _dot_nt(cm, h_g.astype(BF16)) * exp_cum_full[:, g0:g0 + gw]
        y_parts.append(y_diag + y_off)
        xw = (xs[:, g0:g0 + gw] * w_full[:, g0:g0 + gw])
        st = _dot(xw.T.astype(BF16), bm)
        for e in range(SSD_HPG):
            h = g * SSD_HPG + e
            r0 = g0 + e * SSD_HEAD_DIM
            h_scr[r0:r0 + SSD_HEAD_DIM, :] = (h_scr[r0:r0 + SSD_HEAD_DIM, :] * chunk_dec_t[h:h + 1, :]
                                              + st[e * SSD_HEAD_DIM:(e + 1) * SSD_HEAD_DIM, :])

    y = jnp.concatenate(y_parts, axis=1) + dskip_ref[...] * xs
    y = y * _silu(z_ref[...].astype(F32))
    outs = []
    for g in range(SSD_GROUPS):
        gw = SSD_HPG * SSD_HEAD_DIM
        yg = y[:, g * gw:(g + 1) * gw]
        ms = jnp.mean(yg * yg, axis=-1, keepdims=True)
        outs.append(yg * lax.rsqrt(ms + EPS) * ng_ref[:, g * gw:(g + 1) * gw])
    yn_ref[...] = jnp.concatenate(outs, axis=1).astype(yn_ref.dtype)


def _mm01_left(tri01, a):
    hi, mid, lo = _split3(a)
    return _dot(tri01, hi) + (_dot(tri01, mid) + _dot(tri01, lo))


def _ssd_prompt(xbc, z, dt_raw, conv_w, conv_b, dt_bias, a_log, d_skip_full, ssd_norm_g, bsz, seq):
    q = SSD_CHUNK * _SSD_SUB
    nc = seq // q
    row = lambda b, c: (b * nc + c, 0)
    const = lambda b, c: (0, 0)
    return pl.pallas_call(
        _ssd_kernel,
        out_shape=(
            jax.ShapeDtypeStruct((bsz * seq, SSD_INNER), BF16),
            jax.ShapeDtypeStruct((bsz, SSD_INNER, SSD_STATE), F32),
        ),
        grid=(bsz, nc),
        in_specs=[
            pl.BlockSpec((q, SSD_CONV_DIM), row),
            pl.BlockSpec((q, SSD_INNER), row),
            pl.BlockSpec((q, LANES), row),
            pl.BlockSpec((SSD_CONV, SSD_CONV_DIM), const),
            pl.BlockSpec((1, SSD_CONV_DIM), const),
            pl.BlockSpec((1, LANES), const),
            pl.BlockSpec((1, LANES), const),
            pl.BlockSpec((1, SSD_INNER), const),
            pl.BlockSpec((1, SSD_INNER), const),
        ],
        out_specs=(
            pl.BlockSpec((q, SSD_INNER), row),
            pl.BlockSpec((None, SSD_INNER, SSD_STATE), lambda b, c: (b, 0, 0)),
        ),
        scratch_shapes=[
            pltpu.VMEM((SSD_INNER, SSD_STATE), F32),
            pltpu.VMEM((SSD_CHUNK + 8, SSD_CONV_DIM), F32),
        ],
        compiler_params=_cparams(("arbitrary", "arbitrary")),
        name="ssd_prompt",
    )(xbc, z, dt_raw, conv_w, conv_b, dt_bias, a_log, d_skip_full, ssd_norm_g)


def _attn_kernel(q_ref, kc_ref, kp_ref, vc_ref, vp_ref, o_out_ref, lse_out_ref,
                 q_ref_s, kwin, vwin, o_ref, lse_ref, *, tq, band):
    blk = ATT_BLOCK
    j = pl.program_id(2)
    q_ref_s[...] = (q_ref[...] * (ATT_HEAD_DIM ** -0.5)).astype(BF16).reshape(tq, ATT_OUT)
    kwin[0:blk, :] = kp_ref[...].reshape(blk, ATT_OUT)
    kwin[blk:blk + tq, :] = kc_ref[...].reshape(tq, ATT_OUT)
    vwin[0:blk, :] = vp_ref[...].reshape(blk, ATT_OUT)
    vwin[blk:blk + tq, :] = vc_ref[...].reshape(tq, ATT_OUT)
    q_ref = q_ref_s

    qi = lax.broadcasted_iota(jnp.int32, (blk, 2 * blk), 0)
    ki = lax.broadcasted_iota(jnp.int32, (blk, 2 * blk), 1)
    dist = qi + blk - ki
    in_band = (dist >= 0) & (dist <= band)
    lane = lax.broadcasted_iota(jnp.int32, (blk, LANES), 1)
    lane_half = lane < ATT_HEAD_DIM
    scale = ATT_HEAD_DIM ** -0.5
    zero_bf = jnp.zeros((blk, LANES), BF16)

    for i in range(tq // blk):
        if i == 0:
            valid = in_band & ((ki >= blk) | (j > 0))
        else:
            valid = in_band
        lse_tile = jnp.zeros((blk, LANES), F32)
        for hp in range(ATT_HEADS // 2):
            c0 = hp * LANES
            q_pair = q_ref[i * blk:(i + 1) * blk, c0:c0 + LANES]
            k_pair = kwin[i * blk:(i + 2) * blk, c0:c0 + LANES]
            v_pair = vwin[i * blk:(i + 2) * blk, c0:c0 + LANES]
            halves = []
            for hh in range(2):
                q_m = jnp.where(lane_half if hh == 0 else jnp.logical_not(lane_half), q_pair, zero_bf)
                s = _dot_nt(q_m, k_pair)
                s = jnp.where(valid, s, -jnp.inf)
                m = jnp.max(s, axis=-1, keepdims=True)
                e = jnp.exp(s - m)
                den = jnp.sum(e, axis=-1, keepdims=True)
                pv = _dot(e.astype(BF16), v_pair)
                halves.append(pv / den)
                lse = m + jnp.log(den)
                head = hp * 2 + hh
                lse_tile = jnp.where((lane == head) | (lane == ATT_HEADS + head), lse, lse_tile)
            o_ref[i * blk:(i + 1) * blk, c0:c0 + LANES] = jnp.where(lane_half, halves[0], halves[1]).astype(o_ref.dtype)
        lse_ref[i * blk:(i + 1) * blk, :] = lse_tile
    o_out_ref[...] = o_ref[...].reshape(o_out_ref.shape)
    lse_out_ref[...] = lse_ref[...].reshape(lse_out_ref.shape)


def _attn_prompt(a_g, gi, bsz, seq):
    win, dil = ATT_PATTERNS[gi]
    band = win // dil
    blk = ATT_BLOCK
    rows = TOK_TILE // dil
    tiles = seq // TOK_TILE
    length = seq // dil
    tq = min(512, length)
    nj = length // tq
    tq_tiles = tq // rows
    cur = lambda which: pl.BlockSpec((None, tq_tiles, None, rows, ATT_OUT),
                                     lambda b, r, j: (b, j, r, 0, which))
    if rows >= blk:
        prev = lambda which: pl.BlockSpec(
            (None, None, None, blk, ATT_OUT),
            lambda b, r, j: (b, jnp.maximum(j * tq_tiles - 1, 0), r, rows // blk - 1, which))
    else:
        prev = lambda which: pl.BlockSpec(
            (None, blk // rows, None, rows, ATT_OUT),
            lambda b, r, j: (b, jnp.maximum(j * (tq // blk) - 1, 0), r, 0, which))
    return pl.pallas_call(
        functools.partial(_attn_kernel, tq=tq, band=band),
        out_shape=(
            jax.ShapeDtypeStruct((bsz, tiles, dil, rows, ATT_OUT), BF16),
            jax.ShapeDtypeStruct((bsz, tiles, dil, rows, LANES), F32),
        ),
        grid=(bsz, dil, nj),
        in_specs=[cur(0), cur(1), prev(1), cur(2), prev(2)],
        out_specs=(
            pl.BlockSpec((None, tq_tiles, None, rows, ATT_OUT), lambda b, r, j: (b, j, r, 0, 0)),
            pl.BlockSpec((None, tq_tiles, None, rows, LANES), lambda b, r, j: (b, j, r, 0, 0)),
        ),
        scratch_shapes=[
            pltpu.VMEM((tq, ATT_OUT), BF16),
            pltpu.VMEM((blk + tq, ATT_OUT), BF16),
            pltpu.VMEM((blk + tq, ATT_OUT), BF16),
            pltpu.VMEM((tq, ATT_OUT), BF16),
            pltpu.VMEM((tq, LANES), F32),
        ],
        compiler_params=_cparams(("arbitrary", "arbitrary", "arbitrary")),
        name=f"attn_prompt_w{win}",
    )(a_g, a_g, a_g, a_g, a_g)


def _router_gates(logits):
    shape = logits.shape
    lane = lax.broadcasted_iota(jnp.int32, shape, 1)
    big = jnp.int32(1 << 20)
    neg = -jnp.inf
    is_grp = (lane >= MOE_EXPERTS) & (lane < MOE_EXPERTS + MOE_GROUPS)
    lg = jnp.where(is_grp, logits, neg)
    gm = jnp.max(lg, axis=-1, keepdims=True)
    g_lane = jnp.min(jnp.where(lg == gm, lane, big), axis=-1, keepdims=True)
    g_sum = jnp.sum(jnp.exp(lg - gm), axis=-1, keepdims=True)
    g_w = 1.0 / g_sum
    lo = (g_lane - MOE_EXPERTS) * MOE_PER_GROUP
    in_grp = (lane >= lo) & (lane < lo + MOE_PER_GROUP)
    le = jnp.where(in_grp, logits, neg)
    m1 = jnp.max(le, axis=-1, keepdims=True)
    i1 = jnp.min(jnp.where(le == m1, lane, big), axis=-1, keepdims=True)
    le2 = jnp.where(lane == i1, neg, le)
    m2 = jnp.max(le2, axis=-1, keepdims=True)
    i2 = jnp.min(jnp.where(le2 == m2, lane, big), axis=-1, keepdims=True)
    t = jnp.exp(m2 - m1)
    w1 = 1.0 / (1.0 + t)
    w2 = t / (1.0 + t)
    gate = jnp.where(lane == i1, g_w * w1, jnp.where(lane == i2, g_w * w2, 0.0))
    e_lo = jnp.minimum(i1, i2)
    e_hi = jnp.maximum(i1, i2)
    lo_l = e_lo & (MOE_PER_GROUP - 1)
    hi_l = e_hi & (MOE_PER_GROUP - 1)
    cls = (e_lo >> 2) * MOE_PAIRS + ((lo_l * (7 - lo_l)) >> 1) + (hi_l - lo_l - 1)
    return gate, cls


def _post_kernel(*refs, passes, merge_attn):
    if merge_attn:
        (x_ref, yn_ref, o0_ref, o1_ref, o2_ref, l0_ref, l1_ref, l2_ref, gates_ref, mod_ref, n2_ref,
         wssd_ref, wattn_ref, wout_ref, wr_ref, br_ref, x1_ref, h2_ref, pos_ref, cnt_ref,
         o1_scr, o2_scr, l1_scr, l2_scr) = refs
        for src_ref, dst_ref in ((o1_ref, o1_scr), (o2_ref, o2_scr), (l1_ref, l1_scr), (l2_ref, l2_scr)):
            dil, r_len, width = src_ref.shape
            for r in range(dil):
                blk_r = src_ref[r].astype(F32)
                for cbk in range(width // LANES):
                    dst_ref[cbk, pl.ds(r, r_len, stride=dil), :] = blk_r[:, cbk * LANES:(cbk + 1) * LANES]
        o_nat = [o0_ref[...].astype(F32)] + [
            jnp.concatenate([scr[cbk] for cbk in range(ATT_OUT // LANES)], axis=1) for scr in (o1_scr, o2_scr)]
        l0, l1, l2 = l0_ref[...], l1_scr[0], l2_scr[0]
        mx = jnp.maximum(jnp.maximum(l0, l1), l2)
        e0, e1, e2 = jnp.exp(l0 - mx), jnp.exp(l1 - mx), jnp.exp(l2 - mx)
        inv = 1.0 / (e0 + e1 + e2)
        lane = lax.broadcasted_iota(jnp.int32, l0.shape, 1)
        e8 = _head_expand_matrix(LANES, ATT_HEADS, ATT_HEAD_DIM)
        r = lax.broadcasted_iota(jnp.int32, e8.shape, 0)
        c = lax.broadcasted_iota(jnp.int32, e8.shape, 1)
        e8 = jnp.where((c // ATT_HEAD_DIM) == (r - ATT_HEADS), 1.0, e8.astype(F32)).astype(BF16)
        o = None
        for e_g, o_g in zip((e0, e1, e2), o_nat):
            w = e_g * inv
            hi, lo = _split2(w)
            w_exp = _dot(jnp.where(lane < ATT_HEADS, hi, lo), e8)
            term = w_exp * o_g
            o = term if o is None else o + term
    else:
        (x_ref, yn_ref, o_ref, gates_ref, mod_ref, n2_ref,
         wssd_ref, wattn_ref, wout_ref, wr_ref, br_ref, x1_ref, h2_ref, gate_ref) = refs
        o = o_ref[...]
    d = D_MODEL
    g1 = mod_ref[:, 0:d]
    sh2 = mod_ref[:, d:2 * d]
    sc2 = mod_ref[:, 2 * d:3 * d]
    ssd_branch = _mm(yn_ref[...], wssd_ref[...], passes)
    attn_branch = _mm(o, wattn_ref[...], passes)
    ga = gates_ref[:, 0:d].astype(F32)
    gb = gates_ref[:, d:2 * d].astype(F32)
    mixed = _sigmoid(ga) * ssd_branch + _sigmoid(gb) * attn_branch
    x1 = x_ref[...] + g1 * _mm(mixed, wout_ref[...], passes)
    x1_ref[...] = x1
    h2 = _rms_modulate(x1, n2_ref[...], sh2, sc2)
    logits = _mm(h2, wr_ref[...], 3) + br_ref[...]
    gate, cls = _router_gates(logits)
    if not merge_attn:
        h2_ref[...] = h2
        gate_ref[...] = gate
        return
    tm = h2.shape[0]
    for cbk in range(ROW_SUB):
        h2_ref[pl.ds(cbk, tm, stride=ROW_SUB), :] = h2[:, cbk * LANES:(cbk + 1) * LANES]
    step = pl.program_id(0)

    @pl.when(step == 0)
    def _():
        cnt_ref[...] = jnp.zeros_like(cnt_ref)

    lane = lax.broadcasted_iota(jnp.int32, (tm, LANES), 1)
    onehot = lane == cls
    ri = lax.broadcasted_iota(jnp.int32, (tm, tm), 0)
    ci = lax.broadcasted_iota(jnp.int32, (tm, tm), 1)
    before = jnp.where(ci < ri, 1.0, 0.0).astype(BF16)
    seen = _dot(before, jnp.where(onehot, 1.0, 0.0).astype(BF16)) + cnt_ref[...]
    rank = jnp.sum(jnp.where(onehot, seen, 0.0), axis=-1, keepdims=True)
    n_tokens = tm * pl.num_programs(0)
    pos_ref[...] = jnp.broadcast_to(cls.astype(F32) * n_tokens + rank, (tm, LANES))
    cnt_ref[...] = cnt_ref[...] + jnp.sum(jnp.where(onehot, 1.0, 0.0), axis=0, keepdims=True)


def _post(x, yn, attn_in, gates, mod, norm2_g, wssd, wattn, wout, wr, br, *, passes, merge_attn,
          tm, rows_per_mod, h2_dtype):
    n, d = x.shape
    per_row_mod = rows_per_mod == 1
    if per_row_mod:
        mod_spec = pl.BlockSpec((tm, 3 * d), lambda i: (i, 0))
    else:
        tiles = rows_per_mod // tm
        mod_spec = pl.BlockSpec((None, 1, 3 * d), lambda i: (i // tiles, 0, 0))
    row = lambda w: pl.BlockSpec((tm, w), lambda i: (i, 0))
    const = lambda a: pl.BlockSpec(a.shape, lambda i: (0, 0), pipeline_mode=pl.Buffered(1))
    scratch = []
    if merge_attn:
        assert tm == TOK_TILE
        o_list, lse_list = attn_in
        attn_args = list(o_list) + list(lse_list)
        tiles = rows_per_mod // tm

        def tile_spec(a):
            _, _, dil, r_len, w = a.shape
            if dil == 1:
                return pl.BlockSpec((None, None, None, r_len, w), lambda i: (i // tiles, i % tiles, 0, 0, 0))
            return pl.BlockSpec((None, None, dil, r_len, w), lambda i: (i // tiles, i % tiles, 0, 0, 0))

        attn_specs = [tile_spec(a) for a in attn_args]
        scratch = [pltpu.VMEM((ATT_OUT // LANES, tm, LANES), F32), pltpu.VMEM((ATT_OUT // LANES, tm, LANES), F32),
                   pltpu.VMEM((1, tm, LANES), F32), pltpu.VMEM((1, tm, LANES), F32)]
    else:
        attn_args = [attn_in]
        attn_specs = [row(ATT_OUT)]
    if merge_attn:
        out_shape = (jax.ShapeDtypeStruct((n, d), F32), jax.ShapeDtypeStruct((n * ROW_SUB, LANES), F32),
                     jax.ShapeDtypeStruct((n, LANES), F32), jax.ShapeDtypeStruct((1, LANES), F32))
        out_specs = (row(d), pl.BlockSpec((tm * ROW_SUB, LANES), lambda i: (i, 0)), row(LANES),
                     pl.BlockSpec((1, LANES), lambda i: (0, 0)))
    else:
        out_shape = (jax.ShapeDtypeStruct((n, d), F32), jax.ShapeDtypeStruct((n, d), h2_dtype),
                     jax.ShapeDtypeStruct((n, LANES), F32))
        out_specs = (row(d), row(d), row(LANES))
    return pl.pallas_call(
        functools.partial(_post_kernel, passes=passes, merge_attn=merge_attn),
        out_shape=out_shape,
        grid=(n // tm,),
        in_specs=[row(d), row(SSD_INNER)] + attn_specs + [row(IN_GATE), mod_spec, const(norm2_g),
                                                          const(wssd), const(wattn), const(wout),
                                                          const(wr), const(br)],
        out_specs=out_specs,
        scratch_shapes=scratch,
        compiler_params=_cparams(("arbitrary",)),
        name="post_merge" if merge_attn else "post_sample",
    )(x, yn, *attn_args, gates, mod, norm2_g, wssd, wattn, wout, wr, br)


def _moe_kernel(h2_ref, gate_ref, x1_ref, g2_ref, wg_ref, wu_ref, wd_ref, x2_ref, acc_ref, *, passes):
    e = pl.program_id(1)

    @pl.when(e == 0)
    def _():
        acc_ref[...] = jnp.zeros_like(acc_ref)

    h2 = h2_ref[...]
    hg = _mm(h2, wg_ref[...], passes)
    hu = _mm(h2, wu_ref[...], passes)
    gate = gate_ref[...]
    lane = lax.broadcasted_iota(jnp.int32, gate.shape, 1)
    gcol = jnp.sum(jnp.where(lane == e, gate, 0.0), axis=-1, keepdims=True)
    act = _silu(hg) * hu * gcol
    acc_ref[...] += _mm(act, wd_ref[...], passes)

    @pl.when(e == pl.num_programs(1) - 1)
    def _():
        x2_ref[...] = x1_ref[...] + g2_ref[...] * acc_ref[...]


def _moe(layer, h2, gate, x1, g2, wg, wu, wd, *, passes, tm, rows_per_mod):
    n, d = x1.shape
    if rows_per_mod == 1:
        g2_spec = pl.BlockSpec((tm, d), lambda i, e: (i, 0))
    else:
        tiles = rows_per_mod // tm
        g2_spec = pl.BlockSpec((None, 1, d), lambda i, e: (i // tiles, 0, 0))
    row = lambda w: pl.BlockSpec((tm, w), lambda i, e: (i, 0))
    return pl.pallas_call(
        functools.partial(_moe_kernel, passes=passes),
        out_shape=jax.ShapeDtypeStruct((n, d), F32),
        grid=(n // tm, MOE_EXPERTS),
        in_specs=[
            row(d), row(LANES), row(d), g2_spec,
            pl.BlockSpec((None, None, d, MOE_FF), lambda i, e: (layer, e, 0, 0)),
            pl.BlockSpec((None, None, d, MOE_FF), lambda i, e: (layer, e, 0, 0)),
            pl.BlockSpec((None, None, MOE_FF, d), lambda i, e: (layer, e, 0, 0)),
        ],
        out_specs=row(d),
        scratch_shapes=[pltpu.VMEM((tm, d), F32)],
        compiler_params=_cparams(("arbitrary", "arbitrary")),
        name="moe_dense",
    )(h2, gate, x1, g2, wg, wu, wd)


_PAIR_LO = (0, 0, 0, 1, 1, 2)
_PAIR_HI = (1, 2, 3, 2, 3, 3)
_DMA_UNROLL = 8
_ROUTE_TILE = 2048


def _moe_plan(counts, n_tokens):
    tm = MOE_TILE
    blocks_per_class = n_tokens // tm
    ntile = (counts + tm - 1) // tm
    cum = jnp.cumsum(ntile)
    total = cum[-1]
    t_max = blocks_per_class + MOE_CLASSES
    t_eff = jnp.minimum(jnp.arange(t_max, dtype=jnp.int32), total - 1)
    cls = jnp.searchsorted(cum, t_eff, side="right").astype(jnp.int32)
    blk = cls * blocks_per_class + (t_eff - (cum - ntile)[cls])
    grp, pair = cls // MOE_PAIRS, cls % MOE_PAIRS
    e_lo = grp * MOE_PER_GROUP + jnp.asarray(_PAIR_LO, jnp.int32)[pair]
    e_hi = grp * MOE_PER_GROUP + jnp.asarray(_PAIR_HI, jnp.int32)[pair]
    z_need = ((counts % tm) != 0).astype(jnp.int32)
    z_blk = jnp.arange(MOE_CLASSES, dtype=jnp.int32) * blocks_per_class + jnp.maximum(ntile - 1, 0)
    return (blk.astype(jnp.int32), e_lo, e_hi, total.reshape(1).astype(jnp.int32), z_need, z_blk.astype(jnp.int32))


def _row_copy(src, src_row, dst, dst_row, sem):
    return pltpu.make_async_copy(src.at[pl.ds(pl.multiple_of(src_row * ROW_SUB, ROW_SUB), ROW_SUB), :],
                                 dst.at[pl.ds(pl.multiple_of(dst_row * ROW_SUB, ROW_SUB), ROW_SUB), :], sem)


def _dispatch_kernel(pos_ref, zneed_ref, zblk_ref, rows_ref, xs_ref, zero_scr, sem, zsem, *, tm):
    step = pl.program_id(0)
    tile_rows = MOE_TILE * ROW_SUB

    def zero_copy(c):
        start = pl.multiple_of(zblk_ref[c] * tile_rows, tile_rows)
        return pltpu.make_async_copy(zero_scr, xs_ref.at[pl.ds(start, tile_rows), :], zsem)

    @pl.when(step == 0)
    def _():
        zero_scr[...] = jnp.zeros_like(zero_scr)
        for c in range(MOE_CLASSES):
            @pl.when(zneed_ref[c] != 0)
            def _():
                zero_copy(c).start()
        for c in range(MOE_CLASSES):
            @pl.when(zneed_ref[c] != 0)
            def _():
                zero_copy(c).wait()

    base = step * tm

    def body(i8, carry):
        for u in range(_DMA_UNROLL):
            i = i8 * _DMA_UNROLL + u
            _row_copy(rows_ref, i, xs_ref, pos_ref[base + i], sem).start(priority=u % 2)
        return carry

    lax.fori_loop(0, tm // _DMA_UNROLL, body, 0)
    pltpu.make_async_copy(rows_ref, xs_ref.at[pl.ds(0, tm * ROW_SUB), :], sem).wait()


def _dispatch(pos, z_need, z_blk, rows, n_tokens, tm=_ROUTE_TILE):
    sorted_rows = MOE_CLASSES * n_tokens * ROW_SUB
    return pl.pallas_call(
        functools.partial(_dispatch_kernel, tm=tm),
        out_shape=jax.ShapeDtypeStruct((sorted_rows, LANES), F32),
        grid_spec=pltpu.PrefetchScalarGridSpec(
            num_scalar_prefetch=3,
            grid=(n_tokens // tm,),
            in_specs=[pl.BlockSpec((tm * ROW_SUB, LANES), lambda i, *_: (i, 0))],
            out_specs=pl.BlockSpec(memory_space=pl.ANY),
            scratch_shapes=[pltpu.VMEM((MOE_TILE * ROW_SUB, LANES), F32),
                            pltpu.SemaphoreType.DMA(()), pltpu.SemaphoreType.DMA(())],
        ),
        compiler_params=pltpu.CompilerParams(dimension_semantics=("arbitrary",), vmem_limit_bytes=VMEM_LIMIT,
                                             disable_bounds_checks=True),
        name="moe_dispatch",
    )(pos, z_need, z_blk, rows)


def _moe_sparse_kernel(blk_ref, elo_ref, ehi_ref, nact_ref, xs_ref, wr_ref, br_ref,
                       wg_lo_ref, wu_lo_ref, wd_lo_ref, wg_hi_ref, wu_hi_ref, wd_hi_ref, ys_ref):
    t = pl.program_id(0)
    tm = MOE_TILE

    @pl.when(t < nact_ref[0])
    def _():
        x = jnp.concatenate([xs_ref[pl.ds(c, tm, stride=ROW_SUB), :] for c in range(ROW_SUB)], axis=1)
        xb = x.astype(BF16)
        w_hi16, w_lo16 = _split2(wr_ref[...])
        logits = _dot(xb, w_hi16) + _dot(xb, w_lo16) + br_ref[...]
        lane = lax.broadcasted_iota(jnp.int32, logits.shape, 1)
        e_lo, e_hi = elo_ref[t], ehi_ref[t]
        g_lane = MOE_EXPERTS + (e_lo >> 2)
        pick = lambda ln: jnp.sum(jnp.where(lane == ln, logits, 0.0), axis=-1, keepdims=True)
        l_lo, l_hi, l_g = pick(e_lo), pick(e_hi), pick(g_lane)
        is_grp = (lane >= MOE_EXPERTS) & (lane < MOE_EXPERTS + MOE_GROUPS)
        g_w = 1.0 / jnp.sum(jnp.where(is_grp, jnp.exp(logits - l_g), 0.0), axis=-1, keepdims=True)
        w_a = g_w / (1.0 + jnp.exp(l_hi - l_lo))
        w_b = g_w / (1.0 + jnp.exp(l_lo - l_hi))
        bf = lambda w_ref: w_ref[...].astype(BF16)
        act_a = (_silu(_dot(xb, bf(wg_lo_ref))) * _dot(xb, bf(wu_lo_ref)) * w_a).astype(BF16)
        act_b = (_silu(_dot(xb, bf(wg_hi_ref))) * _dot(xb, bf(wu_hi_ref)) * w_b).astype(BF16)
        y = _dot(act_a, bf(wd_lo_ref)) + _dot(act_b, bf(wd_hi_ref))
        for c in range(ROW_SUB):
            ys_ref[pl.ds(c, tm, stride=ROW_SUB), :] = y[:, c * LANES:(c + 1) * LANES]


def _moe_sparse(layer, plan, xs, wr, br, wg, wu, wd, n_tokens):
    blk, e_lo, e_hi, n_act = plan
    tm = MOE_TILE
    t_max = blk.shape[0]
    d = D_MODEL
    tile = pl.BlockSpec((tm * ROW_SUB, LANES), lambda t, blk, lo, hi, na: (blk[t], 0))
    w_lo = lambda shape: pl.BlockSpec((None, None) + shape, lambda t, blk, lo, hi, na: (layer, lo[t], 0, 0))
    w_hi = lambda shape: pl.BlockSpec((None, None) + shape, lambda t, blk, lo, hi, na: (layer, hi[t], 0, 0))
    const = lambda a: pl.BlockSpec(a.shape, lambda t, *_: (0, 0))
    return pl.pallas_call(
        _moe_sparse_kernel,
        out_shape=jax.ShapeDtypeStruct(xs.shape, F32),
        grid_spec=pltpu.PrefetchScalarGridSpec(
            num_scalar_prefetch=4,
            grid=(t_max,),
            in_specs=[tile, const(wr), const(br),
                      w_lo((d, MOE_FF)), w_lo((d, MOE_FF)), w_lo((MOE_FF, d)),
                      w_hi((d, MOE_FF)), w_hi((d, MOE_FF)), w_hi((MOE_FF, d))],
            out_specs=tile,
        ),
        compiler_params=_cparams(("arbitrary",)),
        name="moe_sparse",
    )(blk, e_lo, e_hi, n_act, xs, wr, br, wg, wu, wd, wg, wu, wd)


def _combine_kernel(pos_ref, ys_ref, x1_ref, g2_ref, *rest, tm, final):
    if final:
        fg_ref, x2_ref, buf, sem = rest
    else:
        x2_ref, buf, sem = rest
    base = pl.program_id(0) * tm

    def body(i8, carry):
        for u in range(_DMA_UNROLL):
            i = i8 * _DMA_UNROLL + u
            _row_copy(ys_ref, pos_ref[base + i], buf, i, sem).start(priority=u % 2)
        return carry

    lax.fori_loop(0, tm // _DMA_UNROLL, body, 0)
    pltpu.make_async_copy(ys_ref.at[pl.ds(0, tm * ROW_SUB), :], buf, sem).wait()
    y = jnp.concatenate([buf[pl.ds(c, tm, stride=ROW_SUB), :] for c in range(ROW_SUB)], axis=1)
    x2 = x1_ref[...] + g2_ref[...] * y
    if final:
        ms = jnp.mean(x2 * x2, axis=-1, keepdims=True)
        x2 = x2 * lax.rsqrt(ms + EPS) * fg_ref[...]
    x2_ref[...] = x2


def _combine(pos, ys, x1, g2, rows_per_mod, final_g=None, tm=_ROUTE_TILE):
    n, d = x1.shape
    tiles = rows_per_mod // tm
    final = final_g is not None
    extra_specs = [pl.BlockSpec((1, d), lambda i, *_: (0, 0))] if final else []
    extra_args = [final_g] if final else []
    return pl.pallas_call(
        functools.partial(_combine_kernel, tm=tm, final=final),
        out_shape=jax.ShapeDtypeStruct((n, d), F32),
        grid_spec=pltpu.PrefetchScalarGridSpec(
            num_scalar_prefetch=1,
            grid=(n // tm,),
            in_specs=[pl.BlockSpec(memory_space=pl.ANY),
                      pl.BlockSpec((tm, d), lambda i, *_: (i, 0)),
                      pl.BlockSpec((None, 1, d), lambda i, *_: (i // tiles, 0, 0))] + extra_specs,
            out_specs=pl.BlockSpec((tm, d), lambda i, *_: (i, 0)),
            scratch_shapes=[pltpu.VMEM((tm * ROW_SUB, LANES), F32), pltpu.SemaphoreType.DMA(())],
        ),
        compiler_params=pltpu.CompilerParams(dimension_semantics=("arbitrary",), vmem_limit_bytes=VMEM_LIMIT,
                                             disable_bounds_checks=True),
        name="moe_combine",
    )(pos, ys, x1, g2, *extra_args)


def _final_kernel(x_ref, g_ref, o_ref):
    x = x_ref[...]
    ms = jnp.mean(x * x, axis=-1, keepdims=True)
    o_ref[...] = x * lax.rsqrt(ms + EPS) * g_ref[...]


def _final_norm(x, g, tm):
    n, d = x.shape
    return pl.pallas_call(
        _final_kernel,
        out_shape=jax.ShapeDtypeStruct((n, d), F32),
        grid=(n // tm,),
        in_specs=[pl.BlockSpec((tm, d), lambda i: (i, 0)), pl.BlockSpec((1, d), lambda i: (0, 0))],
        out_specs=pl.BlockSpec((tm, d), lambda i: (i, 0)),
        compiler_params=_cparams(("arbitrary",)),
        name="final_norm",
    )(x, g)


def _step_kernel(z_ref, xbc_ref, dt_ref, cst_ref, h_ref,
                 cw_ref, cb_ref, dtb_ref, alog_ref, dskip_ref, ng_ref, *rest):
    yn_ref, cnew_ref, hnew_ref, col_scr = rest[-4:]
    x_new = xbc_ref[...]
    acc = cb_ref[...] + cw_ref[3:4, :] * x_new
    for k in range(SSD_CONV - 1):
        acc = acc + cw_ref[k:k + 1, :] * cst_ref[k:k + 1, :]
    cnew_ref[0:1, :] = cst_ref[1:2, :]
    cnew_ref[1:2, :] = cst_ref[2:3, :]
    cnew_ref[2:3, :] = x_new
    xc = _silu(acc)
    xs = xc[:, 0:SSD_INNER]

    e_heads = _head_expand_matrix(LANES, SSD_HEADS, SSD_HEAD_DIM)
    dt_raw8 = jnp.broadcast_to(dt_ref[...], (8, LANES))
    dt_full = _softplus(_mm01(dt_raw8, e_heads, 3)[0:1, :] + dtb_ref[...])
    a_full = -jnp.exp(alog_ref[...])
    dec_full = jnp.exp(dt_full * a_full)
    xdt = xs * dt_full
    col_scr[...] = jnp.zeros_like(col_scr)
    col_scr[0:1, :] = xdt
    col_scr[1:2, :] = dec_full
    cols = col_scr[...].T
    y_parts = []
    gw = SSD_HPG * SSD_HEAD_DIM
    for g in range(SSD_GROUPS):
        b_off = SSD_INNER + g * SSD_STATE
        c_off = SSD_INNER + SSD_GROUPS * SSD_STATE + g * SSD_STATE
        bm = xc[:, b_off:b_off + SSD_STATE]
        cm = xc[:, c_off:c_off + SSD_STATE]
        g0 = g * gw
        hn = h_ref[g0:g0 + gw, :] * cols[g0:g0 + gw, 1:2] + cols[g0:g0 + gw, 0:1] * bm
        hnew_ref[g0:g0 + gw, :] = hn
        t = (hn * cm).T
        y_parts.append(jnp.sum(t, axis=0, keepdims=True))
    y = jnp.concatenate(y_parts, axis=1) + dskip_ref[...] * xs
    y = y * _silu(z_ref[...])
    outs = []
    for g in range(SSD_GROUPS):
        yg = y[:, g * gw:(g + 1) * gw]
        ms = jnp.mean(yg * yg, axis=-1, keepdims=True)
        outs.append(yg * lax.rsqrt(ms + EPS) * ng_ref[:, g * gw:(g + 1) * gw])
    yn_ref[...] = jnp.concatenate(outs, axis=1)


def _step_sample(layer, z, xbc, dt_raw, state_conv, state_ssm, prev_ssm,
                 conv_w, conv_b, dtb_full, alog_full, dskip_full, ssd_norm_g):
    bsz = z.shape[0]
    row3 = lambda w: pl.BlockSpec((None, 1, w), lambda b: (b, 0, 0))
    const = lambda a: pl.BlockSpec(a.shape, lambda b: (0,) * a.ndim)
    alias_args = [] if prev_ssm is None else [prev_ssm]
    n_in = 11
    return pl.pallas_call(
        _step_kernel,
        out_shape=(
            jax.ShapeDtypeStruct((bsz, 1, SSD_INNER), F32),
            jax.ShapeDtypeStruct((bsz, SSD_CONV - 1, SSD_CONV_DIM), F32),
            jax.ShapeDtypeStruct(state_ssm.shape, F32),
        ),
        grid=(bsz,),
        in_specs=[
            row3(SSD_INNER), row3(SSD_CONV_DIM), row3(LANES),
            pl.BlockSpec((None, None, SSD_CONV - 1, SSD_CONV_DIM), lambda b: (layer, b, 0, 0)),
            pl.BlockSpec((None, None, SSD_INNER, SSD_STATE), lambda b: (layer, b, 0, 0)),
            const(conv_w), const(conv_b), const(dtb_full), const(alog_full), const(dskip_full), const(ssd_norm_g),
        ] + [pl.BlockSpec(memory_space=pl.ANY)] * len(alias_args),
        out_specs=(
            row3(SSD_INNER),
            pl.BlockSpec((None, SSD_CONV - 1, SSD_CONV_DIM), lambda b: (b, 0, 0)),
            pl.BlockSpec((None, None, SSD_INNER, SSD_STATE), lambda b: (layer, b, 0, 0)),
        ),
        input_output_aliases={n_in: 2} if alias_args else {},
        scratch_shapes=[pltpu.VMEM((LANES, SSD_INNER), F32)],
        compiler_params=_cparams(("arbitrary",)),
        name="step_sample",
    )(z.reshape(bsz, 1, -1), xbc.reshape(bsz, 1, -1), dt_raw.reshape(bsz, 1, -1), state_conv, state_ssm,
      conv_w, conv_b, dtb_full, alog_full, dskip_full, ssd_norm_g, *alias_args)


_HEAD_SPLIT = 1
_HROWS = ATT_OUT // _HEAD_SPLIT


def _cache_attn_kernel(*refs, n_alias):
    qkv_ref = refs[0]
    cache_refs = refs[1:7]
    out_refs = refs[7 + n_alias:13 + n_alias]
    o_ref = refs[13 + n_alias]
    b = pl.program_id(0)
    nh = _HROWS // ATT_HEAD_DIM
    scale = ATT_HEAD_DIM ** -0.5
    qkv = qkv_ref[...]
    lane_b = lax.broadcasted_iota(jnp.int32, qkv.shape, 2)
    cols = jnp.sum(jnp.where(lane_b == b, qkv, 0.0), axis=-1, keepdims=True)

    def per_head_rows(v):
        return jnp.concatenate([jnp.broadcast_to(v[h:h + 1, :], (ATT_HEAD_DIM, 1)) for h in range(nh)], axis=0)

    o_g, lse_g = [], []
    for gi, (_, dil) in enumerate(ATT_PATTERNS):
        k_ref, v_ref = cache_refs[2 * gi], cache_refs[2 * gi + 1]
        ko_ref, vo_ref = out_refs[2 * gi], out_refs[2 * gi + 1]
        q = cols[gi]
        k_new = cols[ATT_GROUPS + gi]
        v_new = cols[2 * ATT_GROUPS + gi]
        kk = k_ref[...]
        vv = v_ref[...]
        length = kk.shape[1]
        lane = lax.broadcasted_iota(jnp.int32, (nh, length), 1)
        s = jnp.sum((kk * q).reshape(nh, ATT_HEAD_DIM, length), axis=1) * scale
        s = jnp.where((lane & (dil - 1)) == 0, s, -jnp.inf)
        s_new = jnp.sum((k_new * q).reshape(nh, ATT_HEAD_DIM, 1), axis=1) * scale
        m = jnp.maximum(jnp.max(s, axis=-1, keepdims=True), s_new)
        e = jnp.exp(s - m)
        e_new = jnp.exp(s_new - m)
        den = jnp.sum(e, axis=-1, keepdims=True) + e_new
        acc = jnp.sum(vv.reshape(nh, ATT_HEAD_DIM, length) * e[:, None, :], axis=-1, keepdims=True)
        acc = acc.reshape(_HROWS, 1) + per_head_rows(e_new) * v_new
        o_g.append(acc / per_head_rows(den))
        lse_g.append(per_head_rows(m + jnp.log(den)))
        lane_full = lax.broadcasted_iota(jnp.int32, kk.shape, 1)
        last = lane_full == length - 1
        ko_ref[...] = jnp.where(last, k_new, pltpu.roll(kk, length - 1, axis=1))
        vo_ref[...] = jnp.where(last, v_new, pltpu.roll(vv, length - 1, axis=1))
    mx = jnp.maximum(jnp.maximum(lse_g[0], lse_g[1]), lse_g[2])
    w = [jnp.exp(l - mx) for l in lse_g]
    tot = w[0] + w[1] + w[2]
    o_ref[...] = (w[0] / tot) * o_g[0] + (w[1] / tot) * o_g[1] + (w[2] / tot) * o_g[2]


def _cache_attn(layer, qkv_t, cache_views, prev_outs):
    depth, bsz, _, _ = cache_views[0].shape
    n_alias = 0 if prev_outs is None else len(prev_outs)
    q4 = qkv_t.reshape(3 * ATT_GROUPS, _HEAD_SPLIT, _HROWS, bsz)
    blk = lambda c: pl.BlockSpec((None, None, _HROWS, c.shape[3]), lambda b, hh: (layer, b, hh, 0))
    any_spec = pl.BlockSpec(memory_space=pl.ANY)
    args = [q4] + list(cache_views) + ([] if prev_outs is None else list(prev_outs))
    res = pl.pallas_call(
        functools.partial(_cache_attn_kernel, n_alias=n_alias),
        out_shape=tuple(jax.ShapeDtypeStruct(c.shape, c.dtype) for c in cache_views)
        + (jax.ShapeDtypeStruct((bsz, _HEAD_SPLIT, _HROWS, 1), F32),),
        grid=(bsz, _HEAD_SPLIT),
        in_specs=[pl.BlockSpec((3 * ATT_GROUPS, None, _HROWS, bsz), lambda b, hh: (0, hh, 0, 0))]
        + [blk(c) for c in cache_views] + [any_spec] * n_alias,
        out_specs=tuple(blk(c) for c in cache_views)
        + (pl.BlockSpec((None, None, _HROWS, 1), lambda b, hh: (b, hh, 0, 0)),),
        input_output_aliases={7 + i: i for i in range(n_alias)},
        compiler_params=_cparams(("arbitrary", "arbitrary")),
        name="cache_attn",
    )(*args)
    return list(res[:6]), res[6].reshape(bsz, ATT_OUT)


def _pad_lanes(v, width=LANES):
    return jnp.pad(v, [(0, 0)] * (v.ndim - 1) + [(0, width - v.shape[-1])])


def kernel(x_prompt, x_sample, c_prompt, c_sample, state_conv, state_ssm, cache_k_win128, cache_v_win128, cache_k_win512, cache_v_win512, cache_k_win2048, cache_v_win2048, norm1_g, w_ada, b_ada, w_in, conv_w, conv_b, dt_bias, a_log, d_skip, ssd_norm_g, w_ssd_proj, w_attn_proj, w_out, norm2_g, w_router_group, b_router_group, w_router_expert, b_router_expert, w_exp_gate, w_exp_up, w_exp_down, final_norm_g):
    depth = w_in.shape[0]
    bp, seq, d = x_prompt.shape
    bs = x_sample.shape[0]
    assert x_sample.shape[1] == 1 and d == D_MODEL and w_in.shape[2] == IN_WIDTH
    assert seq % (ATT_PATTERNS[-1][1] * ATT_BLOCK) == 0
    n_p = bp * seq
    caches = ((cache_k_win128, cache_v_win128), (cache_k_win512, cache_v_win512),
              (cache_k_win2048, cache_v_win2048))

    rows = bp + bs
    rows_pad = -(-rows // 8) * 8
    c_all = jnp.pad(jnp.concatenate([c_prompt, c_sample], axis=0), ((0, rows_pad - rows), (0, 0)))
    mods = _modulation(c_all, w_ada, b_ada)

    xp = x_prompt.reshape(n_p, d)
    xs = x_sample.reshape(bs, d)
    outs = {k: [] for k in ("conv_p", "conv_s", "ssm_p", "ssm_s")}
    kv_p = [[[], []] for _ in ATT_PATTERNS]
    tiles = seq // TOK_TILE
    assert bs <= LANES and all(c.shape[2] == win for (win, _), pair in zip(ATT_PATTERNS, caches) for c in pair)
    cache_views = [jnp.transpose(c, (0, 1, 3, 4, 2)).reshape(depth, bs, ATT_OUT, c.shape[2])
                   for pair in caches for c in pair]
    shifted = None
    ssm_s_all = None
    w_in_t = jnp.swapaxes(w_in, 1, 2)
    w_chunks = _prep_w_in(w_in_t)

    for l in range(depth):
        mod_p = mods[l, :bp]
        mod_s = mods[l, bp:bp + bs]
        wd = _pad_lanes(w_in_t[l, OFF_DT:OFF_QKV].T)
        g1n = norm1_g[l].reshape(1, d)
        g2n = norm2_g[l].reshape(1, d)
        cw = conv_w[l]
        cb = conv_b[l].reshape(1, -1)
        dskip_full = jnp.repeat(d_skip[l], SSD_HEAD_DIM).reshape(1, -1)
        ssd_g = ssd_norm_g[l].reshape(1, -1)
        w_router = _pad_lanes(jnp.concatenate([w_router_expert[l], w_router_group[l]], axis=1))
        b_router = _pad_lanes(jnp.concatenate([b_router_expert[l], b_router_group[l]], axis=0).reshape(1, -1))

        z, xbc, dt_raw, a0, a1, a2, gates = _in_proj_prompt(
            l, xp, mod_p[:, 0:2 * d].reshape(bp, 1, 2 * d), g1n, w_chunks, wd, bp, seq)
        yn, ssm_new = _ssd_prompt(xbc, z, dt_raw, cw, cb, _pad_lanes(dt_bias[l].reshape(1, -1)),
                                  _pad_lanes(a_log[l].reshape(1, -1)), dskip_full, ssd_g, bp, seq)
        outs["ssm_p"].append(ssm_new.reshape(bp, SSD_GROUPS, SSD_HPG, SSD_HEAD_DIM, SSD_STATE))
        outs["conv_p"].append(xbc.reshape(bp, seq, SSD_CONV_DIM)[:, seq - (SSD_CONV - 1):])
        a_groups = [a0.reshape(bp, tiles, 1, TOK_TILE, 3 * ATT_OUT), a1, a2]
        o_list, lse_list = [], []
        for gi, (win, dil) in enumerate(ATT_PATTERNS):
            o_g, lse_g = _attn_prompt(a_groups[gi], gi, bp, seq)
            o_list.append(o_g)
            lse_list.append(lse_g)
            keep = min(win, seq)
            rows = TOK_TILE // dil
            if keep >= TOK_TILE:
                nt = keep // TOK_TILE
                tail = a_groups[gi][:, tiles - nt:, :, :, ATT_OUT:3 * ATT_OUT]
                tail = jnp.transpose(tail, (0, 1, 3, 2, 4)).reshape(bp, keep, 2 * ATT_OUT)
            else:
                assert dil == 1
                tail = a_groups[gi][:, tiles - 1, 0, TOK_TILE - keep:, ATT_OUT:3 * ATT_OUT]
            tail = tail.astype(F32)
            kv_p[gi][0].append(tail[:, :, 0:ATT_OUT].reshape(bp, keep, ATT_HEADS, ATT_HEAD_DIM))
            kv_p[gi][1].append(tail[:, :, ATT_OUT:2 * ATT_OUT].reshape(bp, keep, ATT_HEADS, ATT_HEAD_DIM))
        mod_post = jnp.concatenate([mod_p[:, 2 * d:3 * d], mod_p[:, 3 * d:5 * d]], axis=1).reshape(bp, 1, 3 * d)
        x1, rows, slot, cnt = _post(xp, yn, (o_list, lse_list), gates, mod_post, g2n,
                                    w_ssd_proj[l].astype(BF16), w_attn_proj[l].astype(BF16),
                                    w_out[l].astype(BF16), w_router, b_router, passes=1, merge_attn=True,
                                    tm=TOK_TILE, rows_per_mod=seq, h2_dtype=F32)
        pos = slot[:, 0].astype(jnp.int32)
        blk, e_lo, e_hi, n_act, z_need, z_blk = _moe_plan(cnt[0, :MOE_CLASSES].astype(jnp.int32), n_p)
        xs_sorted = _dispatch(pos, z_need, z_blk, rows, n_p)
        ys_sorted = _moe_sparse(l, (blk, e_lo, e_hi, n_act), xs_sorted, w_router, b_router,
                                w_exp_gate, w_exp_up, w_exp_down, n_p)
        xp = _combine(pos, ys_sorted, x1, mod_p[:, 5 * d:6 * d].reshape(bp, 1, d), seq,
                      final_g=final_norm_g.reshape(1, d) if l == depth - 1 else None)

        u = _in_proj_sample(l, xs, mod_s[:, 0:2 * d], g1n, w_in_t)
        z_s = u[:, 0:OFF_XBC]
        xbc_s = u[:, OFF_XBC:OFF_DT]
        dt_s = _pad_lanes(u[:, OFF_DT:OFF_QKV])
        qkv_s = u[:, OFF_QKV:OFF_GATE]
        gates_s = u[:, OFF_GATE:IN_WIDTH]
        yn_s, conv_new, ssm_s_all = _step_sample(
            l, z_s, xbc_s, dt_s, state_conv, state_ssm.reshape(depth, bs, SSD_INNER, SSD_STATE), ssm_s_all,
            cw, cb, jnp.repeat(dt_bias[l], SSD_HEAD_DIM).reshape(1, -1),
            jnp.repeat(a_log[l], SSD_HEAD_DIM).reshape(1, -1), dskip_full, ssd_g)
        outs["conv_s"].append(conv_new)
        shifted, o_s = _cache_attn(l, qkv_s.T, cache_views, shifted)
        mod_post_s = jnp.concatenate([mod_s[:, 2 * d:3 * d], mod_s[:, 3 * d:5 * d]], axis=1)
        x1_s, h2_s, gate_s = _post(xs, yn_s.reshape(bs, SSD_INNER), o_s, gates_s, mod_post_s,
                                   g2n, w_ssd_proj[l], w_attn_proj[l], w_out[l], w_router, b_router,
                                   passes=3, merge_attn=False, tm=bs, rows_per_mod=1, h2_dtype=F32)
        xs = _moe(l, h2_s, gate_s, x1_s, mod_s[:, 5 * d:6 * d], w_exp_gate, w_exp_up, w_exp_down,
                  passes=3, tm=bs, rows_per_mod=1)

    fg = final_norm_g.reshape(1, d)
    y_prompt = xp.reshape(bp, seq, d)
    y_sample = _final_norm(xs, fg, bs).reshape(bs, 1, d)

    shifted = [jnp.transpose(s.reshape(depth, bs, ATT_HEADS, ATT_HEAD_DIM, s.shape[3]), (0, 1, 4, 2, 3))
               for s in shifted]

    st = jnp.stack
    res = [y_prompt, y_sample, st(outs["conv_p"]), st(outs["conv_s"]), st(outs["ssm_p"]),
           ssm_s_all.reshape(state_ssm.shape)]
    for gi in range(ATT_GROUPS):
        res += [st(kv_p[gi][0]), shifted[2 * gi], st(kv_p[gi][1]), shifted[2 * gi + 1]]
    return tuple(res)
```

```python
import functools
import math

import jax
import jax.numpy as jnp
from jax import lax
from jax.experimental import pallas as pl
from jax.experimental.pallas import tpu as pltpu

F32 = jnp.float32
BF16 = jnp.bfloat16

D_MODEL = 1024
SSD_INNER = 1024
SSD_HEAD_DIM = 64
SSD_HEADS = 16
SSD_GROUPS = 2
SSD_HPG = 8
SSD_STATE = 128
SSD_CONV = 4
SSD_CHUNK = 128
SSD_CONV_DIM = SSD_INNER + 2 * SSD_GROUPS * SSD_STATE
ATT_PATTERNS = ((128, 1), (512, 4), (2048, 16))
ATT_GROUPS = 3
ATT_HEADS = 8
ATT_HEAD_DIM = 64
ATT_BLOCK = 128
ATT_OUT = ATT_HEADS * ATT_HEAD_DIM
MOE_GROUPS = 4
MOE_PER_GROUP = 4
MOE_EXPERTS = 16
MOE_FF = 512
MOE_PAIRS = 6
MOE_CLASSES = MOE_GROUPS * MOE_PAIRS
MOE_TILE = 512
ROW_SUB = 8
IN_QKV = 3 * ATT_GROUPS * ATT_OUT
IN_GATE = 2 * D_MODEL
OFF_XBC = SSD_INNER
OFF_DT = OFF_XBC + SSD_CONV_DIM
OFF_QKV = OFF_DT + SSD_HEADS
OFF_GATE = OFF_QKV + IN_QKV
IN_WIDTH = OFF_GATE + IN_GATE
EPS = 1e-6

LANES = 128
VMEM_LIMIT = 56 * 1024 * 1024


def _cparams(sem, vmem=VMEM_LIMIT):
    return pltpu.CompilerParams(dimension_semantics=sem, vmem_limit_bytes=vmem)


def _split2(a):
    hi = a.astype(BF16)
    lo = (a - hi.astype(F32)).astype(BF16)
    return hi, lo


def _split3(a):
    hi = a.astype(BF16)
    r = a - hi.astype(F32)
    mid = r.astype(BF16)
    lo = (r - mid.astype(F32)).astype(BF16)
    return hi, mid, lo


def _dot(a, b):
    return jnp.dot(a, b, preferred_element_type=F32)


def _dot_nt(a, b):
    return lax.dot_general(a, b, (((1,), (1,)), ((), ())), preferred_element_type=F32)


def _mm(a, w, passes):
    if passes == 1:
        return _dot(a.astype(BF16), w.astype(BF16))
    a = a.astype(F32)
    w = w.astype(F32)
    a_hi, a_lo = _split2(a)
    w_hi, w_lo = _split2(w)
    return _dot(a_hi, w_hi) + (_dot(a_lo, w_hi) + _dot(a_hi, w_lo))


def _mm01(a, e01, terms):
    parts = _split3(a) if terms == 3 else _split2(a)
    out = _dot(parts[0], e01)
    for p in parts[1:]:
        out = out + _dot(p, e01)
    return out


def _sigmoid(x):
    return 1.0 / (1.0 + jnp.exp(-x))


def _silu(x):
    return x * _sigmoid(x)


def _softplus(x):
    return jnp.maximum(x, 0.0) + jnp.log(1.0 + jnp.exp(-jnp.abs(x)))


def _head_expand_matrix(rows, n_heads, width):
    r = lax.broadcasted_iota(jnp.int32, (rows, n_heads * width), 0)
    c = lax.broadcasted_iota(jnp.int32, (rows, n_heads * width), 1)
    return jnp.where((c // width) == r, 1.0, 0.0).astype(BF16)


def _rms_modulate(x, g, sh, sc):
    ms = jnp.mean(x * x, axis=-1, keepdims=True)
    y = x * lax.rsqrt(ms + EPS) * g
    return y * (1.0 + sc) + sh


def _mod_kernel(c_ref, w_ref, b_ref, o_ref):
    a = _silu(c_ref[...])
    o_ref[...] = _mm(a, w_ref[...], 3) + b_ref[...]


def _modulation(c_all, w_ada, b_ada):
    depth, d, n6 = w_ada.shape
    rows = c_all.shape[0]
    tn = 1024
    return pl.pallas_call(
        _mod_kernel,
        out_shape=jax.ShapeDtypeStruct((depth, rows, n6), F32),
        grid=(depth, n6 // tn),
        in_specs=[
            pl.BlockSpec((rows, d), lambda l, j: (0, 0)),
            pl.BlockSpec((None, d, tn), lambda l, j: (l, 0, j)),
            pl.BlockSpec((None, 1, tn), lambda l, j: (l, 0, j)),
        ],
        out_specs=pl.BlockSpec((None, rows, tn), lambda l, j: (l, 0, j)),
        compiler_params=_cparams(("arbitrary", "arbitrary")),
        name="adaln_mod",
    )(c_all, w_ada, b_ada.reshape(depth, 1, n6))


_IN_CHUNK = 512
TOK_TILE = 512

_W_CHUNK_ROWS = (tuple(range(0, OFF_DT, _IN_CHUNK))
                 + tuple(range(OFF_QKV, OFF_GATE, _IN_CHUNK)) + tuple(range(OFF_GATE, IN_WIDTH, _IN_CHUNK)))
_CH_Z = (0, 1)
_CH_XBC = (2, 3, 4)
_CH_QKV0 = 5
_CH_GATES = (14, 15, 16, 17)


def _w_prep_kernel(starts_ref, w_ref, o_ref):
    del starts_ref
    o_ref[...] = w_ref[0].T.astype(BF16)


def _prep_w_in(w_in_t):
    depth, _, d = w_in_t.shape
    n_ch = len(_W_CHUNK_ROWS)
    starts = jnp.asarray(_W_CHUNK_ROWS, jnp.int32)
    return pl.pallas_call(
        _w_prep_kernel,
        out_shape=jax.ShapeDtypeStruct((depth, n_ch, d, _IN_CHUNK), BF16),
        grid_spec=pltpu.PrefetchScalarGridSpec(
            num_scalar_prefetch=1,
            grid=(depth, n_ch),
            in_specs=[pl.BlockSpec((pl.Element(1), pl.Element(_IN_CHUNK), pl.Element(d)),
                                   lambda l, j, st: (l, pl.multiple_of(st[j], 16), 0))],
            out_specs=pl.BlockSpec((None, None, d, _IN_CHUNK), lambda l, j, st: (l, j, 0, 0)),
        ),
        compiler_params=_cparams(("arbitrary", "arbitrary")),
        name="w_in_prep",
    )(starts, w_in_t)


def _in_kernel(x_ref, mod_ref, g_ref, w_ref, wd_ref,
               z_ref, xbc_ref, dt_ref, a0_ref, a1_ref, a2_ref, gates_ref, h_scr):
    sh = mod_ref[:, 0:D_MODEL]
    sc = mod_ref[:, D_MODEL:2 * D_MODEL]
    hf = _rms_modulate(x_ref[...], g_ref[...], sh, sc)
    n_cb = D_MODEL // LANES
    for cbk in range(n_cb):
        h_scr[cbk] = hf[:, cbk * LANES:(cbk + 1) * LANES]
    h = hf.astype(BF16)
    qkv_chunks = lambda gi: tuple(_CH_QKV0 + which * ATT_GROUPS + gi for which in range(3))
    for chunks, o_ref in ((_CH_Z, z_ref), (_CH_XBC, xbc_ref), (qkv_chunks(0), a0_ref), (_CH_GATES, gates_ref)):
        for i, ch in enumerate(chunks):
            o_ref[:, i * _IN_CHUNK:(i + 1) * _IN_CHUNK] = _dot(h, w_ref[ch]).astype(o_ref.dtype)
    dt_ref[...] = _dot(h, wd_ref[...].astype(BF16))
    for gi, a_ref in ((1, a1_ref), (2, a2_ref)):
        d, r_len, _ = a_ref.shape
        hp = jnp.concatenate(
            [jnp.concatenate([h_scr[cbk, pl.ds(r, r_len, stride=d), :] for cbk in range(n_cb)], axis=1)
             for r in range(d)], axis=0).astype(BF16)
        for i, ch in enumerate(qkv_chunks(gi)):
            res = _dot(hp, w_ref[ch]).astype(BF16)
            a_ref[:, :, i * _IN_CHUNK:(i + 1) * _IN_CHUNK] = res.reshape(d, r_len, _IN_CHUNK)


def _in_proj_prompt(layer, x, mod, norm_g, w_chunks, wd, bsz, seq):
    n, d = x.shape
    tm = TOK_TILE
    tiles_per_seq = seq // tm
    qw = 3 * ATT_OUT

    def const(shape):
        return pl.BlockSpec(shape, lambda i: (0, 0), pipeline_mode=pl.Buffered(1))

    a_shapes, a_specs = [], []
    for _, dil in ATT_PATTERNS[1:]:
        a_shapes.append(jax.ShapeDtypeStruct((bsz, tiles_per_seq, dil, tm // dil, qw), BF16))
        a_specs.append(pl.BlockSpec((None, None, dil, tm // dil, qw),
                                    lambda i: (i // tiles_per_seq, i % tiles_per_seq, 0, 0, 0)))
    return pl.pallas_call(
        _in_kernel,
        out_shape=(
            jax.ShapeDtypeStruct((n, SSD_INNER), BF16),
            jax.ShapeDtypeStruct((n, SSD_CONV_DIM), F32),
            jax.ShapeDtypeStruct((n, LANES), F32),
            jax.ShapeDtypeStruct((n, qw), BF16),
            a_shapes[0], a_shapes[1],
            jax.ShapeDtypeStruct((n, IN_GATE), BF16),
        ),
        grid=(n // tm,),
        in_specs=[
            pl.BlockSpec((tm, d), lambda i: (i, 0)),
            pl.BlockSpec((None, 1, 2 * d), lambda i: (i // tiles_per_seq, 0, 0)),
            const((1, d)),
            pl.BlockSpec((None,) + w_chunks.shape[1:], lambda i: (layer, 0, 0, 0), pipeline_mode=pl.Buffered(1)),
            const(wd.shape),
        ],
        out_specs=(
            pl.BlockSpec((tm, SSD_INNER), lambda i: (i, 0)),
            pl.BlockSpec((tm, SSD_CONV_DIM), lambda i: (i, 0)),
            pl.BlockSpec((tm, LANES), lambda i: (i, 0)),
            pl.BlockSpec((tm, qw), lambda i: (i, 0)),
            a_specs[0], a_specs[1],
            pl.BlockSpec((tm, IN_GATE), lambda i: (i, 0)),
        ),
        scratch_shapes=[pltpu.VMEM((d // LANES, tm, LANES), F32)],
        compiler_params=_cparams(("arbitrary",)),
        name="in_proj_prompt",
    )(x, mod, norm_g, w_chunks, wd)


def _in_small_kernel(x_ref, mod_ref, g_ref, w_ref, o_ref):
    sh = mod_ref[:, 0:D_MODEL]
    sc = mod_ref[:, D_MODEL:2 * D_MODEL]
    h = _rms_modulate(x_ref[...], g_ref[...], sh, sc)
    h_hi, h_lo = _split2(h)
    w_hi, w_lo = _split2(w_ref[...])
    o_ref[...] = _dot_nt(h_hi, w_hi) + (_dot_nt(h_lo, w_hi) + _dot_nt(h_hi, w_lo))


def _in_proj_sample(layer, x, mod, norm_g, w_in_t, tn=1024):
    m, d = x.shape
    width = w_in_t.shape[1]
    return pl.pallas_call(
        _in_small_kernel,
        out_shape=jax.ShapeDtypeStruct((m, width), F32),
        grid=(pl.cdiv(width, tn),),
        in_specs=[
            pl.BlockSpec((m, d), lambda j: (0, 0)),
            pl.BlockSpec((m, 2 * d), lambda j: (0, 0)),
            pl.BlockSpec((1, d), lambda j: (0, 0)),
            pl.BlockSpec((None, tn, d), lambda j: (layer, j, 0)),
        ],
        out_specs=pl.BlockSpec((m, tn), lambda j: (0, j)),
        compiler_params=_cparams(("arbitrary",)),
        name="in_proj_sample",
    )(x, mod, norm_g, w_in_t)


_SSD_SUB = 4


def _ssd_kernel(xbc_ref, z_ref, dt_ref, cw_ref, cb_ref, dtb_ref, alog_ref, dskip_ref, ng_ref,
                yn_ref, hout_ref, h_scr, xp_scr):
    q = SSD_CHUNK
    c = pl.program_id(1)

    @pl.when(c == 0)
    def _():
        h_scr[...] = jnp.zeros_like(h_scr)
        xp_scr[0:8, :] = jnp.zeros((8, SSD_CONV_DIM), F32)

    for sub in range(_SSD_SUB):
        rows = slice(sub * q, (sub + 1) * q)
        _ssd_chunk(xbc_ref.at[rows], z_ref.at[rows], dt_ref.at[rows], cw_ref, cb_ref, dtb_ref, alog_ref,
                   dskip_ref, ng_ref, yn_ref.at[rows], h_scr, xp_scr)

    @pl.when(c == pl.num_programs(1) - 1)
    def _():
        hout_ref[...] = h_scr[...]


def _ssd_chunk(xbc_ref, z_ref, dt_ref, cw_ref, cb_ref, dtb_ref, alog_ref, dskip_ref, ng_ref,
               yn_ref, h_scr, xp_scr):
    q = SSD_CHUNK
    xp_scr[8:8 + q, :] = xbc_ref[...]
    acc = cb_ref[...] + cw_ref[3:4, :] * xp_scr[8:8 + q, :]
    for k in range(SSD_CONV - 1):
        acc = acc + cw_ref[k:k + 1, :] * xp_scr[5 + k:5 + k + q, :]
    xp_scr[0:8, :] = xp_scr[q:q + 8, :]
    xc = _silu(acc)
    xs = xc[:, 0:SSD_INNER]
    xs_bf = xs.astype(BF16)

    lane_q = lax.broadcasted_iota(jnp.int32, (q, LANES), 1)
    row_q = lax.broadcasted_iota(jnp.int32, (q, LANES), 0)
    causal = row_q >= lane_q
    tri = jnp.where(causal, 1.0, 0.0).astype(BF16)
    tri_t = jnp.where(lane_q >= row_q, 1.0, 0.0).astype(BF16)
    e_heads = _head_expand_matrix(LANES, SSD_HEADS, SSD_HEAD_DIM)

    dt = _softplus(dt_ref[...] + dtb_ref[...])
    dt = jnp.where(lane_q < SSD_HEADS, dt, 0.0)
    a = -jnp.exp(alog_ref[...])
    d_a = dt * a
    cum = _mm01_left(tri, d_a)
    d_a_t = d_a.T
    dt_t = dt.T
    cum_t = _mm01(d_a_t, tri_t, 3)
    cum_last = cum[q - 1:q, :]
    exp_cum = jnp.exp(cum)
    dec_end = jnp.exp(cum_last - cum)
    stack = jnp.concatenate([exp_cum, dec_end * dt], axis=0)
    full = _mm01(stack, e_heads, 2)
    exp_cum_full = full[0:q]
    w_full = full[q:2 * q]
    chunk_dec_t = jnp.exp(cum_t[:, q - 1:q])

    lane_half = lax.broadcasted_iota(jnp.int32, (q, LANES), 1) < SSD_HEAD_DIM
    y_parts = []
    for g in range(SSD_GROUPS):
        b_off = SSD_INNER + g * SSD_STATE
        c_off = SSD_INNER + SSD_GROUPS * SSD_STATE + g * SSD_STATE
        bm = xc[:, b_off:b_off + SSD_STATE].astype(BF16)
        cm = xc[:, c_off:c_off + SSD_STATE].astype(BF16)
        cbm = _dot_nt(cm, bm)
        gw = SSD_HPG * SSD_HEAD_DIM
        g0 = g * gw
        yd = []
        for pair in range(SSD_HPG // 2):
            x_pair = xs_bf[:, g0 + pair * LANES:g0 + (pair + 1) * LANES]
            halves = []
            for hh in range(2):
                h = g * SSD_HPG + pair * 2 + hh
                seg = cum[:, h:h + 1] - cum_t[h:h + 1, :]
                dec = jnp.exp(jnp.where(causal, seg, -jnp.inf))
                m_h = (cbm * dec * dt_t[h:h + 1, :]).astype(BF16)
                halves.append(_dot(m_h, x_pair))
            yd.append(jnp.where(lane_half, halves[0], halves[1]))
        y_diag = jnp.concatenate(yd, axis=1)
        h_g = h_scr[g0:g0 + gw, :]
        y_off = _dot_nt(cm, h_g.astype(BF16)) * exp_cum_full[:, g0:g0 + gw]
        y_parts.append(y_diag + y_off)
        xw = (xs[:, g0:g0 + gw] * w_full[:, g0:g0 + gw])
        st = _dot(xw.T.astype(BF16), bm)
        for e in range(SSD_HPG):
            h = g * SSD_HPG + e
            r0 = g0 + e * SSD_HEAD_DIM
            h_scr[r0:r0 + SSD_HEAD_DIM, :] = (h_scr[r0:r0 + SSD_HEAD_DIM, :] * chunk_dec_t[h:h + 1, :]
                                              + st[e * SSD_HEAD_DIM:(e + 1) * SSD_HEAD_DIM, :])

    y = jnp.concatenate(y_parts, axis=1) + dskip_ref[...] * xs
    y = y * _silu(z_ref[...].astype(F32))
    outs = []
    for g in range(SSD_GROUPS):
        gw = SSD_HPG * SSD_HEAD_DIM
        yg = y[:, g * gw:(g + 1) * gw]
        ms = jnp.mean(yg * yg, axis=-1, keepdims=True)
        outs.append(yg * lax.rsqrt(ms + EPS) * ng_ref[:, g * gw:(g + 1) * gw])
    yn_ref[...] = jnp.concatenate(outs, axis=1).astype(yn_ref.dtype)


def _mm01_left(tri01, a):
    hi, mid, lo = _split3(a)
    return _dot(tri01, hi) + (_dot(tri01, mid) + _dot(tri01, lo))


def _ssd_prompt(xbc, z, dt_raw, conv_w, conv_b, dt_bias, a_log, d_skip_full, ssd_norm_g, bsz, seq):
    q = SSD_CHUNK * _SSD_SUB
    nc = seq // q
    row = lambda b, c: (b * nc + c, 0)
    const = lambda b, c: (0, 0)
    return pl.pallas_call(
        _ssd_kernel,
        out_shape=(
            jax.ShapeDtypeStruct((bsz * seq, SSD_INNER), BF16),
            jax.ShapeDtypeStruct((bsz, SSD_INNER, SSD_STATE), F32),
        ),
        grid=(bsz, nc),
        in_specs=[
            pl.BlockSpec((q, SSD_CONV_DIM), row),
            pl.BlockSpec((q, SSD_INNER), row),
            pl.BlockSpec((q, LANES), row),
            pl.BlockSpec((SSD_CONV, SSD_CONV_DIM), const),
            pl.BlockSpec((1, SSD_CONV_DIM), const),
            pl.BlockSpec((1, LANES), const),
            pl.BlockSpec((1, LANES), const),
            pl.BlockSpec((1, SSD_INNER), const),
            pl.BlockSpec((1, SSD_INNER), const),
        ],
        out_specs=(
            pl.BlockSpec((q, SSD_INNER), row),
            pl.BlockSpec((None, SSD_INNER, SSD_STATE), lambda b, c: (b, 0, 0)),
        ),
        scratch_shapes=[
            pltpu.VMEM((SSD_INNER, SSD_STATE), F32),
            pltpu.VMEM((SSD_CHUNK + 8, SSD_CONV_DIM), F32),
        ],
        compiler_params=_cparams(("arbitrary", "arbitrary")),
        name="ssd_prompt",
    )(xbc, z, dt_raw, conv_w, conv_b, dt_bias, a_log, d_skip_full, ssd_norm_g)


def _attn_kernel(q_ref, kc_ref, kp_ref, vc_ref, vp_ref, o_out_ref, lse_out_ref,
                 q_ref_s, kwin, vwin, o_ref, lse_ref, *, tq, band):
    blk = ATT_BLOCK
    j = pl.program_id(2)
    q_ref_s[...] = (q_ref[...] * (ATT_HEAD_DIM ** -0.5)).astype(BF16).reshape(tq, ATT_OUT)
    kwin[0:blk, :] = kp_ref[...].reshape(blk, ATT_OUT)
    kwin[blk:blk + tq, :] = kc_ref[...].reshape(tq, ATT_OUT)
    vwin[0:blk, :] = vp_ref[...].reshape(blk, ATT_OUT)
    vwin[blk:blk + tq, :] = vc_ref[...].reshape(tq, ATT_OUT)
    q_ref = q_ref_s

    qi = lax.broadcasted_iota(jnp.int32, (blk, 2 * blk), 0)
    ki = lax.broadcasted_iota(jnp.int32, (blk, 2 * blk), 1)
    dist = qi + blk - ki
    in_band = (dist >= 0) & (dist <= band)
    lane = lax.broadcasted_iota(jnp.int32, (blk, LANES), 1)
    lane_half = lane < ATT_HEAD_DIM
    scale = ATT_HEAD_DIM ** -0.5
    zero_bf = jnp.zeros((blk, LANES), BF16)

    for i in range(tq // blk):
        if i == 0:
            valid = in_band & ((ki >= blk) | (j > 0))
        else:
            valid = in_band
        lse_tile = jnp.zeros((blk, LANES), F32)
        for hp in range(ATT_HEADS // 2):
            c0 = hp * LANES
            q_pair = q_ref[i * blk:(i + 1) * blk, c0:c0 + LANES]
            k_pair = kwin[i * blk:(i + 2) * blk, c0:c0 + LANES]
            v_pair = vwin[i * blk:(i + 2) * blk, c0:c0 + LANES]
            halves = []
            for hh in range(2):
                q_m = jnp.where(lane_half if hh == 0 else jnp.logical_not(lane_half), q_pair, zero_bf)
                s = _dot_nt(q_m, k_pair)
                s = jnp.where(valid, s, -jnp.inf)
                m = jnp.max(s, axis=-1, keepdims=True)
                e = jnp.exp(s - m)
                den = jnp.sum(e, axis=-1, keepdims=True)
                pv = _dot(e.astype(BF16), v_pair)
                halves.append(pv / den)
                lse = m + jnp.log(den)
                head = hp * 2 + hh
                lse_tile = jnp.where((lane == head) | (lane == ATT_HEADS + head), lse, lse_tile)
            o_ref[i * blk:(i + 1) * blk, c0:c0 + LANES] = jnp.where(lane_half, halves[0], halves[1]).astype(o_ref.dtype)
        lse_ref[i * blk:(i + 1) * blk, :] = lse_tile
    o_out_ref[...] = o_ref[...].reshape(o_out_ref.shape)
    lse_out_ref[...] = lse_ref[...].reshape(lse_out_ref.shape)


def _attn_prompt(a_g, gi, bsz, seq):
    win, dil = ATT_PATTERNS[gi]
    band = win // dil
    blk = ATT_BLOCK
    rows = TOK_TILE // dil
    tiles = seq // TOK_TILE
    length = seq // dil
    tq = min(512, length)
    nj = length // tq
    tq_tiles = tq // rows
    cur = lambda which: pl.BlockSpec((None, tq_tiles, None, rows, ATT_OUT),
                                     lambda b, r, j: (b, j, r, 0, which))
    if rows >= blk:
        prev = lambda which: pl.BlockSpec(
            (None, None, None, blk, ATT_OUT),
            lambda b, r, j: (b, jnp.maximum(j * tq_tiles - 1, 0), r, rows // blk - 1, which))
    else:
        prev = lambda which: pl.BlockSpec(
            (None, blk // rows, None, rows, ATT_OUT),
            lambda b, r, j: (b, jnp.maximum(j * (tq // blk) - 1, 0), r, 0, which))
    return pl.pallas_call(
        functools.partial(_attn_kernel, tq=tq, band=band),
        out_shape=(
            jax.ShapeDtypeStruct((bsz, tiles, dil, rows, ATT_OUT), BF16),
            jax.ShapeDtypeStruct((bsz, tiles, dil, rows, LANES), F32),
        ),
        grid=(bsz, dil, nj),
        in_specs=[cur(0), cur(1), prev(1), cur(2), prev(2)],
        out_specs=(
            pl.BlockSpec((None, tq_tiles, None, rows, ATT_OUT), lambda b, r, j: (b, j, r, 0, 0)),
            pl.BlockSpec((None, tq_tiles, None, rows, LANES), lambda b, r, j: (b, j, r, 0, 0)),
        ),
        scratch_shapes=[
            pltpu.VMEM((tq, ATT_OUT), BF16),
            pltpu.VMEM((blk + tq, ATT_OUT), BF16),
            pltpu.VMEM((blk + tq, ATT_OUT), BF16),
            pltpu.VMEM((tq, ATT_OUT), BF16),
            pltpu.VMEM((tq, LANES), F32),
        ],
        compiler_params=_cparams(("arbitrary", "arbitrary", "arbitrary")),
        name=f"attn_prompt_w{win}",
    )(a_g, a_g, a_g, a_g, a_g)


def _router_gates(logits):
    shape = logits.shape
    lane = lax.broadcasted_iota(jnp.int32, shape, 1)
    big = jnp.int32(1 << 20)
    neg = -jnp.inf
    is_grp = (lane >= MOE_EXPERTS) & (lane < MOE_EXPERTS + MOE_GROUPS)
    lg = jnp.where(is_grp, logits, neg)
    gm = jnp.max(lg, axis=-1, keepdims=True)
    g_lane = jnp.min(jnp.where(lg == gm, lane, big), axis=-1, keepdims=True)
    g_sum = jnp.sum(jnp.exp(lg - gm), axis=-1, keepdims=True)
    g_w = 1.0 / g_sum
    lo = (g_lane - MOE_EXPERTS) * MOE_PER_GROUP
    in_grp = (lane >= lo) & (lane < lo + MOE_PER_GROUP)
    le = jnp.where(in_grp, logits, neg)
    m1 = jnp.max(le, axis=-1, keepdims=True)
    i1 = jnp.min(jnp.where(le == m1, lane, big), axis=-1, keepdims=True)
    le2 = jnp.where(lane == i1, neg, le)
    m2 = jnp.max(le2, axis=-1, keepdims=True)
    i2 = jnp.min(jnp.where(le2 == m2, lane, big), axis=-1, keepdims=True)
    t = jnp.exp(m2 - m1)
    w1 = 1.0 / (1.0 + t)
    w2 = t / (1.0 + t)
    gate = jnp.where(lane == i1, g_w * w1, jnp.where(lane == i2, g_w * w2, 0.0))
    e_lo = jnp.minimum(i1, i2)
    e_hi = jnp.maximum(i1, i2)
    lo_l = e_lo & (MOE_PER_GROUP - 1)
    hi_l = e_hi & (MOE_PER_GROUP - 1)
    cls = (e_lo >> 2) * MOE_PAIRS + ((lo_l * (7 - lo_l)) >> 1) + (hi_l - lo_l - 1)
    return gate, cls


def _post_kernel(*refs, passes, merge_attn):
    if merge_attn:
        (x_ref, yn_ref, o0_ref, o1_ref, o2_ref, l0_ref, l1_ref, l2_ref, gates_ref, mod_ref, n2_ref,
         wssd_ref, wattn_ref, wout_ref, wr_ref, br_ref, x1_ref, h2_ref, pos_ref, cnt_ref,
         o1_scr, o2_scr, l1_scr, l2_scr) = refs
        for src_ref, dst_ref in ((o1_ref, o1_scr), (o2_ref, o2_scr), (l1_ref, l1_scr), (l2_ref, l2_scr)):
            dil, r_len, width = src_ref.shape
            for r in range(dil):
                blk_r = src_ref[r].astype(F32)
                for cbk in range(width // LANES):
                    dst_ref[cbk, pl.ds(r, r_len, stride=dil), :] = blk_r[:, cbk * LANES:(cbk + 1) * LANES]
        o_nat = [o0_ref[...].astype(F32)] + [
            jnp.concatenate([scr[cbk] for cbk in range(ATT_OUT // LANES)], axis=1) for scr in (o1_scr, o2_scr)]
        l0, l1, l2 = l0_ref[...], l1_scr[0], l2_scr[0]
        mx = jnp.maximum(jnp.maximum(l0, l1), l2)
        e0, e1, e2 = jnp.exp(l0 - mx), jnp.exp(l1 - mx), jnp.exp(l2 - mx)
        inv = 1.0 / (e0 + e1 + e2)
        lane = lax.broadcasted_iota(jnp.int32, l0.shape, 1)
        e8 = _head_expand_matrix(LANES, ATT_HEADS, ATT_HEAD_DIM)
        r = lax.broadcasted_iota(jnp.int32, e8.shape, 0)
        c = lax.broadcasted_iota(jnp.int32, e8.shape, 1)
        e8 = jnp.where((c // ATT_HEAD_DIM) == (r - ATT_HEADS), 1.0, e8.astype(F32)).astype(BF16)
        o = None
        for e_g, o_g in zip((e0, e1, e2), o_nat):
            w = e_g * inv
            hi, lo = _split2(w)
            w_exp = _dot(jnp.where(lane < ATT_HEADS, hi, lo), e8)
            term = w_exp * o_g
            o = term if o is None else o + term
    else:
        (x_ref, yn_ref, o_ref, gates_ref, mod_ref, n2_ref,
         wssd_ref, wattn_ref, wout_ref, wr_ref, br_ref, x1_ref, h2_ref, gate_ref) = refs
        o = o_ref[...]
    d = D_MODEL
    g1 = mod_ref[:, 0:d]
    sh2 = mod_ref[:, d:2 * d]
    sc2 = mod_ref[:, 2 * d:3 * d]
    ssd_branch = _mm(yn_ref[...], wssd_ref[...], passes)
    attn_branch = _mm(o, wattn_ref[...], passes)
    ga = gates_ref[:, 0:d].astype(F32)
    gb = gates_ref[:, d:2 * d].astype(F32)
    mixed = _sigmoid(ga) * ssd_branch + _sigmoid(gb) * attn_branch
    x1 = x_ref[...] + g1 * _mm(mixed, wout_ref[...], passes)
    x1_ref[...] = x1
    h2 = _rms_modulate(x1, n2_ref[...], sh2, sc2)
    logits = _mm(h2, wr_ref[...], 3) + br_ref[...]
    gate, cls = _router_gates(logits)
    if not merge_attn:
        h2_ref[...] = h2
        gate_ref[...] = gate
        return
    tm = h2.shape[0]
    for cbk in range(ROW_SUB):
        h2_ref[pl.ds(cbk, tm, stride=ROW_SUB), :] = h2[:, cbk * LANES:(cbk + 1) * LANES]
    step = pl.program_id(0)

    @pl.when(step == 0)
    def _():
        cnt_ref[...] = jnp.zeros_like(cnt_ref)

    lane = lax.broadcasted_iota(jnp.int32, (tm, LANES), 1)
    onehot = lane == cls
    ri = lax.broadcasted_iota(jnp.int32, (tm, tm), 0)
    ci = lax.broadcasted_iota(jnp.int32, (tm, tm), 1)
    before = jnp.where(ci < ri, 1.0, 0.0).astype(BF16)
    seen = _dot(before, jnp.where(onehot, 1.0, 0.0).astype(BF16)) + cnt_ref[...]
    rank = jnp.sum(jnp.where(onehot, seen, 0.0), axis=-1, keepdims=True)
    n_tokens = tm * pl.num_programs(0)
    pos_ref[...] = jnp.broadcast_to(cls.astype(F32) * n_tokens + rank, (tm, LANES))
    cnt_ref[...] = cnt_ref[...] + jnp.sum(jnp.where(onehot, 1.0, 0.0), axis=0, keepdims=True)


def _post(x, yn, attn_in, gates, mod, norm2_g, wssd, wattn, wout, wr, br, *, passes, merge_attn,
          tm, rows_per_mod, h2_dtype):
    n, d = x.shape
    per_row_mod = rows_per_mod == 1
    if per_row_mod:
        mod_spec = pl.BlockSpec((tm, 3 * d), lambda i: (i, 0))
    else:
        tiles = rows_per_mod // tm
        mod_spec = pl.BlockSpec((None, 1, 3 * d), lambda i: (i // tiles, 0, 0))
    row = lambda w: pl.BlockSpec((tm, w), lambda i: (i, 0))
    const = lambda a: pl.BlockSpec(a.shape, lambda i: (0, 0), pipeline_mode=pl.Buffered(1))
    scratch = []
    if merge_attn:
        assert tm == TOK_TILE
        o_list, lse_list = attn_in
        attn_args = list(o_list) + list(lse_list)
        tiles = rows_per_mod // tm

        def tile_spec(a):
            _, _, dil, r_len, w = a.shape
            if dil == 1:
                return pl.BlockSpec((None, None, None, r_len, w), lambda i: (i // tiles, i % tiles, 0, 0, 0))
            return pl.BlockSpec((None, None, dil, r_len, w), lambda i: (i // tiles, i % tiles, 0, 0, 0))

        attn_specs = [tile_spec(a) for a in attn_args]
        scratch = [pltpu.VMEM((ATT_OUT // LANES, tm, LANES), F32), pltpu.VMEM((ATT_OUT // LANES, tm, LANES), F32),
                   pltpu.VMEM((1, tm, LANES), F32), pltpu.VMEM((1, tm, LANES), F32)]
    else:
        attn_args = [attn_in]
        attn_specs = [row(ATT_OUT)]
    if merge_attn:
        out_shape = (jax.ShapeDtypeStruct((n, d), F32), jax.ShapeDtypeStruct((n * ROW_SUB, LANES), F32),
                     jax.ShapeDtypeStruct((n, LANES), F32), jax.ShapeDtypeStruct((1, LANES), F32))
        out_specs = (row(d), pl.BlockSpec((tm * ROW_SUB, LANES), lambda i: (i, 0)), row(LANES),
                     pl.BlockSpec((1, LANES), lambda i: (0, 0)))
    else:
        out_shape = (jax.ShapeDtypeStruct((n, d), F32), jax.ShapeDtypeStruct((n, d), h2_dtype),
                     jax.ShapeDtypeStruct((n, LANES), F32))
        out_specs = (row(d), row(d), row(LANES))
    return pl.pallas_call(
        functools.partial(_post_kernel, passes=passes, merge_attn=merge_attn),
        out_shape=out_shape,
        grid=(n // tm,),
        in_specs=[row(d), row(SSD_INNER)] + attn_specs + [row(IN_GATE), mod_spec, const(norm2_g),
                                                          const(wssd), const(wattn), const(wout),
                                                          const(wr), const(br)],
        out_specs=out_specs,
        scratch_shapes=scratch,
        compiler_params=_cparams(("arbitrary",)),
        name="post_merge" if merge_attn else "post_sample",
    )(x, yn, *attn_args, gates, mod, norm2_g, wssd, wattn, wout, wr, br)


def _moe_kernel(h2_ref, gate_ref, x1_ref, g2_ref, wg_ref, wu_ref, wd_ref, x2_ref, acc_ref, *, passes):
    e = pl.program_id(1)

    @pl.when(e == 0)
    def _():
        acc_ref[...] = jnp.zeros_like(acc_ref)

    h2 = h2_ref[...]
    hg = _mm(h2, wg_ref[...], passes)
    hu = _mm(h2, wu_ref[...], passes)
    gate = gate_ref[...]
    lane = lax.broadcasted_iota(jnp.int32, gate.shape, 1)
    gcol = jnp.sum(jnp.where(lane == e, gate, 0.0), axis=-1, keepdims=True)
    act = _silu(hg) * hu * gcol
    acc_ref[...] += _mm(act, wd_ref[...], passes)

    @pl.when(e == pl.num_programs(1) - 1)
    def _():
        x2_ref[...] = x1_ref[...] + g2_ref[...] * acc_ref[...]


def _moe(layer, h2, gate, x1, g2, wg, wu, wd, *, passes, tm, rows_per_mod):
    n, d = x1.shape
    if rows_per_mod == 1:
        g2_spec = pl.BlockSpec((tm, d), lambda i, e: (i, 0))
    else:
        tiles = rows_per_mod // tm
        g2_spec = pl.BlockSpec((None, 1, d), lambda i, e: (i // tiles, 0, 0))
    row = lambda w: pl.BlockSpec((tm, w), lambda i, e: (i, 0))
    return pl.pallas_call(
        functools.partial(_moe_kernel, passes=passes),
        out_shape=jax.ShapeDtypeStruct((n, d), F32),
        grid=(n // tm, MOE_EXPERTS),
        in_specs=[
            row(d), row(LANES), row(d), g2_spec,
            pl.BlockSpec((None, None, d, MOE_FF), lambda i, e: (layer, e, 0, 0)),
            pl.BlockSpec((None, None, d, MOE_FF), lambda i, e: (layer, e, 0, 0)),
            pl.BlockSpec((None, None, MOE_FF, d), lambda i, e: (layer, e, 0, 0)),
        ],
        out_specs=row(d),
        scratch_shapes=[pltpu.VMEM((tm, d), F32)],
        compiler_params=_cparams(("arbitrary", "arbitrary")),
        name="moe_dense",
    )(h2, gate, x1, g2, wg, wu, wd)


_PAIR_LO = (0, 0, 0, 1, 1, 2)
_PAIR_HI = (1, 2, 3, 2, 3, 3)
_DMA_UNROLL = 8
_ROUTE_TILE = 2048


def _moe_plan(counts, n_tokens):
    tm = MOE_TILE
    blocks_per_class = n_tokens // tm
    ntile = (counts + tm - 1) // tm
    cum = jnp.cumsum(ntile)
    total = cum[-1]
    t_max = blocks_per_class + MOE_CLASSES
    t_eff = jnp.minimum(jnp.arange(t_max, dtype=jnp.int32), total - 1)
    cls = jnp.searchsorted(cum, t_eff, side="right").astype(jnp.int32)
    blk = cls * blocks_per_class + (t_eff - (cum - ntile)[cls])
    grp, pair = cls // MOE_PAIRS, cls % MOE_PAIRS
    e_lo = grp * MOE_PER_GROUP + jnp.asarray(_PAIR_LO, jnp.int32)[pair]
    e_hi = grp * MOE_PER_GROUP + jnp.asarray(_PAIR_HI, jnp.int32)[pair]
    z_need = ((counts % tm) != 0).astype(jnp.int32)
    z_blk = jnp.arange(MOE_CLASSES, dtype=jnp.int32) * blocks_per_class + jnp.maximum(ntile - 1, 0)
    return (blk.astype(jnp.int32), e_lo, e_hi, total.reshape(1).astype(jnp.int32), z_need, z_blk.astype(jnp.int32))


def _row_copy(src, src_row, dst, dst_row, sem):
    return pltpu.make_async_copy(src.at[pl.ds(pl.multiple_of(src_row * ROW_SUB, ROW_SUB), ROW_SUB), :],
                                 dst.at[pl.ds(pl.multiple_of(dst_row * ROW_SUB, ROW_SUB), ROW_SUB), :], sem)


def _dispatch_kernel(pos_ref, zneed_ref, zblk_ref, rows_ref, xs_ref, zero_scr, sem, zsem, *, tm):
    step = pl.program_id(0)
    tile_rows = MOE_TILE * ROW_SUB

    def zero_copy(c):
        start = pl.multiple_of(zblk_ref[c] * tile_rows, tile_rows)
        return pltpu.make_async_copy(zero_scr, xs_ref.at[pl.ds(start, tile_rows), :], zsem)

    @pl.when(step == 0)
    def _():
        zero_scr[...] = jnp.zeros_like(zero_scr)
        for c in range(MOE_CLASSES):
            @pl.when(zneed_ref[c] != 0)
            def _():
                zero_copy(c).start()
        for c in range(MOE_CLASSES):
            @pl.when(zneed_ref[c] != 0)
            def _():
                zero_copy(c).wait()

    base = step * tm

    def body(i8, carry):
        for u in range(_DMA_UNROLL):
            i = i8 * _DMA_UNROLL + u
            _row_copy(rows_ref, i, xs_ref, pos_ref[base + i], sem).start(priority=u % 2)
        return carry

    lax.fori_loop(0, tm // _DMA_UNROLL, body, 0)
    pltpu.make_async_copy(rows_ref, xs_ref.at[pl.ds(0, tm * ROW_SUB), :], sem).wait()


def _dispatch(pos, z_need, z_blk, rows, n_tokens, tm=_ROUTE_TILE):
    sorted_rows = MOE_CLASSES * n_tokens * ROW_SUB
    return pl.pallas_call(
        functools.partial(_dispatch_kernel, tm=tm),
        out_shape=jax.ShapeDtypeStruct((sorted_rows, LANES), F32),
        grid_spec=pltpu.PrefetchScalarGridSpec(
            num_scalar_prefetch=3,
            grid=(n_tokens // tm,),
            in_specs=[pl.BlockSpec((tm * ROW_SUB, LANES), lambda i, *_: (i, 0))],
            out_specs=pl.BlockSpec(memory_space=pl.ANY),
            scratch_shapes=[pltpu.VMEM((MOE_TILE * ROW_SUB, LANES), F32),
                            pltpu.SemaphoreType.DMA(()), pltpu.SemaphoreType.DMA(())],
        ),
        compiler_params=pltpu.CompilerParams(dimension_semantics=("arbitrary",), vmem_limit_bytes=VMEM_LIMIT,
                                             disable_bounds_checks=True),
        name="moe_dispatch",
    )(pos, z_need, z_blk, rows)


def _moe_sparse_kernel(blk_ref, elo_ref, ehi_ref, nact_ref, xs_ref, wr_ref, br_ref,
                       wg_lo_ref, wu_lo_ref, wd_lo_ref, wg_hi_ref, wu_hi_ref, wd_hi_ref, ys_ref):
    t = pl.program_id(0)
    tm = MOE_TILE

    @pl.when(t < nact_ref[0])
    def _():
        x = jnp.concatenate([xs_ref[pl.ds(c, tm, stride=ROW_SUB), :] for c in range(ROW_SUB)], axis=1)
        xb = x.astype(BF16)
        w_hi16, w_lo16 = _split2(wr_ref[...])
        logits = _dot(xb, w_hi16) + _dot(xb, w_lo16) + br_ref[...]
        lane = lax.broadcasted_iota(jnp.int32, logits.shape, 1)
        e_lo, e_hi = elo_ref[t], ehi_ref[t]
        g_lane = MOE_EXPERTS + (e_lo >> 2)
        pick = lambda ln: jnp.sum(jnp.where(lane == ln, logits, 0.0), axis=-1, keepdims=True)
        l_lo, l_hi, l_g = pick(e_lo), pick(e_hi), pick(g_lane)
        is_grp = (lane >= MOE_EXPERTS) & (lane < MOE_EXPERTS + MOE_GROUPS)
        g_w = 1.0 / jnp.sum(jnp.where(is_grp, jnp.exp(logits - l_g), 0.0), axis=-1, keepdims=True)
        w_a = g_w / (1.0 + jnp.exp(l_hi - l_lo))
        w_b = g_w / (1.0 + jnp.exp(l_lo - l_hi))
        bf = lambda w_ref: w_ref[...].astype(BF16)
        act_a = (_silu(_dot(xb, bf(wg_lo_ref))) * _dot(xb, bf(wu_lo_ref)) * w_a).astype(BF16)
        act_b = (_silu(_dot(xb, bf(wg_hi_ref))) * _dot(xb, bf(wu_hi_ref)) * w_b).astype(BF16)
        y = _dot(act_a, bf(wd_lo_ref)) + _dot(act_b, bf(wd_hi_ref))
        for c in range(ROW_SUB):
            ys_ref[pl.ds(c, tm, stride=ROW_SUB), :] = y[:, c * LANES:(c + 1) * LANES]


def _moe_sparse(layer, plan, xs, wr, br, wg, wu, wd, n_tokens):
    blk, e_lo, e_hi, n_act = plan
    tm = MOE_TILE
    t_max = blk.shape[0]
    d = D_MODEL
    tile = pl.BlockSpec((tm * ROW_SUB, LANES), lambda t, blk, lo, hi, na: (blk[t], 0))
    w_lo = lambda shape: pl.BlockSpec((None, None) + shape, lambda t, blk, lo, hi, na: (layer, lo[t], 0, 0))
    w_hi = lambda shape: pl.BlockSpec((None, None) + shape, lambda t, blk, lo, hi, na: (layer, hi[t], 0, 0))
    const = lambda a: pl.BlockSpec(a.shape, lambda t, *_: (0, 0))
    return pl.pallas_call(
        _moe_sparse_kernel,
        out_shape=jax.ShapeDtypeStruct(xs.shape, F32),
        grid_spec=pltpu.PrefetchScalarGridSpec(
            num_scalar_prefetch=4,
            grid=(t_max,),
            in_specs=[tile, const(wr), const(br),
                      w_lo((d, MOE_FF)), w_lo((d, MOE_FF)), w_lo((MOE_FF, d)),
                      w_hi((d, MOE_FF)), w_hi((d, MOE_FF)), w_hi((MOE_FF, d))],
            out_specs=tile,
        ),
        compiler_params=_cparams(("arbitrary",)),
        name="moe_sparse",
    )(blk, e_lo, e_hi, n_act, xs, wr, br, wg, wu, wd, wg, wu, wd)


def _combine_kernel(pos_ref, ys_ref, x1_ref, g2_ref, *rest, tm, final):
    if final:
        fg_ref, x2_ref, buf, sem = rest
    else:
        x2_ref, buf, sem = rest
    base = pl.program_id(0) * tm

    def body(i8, carry):
        for u in range(_DMA_UNROLL):
            i = i8 * _DMA_UNROLL + u
            _row_copy(ys_ref, pos_ref[base + i], buf, i, sem).start(priority=u % 2)
        return carry

    lax.fori_loop(0, tm // _DMA_UNROLL, body, 0)
    pltpu.make_async_copy(ys_ref.at[pl.ds(0, tm * ROW_SUB), :], buf, sem).wait()
    y = jnp.concatenate([buf[pl.ds(c, tm, stride=ROW_SUB), :] for c in range(ROW_SUB)], axis=1)
    x2 = x1_ref[...] + g2_ref[...] * y
    if final:
        ms = jnp.mean(x2 * x2, axis=-1, keepdims=True)
        x2 = x2 * lax.rsqrt(ms + EPS) * fg_ref[...]
    x2_ref[...] = x2


def _combine(pos, ys, x1, g2, rows_per_mod, final_g=None, tm=_ROUTE_TILE):
    n, d = x1.shape
    tiles = rows_per_mod // tm
    final = final_g is not None
    extra_specs = [pl.BlockSpec((1, d), lambda i, *_: (0, 0))] if final else []
    extra_args = [final_g] if final else []
    return pl.pallas_call(
        functools.partial(_combine_kernel, tm=tm, final=final),
        out_shape=jax.ShapeDtypeStruct((n, d), F32),
        grid_spec=pltpu.PrefetchScalarGridSpec(
            num_scalar_prefetch=1,
            grid=(n // tm,),
            in_specs=[pl.BlockSpec(memory_space=pl.ANY),
                      pl.BlockSpec((tm, d), lambda i, *_: (i, 0)),
                      pl.BlockSpec((None, 1, d), lambda i, *_: (i // tiles, 0, 0))] + extra_specs,
            out_specs=pl.BlockSpec((tm, d), lambda i, *_: (i, 0)),
            scratch_shapes=[pltpu.VMEM((tm * ROW_SUB, LANES), F32), pltpu.SemaphoreType.DMA(())],
        ),
        compiler_params=pltpu.CompilerParams(dimension_semantics=("arbitrary",), vmem_limit_bytes=VMEM_LIMIT,
                                             disable_bounds_checks=True),
        name="moe_combine",
    )(pos, ys, x1, g2, *extra_args)


def _final_kernel(x_ref, g_ref, o_ref):
    x = x_ref[...]
    ms = jnp.mean(x * x, axis=-1, keepdims=True)
    o_ref[...] = x * lax.rsqrt(ms + EPS) * g_ref[...]


def _final_norm(x, g, tm):
    n, d = x.shape
    return pl.pallas_call(
        _final_kernel,
        out_shape=jax.ShapeDtypeStruct((n, d), F32),
        grid=(n // tm,),
        in_specs=[pl.BlockSpec((tm, d), lambda i: (i, 0)), pl.BlockSpec((1, d), lambda i: (0, 0))],
        out_specs=pl.BlockSpec((tm, d), lambda i: (i, 0)),
        compiler_params=_cparams(("arbitrary",)),
        name="final_norm",
    )(x, g)


def _step_kernel(z_ref, xbc_ref, dt_ref, cst_ref, h_ref,
                 cw_ref, cb_ref, dtb_ref, alog_ref, dskip_ref, ng_ref, *rest):
    yn_ref, cnew_ref, hnew_ref, col_scr = rest[-4:]
    x_new = xbc_ref[...]
    acc = cb_ref[...] + cw_ref[3:4, :] * x_new
    for k in range(SSD_CONV - 1):
        acc = acc + cw_ref[k:k + 1, :] * cst_ref[k:k + 1, :]
    cnew_ref[0:1, :] = cst_ref[1:2, :]
    cnew_ref[1:2, :] = cst_ref[2:3, :]
    cnew_ref[2:3, :] = x_new
    xc = _silu(acc)
    xs = xc[:, 0:SSD_INNER]

    e_heads = _head_expand_matrix(LANES, SSD_HEADS, SSD_HEAD_DIM)
    dt_raw8 = jnp.broadcast_to(dt_ref[...], (8, LANES))
    dt_full = _softplus(_mm01(dt_raw8, e_heads, 3)[0:1, :] + dtb_ref[...])
    a_full = -jnp.exp(alog_ref[...])
    dec_full = jnp.exp(dt_full * a_full)
    xdt = xs * dt_full
    col_scr[...] = jnp.zeros_like(col_scr)
    col_scr[0:1, :] = xdt
    col_scr[1:2, :] = dec_full
    cols = col_scr[...].T
    y_parts = []
    gw = SSD_HPG * SSD_HEAD_DIM
    for g in range(SSD_GROUPS):
        b_off = SSD_INNER + g * SSD_STATE
        c_off = SSD_INNER + SSD_GROUPS * SSD_STATE + g * SSD_STATE
        bm = xc[:, b_off:b_off + SSD_STATE]
        cm = xc[:, c_off:c_off + SSD_STATE]
        g0 = g * gw
        hn = h_ref[g0:g0 + gw, :] * cols[g0:g0 + gw, 1:2] + cols[g0:g0 + gw, 0:1] * bm
        hnew_ref[g0:g0 + gw, :] = hn
        t = (hn * cm).T
        y_parts.append(jnp.sum(t, axis=0, keepdims=True))
    y = jnp.concatenate(y_parts, axis=1) + dskip_ref[...] * xs
    y = y * _silu(z_ref[...])
    outs = []
    for g in range(SSD_GROUPS):
        yg = y[:, g * gw:(g + 1) * gw]
        ms = jnp.mean(yg * yg, axis=-1, keepdims=True)
        outs.append(yg * lax.rsqrt(ms + EPS) * ng_ref[:, g * gw:(g + 1) * gw])
    yn_ref[...] = jnp.concatenate(outs, axis=1)


def _step_sample(layer, z, xbc, dt_raw, state_conv, state_ssm, prev_ssm,
                 conv_w, conv_b, dtb_full, alog_full, dskip_full, ssd_norm_g):
    bsz = z.shape[0]
    row3 = lambda w: pl.BlockSpec((None, 1, w), lambda b: (b, 0, 0))
    const = lambda a: pl.BlockSpec(a.shape, lambda b: (0,) * a.ndim)
    alias_args = [] if prev_ssm is None else [prev_ssm]
    n_in = 11
    return pl.pallas_call(
        _step_kernel,
        out_shape=(
            jax.ShapeDtypeStruct((bsz, 1, SSD_INNER), F32),
            jax.ShapeDtypeStruct((bsz, SSD_CONV - 1, SSD_CONV_DIM), F32),
            jax.ShapeDtypeStruct(state_ssm.shape, F32),
        ),
        grid=(bsz,),
        in_specs=[
            row3(SSD_INNER), row3(SSD_CONV_DIM), row3(LANES),
            pl.BlockSpec((None, None, SSD_CONV - 1, SSD_CONV_DIM), lambda b: (layer, b, 0, 0)),
            pl.BlockSpec((None, None, SSD_INNER, SSD_STATE), lambda b: (layer, b, 0, 0)),
            const(conv_w), const(conv_b), const(dtb_full), const(alog_full), const(dskip_full), const(ssd_norm_g),
        ] + [pl.BlockSpec(memory_space=pl.ANY)] * len(alias_args),
        out_specs=(
            row3(SSD_INNER),
            pl.BlockSpec((None, SSD_CONV - 1, SSD_CONV_DIM), lambda b: (b, 0, 0)),
            pl.BlockSpec((None, None, SSD_INNER, SSD_STATE), lambda b: (layer, b, 0, 0)),
        ),
        input_output_aliases={n_in: 2} if alias_args else {},
        scratch_shapes=[pltpu.VMEM((LANES, SSD_INNER), F32)],
        compiler_params=_cparams(("arbitrary",)),
        name="step_sample",
    )(z.reshape(bsz, 1, -1), xbc.reshape(bsz, 1, -1), dt_raw.reshape(bsz, 1, -1), state_conv, state_ssm,
      conv_w, conv_b, dtb_full, alog_full, dskip_full, ssd_norm_g, *alias_args)


_HEAD_SPLIT = 1
_HROWS = ATT_OUT // _HEAD_SPLIT


def _cache_attn_kernel(*refs, n_alias):
    qkv_ref = refs[0]
    cache_refs = refs[1:7]
    out_refs = refs[7 + n_alias:13 + n_alias]
    o_ref = refs[13 + n_alias]
    b = pl.program_id(0)
    nh = _HROWS // ATT_HEAD_DIM
    scale = ATT_HEAD_DIM ** -0.5
    qkv = qkv_ref[...]
    lane_b = lax.broadcasted_iota(jnp.int32, qkv.shape, 2)
    cols = jnp.sum(jnp.where(lane_b == b, qkv, 0.0), axis=-1, keepdims=True)

    def per_head_rows(v):
        return jnp.concatenate([jnp.broadcast_to(v[h:h + 1, :], (ATT_HEAD_DIM, 1)) for h in range(nh)], axis=0)

    o_g, lse_g = [], []
    for gi, (_, dil) in enumerate(ATT_PATTERNS):
        k_ref, v_ref = cache_refs[2 * gi], cache_refs[2 * gi + 1]
        ko_ref, vo_ref = out_refs[2 * gi], out_refs[2 * gi + 1]
        q = cols[gi]
        k_new = cols[ATT_GROUPS + gi]
        v_new = cols[2 * ATT_GROUPS + gi]
        kk = k_ref[...]
        vv = v_ref[...]
        length = kk.shape[1]
        lane = lax.broadcasted_iota(jnp.int32, (nh, length), 1)
        s = jnp.sum((kk * q).reshape(nh, ATT_HEAD_DIM, length), axis=1) * scale
        s = jnp.where((lane & (dil - 1)) == 0, s, -jnp.inf)
        s_new = jnp.sum((k_new * q).reshape(nh, ATT_HEAD_DIM, 1), axis=1) * scale
        m = jnp.maximum(jnp.max(s, axis=-1, keepdims=True), s_new)
        e = jnp.exp(s - m)
        e_new = jnp.exp(s_new - m)
        den = jnp.sum(e, axis=-1, keepdims=True) + e_new
        acc = jnp.sum(vv.reshape(nh, ATT_HEAD_DIM, length) * e[:, None, :], axis=-1, keepdims=True)
        acc = acc.reshape(_HROWS, 1) + per_head_rows(e_new) * v_new
        o_g.append(acc / per_head_rows(den))
        lse_g.append(per_head_rows(m + jnp.log(den)))
        lane_full = lax.broadcasted_iota(jnp.int32, kk.shape, 1)
        last = lane_full == length - 1
        ko_ref[...] = jnp.where(last, k_new, pltpu.roll(kk, length - 1, axis=1))
        vo_ref[...] = jnp.where(last, v_new, pltpu.roll(vv, length - 1, axis=1))
    mx = jnp.maximum(jnp.maximum(lse_g[0], lse_g[1]), lse_g[2])
    w = [jnp.exp(l - mx) for l in lse_g]
    tot = w[0] + w[1] + w[2]
    o_ref[...] = (w[0] / tot) * o_g[0] + (w[1] / tot) * o_g[1] + (w[2] / tot) * o_g[2]


def _cache_attn(layer, qkv_t, cache_views, prev_outs):
    depth, bsz, _, _ = cache_views[0].shape
    n_alias = 0 if prev_outs is None else len(prev_outs)
    q4 = qkv_t.reshape(3 * ATT_GROUPS, _HEAD_SPLIT, _HROWS, bsz)
    blk = lambda c: pl.BlockSpec((None, None, _HROWS, c.shape[3]), lambda b, hh: (layer, b, hh, 0))
    any_spec = pl.BlockSpec(memory_space=pl.ANY)
    args = [q4] + list(cache_views) + ([] if prev_outs is None else list(prev_outs))
    res = pl.pallas_call(
        functools.partial(_cache_attn_kernel, n_alias=n_alias),
        out_shape=tuple(jax.ShapeDtypeStruct(c.shape, c.dtype) for c in cache_views)
        + (jax.ShapeDtypeStruct((bsz, _HEAD_SPLIT, _HROWS, 1), F32),),
        grid=(bsz, _HEAD_SPLIT),
        in_specs=[pl.BlockSpec((3 * ATT_GROUPS, None, _HROWS, bsz), lambda b, hh: (0, hh, 0, 0))]
        + [blk(c) for c in cache_views] + [any_spec] * n_alias,
        out_specs=tuple(blk(c) for c in cache_views)
        + (pl.BlockSpec((None, None, _HROWS, 1), lambda b, hh: (b, hh, 0, 0)),),
        input_output_aliases={7 + i: i for i in range(n_alias)},
        compiler_params=_cparams(("arbitrary", "arbitrary")),
        name="cache_attn",
    )(*args)
    return list(res[:6]), res[6].reshape(bsz, ATT_OUT)


def _pad_lanes(v, width=LANES):
    return jnp.pad(v, [(0, 0)] * (v.ndim - 1) + [(0, width - v.shape[-1])])


def kernel(x_prompt, x_sample, c_prompt, c_sample, state_conv, state_ssm, cache_k_win128, cache_v_win128, cache_k_win512, cache_v_win512, cache_k_win2048, cache_v_win2048, norm1_g, w_ada, b_ada, w_in, conv_w, conv_b, dt_bias, a_log, d_skip, ssd_norm_g, w_ssd_proj, w_attn_proj, w_out, norm2_g, w_router_group, b_router_group, w_router_expert, b_router_expert, w_exp_gate, w_exp_up, w_exp_down, final_norm_g):
    depth = w_in.shape[0]
    bp, seq, d = x_prompt.shape
    bs = x_sample.shape[0]
    assert x_sample.shape[1] == 1 and d == D_MODEL and w_in.shape[2] == IN_WIDTH
    assert seq % (ATT_PATTERNS[-1][1] * ATT_BLOCK) == 0
    n_p = bp * seq
    caches = ((cache_k_win128, cache_v_win128), (cache_k_win512, cache_v_win512),
              (cache_k_win2048, cache_v_win2048))

    rows = bp + bs
    rows_pad = -(-rows // 8) * 8
    c_all = jnp.pad(jnp.concatenate([c_prompt, c_sample], axis=0), ((0, rows_pad - rows), (0, 0)))
    mods = _modulation(c_all, w_ada, b_ada)

    xp = x_prompt.reshape(n_p, d)
    xs = x_sample.reshape(bs, d)
    outs = {k: [] for k in ("conv_p", "conv_s", "ssm_p", "ssm_s")}
    kv_p = [[[], []] for _ in ATT_PATTERNS]
    tiles = seq // TOK_TILE
    assert bs <= LANES and all(c.shape[2] == win for (win, _), pair in zip(ATT_PATTERNS, caches) for c in pair)
    cache_views = [jnp.transpose(c, (0, 1, 3, 4, 2)).reshape(depth, bs, ATT_OUT, c.shape[2])
                   for pair in caches for c in pair]
    shifted = None
    ssm_s_all = None
    w_in_t = jnp.swapaxes(w_in, 1, 2)
    w_chunks = _prep_w_in(w_in_t)

    for l in range(depth):
        mod_p = mods[l, :bp]
        mod_s = mods[l, bp:bp + bs]
        wd = _pad_lanes(w_in_t[l, OFF_DT:OFF_QKV].T)
        g1n = norm1_g[l].reshape(1, d)
        g2n = norm2_g[l].reshape(1, d)
        cw = conv_w[l]
        cb = conv_b[l].reshape(1, -1)
        dskip_full = jnp.repeat(d_skip[l], SSD_HEAD_DIM).reshape(1, -1)
        ssd_g = ssd_norm_g[l].reshape(1, -1)
        w_router = _pad_lanes(jnp.concatenate([w_router_expert[l], w_router_group[l]], axis=1))
        b_router = _pad_lanes(jnp.concatenate([b_router_expert[l], b_router_group[l]], axis=0).reshape(1, -1))

        z, xbc, dt_raw, a0, a1, a2, gates = _in_proj_prompt(
            l, xp, mod_p[:, 0:2 * d].reshape(bp, 1, 2 * d), g1n, w_chunks, wd, bp, seq)
        yn, ssm_new = _ssd_prompt(xbc, z, dt_raw, cw, cb, _pad_lanes(dt_bias[l].reshape(1, -1)),
                                  _pad_lanes(a_log[l].reshape(1, -1)), dskip_full, ssd_g, bp, seq)
        outs["ssm_p"].append(ssm_new.reshape(bp, SSD_GROUPS, SSD_HPG, SSD_HEAD_DIM, SSD_STATE))
        outs["conv_p"].append(xbc.reshape(bp, seq, SSD_CONV_DIM)[:, seq - (SSD_CONV - 1):])
        a_groups = [a0.reshape(bp, tiles, 1, TOK_TILE, 3 * ATT_OUT), a1, a2]
        o_list, lse_list = [], []
        for gi, (win, dil) in enumerate(ATT_PATTERNS):
            o_g, lse_g = _attn_prompt(a_groups[gi], gi, bp, seq)
            o_list.append(o_g)
            lse_list.append(lse_g)
            keep = min(win, seq)
            rows = TOK_TILE // dil
            if keep >= TOK_TILE:
                nt = keep // TOK_TILE
                tail = a_groups[gi][:, tiles - nt:, :, :, ATT_OUT:3 * ATT_OUT]
                tail = jnp.transpose(tail, (0, 1, 3, 2, 4)).reshape(bp, keep, 2 * ATT_OUT)
            else:
                assert dil == 1
                tail = a_groups[gi][:, tiles - 1, 0, TOK_TILE - keep:, ATT_OUT:3 * ATT_OUT]
            tail = tail.astype(F32)
            kv_p[gi][0].append(tail[:, :, 0:ATT_OUT].reshape(bp, keep, ATT_HEADS, ATT_HEAD_DIM))
            kv_p[gi][1].append(tail[:, :, ATT_OUT:2 * ATT_OUT].reshape(bp, keep, ATT_HEADS, ATT_HEAD_DIM))
        mod_post = jnp.concatenate([mod_p[:, 2 * d:3 * d], mod_p[:, 3 * d:5 * d]], axis=1).reshape(bp, 1, 3 * d)
        x1, rows, slot, cnt = _post(xp, yn, (o_list, lse_list), gates, mod_post, g2n,
                                    w_ssd_proj[l].astype(BF16), w_attn_proj[l].astype(BF16),
                                    w_out[l].astype(BF16), w_router, b_router, passes=1, merge_attn=True,
                                    tm=TOK_TILE, rows_per_mod=seq, h2_dtype=F32)
        pos = slot[:, 0].astype(jnp.int32)
        blk, e_lo, e_hi, n_act, z_need, z_blk = _moe_plan(cnt[0, :MOE_CLASSES].astype(jnp.int32), n_p)
        xs_sorted = _dispatch(pos, z_need, z_blk, rows, n_p)
        ys_sorted = _moe_sparse(l, (blk, e_lo, e_hi, n_act), xs_sorted, w_router, b_router,
                                w_exp_gate, w_exp_up, w_exp_down, n_p)
        xp = _combine(pos, ys_sorted, x1, mod_p[:, 5 * d:6 * d].reshape(bp, 1, d), seq,
                      final_g=final_norm_g.reshape(1, d) if l == depth - 1 else None)

        u = _in_proj_sample(l, xs, mod_s[:, 0:2 * d], g1n, w_in_t)
        z_s = u[:, 0:OFF_XBC]
        xbc_s = u[:, OFF_XBC:OFF_DT]
        dt_s = _pad_lanes(u[:, OFF_DT:OFF_QKV])
        qkv_s = u[:, OFF_QKV:OFF_GATE]
        gates_s = u[:, OFF_GATE:IN_WIDTH]
        yn_s, conv_new, ssm_s_all = _step_sample(
            l, z_s, xbc_s, dt_s, state_conv, state_ssm.reshape(depth, bs, SSD_INNER, SSD_STATE), ssm_s_all,
            cw, cb, jnp.repeat(dt_bias[l], SSD_HEAD_DIM).reshape(1, -1),
            jnp.repeat(a_log[l], SSD_HEAD_DIM).reshape(1, -1), dskip_full, ssd_g)
        outs["conv_s"].append(conv_new)
        shifted, o_s = _cache_attn(l, qkv_s.T, cache_views, shifted)
        mod_post_s = jnp.concatenate([mod_s[:, 2 * d:3 * d], mod_s[:, 3 * d:5 * d]], axis=1)
        x1_s, h2_s, gate_s = _post(xs, yn_s.reshape(bs, SSD_INNER), o_s, gates_s, mod_post_s,
                                   g2n, w_ssd_proj[l], w_attn_proj[l], w_out[l], w_router, b_router,
                                   passes=3, merge_attn=False, tm=bs, rows_per_mod=1, h2_dtype=F32)
        xs = _moe(l, h2_s, gate_s, x1_s, mod_s[:, 5 * d:6 * d], w_exp_gate, w_exp_up, w_exp_down,
                  passes=3, tm=bs, rows_per_mod=1)

    fg = final_norm_g.reshape(1, d)
    y_prompt = xp.reshape(bp, seq, d)
    y_sample = _final_norm(xs, fg, bs).reshape(bs, 1, d)

    shifted = [jnp.transpose(s.reshape(depth, bs, ATT_HEADS, ATT_HEAD_DIM, s.shape[3]), (0, 1, 4, 2, 3))
               for s in shifted]

    st = jnp.stack
    res = [y_prompt, y_sample, st(outs["conv_p"]), st(outs["conv_s"]), st(outs["ssm_p"]),
           ssm_s_all.reshape(state_ssm.shape)]
    for gi in range(ATT_GROUPS):
        res += [st(kv_p[gi][0]), shifted[2 * gi], st(kv_p[gi][1]), shifted[2 * gi + 1]]
    return tuple(res)
```

```python
import functools
import math

import jax
import jax.numpy as jnp
from jax import lax
from jax.experimental import pallas as pl
from jax.experimental.pallas import tpu as pltpu

F32 = jnp.float32
BF16 = jnp.bfloat16

D_MODEL = 1024
SSD_INNER = 1024
SSD_HEAD_DIM = 64
SSD_HEADS = 16
SSD_GROUPS = 2
SSD_HPG = 8
SSD_STATE = 128
SSD_CONV = 4
SSD_CHUNK = 128
SSD_CONV_DIM = SSD_INNER + 2 * SSD_GROUPS * SSD_STATE
ATT_PATTERNS = ((128, 1), (512, 4), (2048, 16))
ATT_GROUPS = 3
ATT_HEADS = 8
ATT_HEAD_DIM = 64
ATT_BLOCK = 128
ATT_OUT = ATT_HEADS * ATT_HEAD_DIM
MOE_GROUPS = 4
MOE_PER_GROUP = 4
MOE_EXPERTS = 16
MOE_FF = 512
MOE_PAIRS = 6
MOE_CLASSES = MOE_GROUPS * MOE_PAIRS
MOE_TILE = 512
ROW_SUB = 8
IN_QKV = 3 * ATT_GROUPS * ATT_OUT
IN_GATE = 2 * D_MODEL
OFF_XBC = SSD_INNER
OFF_DT = OFF_XBC + SSD_CONV_DIM
OFF_QKV = OFF_DT + SSD_HEADS
OFF_GATE = OFF_QKV + IN_QKV
IN_WIDTH = OFF_GATE + IN_GATE
EPS = 1e-6

LANES = 128
VMEM_LIMIT = 56 * 1024 * 1024


def _cparams(sem, vmem=VMEM_LIMIT):
    return pltpu.CompilerParams(dimension_semantics=sem, vmem_limit_bytes=vmem)


def _split2(a):
    hi = a.astype(BF16)
    lo = (a - hi.astype(F32)).astype(BF16)
    return hi, lo


def _split3(a):
    hi = a.astype(BF16)
    r = a - hi.astype(F32)
    mid = r.astype(BF16)
    lo = (r - mid.astype(F32)).astype(BF16)
    return hi, mid, lo


def _dot(a, b):
    return jnp.dot(a, b, preferred_element_type=F32)


def _dot_nt(a, b):
    return lax.dot_general(a, b, (((1,), (1,)), ((), ())), preferred_element_type=F32)


def _mm(a, w, passes):
    if passes == 1:
        return _dot(a.astype(BF16), w.astype(BF16))
    a = a.astype(F32)
    w = w.astype(F32)
    a_hi, a_lo = _split2(a)
    w_hi, w_lo = _split2(w)
    return _dot(a_hi, w_hi) + (_dot(a_lo, w_hi) + _dot(a_hi, w_lo))


def _mm01(a, e01, terms):
    parts = _split3(a) if terms == 3 else _split2(a)
    out = _dot(parts[0], e01)
    for p in parts[1:]:
        out = out + _dot(p, e01)
    return out


def _sigmoid(x):
    return 1.0 / (1.0 + jnp.exp(-x))


def _silu(x):
    return x * _sigmoid(x)


def _softplus(x):
    return jnp.maximum(x, 0.0) + jnp.log(1.0 + jnp.exp(-jnp.abs(x)))


def _head_expand_matrix(rows, n_heads, width):
    r = lax.broadcasted_iota(jnp.int32, (rows, n_heads * width), 0)
    c = lax.broadcasted_iota(jnp.int32, (rows, n_heads * width), 1)
    return jnp.where((c // width) == r, 1.0, 0.0).astype(BF16)


def _rms_modulate(x, g, sh, sc):
    ms = jnp.mean(x * x, axis=-1, keepdims=True)
    y = x * lax.rsqrt(ms + EPS) * g
    return y * (1.0 + sc) + sh


def _mod_kernel(c_ref, w_ref, b_ref, o_ref):
    a = _silu(c_ref[...])
    o_ref[...] = _mm(a, w_ref[...], 3) + b_ref[...]


def _modulation(c_all, w_ada, b_ada):
    depth, d, n6 = w_ada.shape
    rows = c_all.shape[0]
    tn = 1024
    return pl.pallas_call(
        _mod_kernel,
        out_shape=jax.ShapeDtypeStruct((depth, rows, n6), F32),
        grid=(depth, n6 // tn),
        in_specs=[
            pl.BlockSpec((rows, d), lambda l, j: (0, 0)),
            pl.BlockSpec((None, d, tn), lambda l, j: (l, 0, j)),
            pl.BlockSpec((None, 1, tn), lambda l, j: (l, 0, j)),
        ],
        out_specs=pl.BlockSpec((None, rows, tn), lambda l, j: (l, 0, j)),
        compiler_params=_cparams(("arbitrary", "arbitrary")),
        name="adaln_mod",
    )(c_all, w_ada, b_ada.reshape(depth, 1, n6))


_IN_CHUNK = 512
TOK_TILE = 512

_W_CHUNK_ROWS = (tuple(range(0, OFF_DT, _IN_CHUNK))
                 + tuple(range(OFF_QKV, OFF_GATE, _IN_CHUNK)) + tuple(range(OFF_GATE, IN_WIDTH, _IN_CHUNK)))
_CH_Z = (0, 1)
_CH_XBC = (2, 3, 4)
_CH_QKV0 = 5
_CH_GATES = (14, 15, 16, 17)


def _w_prep_kernel(starts_ref, w_ref, o_ref):
    del starts_ref
    o_ref[...] = w_ref[0].T.astype(BF16)


def _prep_w_in(w_in_t):
    depth, _, d = w_in_t.shape
    n_ch = len(_W_CHUNK_ROWS)
    starts = jnp.asarray(_W_CHUNK_ROWS, jnp.int32)
    return pl.pallas_call(
        _w_prep_kernel,
        out_shape=jax.ShapeDtypeStruct((depth, n_ch, d, _IN_CHUNK), BF16),
        grid_spec=pltpu.PrefetchScalarGridSpec(
            num_scalar_prefetch=1,
            grid=(depth, n_ch),
            in_specs=[pl.BlockSpec((pl.Element(1), pl.Element(_IN_CHUNK), pl.Element(d)),
                                   lambda l, j, st: (l, pl.multiple_of(st[j], 16), 0))],
            out_specs=pl.BlockSpec((None, None, d, _IN_CHUNK), lambda l, j, st: (l, j, 0, 0)),
        ),
        compiler_params=_cparams(("arbitrary", "arbitrary")),
        name="w_in_prep",
    )(starts, w_in_t)


def _in_kernel(x_ref, mod_ref, g_ref, w_ref, wd_ref,
               z_ref, xbc_ref, dt_ref, a0_ref, a1_ref, a2_ref, gates_ref, h_scr):
    sh = mod_ref[:, 0:D_MODEL]
    sc = mod_ref[:, D_MODEL:2 * D_MODEL]
    hf = _rms_modulate(x_ref[...], g_ref[...], sh, sc)
    n_cb = D_MODEL // LANES
    for cbk in range(n_cb):
        h_scr[cbk] = hf[:, cbk * LANES:(cbk + 1) * LANES]
    h = hf.astype(BF16)
    qkv_chunks = lambda gi: tuple(_CH_QKV0 + which * ATT_GROUPS + gi for which in range(3))
    for chunks, o_ref in ((_CH_Z, z_ref), (_CH_XBC, xbc_ref), (qkv_chunks(0), a0_ref), (_CH_GATES, gates_ref)):
        for i, ch in enumerate(chunks):
            o_ref[:, i * _IN_CHUNK:(i + 1) * _IN_CHUNK] = _dot(h, w_ref[ch]).astype(o_ref.dtype)
    dt_ref[...] = _dot(h, wd_ref[...].astype(BF16))
    for gi, a_ref in ((1, a1_ref), (2, a2_ref)):
        d, r_len, _ = a_ref.shape
        hp = jnp.concatenate(
            [jnp.concatenate([h_scr[cbk, pl.ds(r, r_len, stride=d), :] for cbk in range(n_cb)], axis=1)
             for r in range(d)], axis=0).astype(BF16)
        for i, ch in enumerate(qkv_chunks(gi)):
            res = _dot(hp, w_ref[ch]).astype(BF16)
            a_ref[:, :, i * _IN_CHUNK:(i + 1) * _IN_CHUNK] = res.reshape(d, r_len, _IN_CHUNK)


def _in_proj_prompt(layer, x, mod, norm_g, w_chunks, wd, bsz, seq):
    n, d = x.shape
    tm = TOK_TILE
    tiles_per_seq = seq // tm
    qw = 3 * ATT_OUT

    def const(shape):
        return pl.BlockSpec(shape, lambda i: (0, 0), pipeline_mode=pl.Buffered(1))

    a_shapes, a_specs = [], []
    for _, dil in ATT_PATTERNS[1:]:
        a_shapes.append(jax.ShapeDtypeStruct((bsz, tiles_per_seq, dil, tm // dil, qw), BF16))
        a_specs.append(pl.BlockSpec((None, None, dil, tm // dil, qw),
                                    lambda i: (i // tiles_per_seq, i % tiles_per_seq, 0, 0, 0)))
    return pl.pallas_call(
        _in_kernel,
        out_shape=(
            jax.ShapeDtypeStruct((n, SSD_INNER), BF16),
            jax.ShapeDtypeStruct((n, SSD_CONV_DIM), F32),
            jax.ShapeDtypeStruct((n, LANES), F32),
            jax.ShapeDtypeStruct((n, qw), BF16),
            a_shapes[0], a_shapes[1],
            jax.ShapeDtypeStruct((n, IN_GATE), BF16),
        ),
        grid=(n // tm,),
        in_specs=[
            pl.BlockSpec((tm, d), lambda i: (i, 0)),
            pl.BlockSpec((None, 1, 2 * d), lambda i: (i // tiles_per_seq, 0, 0)),
            const((1, d)),
            pl.BlockSpec((None,) + w_chunks.shape[1:], lambda i: (layer, 0, 0, 0), pipeline_mode=pl.Buffered(1)),
            const(wd.shape),
        ],
        out_specs=(
            pl.BlockSpec((tm, SSD_INNER), lambda i: (i, 0)),
            pl.BlockSpec((tm, SSD_CONV_DIM), lambda i: (i, 0)),
            pl.BlockSpec((tm, LANES), lambda i: (i, 0)),
            pl.BlockSpec((tm, qw), lambda i: (i, 0)),
            a_specs[0], a_specs[1],
            pl.BlockSpec((tm, IN_GATE), lambda i: (i, 0)),
        ),
        scratch_shapes=[pltpu.VMEM((d // LANES, tm, LANES), F32)],
        compiler_params=_cparams(("arbitrary",)),
        name="in_proj_prompt",
    )(x, mod, norm_g, w_chunks, wd)


def _in_small_kernel(x_ref, mod_ref, g_ref, w_ref, o_ref):
    sh = mod_ref[:, 0:D_MODEL]
    sc = mod_ref[:, D_MODEL:2 * D_MODEL]
    h = _rms_modulate(x_ref[...], g_ref[...], sh, sc)
    h_hi, h_lo = _split2(h)
    w_hi, w_lo = _split2(w_ref[...])
    o_ref[...] = _dot_nt(h_hi, w_hi) + (_dot_nt(h_lo, w_hi) + _dot_nt(h_hi, w_lo))


def _in_proj_sample(layer, x, mod, norm_g, w_in_t, tn=1024):
    m, d = x.shape
    width = w_in_t.shape[1]
    return pl.pallas_call(
        _in_small_kernel,
        out_shape=jax.ShapeDtypeStruct((m, width), F32),
        grid=(pl.cdiv(width, tn),),
        in_specs=[
            pl.BlockSpec((m, d), lambda j: (0, 0)),
            pl.BlockSpec((m, 2 * d), lambda j: (0, 0)),
            pl.BlockSpec((1, d), lambda j: (0, 0)),
            pl.BlockSpec((None, tn, d), lambda j: (layer, j, 0)),
        ],
        out_specs=pl.BlockSpec((m, tn), lambda j: (0, j)),
        compiler_params=_cparams(("arbitrary",)),
        name="in_proj_sample",
    )(x, mod, norm_g, w_in_t)


_SSD_SUB = 4


def _ssd_kernel(xbc_ref, z_ref, dt_ref, cw_ref, cb_ref, dtb_ref, alog_ref, dskip_ref, ng_ref,
                yn_ref, hout_ref, h_scr, xp_scr):
    q = SSD_CHUNK
    c = pl.program_id(1)

    @pl.when(c == 0)
    def _():
        h_scr[...] = jnp.zeros_like(h_scr)
        xp_scr[0:8, :] = jnp.zeros((8, SSD_CONV_DIM), F32)

    for sub in range(_SSD_SUB):
        rows = slice(sub * q, (sub + 1) * q)
        _ssd_chunk(xbc_ref.at[rows], z_ref.at[rows], dt_ref.at[rows], cw_ref, cb_ref, dtb_ref, alog_ref,
                   dskip_ref, ng_ref, yn_ref.at[rows], h_scr, xp_scr)

    @pl.when(c == pl.num_programs(1) - 1)
    def _():
        hout_ref[...] = h_scr[...]


def _ssd_chunk(xbc_ref, z_ref, dt_ref, cw_ref, cb_ref, dtb_ref, alog_ref, dskip_ref, ng_ref,
               yn_ref, h_scr, xp_scr):
    q = SSD_CHUNK
    xp_scr[8:8 + q, :] = xbc_ref[...]
    acc = cb_ref[...] + cw_ref[3:4, :] * xp_scr[8:8 + q, :]
    for k in range(SSD_CONV - 1):
        acc = acc + cw_ref[k:k + 1, :] * xp_scr[5 + k:5 + k + q, :]
    xp_scr[0:8, :] = xp_scr[q:q + 8, :]
    xc = _silu(acc)
    xs = xc[:, 0:SSD_INNER]
    xs_bf = xs.astype(BF16)

    lane_q = lax.broadcasted_iota(jnp.int32, (q, LANES), 1)
    row_q = lax.broadcasted_iota(jnp.int32, (q, LANES), 0)
    causal = row_q >= lane_q
    tri = jnp.where(causal, 1.0, 0.0).astype(BF16)
    tri_t = jnp.where(lane_q >= row_q, 1.0, 0.0).astype(BF16)
    e_heads = _head_expand_matrix(LANES, SSD_HEADS, SSD_HEAD_DIM)

    dt = _softplus(dt_ref[...] + dtb_ref[...])
    dt = jnp.where(lane_q < SSD_HEADS, dt, 0.0)
    a = -jnp.exp(alog_ref[...])
    d_a = dt * a
    cum = _mm01_left(tri, d_a)
    d_a_t = d_a.T
    dt_t = dt.T
    cum_t = _mm01(d_a_t, tri_t, 3)
    cum_last = cum[q - 1:q, :]
    exp_cum = jnp.exp(cum)
    dec_end = jnp.exp(cum_last - cum)
    stack = jnp.concatenate([exp_cum, dec_end * dt], axis=0)
    full = _mm01(stack, e_heads, 2)
    exp_cum_full = full[0:q]
    w_full = full[q:2 * q]
    chunk_dec_t = jnp.exp(cum_t[:, q - 1:q])

    lane_half = lax.broadcasted_iota(jnp.int32, (q, LANES), 1) < SSD_HEAD_DIM
    y_parts = []
    for g in range(SSD_GROUPS):
        b_off = SSD_INNER + g * SSD_STATE
        c_off = SSD_INNER + SSD_GROUPS * SSD_STATE + g * SSD_STATE
        bm = xc[:, b_off:b_off + SSD_STATE].astype(BF16)
        cm = xc[:, c_off:c_off + SSD_STATE].astype(BF16)
        cbm = _dot_nt(cm, bm)
        gw = SSD_HPG * SSD_HEAD_DIM
        g0 = g * gw
        yd = []
        for pair in range(SSD_HPG // 2):
            x_pair = xs_bf[:, g0 + pair * LANES:g0 + (pair + 1) * LANES]
            halves = []
            for hh in range(2):
                h = g * SSD_HPG + pair * 2 + hh
                seg = cum[:, h:h + 1] - cum_t[h:h + 1, :]
                dec = jnp.exp(jnp.where(causal, seg, -jnp.inf))
                m_h = (cbm * dec * dt_t[h:h + 1, :]).astype(BF16)
                halves.append(_dot(m_h, x_pair))
            yd.append(jnp.where(lane_half, halves[0], halves[1]))
        y_diag = jnp.concatenate(yd, axis=1)
        h_g = h_scr[g0:g0 + gw, :]
        y_off = _dot_nt(cm, h_g.astype(BF16)) * exp_cum_full[:, g0:g0 + gw]
        y_parts.append(y_diag + y_off)
        xw = (xs[:, g0:g0 + gw] * w_full[:, g0:g0 + gw])
        st = _dot(xw.T.astype(BF16), bm)
        for e in range(SSD_HPG):
            h = g * SSD_HPG + e
            r0 = g0 + e * SSD_HEAD_DIM
            h_scr[r0:r0 + SSD_HEAD_DIM, :] = (h_scr[r0:r0 + SSD_HEAD_DIM, :] * chunk_dec_t[h:h + 1, :]
                                              + st[e * SSD_HEAD_DIM:(e + 1) * SSD_HEAD_DIM, :])

    y = jnp.concatenate(y_parts, axis=1) + dskip_ref[...] * xs
    y = y * _silu(z_ref[...].astype(F32))
    outs = []
    for g in range(SSD_GROUPS):
        gw = SSD_HPG * SSD_HEAD_DIM
        yg = y[:, g * gw:(g + 1) * gw]
        ms = jnp.mean(yg * yg, axis=-1, keepdims=True)
        outs.append(yg * lax.rsqrt(ms + EPS) * ng_ref[:, g * gw:(g + 1) * gw])
    yn_ref[...] = jnp.concatenate(outs, axis=1).astype(yn_ref.dtype)


def _mm01_left(tri01, a):
    hi, mid, lo = _split3(a)
    return _dot(tri01, hi) + (_dot(tri01, mid) + _dot(tri01, lo))


def _ssd_prompt(xbc, z, dt_raw, conv_w, conv_b, dt_bias, a_log, d_skip_full, ssd_norm_g, bsz, seq):
    q = SSD_CHUNK * _SSD_SUB
    nc = seq // q
    row = lambda b, c: (b * nc + c, 0)
    const = lambda b, c: (0, 0)
    return pl.pallas_call(
        _ssd_kernel,
        out_shape=(
            jax.ShapeDtypeStruct((bsz * seq, SSD_INNER), BF16),
            jax.ShapeDtypeStruct((bsz, SSD_INNER, SSD_STATE), F32),
        ),
        grid=(bsz, nc),
        in_specs=[
            pl.BlockSpec((q, SSD_CONV_DIM), row),
            pl.BlockSpec((q, SSD_INNER), row),
            pl.BlockSpec((q, LANES), row),
            pl.BlockSpec((SSD_CONV, SSD_CONV_DIM), const),
            pl.BlockSpec((1, SSD_CONV_DIM), const),
            pl.BlockSpec((1, LANES), const),
            pl.BlockSpec((1, LANES), const),
            pl.BlockSpec((1, SSD_INNER), const),
            pl.BlockSpec((1, SSD_INNER), const),
        ],
        out_specs=(
            pl.BlockSpec((q, SSD_INNER), row),
            pl.BlockSpec((None, SSD_INNER, SSD_STATE), lambda b, c: (b, 0, 0)),
        ),
        scratch_shapes=[
            pltpu.VMEM((SSD_INNER, SSD_STATE), F32),
            pltpu.VMEM((SSD_CHUNK + 8, SSD_CONV_DIM), F32),
        ],
        compiler_params=_cparams(("arbitrary", "arbitrary")),
        name="ssd_prompt",
    )(xbc, z, dt_raw, conv_w, conv_b, dt_bias, a_log, d_skip_full, ssd_norm_g)


def _attn_kernel(q_ref, kc_ref, kp_ref, vc_ref, vp_ref, o_out_ref, lse_out_ref,
                 q_ref_s, kwin, vwin, o_ref, lse_ref, *, tq, band):
    blk = ATT_BLOCK
    j = pl.program_id(2)
    q_ref_s[...] = (q_ref[...] * (ATT_HEAD_DIM ** -0.5)).astype(BF16).reshape(tq, ATT_OUT)
    kwin[0:blk, :] = kp_ref[...].reshape(blk, ATT_OUT)
    kwin[blk:blk + tq, :] = kc_ref[...].reshape(tq, ATT_OUT)
    vwin[0:blk, :] = vp_ref[...].reshape(blk, ATT_OUT)
    vwin[blk:blk + tq, :] = vc_ref[...].reshape(tq, ATT_OUT)
    q_ref = q_ref_s

    qi = lax.broadcasted_iota(jnp.int32, (blk, 2 * blk), 0)
    ki = lax.broadcasted_iota(jnp.int32, (blk, 2 * blk), 1)
    dist = qi + blk - ki
    in_band = (dist >= 0) & (dist <= band)
    lane = lax.broadcasted_iota(jnp.int32, (blk, LANES), 1)
    lane_half = lane < ATT_HEAD_DIM
    scale = ATT_HEAD_DIM ** -0.5
    zero_bf = jnp.zeros((blk, LANES), BF16)

    for i in range(tq // blk):
        if i == 0:
            valid = in_band & ((ki >= blk) | (j > 0))
        else:
            valid = in_band
        lse_tile = jnp.zeros((blk, LANES), F32)
        for hp in range(ATT_HEADS // 2):
            c0 = hp * LANES
            q_pair = q_ref[i * blk:(i + 1) * blk, c0:c0 + LANES]
            k_pair = kwin[i * blk:(i + 2) * blk, c0:c0 + LANES]
            v_pair = vwin[i * blk:(i + 2) * blk, c0:c0 + LANES]
            halves = []
            for hh in range(2):
                q_m = jnp.where(lane_half if hh == 0 else jnp.logical_not(lane_half), q_pair, zero_bf)
                s = _dot_nt(q_m, k_pair)
                s = jnp.where(valid, s, -jnp.inf)
                m = jnp.max(s, axis=-1, keepdims=True)
                e = jnp.exp(s - m)
                den = jnp.sum(e, axis=-1, keepdims=True)
                pv = _dot(e.astype(BF16), v_pair)
                halves.append(pv / den)
                lse = m + jnp.log(den)
                head = hp * 2 + hh
                lse_tile = jnp.where((lane == head) | (lane == ATT_HEADS + head), lse, lse_tile)
            o_ref[i * blk:(i + 1) * blk, c0:c0 + LANES] = jnp.where(lane_half, halves[0], halves[1]).astype(o_ref.dtype)
        lse_ref[i * blk:(i + 1) * blk, :] = lse_tile
    o_out_ref[...] = o_ref[...].reshape(o_out_ref.shape)
    lse_out_ref[...] = lse_ref[...].reshape(lse_out_ref.shape)


def _attn_prompt(a_g, gi, bsz, seq):
    win, dil = ATT_PATTERNS[gi]
    band = win // dil
    blk = ATT_BLOCK
    rows = TOK_TILE // dil
    tiles = seq // TOK_TILE
    length = seq // dil
    tq = min(512, length)
    nj = length // tq
    tq_tiles = tq // rows
    cur = lambda which: pl.BlockSpec((None, tq_tiles, None, rows, ATT_OUT),
                                     lambda b, r, j: (b, j, r, 0, which))
    if rows >= blk:
        prev = lambda which: pl.BlockSpec(
            (None, None, None, blk, ATT_OUT),
            lambda b, r, j: (b, jnp.maximum(j * tq_tiles - 1, 0), r, rows // blk - 1, which))
    else:
        prev = lambda which: pl.BlockSpec(
            (None, blk // rows, None, rows, ATT_OUT),
            lambda b, r, j: (b, jnp.maximum(j * (tq // blk) - 1, 0), r, 0, which))
    return pl.pallas_call(
        functools.partial(_attn_kernel, tq=tq, band=band),
        out_shape=(
            jax.ShapeDtypeStruct((bsz, tiles, dil, rows, ATT_OUT), BF16),
            jax.ShapeDtypeStruct((bsz, tiles, dil, rows, LANES), F32),
        ),
        grid=(bsz, dil, nj),
        in_specs=[cur(0), cur(1), prev(1), cur(2), prev(2)],
        out_specs=(
            pl.BlockSpec((None, tq_tiles, None, rows, ATT_OUT), lambda b, r, j: (b, j, r, 0, 0)),
            pl.BlockSpec((None, tq_tiles, None, rows, LANES), lambda b, r, j: (b, j, r, 0, 0)),
        ),
        scratch_shapes=[
            pltpu.VMEM((tq, ATT_OUT), BF16),
            pltpu.VMEM((blk + tq, ATT_OUT), BF16),
            pltpu.VMEM((blk + tq, ATT_OUT), BF16),
            pltpu.VMEM((tq, ATT_OUT), BF16),
            pltpu.VMEM((tq, LANES), F32),
        ],
        compiler_params=_cparams(("arbitrary", "arbitrary", "arbitrary")),
        name=f"attn_prompt_w{win}",
    )(a_g, a_g, a_g, a_g, a_g)


def _router_gates(logits):
    shape = logits.shape
    lane = lax.broadcasted_iota(jnp.int32, shape, 1)
    big = jnp.int32(1 << 20)
    neg = -jnp.inf
    is_grp = (lane >= MOE_EXPERTS) & (lane < MOE_EXPERTS + MOE_GROUPS)
    lg = jnp.where(is_grp, logits, neg)
    gm = jnp.max(lg, axis=-1, keepdims=True)
    g_lane = jnp.min(jnp.where(lg == gm, lane, big), axis=-1, keepdims=True)
    g_sum = jnp.sum(jnp.exp(lg - gm), axis=-1, keepdims=True)
    g_w = 1.0 / g_sum
    lo = (g_lane - MOE_EXPERTS) * MOE_PER_GROUP
    in_grp = (lane >= lo) & (lane < lo + MOE_PER_GROUP)
    le = jnp.where(in_grp, logits, neg)
    m1 = jnp.max(le, axis=-1, keepdims=True)
    i1 = jnp.min(jnp.where(le == m1, lane, big), axis=-1, keepdims=True)
    le2 = jnp.where(lane == i1, neg, le)
    m2 = jnp.max(le2, axis=-1, keepdims=True)
    i2 = jnp.min(jnp.where(le2 == m2, lane, big), axis=-1, keepdims=True)
    t = jnp.exp(m2 - m1)
    w1 = 1.0 / (1.0 + t)
    w2 = t / (1.0 + t)
    gate = jnp.where(lane == i1, g_w * w1, jnp.where(lane == i2, g_w * w2, 0.0))
    e_lo = jnp.minimum(i1, i2)
    e_hi = jnp.maximum(i1, i2)
    lo_l = e_lo & (MOE_PER_GROUP - 1)
    hi_l = e_hi & (MOE_PER_GROUP - 1)
    pair = ((lo_l * (7 - lo_l)) >> 1) + (hi_l - lo_l - 1)
    pair = jnp.where(pair == 2, 4, jnp.where(pair == 3, 2, jnp.where(pair == 4, 3, pair)))
    cls = (e_lo >> 2) * MOE_PAIRS + pair
    return gate, cls


def _post_kernel(*refs, passes, merge_attn):
    if merge_attn:
        (x_ref, yn_ref, o0_ref, o1_ref, o2_ref, l0_ref, l1_ref, l2_ref, gates_ref, mod_ref, n2_ref,
         wssd_ref, wattn_ref, wout_ref, wr_ref, br_ref, x1_ref, h2_ref, pos_ref, cnt_ref,
         o1_scr, o2_scr, l1_scr, l2_scr) = refs
        for src_ref, dst_ref in ((o1_ref, o1_scr), (o2_ref, o2_scr), (l1_ref, l1_scr), (l2_ref, l2_scr)):
            dil, r_len, width = src_ref.shape
            for r in range(dil):
                blk_r = src_ref[r].astype(F32)
                for cbk in range(width // LANES):
                    dst_ref[cbk, pl.ds(r, r_len, stride=dil), :] = blk_r[:, cbk * LANES:(cbk + 1) * LANES]
        o_nat = [o0_ref[...].astype(F32)] + [
            jnp.concatenate([scr[cbk] for cbk in range(ATT_OUT // LANES)], axis=1) for scr in (o1_scr, o2_scr)]
        l0, l1, l2 = l0_ref[...], l1_scr[0], l2_scr[0]
        mx = jnp.maximum(jnp.maximum(l0, l1), l2)
        e0, e1, e2 = jnp.exp(l0 - mx), jnp.exp(l1 - mx), jnp.exp(l2 - mx)
        inv = 1.0 / (e0 + e1 + e2)
        lane = lax.broadcasted_iota(jnp.int32, l0.shape, 1)
        e8 = _head_expand_matrix(LANES, ATT_HEADS, ATT_HEAD_DIM)
        r = lax.broadcasted_iota(jnp.int32, e8.shape, 0)
        c = lax.broadcasted_iota(jnp.int32, e8.shape, 1)
        e8 = jnp.where((c // ATT_HEAD_DIM) == (r - ATT_HEADS), 1.0, e8.astype(F32)).astype(BF16)
        o = None
        for e_g, o_g in zip((e0, e1, e2), o_nat):
            w = e_g * inv
            hi, lo = _split2(w)
            w_exp = _dot(jnp.where(lane < ATT_HEADS, hi, lo), e8)
            term = w_exp * o_g
            o = term if o is None else o + term
    else:
        (x_ref, yn_ref, o_ref, gates_ref, mod_ref, n2_ref,
         wssd_ref, wattn_ref, wout_ref, wr_ref, br_ref, x1_ref, h2_ref, gate_ref) = refs
        o = o_ref[...]
    d = D_MODEL
    g1 = mod_ref[:, 0:d]
    sh2 = mod_ref[:, d:2 * d]
    sc2 = mod_ref[:, 2 * d:3 * d]
    ssd_branch = _mm(yn_ref[...], wssd_ref[...], passes)
    attn_branch = _mm(o, wattn_ref[...], passes)
    ga = gates_ref[:, 0:d].astype(F32)
    gb = gates_ref[:, d:2 * d].astype(F32)
    mixed = _sigmoid(ga) * ssd_branch + _sigmoid(gb) * attn_branch
    x1 = x_ref[...] + g1 * _mm(mixed, wout_ref[...], passes)
    x1_ref[...] = x1
    h2 = _rms_modulate(x1, n2_ref[...], sh2, sc2)
    logits = _mm(h2, wr_ref[...], 3) + br_ref[...]
    gate, cls = _router_gates(logits)
    if not merge_attn:
        h2_ref[...] = h2
        gate_ref[...] = gate
        return
    tm = h2.shape[0]
    for cbk in range(ROW_SUB):
        h2_ref[pl.ds(cbk, tm, stride=ROW_SUB), :] = h2[:, cbk * LANES:(cbk + 1) * LANES]
    step = pl.program_id(0)

    @pl.when(step == 0)
    def _():
        cnt_ref[...] = jnp.zeros_like(cnt_ref)

    lane = lax.broadcasted_iota(jnp.int32, (tm, LANES), 1)
    onehot = lane == cls
    ri = lax.broadcasted_iota(jnp.int32, (tm, tm), 0)
    ci = lax.broadcasted_iota(jnp.int32, (tm, tm), 1)
    before = jnp.where(ci < ri, 1.0, 0.0).astype(BF16)
    seen = _dot(before, jnp.where(onehot, 1.0, 0.0).astype(BF16)) + cnt_ref[...]
    rank = jnp.sum(jnp.where(onehot, seen, 0.0), axis=-1, keepdims=True)
    n_tokens = tm * pl.num_programs(0)
    pos_ref[...] = jnp.broadcast_to(cls.astype(F32) * n_tokens + rank, (tm, LANES))
    cnt_ref[...] = cnt_ref[...] + jnp.sum(jnp.where(onehot, 1.0, 0.0), axis=0, keepdims=True)


def _post(x, yn, attn_in, gates, mod, norm2_g, wssd, wattn, wout, wr, br, *, passes, merge_attn,
          tm, rows_per_mod, h2_dtype):
    n, d = x.shape
    per_row_mod = rows_per_mod == 1
    if per_row_mod:
        mod_spec = pl.BlockSpec((tm, 3 * d), lambda i: (i, 0))
    else:
        tiles = rows_per_mod // tm
        mod_spec = pl.BlockSpec((None, 1, 3 * d), lambda i: (i // tiles, 0, 0))
    row = lambda w: pl.BlockSpec((tm, w), lambda i: (i, 0))
    const = lambda a: pl.BlockSpec(a.shape, lambda i: (0, 0), pipeline_mode=pl.Buffered(1))
    scratch = []
    if merge_attn:
        assert tm == TOK_TILE
        o_list, lse_list = attn_in
        attn_args = list(o_list) + list(lse_list)
        tiles = rows_per_mod // tm

        def tile_spec(a):
            _, _, dil, r_len, w = a.shape
            if dil == 1:
                return pl.BlockSpec((None, None, None, r_len, w), lambda i: (i // tiles, i % tiles, 0, 0, 0))
            return pl.BlockSpec((None, None, dil, r_len, w), lambda i: (i // tiles, i % tiles, 0, 0, 0))

        attn_specs = [tile_spec(a) for a in attn_args]
        scratch = [pltpu.VMEM((ATT_OUT // LANES, tm, LANES), F32), pltpu.VMEM((ATT_OUT // LANES, tm, LANES), F32),
                   pltpu.VMEM((1, tm, LANES), F32), pltpu.VMEM((1, tm, LANES), F32)]
    else:
        attn_args = [attn_in]
        attn_specs = [row(ATT_OUT)]
    if merge_attn:
        out_shape = (jax.ShapeDtypeStruct((n, d), F32), jax.ShapeDtypeStruct((n * ROW_SUB, LANES), F32),
                     jax.ShapeDtypeStruct((n, LANES), F32), jax.ShapeDtypeStruct((1, LANES), F32))
        out_specs = (row(d), pl.BlockSpec((tm * ROW_SUB, LANES), lambda i: (i, 0)), row(LANES),
                     pl.BlockSpec((1, LANES), lambda i: (0, 0)))
    else:
        out_shape = (jax.ShapeDtypeStruct((n, d), F32), jax.ShapeDtypeStruct((n, d), h2_dtype),
                     jax.ShapeDtypeStruct((n, LANES), F32))
        out_specs = (row(d), row(d), row(LANES))
    return pl.pallas_call(
        functools.partial(_post_kernel, passes=passes, merge_attn=merge_attn),
        out_shape=out_shape,
        grid=(n // tm,),
        in_specs=[row(d), row(SSD_INNER)] + attn_specs + [row(IN_GATE), mod_spec, const(norm2_g),
                                                          const(wssd), const(wattn), const(wout),
                                                          const(wr), const(br)],
        out_specs=out_specs,
        scratch_shapes=scratch,
        compiler_params=_cparams(("arbitrary",)),
        name="post_merge" if merge_attn else "post_sample",
    )(x, yn, *attn_args, gates, mod, norm2_g, wssd, wattn, wout, wr, br)


def _moe_kernel(h2_ref, gate_ref, x1_ref, g2_ref, wg_ref, wu_ref, wd_ref, x2_ref, acc_ref, *, passes):
    e = pl.program_id(1)

    @pl.when(e == 0)
    def _():
        acc_ref[...] = jnp.zeros_like(acc_ref)

    h2 = h2_ref[...]
    hg = _mm(h2, wg_ref[...], passes)
    hu = _mm(h2, wu_ref[...], passes)
    gate = gate_ref[...]
    lane = lax.broadcasted_iota(jnp.int32, gate.shape, 1)
    gcol = jnp.sum(jnp.where(lane == e, gate, 0.0), axis=-1, keepdims=True)
    act = _silu(hg) * hu * gcol
    acc_ref[...] += _mm(act, wd_ref[...], passes)

    @pl.when(e == pl.num_programs(1) - 1)
    def _():
        x2_ref[...] = x1_ref[...] + g2_ref[...] * acc_ref[...]


def _moe(layer, h2, gate, x1, g2, wg, wu, wd, *, passes, tm, rows_per_mod):
    n, d = x1.shape
    if rows_per_mod == 1:
        g2_spec = pl.BlockSpec((tm, d), lambda i, e: (i, 0))
    else:
        tiles = rows_per_mod // tm
        g2_spec = pl.BlockSpec((None, 1, d), lambda i, e: (i // tiles, 0, 0))
    row = lambda w: pl.BlockSpec((tm, w), lambda i, e: (i, 0))
    return pl.pallas_call(
        functools.partial(_moe_kernel, passes=passes),
        out_shape=jax.ShapeDtypeStruct((n, d), F32),
        grid=(n // tm, MOE_EXPERTS),
        in_specs=[
            row(d), row(LANES), row(d), g2_spec,
            pl.BlockSpec((None, None, d, MOE_FF), lambda i, e: (layer, e, 0, 0)),
            pl.BlockSpec((None, None, d, MOE_FF), lambda i, e: (layer, e, 0, 0)),
            pl.BlockSpec((None, None, MOE_FF, d), lambda i, e: (layer, e, 0, 0)),
        ],
        out_specs=row(d),
        scratch_shapes=[pltpu.VMEM((tm, d), F32)],
        compiler_params=_cparams(("arbitrary", "arbitrary")),
        name="moe_dense",
    )(h2, gate, x1, g2, wg, wu, wd)


_PAIR_LO = (0, 0, 1, 1, 0, 2)
_PAIR_HI = (1, 2, 2, 3, 3, 3)
_DMA_UNROLL = 8
_ROUTE_TILE = 2048


def _moe_plan(counts, n_tokens):
    tm = MOE_TILE
    blocks_per_class = n_tokens // tm
    ntile = (counts + tm - 1) // tm
    cum = jnp.cumsum(ntile)
    total = cum[-1]
    t_max = blocks_per_class + MOE_CLASSES
    t_eff = jnp.minimum(jnp.arange(t_max, dtype=jnp.int32), total - 1)
    cls = jnp.searchsorted(cum, t_eff, side="right").astype(jnp.int32)
    blk = cls * blocks_per_class + (t_eff - (cum - ntile)[cls])
    grp, pair = cls // MOE_PAIRS, cls % MOE_PAIRS
    e_lo = grp * MOE_PER_GROUP + jnp.asarray(_PAIR_LO, jnp.int32)[pair]
    e_hi = grp * MOE_PER_GROUP + jnp.asarray(_PAIR_HI, jnp.int32)[pair]
    z_need = ((counts % tm) != 0).astype(jnp.int32)
    z_blk = jnp.arange(MOE_CLASSES, dtype=jnp.int32) * blocks_per_class + jnp.maximum(ntile - 1, 0)
    return (blk.astype(jnp.int32), e_lo, e_hi, total.reshape(1).astype(jnp.int32), z_need, z_blk.astype(jnp.int32))


def _row_copy(src, src_row, dst, dst_row, sem):
    return pltpu.make_async_copy(src.at[pl.ds(pl.multiple_of(src_row * ROW_SUB, ROW_SUB), ROW_SUB), :],
                                 dst.at[pl.ds(pl.multiple_of(dst_row * ROW_SUB, ROW_SUB), ROW_SUB), :], sem)


def _dispatch_kernel(pos_ref, zneed_ref, zblk_ref, rows_ref, xs_ref, zero_scr, sem, zsem, *, tm):
    step = pl.program_id(0)
    tile_rows = MOE_TILE * ROW_SUB

    def zero_copy(c):
        start = pl.multiple_of(zblk_ref[c] * tile_rows, tile_rows)
        return pltpu.make_async_copy(zero_scr, xs_ref.at[pl.ds(start, tile_rows), :], zsem)

    @pl.when(step == 0)
    def _():
        zero_scr[...] = jnp.zeros_like(zero_scr)
        for c in range(MOE_CLASSES):
            @pl.when(zneed_ref[c] != 0)
            def _():
                zero_copy(c).start()
        for c in range(MOE_CLASSES):
            @pl.when(zneed_ref[c] != 0)
            def _():
                zero_copy(c).wait()

    base = step * tm

    def body(i8, carry):
        for u in range(_DMA_UNROLL):
            i = i8 * _DMA_UNROLL + u
            _row_copy(rows_ref, i, xs_ref, pos_ref[base + i], sem).start(priority=u % 2)
        return carry

    lax.fori_loop(0, tm // _DMA_UNROLL, body, 0)
    pltpu.make_async_copy(rows_ref, xs_ref.at[pl.ds(0, tm * ROW_SUB), :], sem).wait()


def _dispatch(pos, z_need, z_blk, rows, n_tokens, tm=_ROUTE_TILE):
    sorted_rows = MOE_CLASSES * n_tokens * ROW_SUB
    return pl.pallas_call(
        functools.partial(_dispatch_kernel, tm=tm),
        out_shape=jax.ShapeDtypeStruct((sorted_rows, LANES), F32),
        grid_spec=pltpu.PrefetchScalarGridSpec(
            num_scalar_prefetch=3,
            grid=(n_tokens // tm,),
            in_specs=[pl.BlockSpec((tm * ROW_SUB, LANES), lambda i, *_: (i, 0))],
            out_specs=pl.BlockSpec(memory_space=pl.ANY),
            scratch_shapes=[pltpu.VMEM((MOE_TILE * ROW_SUB, LANES), F32),
                            pltpu.SemaphoreType.DMA(()), pltpu.SemaphoreType.DMA(())],
        ),
        compiler_params=pltpu.CompilerParams(dimension_semantics=("arbitrary",), vmem_limit_bytes=VMEM_LIMIT,
                                             disable_bounds_checks=True),
        name="moe_dispatch",
    )(pos, z_need, z_blk, rows)


def _moe_sparse_kernel(blk_ref, elo_ref, ehi_ref, nact_ref, xs_ref, wr_ref, br_ref,
                       wg_lo_ref, wu_lo_ref, wd_lo_ref, wg_hi_ref, wu_hi_ref, wd_hi_ref, ys_ref):
    t = pl.program_id(0)
    tm = MOE_TILE

    @pl.when(t < nact_ref[0])
    def _():
        x = jnp.concatenate([xs_ref[pl.ds(c, tm, stride=ROW_SUB), :] for c in range(ROW_SUB)], axis=1)
        xb = x.astype(BF16)
        w_hi16, w_lo16 = _split2(wr_ref[...])
        logits = _dot(xb, w_hi16) + _dot(xb, w_lo16) + br_ref[...]
        lane = lax.broadcasted_iota(jnp.int32, logits.shape, 1)
        e_lo, e_hi = elo_ref[t], ehi_ref[t]
        g_lane = MOE_EXPERTS + (e_lo >> 2)
        pick = lambda ln: jnp.sum(jnp.where(lane == ln, logits, 0.0), axis=-1, keepdims=True)
        l_lo, l_hi, l_g = pick(e_lo), pick(e_hi), pick(g_lane)
        is_grp = (lane >= MOE_EXPERTS) & (lane < MOE_EXPERTS + MOE_GROUPS)
        g_w = 1.0 / jnp.sum(jnp.where(is_grp, jnp.exp(logits - l_g), 0.0), axis=-1, keepdims=True)
        w_a = g_w / (1.0 + jnp.exp(l_hi - l_lo))
        w_b = g_w / (1.0 + jnp.exp(l_lo - l_hi))
        bf = lambda w_ref: w_ref[...].astype(BF16)
        act_a = (_silu(_dot(xb, bf(wg_lo_ref))) * _dot(xb, bf(wu_lo_ref)) * w_a).astype(BF16)
        act_b = (_silu(_dot(xb, bf(wg_hi_ref))) * _dot(xb, bf(wu_hi_ref)) * w_b).astype(BF16)
        y = _dot(act_a, bf(wd_lo_ref)) + _dot(act_b, bf(wd_hi_ref))
        for c in range(ROW_SUB):
            ys_ref[pl.ds(c, tm, stride=ROW_SUB), :] = y[:, c * LANES:(c + 1) * LANES]


def _moe_sparse(layer, plan, xs, wr, br, wg, wu, wd, n_tokens):
    blk, e_lo, e_hi, n_act = plan
    tm = MOE_TILE
    t_max = blk.shape[0]
    d = D_MODEL
    tile = pl.BlockSpec((tm * ROW_SUB, LANES), lambda t, blk, lo, hi, na: (blk[t], 0))
    w_lo = lambda shape: pl.BlockSpec((None, None) + shape, lambda t, blk, lo, hi, na: (layer, lo[t], 0, 0))
    w_hi = lambda shape: pl.BlockSpec((None, None) + shape, lambda t, blk, lo, hi, na: (layer, hi[t], 0, 0))
    const = lambda a: pl.BlockSpec(a.shape, lambda t, *_: (0, 0))
    return pl.pallas_call(
        _moe_sparse_kernel,
        out_shape=jax.ShapeDtypeStruct(xs.shape, F32),
        grid_spec=pltpu.PrefetchScalarGridSpec(
            num_scalar_prefetch=4,
            grid=(t_max,),
            in_specs=[tile, const(wr), const(br),
                      w_lo((d, MOE_FF)), w_lo((d, MOE_FF)), w_lo((MOE_FF, d)),
                      w_hi((d, MOE_FF)), w_hi((d, MOE_FF)), w_hi((MOE_FF, d))],
            out_specs=tile,
        ),
        compiler_params=_cparams(("arbitrary",)),
        name="moe_sparse",
    )(blk, e_lo, e_hi, n_act, xs, wr, br, wg, wu, wd, wg, wu, wd)


def _combine_kernel(pos_ref, ys_ref, x1_ref, g2_ref, *rest, tm, final):
    if final:
        fg_ref, x2_ref, buf, sem = rest
    else:
        x2_ref, buf, sem = rest
    base = pl.program_id(0) * tm

    def body(i8, carry):
        for u in range(_DMA_UNROLL):
            i = i8 * _DMA_UNROLL + u
            _row_copy(ys_ref, pos_ref[base + i], buf, i, sem).start(priority=u % 2)
        return carry

    lax.fori_loop(0, tm // _DMA_UNROLL, body, 0)
    pltpu.make_async_copy(ys_ref.at[pl.ds(0, tm * ROW_SUB), :], buf, sem).wait()
    y = jnp.concatenate([buf[pl.ds(c, tm, stride=ROW_SUB), :] for c in range(ROW_SUB)], axis=1)
    x2 = x1_ref[...] + g2_ref[...] * y
    if final:
        ms = jnp.mean(x2 * x2, axis=-1, keepdims=True)
        x2 = x2 * lax.rsqrt(ms + EPS) * fg_ref[...]
    x2_ref[...] = x2


def _combine(pos, ys, x1, g2, rows_per_mod, final_g=None, tm=_ROUTE_TILE):
    n, d = x1.shape
    tiles = rows_per_mod // tm
    final = final_g is not None
    extra_specs = [pl.BlockSpec((1, d), lambda i, *_: (0, 0))] if final else []
    extra_args = [final_g] if final else []
    return pl.pallas_call(
        functools.partial(_combine_kernel, tm=tm, final=final),
        out_shape=jax.ShapeDtypeStruct((n, d), F32),
        grid_spec=pltpu.PrefetchScalarGridSpec(
            num_scalar_prefetch=1,
            grid=(n // tm,),
            in_specs=[pl.BlockSpec(memory_space=pl.ANY),
                      pl.BlockSpec((tm, d), lambda i, *_: (i, 0)),
                      pl.BlockSpec((None, 1, d), lambda i, *_: (i // tiles, 0, 0))] + extra_specs,
            out_specs=pl.BlockSpec((tm, d), lambda i, *_: (i, 0)),
            scratch_shapes=[pltpu.VMEM((tm * ROW_SUB, LANES), F32), pltpu.SemaphoreType.DMA(())],
        ),
        compiler_params=pltpu.CompilerParams(dimension_semantics=("arbitrary",), vmem_limit_bytes=VMEM_LIMIT,
                                             disable_bounds_checks=True),
        name="moe_combine",
    )(pos, ys, x1, g2, *extra_args)


def _final_kernel(x_ref, g_ref, o_ref):
    x = x_ref[...]
    ms = jnp.mean(x * x, axis=-1, keepdims=True)
    o_ref[...] = x * lax.rsqrt(ms + EPS) * g_ref[...]


def _final_norm(x, g, tm):
    n, d = x.shape
    return pl.pallas_call(
        _final_kernel,
        out_shape=jax.ShapeDtypeStruct((n, d), F32),
        grid=(n // tm,),
        in_specs=[pl.BlockSpec((tm, d), lambda i: (i, 0)), pl.BlockSpec((1, d), lambda i: (0, 0))],
        out_specs=pl.BlockSpec((tm, d), lambda i: (i, 0)),
        compiler_params=_cparams(("arbitrary",)),
        name="final_norm",
    )(x, g)


def _step_kernel(z_ref, xbc_ref, dt_ref, cst_ref, h_ref,
                 cw_ref, cb_ref, dtb_ref, alog_ref, dskip_ref, ng_ref, *rest):
    yn_ref, cnew_ref, hnew_ref, col_scr = rest[-4:]
    x_new = xbc_ref[...]
    acc = cb_ref[...] + cw_ref[3:4, :] * x_new
    for k in range(SSD_CONV - 1):
        acc = acc + cw_ref[k:k + 1, :] * cst_ref[k:k + 1, :]
    cnew_ref[0:1, :] = cst_ref[1:2, :]
    cnew_ref[1:2, :] = cst_ref[2:3, :]
    cnew_ref[2:3, :] = x_new
    xc = _silu(acc)
    xs = xc[:, 0:SSD_INNER]

    e_heads = _head_expand_matrix(LANES, SSD_HEADS, SSD_HEAD_DIM)
    dt_raw8 = jnp.broadcast_to(dt_ref[...], (8, LANES))
    dt_full = _softplus(_mm01(dt_raw8, e_heads, 3)[0:1, :] + dtb_ref[...])
    a_full = -jnp.exp(alog_ref[...])
    dec_full = jnp.exp(dt_full * a_full)
    xdt = xs * dt_full
    col_scr[...] = jnp.zeros_like(col_scr)
    col_scr[0:1, :] = xdt
    col_scr[1:2, :] = dec_full
    cols = col_scr[...].T
    y_parts = []
    gw = SSD_HPG * SSD_HEAD_DIM
    for g in range(SSD_GROUPS):
        b_off = SSD_INNER + g * SSD_STATE
        c_off = SSD_INNER + SSD_GROUPS * SSD_STATE + g * SSD_STATE
        bm = xc[:, b_off:b_off + SSD_STATE]
        cm = xc[:, c_off:c_off + SSD_STATE]
        g0 = g * gw
        hn = h_ref[g0:g0 + gw, :] * cols[g0:g0 + gw, 1:2] + cols[g0:g0 + gw, 0:1] * bm
        hnew_ref[g0:g0 + gw, :] = hn
        t = (hn * cm).T
        y_parts.append(jnp.sum(t, axis=0, keepdims=True))
    y = jnp.concatenate(y_parts, axis=1) + dskip_ref[...] * xs
    y = y * _silu(z_ref[...])
    outs = []
    for g in range(SSD_GROUPS):
        yg = y[:, g * gw:(g + 1) * gw]
        ms = jnp.mean(yg * yg, axis=-1, keepdims=True)
        outs.append(yg * lax.rsqrt(ms + EPS) * ng_ref[:, g * gw:(g + 1) * gw])
    yn_ref[...] = jnp.concatenate(outs, axis=1)


def _step_sample(layer, z, xbc, dt_raw, state_conv, state_ssm, prev_ssm,
                 conv_w, conv_b, dtb_full, alog_full, dskip_full, ssd_norm_g):
    bsz = z.shape[0]
    row3 = lambda w: pl.BlockSpec((None, 1, w), lambda b: (b, 0, 0))
    const = lambda a: pl.BlockSpec(a.shape, lambda b: (0,) * a.ndim)
    alias_args = [] if prev_ssm is None else [prev_ssm]
    n_in = 11
    return pl.pallas_call(
        _step_kernel,
        out_shape=(
            jax.ShapeDtypeStruct((bsz, 1, SSD_INNER), F32),
            jax.ShapeDtypeStruct((bsz, SSD_CONV - 1, SSD_CONV_DIM), F32),
            jax.ShapeDtypeStruct(state_ssm.shape, F32),
        ),
        grid=(bsz,),
        in_specs=[
            row3(SSD_INNER), row3(SSD_CONV_DIM), row3(LANES),
            pl.BlockSpec((None, None, SSD_CONV - 1, SSD_CONV_DIM), lambda b: (layer, b, 0, 0)),
            pl.BlockSpec((None, None, SSD_INNER, SSD_STATE), lambda b: (layer, b, 0, 0)),
            const(conv_w), const(conv_b), const(dtb_full), const(alog_full), const(dskip_full), const(ssd_norm_g),
        ] + [pl.BlockSpec(memory_space=pl.ANY)] * len(alias_args),
        out_specs=(
            row3(SSD_INNER),
            pl.BlockSpec((None, SSD_CONV - 1, SSD_CONV_DIM), lambda b: (b, 0, 0)),
            pl.BlockSpec((None, None, SSD_INNER, SSD_STATE), lambda b: (layer, b, 0, 0)),
        ),
        input_output_aliases={n_in: 2} if alias_args else {},
        scratch_shapes=[pltpu.VMEM((LANES, SSD_INNER), F32)],
        compiler_params=_cparams(("arbitrary",)),
        name="step_sample",
    )(z.reshape(bsz, 1, -1), xbc.reshape(bsz, 1, -1), dt_raw.reshape(bsz, 1, -1), state_conv, state_ssm,
      conv_w, conv_b, dtb_full, alog_full, dskip_full, ssd_norm_g, *alias_args)


_HEAD_SPLIT = 1
_HROWS = ATT_OUT // _HEAD_SPLIT


def _cache_attn_kernel(*refs, n_alias):
    qkv_ref = refs[0]
    cache_refs = refs[1:7]
    out_refs = refs[7 + n_alias:13 + n_alias]
    o_ref = refs[13 + n_alias]
    b = pl.program_id(0)
    nh = _HROWS // ATT_HEAD_DIM
    scale = ATT_HEAD_DIM ** -0.5
    qkv = qkv_ref[...]
    lane_b = lax.broadcasted_iota(jnp.int32, qkv.shape, 2)
    cols = jnp.sum(jnp.where(lane_b == b, qkv, 0.0), axis=-1, keepdims=True)

    def per_head_rows(v):
        return jnp.concatenate([jnp.broadcast_to(v[h:h + 1, :], (ATT_HEAD_DIM, 1)) for h in range(nh)], axis=0)

    o_g, lse_g = [], []
    for gi, (_, dil) in enumerate(ATT_PATTERNS):
        k_ref, v_ref = cache_refs[2 * gi], cache_refs[2 * gi + 1]
        ko_ref, vo_ref = out_refs[2 * gi], out_refs[2 * gi + 1]
        q = cols[gi]
        k_new = cols[ATT_GROUPS + gi]
        v_new = cols[2 * ATT_GROUPS + gi]
        kk = k_ref[...]
        vv = v_ref[...]
        length = kk.shape[1]
        lane = lax.broadcasted_iota(jnp.int32, (nh, length), 1)
        s = jnp.sum((kk * q).reshape(nh, ATT_HEAD_DIM, length), axis=1) * scale
        s = jnp.where((lane & (dil - 1)) == 0, s, -jnp.inf)
        s_new = jnp.sum((k_new * q).reshape(nh, ATT_HEAD_DIM, 1), axis=1) * scale
        m = jnp.maximum(jnp.max(s, axis=-1, keepdims=True), s_new)
        e = jnp.exp(s - m)
        e_new = jnp.exp(s_new - m)
        den = jnp.sum(e, axis=-1, keepdims=True) + e_new
        acc = jnp.sum(vv.reshape(nh, ATT_HEAD_DIM, length) * e[:, None, :], axis=-1, keepdims=True)
        acc = acc.reshape(_HROWS, 1) + per_head_rows(e_new) * v_new
        o_g.append(acc / per_head_rows(den))
        lse_g.append(per_head_rows(m + jnp.log(den)))
        lane_full = lax.broadcasted_iota(jnp.int32, kk.shape, 1)
        last = lane_full == length - 1
        ko_ref[...] = jnp.where(last, k_new, pltpu.roll(kk, length - 1, axis=1))
        vo_ref[...] = jnp.where(last, v_new, pltpu.roll(vv, length - 1, axis=1))
    mx = jnp.maximum(jnp.maximum(lse_g[0], lse_g[1]), lse_g[2])
    w = [jnp.exp(l - mx) for l in lse_g]
    tot = w[0] + w[1] + w[2]
    o_ref[...] = (w[0] / tot) * o_g[0] + (w[1] / tot) * o_g[1] + (w[2] / tot) * o_g[2]


def _cache_attn(layer, qkv_t, cache_views, prev_outs):
    depth, bsz, _, _ = cache_views[0].shape
    n_alias = 0 if prev_outs is None else len(prev_outs)
    q4 = qkv_t.reshape(3 * ATT_GROUPS, _HEAD_SPLIT, _HROWS, bsz)
    blk = lambda c: pl.BlockSpec((None, None, _HROWS, c.shape[3]), lambda b, hh: (layer, b, hh, 0))
    any_spec = pl.BlockSpec(memory_space=pl.ANY)
    args = [q4] + list(cache_views) + ([] if prev_outs is None else list(prev_outs))
    res = pl.pallas_call(
        functools.partial(_cache_attn_kernel, n_alias=n_alias),
        out_shape=tuple(jax.ShapeDtypeStruct(c.shape, c.dtype) for c in cache_views)
        + (jax.ShapeDtypeStruct((bsz, _HEAD_SPLIT, _HROWS, 1), F32),),
        grid=(bsz, _HEAD_SPLIT),
        in_specs=[pl.BlockSpec((3 * ATT_GROUPS, None, _HROWS, bsz), lambda b, hh: (0, hh, 0, 0))]
        + [blk(c) for c in cache_views] + [any_spec] * n_alias,
        out_specs=tuple(blk(c) for c in cache_views)
        + (pl.BlockSpec((None, None, _HROWS, 1), lambda b, hh: (b, hh, 0, 0)),),
        input_output_aliases={7 + i: i for i in range(n_alias)},
        compiler_params=_cparams(("arbitrary", "arbitrary")),
        name="cache_attn",
    )(*args)
    return list(res[:6]), res[6].reshape(bsz, ATT_OUT)


def _pad_lanes(v, width=LANES):
    return jnp.pad(v, [(0, 0)] * (v.ndim - 1) + [(0, width - v.shape[-1])])


def kernel(x_prompt, x_sample, c_prompt, c_sample, state_conv, state_ssm, cache_k_win128, cache_v_win128, cache_k_win512, cache_v_win512, cache_k_win2048, cache_v_win2048, norm1_g, w_ada, b_ada, w_in, conv_w, conv_b, dt_bias, a_log, d_skip, ssd_norm_g, w_ssd_proj, w_attn_proj, w_out, norm2_g, w_router_group, b_router_group, w_router_expert, b_router_expert, w_exp_gate, w_exp_up, w_exp_down, final_norm_g):
    depth = w_in.shape[0]
    bp, seq, d = x_prompt.shape
    bs = x_sample.shape[0]
    assert x_sample.shape[1] == 1 and d == D_MODEL and w_in.shape[2] == IN_WIDTH
    assert seq % (ATT_PATTERNS[-1][1] * ATT_BLOCK) == 0
    n_p = bp * seq
    caches = ((cache_k_win128, cache_v_win128), (cache_k_win512, cache_v_win512),
              (cache_k_win2048, cache_v_win2048))

    rows = bp + bs
    rows_pad = -(-rows // 8) * 8
    c_all = jnp.pad(jnp.concatenate([c_prompt, c_sample], axis=0), ((0, rows_pad - rows), (0, 0)))
    mods = _modulation(c_all, w_ada, b_ada)

    xp = x_prompt.reshape(n_p, d)
    xs = x_sample.reshape(bs, d)
    outs = {k: [] for k in ("conv_p", "conv_s", "ssm_p", "ssm_s")}
    kv_p = [[[], []] for _ in ATT_PATTERNS]
    tiles = seq // TOK_TILE
    assert bs <= LANES and all(c.shape[2] == win for (win, _), pair in zip(ATT_PATTERNS, caches) for c in pair)
    cache_views = [jnp.transpose(c, (0, 1, 3, 4, 2)).reshape(depth, bs, ATT_OUT, c.shape[2])
                   for pair in caches for c in pair]
    shifted = None
    ssm_s_all = None
    w_in_t = jnp.swapaxes(w_in, 1, 2)
    w_chunks = _prep_w_in(w_in_t)

    for l in range(depth):
        mod_p = mods[l, :bp]
        mod_s = mods[l, bp:bp + bs]
        wd = _pad_lanes(w_in_t[l, OFF_DT:OFF_QKV].T)
        g1n = norm1_g[l].reshape(1, d)
        g2n = norm2_g[l].reshape(1, d)
        cw = conv_w[l]
        cb = conv_b[l].reshape(1, -1)
        dskip_full = jnp.repeat(d_skip[l], SSD_HEAD_DIM).reshape(1, -1)
        ssd_g = ssd_norm_g[l].reshape(1, -1)
        w_router = _pad_lanes(jnp.concatenate([w_router_expert[l], w_router_group[l]], axis=1))
        b_router = _pad_lanes(jnp.concatenate([b_router_expert[l], b_router_group[l]], axis=0).reshape(1, -1))

        z, xbc, dt_raw, a0, a1, a2, gates = _in_proj_prompt(
            l, xp, mod_p[:, 0:2 * d].reshape(bp, 1, 2 * d), g1n, w_chunks, wd, bp, seq)
        yn, ssm_new = _ssd_prompt(xbc, z, dt_raw, cw, cb, _pad_lanes(dt_bias[l].reshape(1, -1)),
                                  _pad_lanes(a_log[l].reshape(1, -1)), dskip_full, ssd_g, bp, seq)
        outs["ssm_p"].append(ssm_new.reshape(bp, SSD_GROUPS, SSD_HPG, SSD_HEAD_DIM, SSD_STATE))
        outs["conv_p"].append(xbc.reshape(bp, seq, SSD_CONV_DIM)[:, seq - (SSD_CONV - 1):])
        a_groups = [a0.reshape(bp, tiles, 1, TOK_TILE, 3 * ATT_OUT), a1, a2]
        o_list, lse_list = [], []
        for gi, (win, dil) in enumerate(ATT_PATTERNS):
            o_g, lse_g = _attn_prompt(a_groups[gi], gi, bp, seq)
            o_list.append(o_g)
            lse_list.append(lse_g)
            keep = min(win, seq)
            rows = TOK_TILE // dil
            if keep >= TOK_TILE:
                nt = keep // TOK_TILE
                tail = a_groups[gi][:, tiles - nt:, :, :, ATT_OUT:3 * ATT_OUT]
                tail = jnp.transpose(tail, (0, 1, 3, 2, 4)).reshape(bp, keep, 2 * ATT_OUT)
            else:
                assert dil == 1
                tail = a_groups[gi][:, tiles - 1, 0, TOK_TILE - keep:, ATT_OUT:3 * ATT_OUT]
            tail = tail.astype(F32)
            kv_p[gi][0].append(tail[:, :, 0:ATT_OUT].reshape(bp, keep, ATT_HEADS, ATT_HEAD_DIM))
            kv_p[gi][1].append(tail[:, :, ATT_OUT:2 * ATT_OUT].reshape(bp, keep, ATT_HEADS, ATT_HEAD_DIM))
        mod_post = jnp.concatenate([mod_p[:, 2 * d:3 * d], mod_p[:, 3 * d:5 * d]], axis=1).reshape(bp, 1, 3 * d)
        x1, rows, slot, cnt = _post(xp, yn, (o_list, lse_list), gates, mod_post, g2n,
                                    w_ssd_proj[l].astype(BF16), w_attn_proj[l].astype(BF16),
                                    w_out[l].astype(BF16), w_router, b_router, passes=1, merge_attn=True,
                                    tm=TOK_TILE, rows_per_mod=seq, h2_dtype=F32)
        pos = slot[:, 0].astype(jnp.int32)
        blk, e_lo, e_hi, n_act, z_need, z_blk = _moe_plan(cnt[0, :MOE_CLASSES].astype(jnp.int32), n_p)
        xs_sorted = _dispatch(pos, z_need, z_blk, rows, n_p)
        ys_sorted = _moe_sparse(l, (blk, e_lo, e_hi, n_act), xs_sorted, w_router, b_router,
                                w_exp_gate, w_exp_up, w_exp_down, n_p)
        xp = _combine(pos, ys_sorted, x1, mod_p[:, 5 * d:6 * d].reshape(bp, 1, d), seq,
                      final_g=final_norm_g.reshape(1, d) if l == depth - 1 else None)

        u = _in_proj_sample(l, xs, mod_s[:, 0:2 * d], g1n, w_in_t)
        z_s = u[:, 0:OFF_XBC]
        xbc_s = u[:, OFF_XBC:OFF_DT]
        dt_s = _pad_lanes(u[:, OFF_DT:OFF_QKV])
        qkv_s = u[:, OFF_QKV:OFF_GATE]
        gates_s = u[:, OFF_GATE:IN_WIDTH]
        yn_s, conv_new, ssm_s_all = _step_sample(
            l, z_s, xbc_s, dt_s, state_conv, state_ssm.reshape(depth, bs, SSD_INNER, SSD_STATE), ssm_s_all,
            cw, cb, jnp.repeat(dt_bias[l], SSD_HEAD_DIM).reshape(1, -1),
            jnp.repeat(a_log[l], SSD_HEAD_DIM).reshape(1, -1), dskip_full, ssd_g)
        outs["conv_s"].append(conv_new)
        shifted, o_s = _cache_attn(l, qkv_s.T, cache_views, shifted)
        mod_post_s = jnp.concatenate([mod_s[:, 2 * d:3 * d], mod_s[:, 3 * d:5 * d]], axis=1)
        x1_s, h2_s, gate_s = _post(xs, yn_s.reshape(bs, SSD_INNER), o_s, gates_s, mod_post_s,
                                   g2n, w_ssd_proj[l], w_attn_proj[l], w_out[l], w_router, b_router,
                                   passes=3, merge_attn=False, tm=bs, rows_per_mod=1, h2_dtype=F32)
        xs = _moe(l, h2_s, gate_s, x1_s, mod_s[:, 5 * d:6 * d], w_exp_gate, w_exp_up, w_exp_down,
                  passes=3, tm=bs, rows_per_mod=1)

    fg = final_norm_g.reshape(1, d)
    y_prompt = xp.reshape(bp, seq, d)
    y_sample = _final_norm(xs, fg, bs).reshape(bs, 1, d)

    shifted = [jnp.transpose(s.reshape(depth, bs, ATT_HEADS, ATT_HEAD_DIM, s.shape[3]), (0, 1, 4, 2, 3))
               for s in shifted]

    st = jnp.stack
    res = [y_prompt, y_sample, st(outs["conv_p"]), st(outs["conv_s"]), st(outs["ssm_p"]),
           ssm_s_all.reshape(state_ssm.shape)]
    for gi in range(ATT_GROUPS):
        res += [st(kv_p[gi][0]), shifted[2 * gi], st(kv_p[gi][1]), shifted[2 * gi + 1]]
    return tuple(res)
```
